```python
import math
import jax, jax.numpy as jnp
from jax import lax
import numpy as np

D_MODEL = 4096
BATCH = 4
SEQ = 2048
DEPTH = 4
DEC_BATCH = 8
DEC_SEQ = 8
PAST_LEN = 8192
PAGE_SIZE = 128

W_DIFF = 3 * D_MODEL // 8
W_DSA = 3 * D_MODEL // 8
W_MEM = D_MODEL // 4
H_DIFF = 12
D_DH = W_DIFF // (2 * H_DIFF)
H_DSA = 12
D_DSA = W_DSA // H_DSA
H_IDX = 16
D_IDX = 64
H_MEM = 4
D_MEM = W_MEM // H_MEM
N_MEM = 256
TOPK_MAX = 256
Q_BLOCK = 128
ROPE_THETA = 10000.0
EPS = 1e-6
IN_SPLITS = (2 * H_DIFF * D_DH, 2 * H_DIFF * D_DH, 2 * H_DIFF * D_DH,
             H_DSA * D_DSA, D_DSA, D_DSA,
             H_IDX * D_IDX, D_IDX, H_IDX,
             H_MEM * D_MEM,
             W_DIFF + W_DSA + W_MEM)
C_IN = sum(IN_SPLITS)

kernel_name = 'hymba_diff_dsa_mem_decode_step'


def rmsnorm(x, g):
    xf = x.astype(jnp.float32)
    y = xf * lax.rsqrt(jnp.mean(xf * xf, axis=-1, keepdims=True) + EPS) * g.astype(jnp.float32)
    return y.astype(x.dtype)


def rope(x, pos):
    half = x.shape[-1] // 2
    inv = ROPE_THETA ** (-jnp.arange(half, dtype=jnp.float32) / half)
    ang = pos.astype(jnp.float32)[:, None] * inv[None, :]
    shape = (1, pos.shape[0]) + (1,) * (x.ndim - 3) + (half,)
    cos = jnp.cos(ang).reshape(shape)
    sin = jnp.sin(ang).reshape(shape)
    xf = x.astype(jnp.float32)
    x1, x2 = xf[..., :half], xf[..., half:]
    return jnp.concatenate([x1 * cos - x2 * sin, x2 * cos + x1 * sin], axis=-1).astype(x.dtype)


def gather_pages(pool, table):
    g = pool[table]
    return g.reshape((table.shape[0], table.shape[1] * pool.shape[1]) + pool.shape[2:])


def over_query_blocks(fn, per_query, q_pos):
    t = q_pos.shape[0]
    nb = t // Q_BLOCK
    split = lambda a: jnp.moveaxis(a.reshape((a.shape[0], nb, Q_BLOCK) + a.shape[2:]), 1, 0)
    xs = tuple(split(a) for a in per_query) + (q_pos.reshape(nb, Q_BLOCK),)
    ys = lax.map(lambda args: fn(*args), xs)
    ys = jnp.moveaxis(ys, 0, 1)
    return ys.reshape((ys.shape[0], t) + ys.shape[3:])


def mixer_inputs(x, pos, p):
    b, t = x.shape[:2]
    h = rmsnorm(x, p['g_in'])
    z = jnp.einsum('btd,dc->btc', h, p['w_in'])
    offs = [int(o) for o in np.cumsum(IN_SPLITS)[:-1]]
    qd, kd, vd, qs, ks, vs, qi, ki, wi, qm, gate = jnp.split(z, offs, axis=-1)
    return dict(
        qd=rope(rmsnorm(qd.reshape(b, t, H_DIFF, 2, D_DH), p['g_q_diff']), pos),
        kd=rope(rmsnorm(kd.reshape(b, t, H_DIFF, 2, D_DH), p['g_k_diff']), pos),
        vd=vd.reshape(b, t, H_DIFF, 2 * D_DH),
        qs=rope(rmsnorm(qs.reshape(b, t, H_DSA, D_DSA), p['g_q_dsa']), pos),
        ks=rope(rmsnorm(ks, p['g_k_dsa']), pos),
        vs=vs,
        qi=rope(qi.reshape(b, t, H_IDX, D_IDX), pos),
        ki=rope(rmsnorm(ki, p['g_k_idx']), pos),
        wi=wi,
        qm=rmsnorm(qm.reshape(b, t, H_MEM, D_MEM), p['g_q_mem']),
        gate=gate)


def memory_kv(mem, p):
    b, m = mem.shape[:2]
    z = jnp.einsum('bmd,dc->bmc', rmsnorm(mem, p['g_mem']), p['w_mem_kv'])
    k, v = jnp.split(z, 2, axis=-1)
    return rmsnorm(k.reshape(b, m, H_MEM, D_MEM), p['g_k_mem']), v.reshape(b, m, H_MEM, D_MEM)


def diff_attention(q, k, v, q_pos, k_pos, lam):
    s = jnp.einsum('bthcd,blhcd->bhctl', q, k).astype(jnp.float32) * D_DH ** -0.5
    causal = k_pos[None, :] <= q_pos[:, None]
    s = jnp.where(causal[None, None, None], s, -jnp.inf)
    pr = jax.nn.softmax(s, axis=-1)
    a = pr[:, :, 0] - lam * pr[:, :, 1]
    return jnp.einsum('bhtl,blhe->bthe', a.astype(v.dtype), v)


def dsa_attention(q, qi, wi, q_pos, k, v, ki, k_pos, top_k):
    sc = jnp.einsum('bthd,bld->bthl', qi, ki).astype(jnp.float32) * D_IDX ** -0.5
    idx = jnp.einsum('bthl,bth->btl', jax.nn.relu(sc), wi.astype(jnp.float32) * H_IDX ** -0.5)
    causal = k_pos[None, :] <= q_pos[:, None]
    idx = jnp.where(causal[None], idx, -jnp.inf)
    _, sel = lax.top_k(idx, top_k)
    valid = k_pos[sel] <= q_pos[None, :, None]
    ks = jax.vmap(lambda a, i: a[i])(k, sel)
    vs = jax.vmap(lambda a, i: a[i])(v, sel)
    s = jnp.einsum('bthd,btkd->bthk', q, ks).astype(jnp.float32) * D_DSA ** -0.5
    s = jnp.where(valid[:, :, None, :], s, -jnp.inf)
    pr = jax.nn.softmax(s, axis=-1)
    return jnp.einsum('bthk,btkd->bthd', pr.astype(vs.dtype), vs)


def mem_attention(q, k, v):
    s = jnp.einsum('bthd,bmhd->bhtm', q, k).astype(jnp.float32) * D_MEM ** -0.5
    pr = jax.nn.softmax(s, axis=-1)
    return jnp.einsum('bhtm,bmhd->bthd', pr.astype(v.dtype), v)


def mixer_output(x, od, os_, om, gate, p, lam_init):
    b, t = x.shape[:2]
    od = rmsnorm(od, p['g_sub_diff']) * (1.0 - lam_init)
    o = jnp.concatenate([od.reshape(b, t, W_DIFF), os_.reshape(b, t, W_DSA),
                         om.reshape(b, t, W_MEM)], axis=-1)
    o = o * jax.nn.silu(gate)
    return x + jnp.einsum('btc,cd->btd', o, p['w_out'])


def setup_inputs(seed: int = 0) -> dict:
    key = jax.random.key(seed)
    ks = jax.random.split(key, 32)
    n_pages = PAST_LEN // PAGE_SIZE
    n_used = DEC_BATCH * n_pages
    n_phys = n_used + n_used // 4
    f32 = jnp.float32
    nrm = lambda k, shape, s=1.0: jax.random.normal(k, shape, f32) * s
    gain = lambda k, shape: 1.0 + 0.02 * jax.random.normal(k, shape, f32)
    perm = jax.random.permutation(ks[0], n_phys)
    page_table = perm[:n_used].reshape(DEC_BATCH, n_pages).astype(jnp.int32)
    return {
        'x_prompt': nrm(ks[1], (BATCH, SEQ, D_MODEL)),
        'x_sample': nrm(ks[2], (DEC_BATCH, DEC_SEQ, D_MODEL)),
        'mem_prompt': nrm(ks[3], (BATCH, N_MEM, D_MODEL)),
        'cache_diff_k': nrm(ks[4], (DEPTH, n_phys, PAGE_SIZE, H_DIFF, 2, D_DH)),
        'cache_diff_v': nrm(ks[5], (DEPTH, n_phys, PAGE_SIZE, H_DIFF, 2 * D_DH)),
        'cache_dsa_k': nrm(ks[6], (DEPTH, n_phys, PAGE_SIZE, D_DSA)),
        'cache_dsa_v': nrm(ks[7], (DEPTH, n_phys, PAGE_SIZE, D_DSA)),
        'cache_idx_k': nrm(ks[8], (DEPTH, n_phys, PAGE_SIZE, D_IDX)),
        'cache_mem_k': nrm(ks[9], (DEPTH, DEC_BATCH, N_MEM, H_MEM, D_MEM)),
        'cache_mem_v': nrm(ks[10], (DEPTH, DEC_BATCH, N_MEM, H_MEM, D_MEM)),
        'page_table': page_table,
        'w_in': nrm(ks[11], (DEPTH, D_MODEL, C_IN), D_MODEL ** -0.5),
        'w_out': nrm(ks[12], (DEPTH, W_DIFF + W_DSA + W_MEM, D_MODEL), (W_DIFF + W_DSA + W_MEM) ** -0.5),
        'w_mem_kv': nrm(ks[13], (DEPTH, D_MODEL, 2 * H_MEM * D_MEM), D_MODEL ** -0.5),
        'g_in': gain(ks[14], (DEPTH, D_MODEL)),
        'g_mem': gain(ks[15], (DEPTH, D_MODEL)),
        'g_q_diff': gain(ks[16], (DEPTH, D_DH)),
        'g_k_diff': gain(ks[17], (DEPTH, D_DH)),
        'g_sub_diff': gain(ks[18], (DEPTH, 2 * D_DH)),
        'lam_q1': nrm(ks[19], (DEPTH, D_DH), 0.1),
        'lam_k1': nrm(ks[20], (DEPTH, D_DH), 0.1),
        'lam_q2': nrm(ks[21], (DEPTH, D_DH), 0.1),
        'lam_k2': nrm(ks[22], (DEPTH, D_DH), 0.1),
        'g_q_dsa': gain(ks[23], (DEPTH, D_DSA)),
        'g_k_dsa': gain(ks[24], (DEPTH, D_DSA)),
        'g_k_idx': gain(ks[25], (DEPTH, D_IDX)),
        'g_q_mem': gain(ks[26], (DEPTH, D_MEM)),
        'g_k_mem': gain(ks[27], (DEPTH, D_MEM)),
    }


def reference(x_prompt, x_sample, mem_prompt, cache_diff_k, cache_diff_v, cache_dsa_k, cache_dsa_v,
              cache_idx_k, cache_mem_k, cache_mem_v, page_table, w_in, w_out, w_mem_kv, g_in, g_mem,
              g_q_diff, g_k_diff, g_sub_diff, lam_q1, lam_k1, lam_q2, lam_k2, g_q_dsa, g_k_dsa,
              g_k_idx, g_q_mem, g_k_mem):
    seq = x_prompt.shape[1]
    dec_seq = x_sample.shape[1]
    past = page_table.shape[1] * PAGE_SIZE
    pos_p = jnp.arange(seq, dtype=jnp.int32)
    pos_s = past + jnp.arange(dec_seq, dtype=jnp.int32)
    kpos_s = jnp.arange(past + dec_seq, dtype=jnp.int32)
    topk_p = min(TOPK_MAX, seq // 4)
    topk_s = min(TOPK_MAX, (past + dec_seq) // 4)
    x_p, x_s = x_prompt, x_sample
    pdk, pdv, psk, psv, pik, pmk, pmv = [], [], [], [], [], [], []
    sdk, sdv, ssk, ssv, sik = [], [], [], [], []
    for l in range(DEPTH):
        p = dict(w_in=w_in[l], w_out=w_out[l], w_mem_kv=w_mem_kv[l], g_in=g_in[l], g_mem=g_mem[l],
                 g_q_diff=g_q_diff[l], g_k_diff=g_k_diff[l], g_sub_diff=g_sub_diff[l],
                 g_q_dsa=g_q_dsa[l], g_k_dsa=g_k_dsa[l], g_k_idx=g_k_idx[l],
                 g_q_mem=g_q_mem[l], g_k_mem=g_k_mem[l])
        lam_init = 0.8 - 0.6 * math.exp(-0.3 * l)
        lam = (jnp.exp(jnp.sum(lam_q1[l] * lam_k1[l]).astype(jnp.float32))
               - jnp.exp(jnp.sum(lam_q2[l] * lam_k2[l]).astype(jnp.float32)) + lam_init)

        t = mixer_inputs(x_p, pos_p, p)
        mk, mv = memory_kv(mem_prompt, p)
        od = over_query_blocks(
            lambda q, qp: diff_attention(q, t['kd'], t['vd'], qp, pos_p, lam), (t['qd'],), pos_p)
        os_ = over_query_blocks(
            lambda q, qi, wi, qp: dsa_attention(q, qi, wi, qp, t['ks'], t['vs'], t['ki'], pos_p, topk_p),
            (t['qs'], t['qi'], t['wi']), pos_p)
        om = mem_attention(t['qm'], mk, mv)
        x_p = mixer_output(x_p, od, os_, om, t['gate'], p, lam_init)
        pdk.append(t['kd']); pdv.append(t['vd']); psk.append(t['ks']); psv.append(t['vs'])
        pik.append(t['ki']); pmk.append(mk); pmv.append(mv)

        u = mixer_inputs(x_s, pos_s, p)
        kd_all = jnp.concatenate([gather_pages(cache_diff_k[l], page_table), u['kd']], axis=1)
        vd_all = jnp.concatenate([gather_pages(cache_diff_v[l], page_table), u['vd']], axis=1)
        ks_all = jnp.concatenate([gather_pages(cache_dsa_k[l], page_table), u['ks']], axis=1)
        vs_all = jnp.concatenate([gather_pages(cache_dsa_v[l], page_table), u['vs']], axis=1)
        ki_all = jnp.concatenate([gather_pages(cache_idx_k[l], page_table), u['ki']], axis=1)
        od_s = diff_attention(u['qd'], kd_all, vd_all, pos_s, kpos_s, lam)
        os_s = dsa_attention(u['qs'], u['qi'], u['wi'], pos_s, ks_all, vs_all, ki_all, kpos_s, topk_s)
        om_s = mem_attention(u['qm'], cache_mem_k[l], cache_mem_v[l])
        x_s = mixer_output(x_s, od_s, os_s, om_s, u['gate'], p, lam_init)
        sdk.append(u['kd']); sdv.append(u['vd']); ssk.append(u['ks']); ssv.append(u['vs'])
        sik.append(u['ki'])

    return (x_p, x_s,
            jnp.stack(pdk), jnp.stack(pdv), jnp.stack(psk), jnp.stack(psv), jnp.stack(pik),
            jnp.stack(pmk), jnp.stack(pmv),
            jnp.stack(sdk), jnp.stack(sdv), jnp.stack(ssk), jnp.stack(ssv), jnp.stack(sik))
```

```python
import functools
import math

import jax
import jax.numpy as jnp
from jax import lax
from jax.experimental import pallas as pl
from jax.experimental.pallas import tpu as pltpu

EPS = 1e-6
ROPE_THETA = 10000.0
TOPK_MAX = 256
LANES = 128
D_DH = 64
D_DSA = 128
D_IDX = 64
H_IDX = 16
D_MEM = 256
NEG_BIG = -1e30
KEY_NEG_INF = -2139095041
INT_MIN = -2147483648
MXU_DTYPE = jnp.bfloat16
VMEM_LIMIT_BYTES = 52 * 1024 * 1024
PAGES_PER_STEP = 4

f32 = jnp.float32
i32 = jnp.int32


def _cparams(sem):
    return pltpu.CompilerParams(dimension_semantics=sem, vmem_limit_bytes=VMEM_LIMIT_BYTES)


def _dot(a, b):
    return jnp.dot(a, b, preferred_element_type=f32)


def _dot_nt(a, b):
    return lax.dot_general(a, b, (((1,), (1,)), ((), ())), preferred_element_type=f32)


def _iota(shape, dim):
    return lax.broadcasted_iota(i32, shape, dim)


def _group_mat(gsize):
    r = _iota((LANES, LANES), 0) // gsize
    c = _iota((LANES, LANES), 1) // gsize
    return (r == c).astype(f32).astype(MXU_DTYPE)


def _group_sum(x, gmat):
    hi = x.astype(MXU_DTYPE)
    lo = (x - hi.astype(f32)).astype(MXU_DTYPE)
    return _dot(hi, gmat) + _dot(lo, gmat)


def _rope_chunk(n, cos, sin_signed, half):
    if 2 * half == LANES:
        rot = pltpu.roll(n, half, 1)
    else:
        first = (_iota(n.shape, 1) % (2 * half)) < half
        rot = jnp.where(first, pltpu.roll(n, LANES - half, 1), pltpu.roll(n, half, 1))
    return n * cos + rot * sin_signed


def _f2key(x):
    b = pltpu.bitcast(x, i32)
    return b ^ (lax.shift_right_arithmetic(b, 31) & 0x7FFFFFFF)


def _rmsnorm_kernel(x_ref, g_ref, o_ref):
    x = x_ref[...]
    ms = jnp.mean(x * x, axis=-1, keepdims=True)
    o_ref[...] = (x * lax.rsqrt(ms + EPS) * g_ref[...]).astype(o_ref.dtype)


def _rmsnorm(x, g, tm):
    m, d = x.shape
    return pl.pallas_call(
        _rmsnorm_kernel,
        grid=(m // tm,),
        in_specs=[pl.BlockSpec((tm, d), lambda i: (i, 0)), pl.BlockSpec((1, d), lambda i: (0, 0))],
        out_specs=pl.BlockSpec((tm, d), lambda i: (i, 0)),
        out_shape=jax.ShapeDtypeStruct((m, d), MXU_DTYPE),
        compiler_params=_cparams(("arbitrary",)),
        name="rmsnorm",
    )(x, g.reshape(1, d))


def _epi_raw(z, aux, outs):
    for o in outs:
        o[...] = z.astype(o.dtype)


def _epi_silu(z, aux, outs):
    outs[0][...] = (z / (1.0 + jnp.exp(-z))).astype(outs[0].dtype)


def _epi_norm_rope(z, aux, outs, *, gsize, half, scale):
    gain, cos, sin = aux
    gmat = _group_mat(gsize)
    for c in range(z.shape[1] // LANES):
        sl = slice(c * LANES, (c + 1) * LANES)
        zc = z[:, sl]
        ss = _group_sum(zc * zc, gmat)
        n = zc * lax.rsqrt(ss * (1.0 / gsize) + EPS) * gain[:, sl]
        r = _rope_chunk(n, cos, sin, half)
        if scale != 1.0:
            r = r * scale
        for o in outs:
            o[:, sl] = r.astype(o.dtype)


def _epi_rope(z, aux, outs, *, half, scale):
    cos, sin = aux
    for c in range(z.shape[1] // LANES):
        sl = slice(c * LANES, (c + 1) * LANES)
        r = _rope_chunk(z[:, sl], cos, sin, half) * scale
        outs[0][:, sl] = r.astype(outs[0].dtype)


def _epi_norm256(z, aux, outs, *, scale):
    (gain,) = aux
    gmat = _group_mat(LANES)
    for c in range(z.shape[1] // D_MEM):
        a = z[:, c * D_MEM:c * D_MEM + LANES]
        b = z[:, c * D_MEM + LANES:(c + 1) * D_MEM]
        ss = _group_sum(a * a, gmat) + _group_sum(b * b, gmat)
        inv = lax.rsqrt(ss * (1.0 / D_MEM) + EPS)
        for k, v in enumerate((a, b)):
            sl = slice(c * D_MEM + k * LANES, c * D_MEM + (k + 1) * LANES)
            r = v * inv * gain[:, sl]
            if scale != 1.0:
                r = r * scale
            for o in outs:
                o[:, sl] = r.astype(o.dtype)


def _epi_kiwi(z, aux, outs):
    gain, cos, sin = aux
    gmat = _group_mat(D_IDX)
    ss = _group_sum(z * z, gmat)
    n = z * lax.rsqrt(ss * (1.0 / D_IDX) + EPS) * gain
    r = _rope_chunk(n, cos, sin, D_IDX // 2)
    lo = _iota(z.shape, 1) < D_IDX
    outs[0][...] = jnp.where(lo, r, z)
    r_lo = jnp.where(lo, r, 0.0)
    outs[1][...] = (r_lo + pltpu.roll(r_lo, D_IDX, 1)).astype(outs[1].dtype)


def _proj_kernel(*refs, epi, n_aux):
    h_ref, w_ref = refs[:2]
    aux = [r[...] for r in refs[2:2 + n_aux]]
    outs = refs[2 + n_aux:]
    z = _dot(h_ref[...], w_ref[...])
    epi(z, aux, outs)


def _proj(h, w, epi, out_dtypes, aux=(), *, tm, tn, name):
    m, k = h.shape
    n = w.shape[1]
    tn = min(tn, n)
    assert m % tm == 0 and n % tn == 0
    in_specs = [pl.BlockSpec((tm, k), lambda i, j: (i, 0)), pl.BlockSpec((k, tn), lambda i, j: (0, j))]
    args = [h, w]
    for kind, a in aux:
        if kind == "col":
            in_specs.append(pl.BlockSpec((1, tn), lambda i, j: (0, j)))
        else:
            nrb = a.shape[0] // tm
            in_specs.append(pl.BlockSpec((tm, LANES), lambda i, j, nrb=nrb: (i % nrb, 0)))
        args.append(a)
    outs = pl.pallas_call(
        functools.partial(_proj_kernel, epi=epi, n_aux=len(aux)),
        grid=(m // tm, n // tn),
        in_specs=in_specs,
        out_specs=[pl.BlockSpec((tm, tn), lambda i, j: (i, j)) for _ in out_dtypes],
        out_shape=[jax.ShapeDtypeStruct((m, n), dt) for dt in out_dtypes],
        compiler_params=_cparams(("arbitrary", "arbitrary")),
        name=name,
    )(*args)
    return outs


def _outproj_kernel(od_ref, os_ref, om_ref, w_ref, x_ref, o_ref, *, wd, ws):
    acc = _dot(od_ref[...], w_ref[0:wd, :])
    acc += _dot(os_ref[...], w_ref[wd:wd + ws, :])
    acc += _dot(om_ref[...], w_ref[wd + ws:, :])
    o_ref[...] = x_ref[...] + acc


def _outproj(od, os_, om, w, x, *, tm, tn):
    m, d = x.shape
    wd, ws, wm = od.shape[1], os_.shape[1], om.shape[1]
    return pl.pallas_call(
        functools.partial(_outproj_kernel, wd=wd, ws=ws),
        grid=(m // tm, d // tn),
        in_specs=[
            pl.BlockSpec((tm, wd), lambda i, j: (i, 0)),
            pl.BlockSpec((tm, ws), lambda i, j: (i, 0)),
            pl.BlockSpec((tm, wm), lambda i, j: (i, 0)),
            pl.BlockSpec((wd + ws + wm, tn), lambda i, j: (0, j)),
            pl.BlockSpec((tm, tn), lambda i, j: (i, j)),
        ],
        out_specs=pl.BlockSpec((tm, tn), lambda i, j: (i, j)),
        out_shape=jax.ShapeDtypeStruct((m, d), f32),
        compiler_params=_cparams(("arbitrary", "arbitrary")),
        name="outproj",
    )(od, os_, om, w, x)


def _sub_rmsnorm_gate(od, g, post_scale, sg):
    ms = jnp.mean(od * od, axis=-1, keepdims=True)
    return od * lax.rsqrt(ms + EPS) * g * post_scale * sg.astype(f32)


def _diff_prompt_kernel(lam_ref, q_ref, k_ref, v_ref, g_ref, sg_ref, o_ref, *, tq, tk, post_scale):
    i = pl.program_id(2)
    q = q_ref[...]
    lo = _iota(q.shape, 1) < D_DH
    zero = jnp.zeros_like(q)
    qs = jnp.concatenate([jnp.where(lo, q, zero), jnp.where(lo, zero, q)], axis=0)

    def step(j, carry, masked):
        m, l, acc = carry
        kc = k_ref[pl.ds(j * tk, tk), :]
        vc = v_ref[pl.ds(j * tk, tk), :]
        s = _dot_nt(qs, kc)
        if masked:
            row = _iota(s.shape, 0) % tq + i * tq
            col = _iota(s.shape, 1) + j * tk
            s = jnp.where(col <= row, s, NEG_BIG)
        m_new = jnp.maximum(m, jnp.max(s, axis=1, keepdims=True))
        alpha = jnp.exp(m - m_new)
        p = jnp.exp(s - m_new)
        l = alpha * l + jnp.sum(p, axis=1, keepdims=True)
        acc = alpha * acc + _dot(p.astype(MXU_DTYPE), vc)
        return m_new, l, acc

    nd = tq // tk
    carry = (jnp.full((2 * tq, 1), NEG_BIG, f32), jnp.zeros((2 * tq, 1), f32), jnp.zeros((2 * tq, LANES), f32))
    carry = lax.fori_loop(0, i * nd, functools.partial(step, masked=False), carry)
    for d in range(nd):
        carry = step(i * nd + d, carry, True)
    _, l, acc = carry
    od = acc[:tq] / l[:tq] - lam_ref[0] * (acc[tq:] / l[tq:])
    o_ref[...] = _sub_rmsnorm_gate(od, g_ref[...], post_scale, sg_ref[...]).astype(o_ref.dtype)


def _diff_prompt(lam, qd, kd, vd, g_sub, sg, *, post_scale, tq=256, tk=256):
    b, t, w = qd.shape
    nh = w // LANES
    return pl.pallas_call(
        functools.partial(_diff_prompt_kernel, tq=tq, tk=tk, post_scale=post_scale),
        grid=(b, nh, t // tq),
        in_specs=[
            pl.BlockSpec(memory_space=pltpu.SMEM),
            pl.BlockSpec((None, tq, LANES), lambda bb, h, i: (bb, i, h)),
            pl.BlockSpec((None, t, LANES), lambda bb, h, i: (bb, 0, h)),
            pl.BlockSpec((None, t, LANES), lambda bb, h, i: (bb, 0, h)),
            pl.BlockSpec((1, LANES), lambda bb, h, i: (0, 0)),
            pl.BlockSpec((None, tq, LANES), lambda bb, h, i: (bb, i, h)),
        ],
        out_specs=pl.BlockSpec((None, tq, LANES), lambda bb, h, i: (bb, i, h)),
        out_shape=jax.ShapeDtypeStruct((b, t, w), MXU_DTYPE),
        compiler_params=_cparams(("arbitrary", "arbitrary", "arbitrary")),
        name="diff_prompt",
    )(lam, qd, kd, vd, g_sub, sg)


def _kth_largest_key(key_scr, nblk, rows, topk):
    def count_ge(cand):
        def body(c, acc):
            k = key_scr[pl.ds(pl.multiple_of(c * rows, rows), rows), :]
            return acc + jnp.sum((k >= cand).astype(i32).reshape(rows // 8, 8, LANES), axis=0)

        acc = lax.fori_loop(0, nblk, body, jnp.zeros((8, LANES), i32))
        return jnp.sum(acc, axis=0, keepdims=True)

    def bit_body(it, prefix):
        cand = prefix + lax.shift_left(jnp.int32(1), 31 - it)
        return jnp.where(count_ge(cand) >= topk, cand, prefix)

    kth = lax.fori_loop(0, 32, bit_body, jnp.full((1, LANES), INT_MIN, i32))
    return kth, count_ge


def _demote_surplus_ties(key_scr, nblk, rows, topk, kth, count_ge):
    tie = (count_ge(kth) > topk) & (kth > KEY_NEG_INF)

    @pl.when(jnp.max(tie.astype(i32)) > 0)
    def _():
        need = (topk - count_ge(kth + 1)).astype(f32)
        tri = (_iota((rows, rows), 0) >= _iota((rows, rows), 1)).astype(f32).astype(MXU_DTYPE)

        def body(c, run):
            sl = pl.ds(pl.multiple_of(c * rows, rows), rows)
            k = key_scr[sl, :]
            eq = k == kth
            incl = _dot(tri, eq.astype(f32).astype(MXU_DTYPE))
            drop = eq & ((run + incl) > need)
            key_scr[sl, :] = jnp.where(drop, KEY_NEG_INF, k)
            return run + incl[rows - 1:rows, :]

        lax.fori_loop(0, nblk, body, jnp.zeros((1, LANES), f32))


def _dsa_prompt_kernel(qs_ref, qi_ref, kiwi_ref, ki2_ref, ks_ref, vst_ref, sg_ref, o_ref,
                       key_scr, qm_scr, wt_scr, qst_scr, acc_scr, *, nh, topk, tc):
    qb = pl.program_id(1)
    tq = LANES
    nch = (qb * tq + tq + tc - 1) // tc

    lo = _iota((tq, LANES), 1) < D_IDX
    for h in range(H_IDX):
        chunk = qi_ref[:, (h // 2) * LANES:(h // 2 + 1) * LANES]
        keep = lo if h % 2 == 0 else jnp.logical_not(lo)
        qm_scr[h * tq:(h + 1) * tq, :] = jnp.where(keep, chunk, jnp.zeros_like(chunk))
    wt_scr[...] = kiwi_ref[...].T
    for h in range(nh):
        qst_scr[h * tq:(h + 1) * tq, :] = qs_ref[:, h * LANES:(h + 1) * LANES]

    tpos = qb * tq + _iota((1, LANES), 1)

    def idx_body(c, carry):
        sl = pl.ds(pl.multiple_of(c * tc, tc), tc)
        kc = ki2_ref[sl, :]
        acc = jnp.zeros((tc, LANES), f32)
        for h in range(H_IDX):
            sc = _dot_nt(kc, qm_scr[h * tq:(h + 1) * tq, :])
            acc = acc + jnp.maximum(sc, 0.0) * wt_scr[D_IDX + h:D_IDX + h + 1, :]
        acc = jnp.where(acc == 0.0, 0.0, acc)
        kpos = c * tc + _iota((tc, LANES), 0)
        acc = jnp.where(kpos <= tpos, acc, -jnp.inf)
        key_scr[sl, :] = _f2key(acc)
        return carry

    lax.fori_loop(0, nch, idx_body, 0)

    kth, count_ge = _kth_largest_key(key_scr, nch, tc, topk)
    _demote_surplus_ties(key_scr, nch, tc, topk, kth, count_ge)
    thr = jnp.maximum(kth, KEY_NEG_INF + 1)

    acc_scr[...] = jnp.zeros_like(acc_scr)

    def att_body(c, carry):
        m, l = carry
        sl = pl.ds(pl.multiple_of(c * tc, tc), tc)
        s = _dot_nt(ks_ref[sl, :], qst_scr[...])
        msk = key_scr[sl, :] >= thr
        s = jnp.where(jnp.concatenate([msk] * nh, axis=1), s, NEG_BIG)
        m_new = jnp.maximum(m, jnp.max(s, axis=0, keepdims=True))
        alpha = jnp.exp(m - m_new)
        p = jnp.exp(s - m_new)
        l = alpha * l + jnp.sum(p, axis=0, keepdims=True)
        acc_scr[...] = alpha * acc_scr[...] + _dot(vst_ref[c], p.astype(MXU_DTYPE))
        return m_new, l

    init = (jnp.full((1, nh * tq), NEG_BIG, f32), jnp.zeros((1, nh * tq), f32))
    _, l = lax.fori_loop(0, nch, att_body, init)
    out_t = acc_scr[...] / l
    for h in range(nh):
        sl = slice(h * LANES, (h + 1) * LANES)
        o_ref[:, sl] = (out_t[:, sl].T * sg_ref[:, sl].astype(f32)).astype(o_ref.dtype)


def _dsa_prompt(qs, qi, kiwi, ki2, ks, vst, sg, *, topk, tc=256):
    b, t, w = qs.shape
    nh = w // LANES
    tq = LANES
    assert t % tc == 0 and tc % tq == 0
    return pl.pallas_call(
        functools.partial(_dsa_prompt_kernel, nh=nh, topk=topk, tc=tc),
        grid=(b, t // tq),
        in_specs=[
            pl.BlockSpec((None, tq, w), lambda bb, i: (bb, i, 0)),
            pl.BlockSpec((None, tq, H_IDX * D_IDX), lambda bb, i: (bb, i, 0)),
            pl.BlockSpec((None, tq, LANES), lambda bb, i: (bb, i, 0)),
            pl.BlockSpec((None, t, LANES), lambda bb, i: (bb, 0, 0)),
            pl.BlockSpec((None, t, LANES), lambda bb, i: (bb, 0, 0)),
            pl.BlockSpec((None, t // tc, LANES, tc), lambda bb, i: (bb, 0, 0, 0)),
            pl.BlockSpec((None, tq, w), lambda bb, i: (bb, i, 1)),
        ],
        out_specs=pl.BlockSpec((None, tq, w), lambda bb, i: (bb, i, 0)),
        out_shape=jax.ShapeDtypeStruct((b, t, w), MXU_DTYPE),
        scratch_shapes=[
            pltpu.VMEM((t, LANES), i32),
            pltpu.VMEM((H_IDX * tq, LANES), MXU_DTYPE),
            pltpu.VMEM((LANES, LANES), f32),
            pltpu.VMEM((nh * tq, LANES), MXU_DTYPE),
            pltpu.VMEM((LANES, nh * tq), f32),
        ],
        compiler_params=_cparams(("arbitrary", "arbitrary")),
        name="dsa_prompt",
    )(qs, qi, kiwi, ki2, ks, vst, sg)


def _mem_attn_kernel(q_ref, k_ref, v_ref, sg_ref, o_ref):
    s = _dot_nt(q_ref[...], k_ref[...].astype(MXU_DTYPE))
    m = jnp.max(s, axis=1, keepdims=True)
    p = jnp.exp(s - m)
    l = jnp.sum(p, axis=1, keepdims=True)
    o = _dot(p.astype(MXU_DTYPE), v_ref[...].astype(MXU_DTYPE)) / l
    o_ref[...] = (o * sg_ref[...].astype(f32)).astype(o_ref.dtype)


def _mem_attn(q, k, v, sg, *, layer, sg_col0, tq):
    b, t, w = q.shape
    nh = w // D_MEM
    nm = k.shape[2]
    return pl.pallas_call(
        _mem_attn_kernel,
        grid=(b, t // tq, nh),
        in_specs=[
            pl.BlockSpec((None, tq, D_MEM), lambda bb, i, h: (bb, i, h)),
            pl.BlockSpec((None, None, nm, D_MEM), lambda bb, i, h: (layer, bb, 0, h)),
            pl.BlockSpec((None, None, nm, D_MEM), lambda bb, i, h: (layer, bb, 0, h)),
            pl.BlockSpec((None, tq, D_MEM), lambda bb, i, h: (bb, i, sg_col0 + h)),
        ],
        out_specs=pl.BlockSpec((None, tq, D_MEM), lambda bb, i, h: (bb, i, h)),
        out_shape=jax.ShapeDtypeStruct((b, t, w), MXU_DTYPE),
        compiler_params=_cparams(("arbitrary", "arbitrary", "arbitrary")),
        name="mem_attn",
    )(q, k, v, sg)


def _diff_sample_kernel(pt_ref, lam_ref, qbd_ref, *refs, nh, nt, pps, nsteps, post_scale):
    k_refs = refs[:pps]
    v_refs = refs[pps:2 * pps]
    knew_ref, vnew_ref, g_ref, sg_ref, o_ref, m_scr, l_scr, acc_scr = refs[2 * pps:]
    s_id = pl.program_id(1)
    nr = 2 * nt

    @pl.when(s_id == 0)
    def _():
        m_scr[...] = jnp.full_like(m_scr, NEG_BIG)
        l_scr[...] = jnp.zeros_like(l_scr)
        acc_scr[...] = jnp.zeros_like(acc_scr)

    def update(kp, vp, masked):
        s = _dot_nt(qbd_ref[...], kp.astype(MXU_DTYPE))
        if masked:
            tpos = _iota(s.shape, 0) % nt
            s = jnp.where(_iota(s.shape, 1) <= tpos, s, NEG_BIG)
        m = m_scr[...]
        m_new = jnp.maximum(m, jnp.max(s, axis=1, keepdims=True))
        alpha = jnp.exp(m - m_new)
        p = jnp.exp(s - m_new)
        l_scr[...] = alpha * l_scr[...] + jnp.sum(p, axis=1, keepdims=True)
        pv = _dot(p.astype(MXU_DTYPE), vp.astype(MXU_DTYPE))
        diag = jnp.concatenate(
            [pv[h * nr:(h + 1) * nr, h * LANES:(h + 1) * LANES] for h in range(nh)], axis=0)
        acc_scr[...] = alpha * acc_scr[...] + diag
        m_scr[...] = m_new

    @pl.when(s_id < nsteps)
    def _():
        for r in range(pps):
            update(k_refs[r][...], v_refs[r][...], False)

    @pl.when(s_id == nsteps)
    def _():
        update(knew_ref[...], vnew_ref[...], True)
        o = acc_scr[...] / l_scr[...]
        for h in range(nh):
            od = o[h * nr:h * nr + nt] - lam_ref[0] * o[h * nr + nt:(h + 1) * nr]
            sl = slice(h * LANES, (h + 1) * LANES)
            o_ref[:, sl] = _sub_rmsnorm_gate(od, g_ref[...], post_scale, sg_ref[:, sl]).astype(o_ref.dtype)


def _diff_sample(page_table, lam, qbd, cache_k, cache_v, knew, vnew, g_sub, sg, *, layer, nt, post_scale):
    db, rows, w = qbd.shape
    nh = w // LANES
    page = cache_k.shape[2]
    npages = page_table.shape[1]
    pps = PAGES_PER_STEP
    nsteps = npages // pps

    def page_map(r):
        def f(bb, s, pt):
            return (layer, pt[bb, jnp.minimum(s * pps + r, npages - 1)], 0, 0)
        return f

    page_specs = [pl.BlockSpec((None, None, page, w), page_map(r)) for r in range(pps)]
    grid_spec = pltpu.PrefetchScalarGridSpec(
        num_scalar_prefetch=1,
        grid=(db, nsteps + 1),
        in_specs=[
            pl.BlockSpec(memory_space=pltpu.SMEM),
            pl.BlockSpec((None, rows, w), lambda bb, s, pt: (bb, 0, 0)),
            *page_specs, *page_specs,
            pl.BlockSpec((None, page, w), lambda bb, s, pt: (bb, 0, 0)),
            pl.BlockSpec((None, page, w), lambda bb, s, pt: (bb, 0, 0)),
            pl.BlockSpec((1, LANES), lambda bb, s, pt: (0, 0)),
            pl.BlockSpec((None, nt, w), lambda bb, s, pt: (bb, 0, 0)),
        ],
        out_specs=pl.BlockSpec((None, nt, w), lambda bb, s, pt: (bb, 0, 0)),
        scratch_shapes=[
            pltpu.VMEM((rows, 1), f32),
            pltpu.VMEM((rows, 1), f32),
            pltpu.VMEM((rows, LANES), f32),
        ],
    )
    return pl.pallas_call(
        functools.partial(_diff_sample_kernel, nh=nh, nt=nt, pps=pps, nsteps=nsteps, post_scale=post_scale),
        grid_spec=grid_spec,
        out_shape=jax.ShapeDtypeStruct((db, nt, w), MXU_DTYPE),
        compiler_params=_cparams(("arbitrary", "arbitrary")),
        name="diff_sample",
    )(page_table, lam, qbd, *([cache_k] * pps), *([cache_v] * pps), knew, vnew, g_sub, sg)


def _dsa_sample_kernel(pt_ref, qst_ref, qit_ref, w_ref, *refs, nt, pps, nsteps, topk, page):
    ki_refs = refs[:pps]
    ks_refs = refs[pps:2 * pps]
    vs_refs = refs[2 * pps:3 * pps]
    kin_ref, ksn_ref, vsn_ref, sg_ref, o_ref, key_scr, ks_scr, vst_scr = refs[3 * pps:]
    s_id = pl.program_id(1)
    gsz = LANES // nt
    gmat = _group_mat(gsz)

    def index_keys(ki, causal_new):
        sc = _dot(ki.astype(MXU_DTYPE), qit_ref[...])
        acc = _group_sum(jnp.maximum(sc, 0.0) * w_ref[...], gmat)
        acc = jnp.where(acc == 0.0, 0.0, acc)
        if causal_new:
            kpos = _iota(acc.shape, 0)
            tpos = _iota(acc.shape, 1) // gsz
            acc = jnp.where(kpos <= tpos, acc, -jnp.inf)
        return _f2key(acc)

    @pl.when(s_id < nsteps)
    def _():
        for r in range(pps):
            blk = s_id * pps + r
            sl = pl.ds(pl.multiple_of(blk * page, page), page)
            key_scr[sl, :] = index_keys(ki_refs[r][...], False)
            ks_scr[sl, :] = ks_refs[r][...].astype(MXU_DTYPE)
            vst_scr[blk] = vs_refs[r][...].T.astype(MXU_DTYPE)

    @pl.when(s_id == nsteps)
    def _():
        nblk = nsteps * pps + 1
        last = pl.ds((nblk - 1) * page, page)
        key_scr[last, :] = index_keys(kin_ref[...], True)
        ks_scr[last, :] = ksn_ref[...].astype(MXU_DTYPE)
        vst_scr[nblk - 1] = vsn_ref[...].T.astype(MXU_DTYPE)

        kth, count_ge = _kth_largest_key(key_scr, nblk, page, topk)
        _demote_surplus_ties(key_scr, nblk, page, topk, kth, count_ge)
        thr = jnp.maximum(kth, KEY_NEG_INF + 1)

        def att_body(c, carry):
            m, l, acc = carry
            sl = pl.ds(pl.multiple_of(c * page, page), page)
            s = _dot(ks_scr[sl, :], qst_ref[...])
            s = jnp.where(key_scr[sl, :] >= thr, s, NEG_BIG)
            m_new = jnp.maximum(m, jnp.max(s, axis=0, keepdims=True))
            alpha = jnp.exp(m - m_new)
            p = jnp.exp(s - m_new)
            l = alpha * l + jnp.sum(p, axis=0, keepdims=True)
            acc = alpha * acc + _dot(vst_scr[c], p.astype(MXU_DTYPE))
            return m_new, l, acc

        init = (jnp.full((1, LANES), NEG_BIG, f32), jnp.zeros((1, LANES), f32), jnp.zeros((LANES, LANES), f32))
        _, l, acc = lax.fori_loop(0, nblk, att_body, init)
        o_ref[...] = ((acc / l).T * sg_ref[...].astype(f32)).astype(o_ref.dtype)


def _dsa_sample(page_table, qst, qit, wrow, cache_ki, cache_ks, cache_vs, ki_new, ks_new, vs_new, sg_perm,
                *, layer, nt, topk):
    db = qst.shape[0]
    page = cache_ks.shape[2]
    npages = page_table.shape[1]
    pps = PAGES_PER_STEP
    nsteps = npages // pps
    nblk = npages + 1

    def page_map(r):
        def f(bb, s, pt):
            return (layer, pt[bb, jnp.minimum(s * pps + r, npages - 1)], 0, 0)
        return f

    def pspecs(width):
        return [pl.BlockSpec((None, None, page, width), page_map(r)) for r in range(pps)]

    per_b = lambda shape: pl.BlockSpec((None,) + shape, lambda bb, s, pt: (bb, 0, 0))
    grid_spec = pltpu.PrefetchScalarGridSpec(
        num_scalar_prefetch=1,
        grid=(db, nsteps + 1),
        in_specs=[
            per_b((D_DSA, LANES)), per_b((D_IDX, LANES)), per_b((1, LANES)),
            *pspecs(D_IDX), *pspecs(D_DSA), *pspecs(D_DSA),
            per_b((page, D_IDX)), per_b((page, D_DSA)), per_b((page, D_DSA)),
            per_b((LANES, D_DSA)),
        ],
        out_specs=per_b((LANES, D_DSA)),
        scratch_shapes=[
            pltpu.VMEM((nblk * page, LANES), i32),
            pltpu.VMEM((nblk * page, D_DSA), MXU_DTYPE),
            pltpu.VMEM((nblk, D_DSA, page), MXU_DTYPE),
        ],
    )
    return pl.pallas_call(
        functools.partial(_dsa_sample_kernel, nt=nt, pps=pps, nsteps=nsteps, topk=topk, page=page),
        grid_spec=grid_spec,
        out_shape=jax.ShapeDtypeStruct((db, LANES, D_DSA), MXU_DTYPE),
        compiler_params=_cparams(("arbitrary", "arbitrary")),
        name="dsa_sample",
    )(page_table, qst, qit, wrow, *([cache_ki] * pps), *([cache_ks] * pps), *([cache_vs] * pps),
      ki_new, ks_new, vs_new, sg_perm)


def _rope_tables(pos, head_dim):
    half = head_dim // 2
    lane = jnp.arange(LANES)
    inv = ROPE_THETA ** (-(lane % half).astype(f32) / half)
    ang = pos.astype(f32)[:, None] * inv[None, :]
    sign = jnp.where((lane % head_dim) < half, -1.0, 1.0).astype(f32)
    return jnp.cos(ang), jnp.sin(ang) * sign[None, :]


def _tile_gain(g, n):
    return jnp.tile(g, n // g.shape[0]).reshape(1, n).astype(f32)


def _mixer_inputs(x2d, pos_tab, wl, gl, *, tm):
    cos64, sin64, cos128, sin128 = pos_tab
    h = _rmsnorm(x2d, gl["g_in"], min(tm, 256))
    proj = functools.partial(_proj, h, tm=tm, tn=512)
    rope64 = [("row", cos64), ("row", sin64)]
    rope128 = [("row", cos128), ("row", sin128)]
    nqd = wl["qd"].shape[1]
    nqs = wl["qs"].shape[1]
    out = {}
    (out["qd"],) = proj(wl["qd"], functools.partial(_epi_norm_rope, gsize=D_DH, half=D_DH // 2, scale=D_DH ** -0.5),
                        [MXU_DTYPE], [("col", _tile_gain(gl["g_q_diff"], nqd))] + rope64, name="proj_qd")
    out["kd"], out["kd_c"] = proj(wl["kd"], functools.partial(_epi_norm_rope, gsize=D_DH, half=D_DH // 2, scale=1.0),
                                  [f32, MXU_DTYPE], [("col", _tile_gain(gl["g_k_diff"], nqd))] + rope64, name="proj_kd")
    out["vd"], out["vd_c"] = proj(wl["vd"], _epi_raw, [f32, MXU_DTYPE], name="proj_vd")
    (out["qs"],) = proj(wl["qs"], functools.partial(_epi_norm_rope, gsize=D_DSA, half=D_DSA // 2, scale=D_DSA ** -0.5),
                        [MXU_DTYPE], [("col", _tile_gain(gl["g_q_dsa"], nqs))] + rope128, name="proj_qs")
    out["ks"], out["ks_c"] = proj(wl["ks"], functools.partial(_epi_norm_rope, gsize=D_DSA, half=D_DSA // 2, scale=1.0),
                                  [f32, MXU_DTYPE], [("col", gl["g_k_dsa"].reshape(1, D_DSA))] + rope128, name="proj_ks")
    out["vs"], out["vs_c"] = proj(wl["vs"], _epi_raw, [f32, MXU_DTYPE], name="proj_vs")
    (out["qi"],) = proj(wl["qi"], functools.partial(_epi_rope, half=D_IDX // 2, scale=D_IDX ** -0.5 * H_IDX ** -0.5),
                        [MXU_DTYPE], rope64, name="proj_qi")
    g_kiwi = jnp.concatenate([gl["g_k_idx"], jnp.ones((LANES - D_IDX,), f32)]).reshape(1, LANES)
    out["kiwi"], out["ki2"] = proj(wl["kiwi"], _epi_kiwi, [f32, MXU_DTYPE], [("col", g_kiwi)] + rope64, name="proj_kiwi")
    (out["qm"],) = proj(wl["qm"], functools.partial(_epi_norm256, scale=D_MEM ** -0.5), [MXU_DTYPE],
                        [("col", _tile_gain(gl["g_q_mem"], wl["qm"].shape[1]))], name="proj_qm")
    (out["sg"],) = proj(wl["gate"], _epi_silu, [MXU_DTYPE], name="proj_gate")
    return out


def kernel(x_prompt, x_sample, mem_prompt, cache_diff_k, cache_diff_v, cache_dsa_k, cache_dsa_v, cache_idx_k,
           cache_mem_k, cache_mem_v, page_table, w_in, w_out, w_mem_kv, g_in, g_mem, g_q_diff, g_k_diff,
           g_sub_diff, lam_q1, lam_k1, lam_q2, lam_k2, g_q_dsa, g_k_dsa, g_k_idx, g_q_mem, g_k_mem):
    depth = w_in.shape[0]
    b, t, d = x_prompt.shape
    db, nt, _ = x_sample.shape
    n_mem = mem_prompt.shape[1]
    n_phys, page = cache_dsa_k.shape[1], cache_dsa_k.shape[2]
    npages = page_table.shape[1]
    past = npages * page
    h_diff = cache_diff_k.shape[3]
    w_diff = h_diff * 2 * D_DH
    h_dsa = (3 * d // 8) // D_DSA
    w_dsa = h_dsa * D_DSA
    h_mem = cache_mem_k.shape[3]
    w_mem = h_mem * D_MEM
    assert w_diff + w_dsa + w_mem == d and LANES % nt == 0 and h_dsa <= LANES // nt
    topk_p = min(TOPK_MAX, t // 4)
    topk_s = min(TOPK_MAX, (past + nt) // 4)
    m_p = b * t
    m_s = db * nt
    tm_p = min(1024, m_p)

    widths = (w_diff, w_diff, w_diff, w_dsa, D_DSA, D_DSA, H_IDX * D_IDX, D_IDX, H_IDX, w_mem, d)
    offs = [0]
    for wdt in widths:
        offs.append(offs[-1] + wdt)
    names = ("qd", "kd", "vd", "qs", "ks", "vs", "qi", "ki", "wi", "qm", "gate")
    seg = {n: (offs[k], offs[k + 1]) for k, n in enumerate(names)}

    assert w_diff == w_dsa and (tm_p % t == 0 or t % tm_p == 0)
    pos_p = jnp.tile(jnp.arange(t, dtype=i32), max(1, tm_p // t))
    pos_s = jnp.tile(past + jnp.arange(nt, dtype=i32), db)
    tab_p = _rope_tables(pos_p, D_DH) + _rope_tables(pos_p, D_DSA)
    tab_s = _rope_tables(pos_s, D_DH) + _rope_tables(pos_s, D_DSA)

    cdk = cache_diff_k.reshape(depth, n_phys, page, w_diff)
    cdv = cache_diff_v.reshape(depth, n_phys, page, w_diff)
    cmk = cache_mem_k.reshape(depth, db, n_mem, w_mem)
    cmv = cache_mem_v.reshape(depth, db, n_mem, w_mem)

    x_p = x_prompt.reshape(m_p, d)
    x_s = x_sample.reshape(m_s, d)
    mem2d = mem_prompt.reshape(b * n_mem, d)
    outs = {k: [] for k in ("pdk", "pdv", "psk", "psv", "pik", "pmk", "pmv", "sdk", "sdv", "ssk", "ssv", "sik")}

    for l in range(depth):
        wl_full = w_in[l]
        wl = {n: wl_full[:, seg[n][0]:seg[n][1]].astype(MXU_DTYPE)
              for n in ("qd", "kd", "vd", "qs", "ks", "vs", "qi", "qm", "gate")}
        wl["kiwi"] = jnp.pad(wl_full[:, seg["ki"][0]:seg["wi"][1]],
                             ((0, 0), (0, LANES - D_IDX - H_IDX))).astype(MXU_DTYPE)
        gl = dict(g_in=g_in[l], g_q_diff=g_q_diff[l], g_k_diff=g_k_diff[l], g_q_dsa=g_q_dsa[l],
                  g_k_dsa=g_k_dsa[l], g_k_idx=g_k_idx[l], g_q_mem=g_q_mem[l])
        w_out_l = w_out[l].astype(MXU_DTYPE)
        w_mkv = w_mem_kv[l].astype(MXU_DTYPE)
        lam_init = 0.8 - 0.6 * math.exp(-0.3 * l)
        lam = (jnp.exp(jnp.sum(lam_q1[l] * lam_k1[l])) - jnp.exp(jnp.sum(lam_q2[l] * lam_k2[l])) + lam_init)
        lam = lam.astype(f32).reshape(1)
        g_sub = g_sub_diff[l].reshape(1, 2 * D_DH)
        post = 1.0 - lam_init

        tp = _mixer_inputs(x_p, tab_p, wl, gl, tm=tm_p)
        hm = _rmsnorm(mem2d, g_mem[l], 256)
        mk, = _proj(hm, w_mkv[:, :w_mem], functools.partial(_epi_norm256, scale=1.0), [f32],
                    [("col", _tile_gain(g_k_mem[l], w_mem))], tm=b * n_mem, tn=512, name="proj_mk")
        mv, = _proj(hm, w_mkv[:, w_mem:], _epi_raw, [f32], tm=b * n_mem, tn=512, name="proj_mv")
        r3 = lambda a: a.reshape(b, t, a.shape[-1])
        sg = r3(tp["sg"])
        od = _diff_prompt(lam, r3(tp["qd"]), r3(tp["kd_c"]), r3(tp["vd_c"]), g_sub, sg, post_scale=post)
        tc = 256
        vst = tp["vs_c"].reshape(b, t // tc, tc, D_DSA).transpose(0, 1, 3, 2)
        os_ = _dsa_prompt(r3(tp["qs"]), r3(tp["qi"]), r3(tp["kiwi"]), r3(tp["ki2"]), r3(tp["ks_c"]), vst, sg,
                          topk=topk_p, tc=tc)
        om = _mem_attn(r3(tp["qm"]), mk.reshape(1, b, n_mem, w_mem), mv.reshape(1, b, n_mem, w_mem), sg,
                       layer=0, sg_col0=(w_diff + w_dsa) // D_MEM, tq=min(512, t))
        x_p = _outproj(od.reshape(m_p, w_diff), os_.reshape(m_p, w_dsa), om.reshape(m_p, w_mem), w_out_l, x_p,
                       tm=tm_p, tn=512)
        outs["pdk"].append(tp["kd"].reshape(b, t, h_diff, 2, D_DH))
        outs["pdv"].append(tp["vd"].reshape(b, t, h_diff, 2 * D_DH))
        outs["psk"].append(tp["ks"].reshape(b, t, D_DSA))
        outs["psv"].append(tp["vs"].reshape(b, t, D_DSA))
        outs["pik"].append(tp["kiwi"][:, :D_IDX].reshape(b, t, D_IDX))
        outs["pmk"].append(mk.reshape(b, n_mem, h_mem, D_MEM))
        outs["pmv"].append(mv.reshape(b, n_mem, h_mem, D_MEM))

        ts = _mixer_inputs(x_s, tab_s, wl, gl, tm=m_s)
        s3 = lambda a: a.reshape(db, nt, a.shape[-1])
        sg_s = s3(ts["sg"])
        padk = lambda a: jnp.pad(s3(a), ((0, 0), (0, page - nt), (0, 0)))
        q5 = ts["qd"].reshape(db, nt, h_diff, 2, D_DH).transpose(0, 2, 3, 1, 4)
        eye_h = jnp.eye(h_diff, dtype=MXU_DTYPE)
        eye_c = jnp.eye(2, dtype=MXU_DTYPE)
        qbd = (q5[:, :, :, :, None, None, :] * eye_h[None, :, None, None, :, None, None]
               * eye_c[None, None, :, None, None, :, None]).reshape(db, h_diff * 2 * nt, w_diff)
        od_s = _diff_sample(page_table, lam, qbd, cdk, cdv, padk(ts["kd"]), padk(ts["vd"]), g_sub, sg_s,
                            layer=l, nt=nt, post_scale=post)
        gsz = LANES // nt
        qs4 = ts["qs"].reshape(db, nt, h_dsa, D_DSA)
        qst = jnp.pad(qs4, ((0, 0), (0, 0), (0, gsz - h_dsa), (0, 0))).reshape(db, LANES, D_DSA).transpose(0, 2, 1)
        qit = ts["qi"].reshape(db, nt * H_IDX, D_IDX).transpose(0, 2, 1)
        assert gsz == H_IDX
        wrow = ts["kiwi"][:, D_IDX:D_IDX + H_IDX].reshape(db, 1, nt * H_IDX)
        sg_dsa = sg_s[:, :, w_diff:w_diff + w_dsa].reshape(db, nt, h_dsa, D_DSA)
        sg_perm = jnp.pad(sg_dsa, ((0, 0), (0, 0), (0, gsz - h_dsa), (0, 0))).reshape(db, LANES, D_DSA)
        os_s = _dsa_sample(page_table, qst, qit, wrow, cache_idx_k, cache_dsa_k, cache_dsa_v,
                           padk(ts["kiwi"][:, :D_IDX]), padk(ts["ks"]), padk(ts["vs"]), sg_perm,
                           layer=l, nt=nt, topk=topk_s)
        os_s = os_s.reshape(db, nt, gsz, D_DSA)[:, :, :h_dsa].reshape(m_s, w_dsa)
        om_s = _mem_attn(s3(ts["qm"]), cmk, cmv, sg_s, layer=l, sg_col0=(w_diff + w_dsa) // D_MEM, tq=nt)
        x_s = _outproj(od_s.reshape(m_s, w_diff), os_s, om_s.reshape(m_s, w_mem), w_out_l, x_s, tm=m_s, tn=512)
        outs["sdk"].append(ts["kd"].reshape(db, nt, h_diff, 2, D_DH))
        outs["sdv"].append(ts["vd"].reshape(db, nt, h_diff, 2 * D_DH))
        outs["ssk"].append(ts["ks"].reshape(db, nt, D_DSA))
        outs["ssv"].append(ts["vs"].reshape(db, nt, D_DSA))
        outs["sik"].append(ts["kiwi"][:, :D_IDX].reshape(db, nt, D_IDX))

    st = lambda k: jnp.stack(outs[k])
    return (x_p.reshape(b, t, d), x_s.reshape(db, nt, d),
            st("pdk"), st("pdv"), st("psk"), st("psv"), st("pik"), st("pmk"), st("pmv"),
            st("sdk"), st("sdv"), st("ssk"), st("ssv"), st("sik"))
```

```python
import functools
import math

import jax
import jax.numpy as jnp
from jax import lax
from jax.experimental import pallas as pl
from jax.experimental.pallas import tpu as pltpu

EPS = 1e-6
ROPE_THETA = 10000.0
TOPK_MAX = 256
LANES = 128
D_DH = 64
D_DSA = 128
D_IDX = 64
H_IDX = 16
D_MEM = 256
NEG_BIG = -1e30
KEY_NEG_INF = -2139095041
INT_MIN = -2147483648
MXU_DTYPE = jnp.bfloat16
VMEM_LIMIT_BYTES = 52 * 1024 * 1024
PAGES_PER_STEP = 4

f32 = jnp.float32
i32 = jnp.int32


def _cparams(sem):
    return pltpu.CompilerParams(dimension_semantics=sem, vmem_limit_bytes=VMEM_LIMIT_BYTES)


def _dot(a, b):
    return jnp.dot(a, b, preferred_element_type=f32)


def _dot_nt(a, b):
    return lax.dot_general(a, b, (((1,), (1,)), ((), ())), preferred_element_type=f32)


def _iota(shape, dim):
    return lax.broadcasted_iota(i32, shape, dim)


def _group_mat(gsize):
    r = _iota((LANES, LANES), 0) // gsize
    c = _iota((LANES, LANES), 1) // gsize
    return (r == c).astype(f32).astype(MXU_DTYPE)


def _group_sum(x, gmat):
    hi = x.astype(MXU_DTYPE)
    lo = (x - hi.astype(f32)).astype(MXU_DTYPE)
    return _dot(hi, gmat) + _dot(lo, gmat)


def _rope_chunk(n, cos, sin_signed, half):
    if 2 * half == LANES:
        rot = pltpu.roll(n, half, 1)
    else:
        first = (_iota(n.shape, 1) % (2 * half)) < half
        rot = jnp.where(first, pltpu.roll(n, LANES - half, 1), pltpu.roll(n, half, 1))
    return n * cos + rot * sin_signed


def _f2key(x):
    b = pltpu.bitcast(x, i32)
    return b ^ (lax.shift_right_arithmetic(b, 31) & 0x7FFFFFFF)


def _rmsnorm_kernel(x_ref, g_ref, o_ref):
    x = x_ref[...]
    ms = jnp.mean(x * x, axis=-1, keepdims=True)
    o_ref[...] = (x * lax.rsqrt(ms + EPS) * g_ref[...]).astype(o_ref.dtype)


def _rmsnorm(x, g, tm):
    m, d = x.shape
    return pl.pallas_call(
        _rmsnorm_kernel,
        grid=(m // tm,),
        in_specs=[pl.BlockSpec((tm, d), lambda i: (i, 0)), pl.BlockSpec((1, d), lambda i: (0, 0))],
        out_specs=pl.BlockSpec((tm, d), lambda i: (i, 0)),
        out_shape=jax.ShapeDtypeStruct((m, d), MXU_DTYPE),
        compiler_params=_cparams(("arbitrary",)),
        name="rmsnorm",
    )(x, g.reshape(1, d))


def _chunks_raw(z, aux, store):
    for c in range(z.shape[1] // LANES):
        store(c, z[:, c * LANES:(c + 1) * LANES])


def _chunks_silu(z, aux, store):
    for c in range(z.shape[1] // LANES):
        zc = z[:, c * LANES:(c + 1) * LANES]
        store(c, zc / (1.0 + jnp.exp(-zc)))


def _chunks_norm_rope(z, aux, store, *, gsize, half, scale):
    gain, cos, sin = aux
    gmat = _group_mat(gsize)
    for c in range(z.shape[1] // LANES):
        sl = slice(c * LANES, (c + 1) * LANES)
        zc = z[:, sl]
        ss = _group_sum(zc * zc, gmat)
        n = zc * lax.rsqrt(ss * (1.0 / gsize) + EPS) * gain[:, sl]
        r = _rope_chunk(n, cos, sin, half)
        store(c, r * scale if scale != 1.0 else r)


def _chunks_rope(z, aux, store, *, half, scale):
    cos, sin = aux
    for c in range(z.shape[1] // LANES):
        store(c, _rope_chunk(z[:, c * LANES:(c + 1) * LANES], cos, sin, half) * scale)


def _chunks_norm256(z, aux, store, *, scale):
    (gain,) = aux
    gmat = _group_mat(LANES)
    for c in range(z.shape[1] // D_MEM):
        a = z[:, c * D_MEM:c * D_MEM + LANES]
        b = z[:, c * D_MEM + LANES:(c + 1) * D_MEM]
        ss = _group_sum(a * a, gmat) + _group_sum(b * b, gmat)
        inv = lax.rsqrt(ss * (1.0 / D_MEM) + EPS)
        for k, v in enumerate((a, b)):
            sl = slice(c * D_MEM + k * LANES, c * D_MEM + (k + 1) * LANES)
            r = v * inv * gain[:, sl]
            store(2 * c + k, r * scale if scale != 1.0 else r)


def _chunks_kiwi(z, aux, store):
    gain, cos, sin = aux
    gmat = _group_mat(D_IDX)
    ss = _group_sum(z * z, gmat)
    n = z * lax.rsqrt(ss * (1.0 / D_IDX) + EPS) * gain
    r = _rope_chunk(n, cos, sin, D_IDX // 2)
    lo = _iota(z.shape, 1) < D_IDX
    store(0, jnp.where(lo, r, z))
    r_lo = jnp.where(lo, r, 0.0)
    store(1, r_lo + pltpu.roll(r_lo, D_IDX, 1))


def _store_rows(outs, c, r):
    for o in outs:
        o[:, c * LANES:(c + 1) * LANES] = r.astype(o.dtype)


def _store_kd_prompt(outs, c, r):
    outs[0][c] = r.T
    outs[1][:, c * LANES:(c + 1) * LANES] = r.astype(outs[1].dtype)


def _store_heads(outs, c, r):
    for o in outs:
        o[c] = r.astype(o.dtype)


def _store_kiwi_rows(outs, c, r):
    outs[c][...] = r.astype(outs[c].dtype)


def _store_kiwi_prompt(outs, c, r):
    if c == 0:
        outs[0][...] = r
        outs[2][...] = r.T[:D_IDX, :]
    else:
        outs[1][...] = r.astype(outs[1].dtype)


def _proj_kernel(*refs, chunk_fn, store_fn, n_aux, n_alias, nt):
    h_ref, w_ref = refs[:2]
    aux = [r[...] for r in refs[2:2 + n_aux]]
    outs = refs[2 + n_aux + n_alias:]
    z = _dot_nt(h_ref[...], w_ref[...]) if nt else _dot(h_ref[...], w_ref[...])
    chunk_fn(z, aux, functools.partial(store_fn, outs))


def _rows_out(m, n, dtype, tm, tn):
    tn = min(tn, n)
    return dict(shape=(m, n), dtype=dtype, block=(tm, tn), index=lambda i, j: (i, j), alias=None)


def _proj(h, w, chunk_fn, store_fn, outs, aux=(), *, tm, tn, nt, name):
    m, k = h.shape
    n = w.shape[0] if nt else w.shape[1]
    tn = min(tn, n)
    assert m % tm == 0 and n % tn == 0
    w_spec = pl.BlockSpec((tn, k), lambda i, j: (j, 0)) if nt else pl.BlockSpec((k, tn), lambda i, j: (0, j))
    in_specs = [pl.BlockSpec((tm, k), lambda i, j: (i, 0)), w_spec]
    args = [h, w]
    for kind, a in aux:
        if kind == "col":
            in_specs.append(pl.BlockSpec((1, tn), lambda i, j: (0, j)))
        else:
            nrb = a.shape[0] // tm
            in_specs.append(pl.BlockSpec((tm, LANES), lambda i, j, nrb=nrb: (i % nrb, 0)))
        args.append(a)
    aliases = {}
    for k_out, o in enumerate(outs):
        if o["alias"] is not None:
            aliases[len(args)] = k_out
            in_specs.append(pl.BlockSpec(memory_space=pl.ANY))
            args.append(o["alias"])
    return pl.pallas_call(
        functools.partial(_proj_kernel, chunk_fn=chunk_fn, store_fn=store_fn, n_aux=len(aux),
                          n_alias=len(aliases), nt=nt),
        grid=(m // tm, n // tn),
        in_specs=in_specs,
        out_specs=[pl.BlockSpec(o["block"], o["index"]) for o in outs],
        out_shape=[jax.ShapeDtypeStruct(o["shape"], o["dtype"]) for o in outs],
        input_output_aliases=aliases,
        compiler_params=_cparams(("arbitrary", "arbitrary")),
        name=name,
    )(*args)


def _outproj_kernel(od_ref, os_ref, om_ref, w_ref, x_ref, o_ref, *, wd, ws):
    acc = _dot(od_ref[...], w_ref[0:wd, :])
    acc += _dot(os_ref[...], w_ref[wd:wd + ws, :])
    acc += _dot(om_ref[...], w_ref[wd + ws:, :])
    o_ref[...] = x_ref[...] + acc


def _outproj(od, os_, om, w, x, *, tm, tn):
    m, d = x.shape
    wd, ws, wm = od.shape[1], os_.shape[1], om.shape[1]
    return pl.pallas_call(
        functools.partial(_outproj_kernel, wd=wd, ws=ws),
        grid=(m // tm, d // tn),
        in_specs=[
            pl.BlockSpec((tm, wd), lambda i, j: (i, 0)),
            pl.BlockSpec((tm, ws), lambda i, j: (i, 0)),
            pl.BlockSpec((tm, wm), lambda i, j: (i, 0)),
            pl.BlockSpec((wd + ws + wm, tn), lambda i, j: (0, j)),
            pl.BlockSpec((tm, tn), lambda i, j: (i, j)),
        ],
        out_specs=pl.BlockSpec((tm, tn), lambda i, j: (i, j)),
        out_shape=jax.ShapeDtypeStruct((m, d), f32),
        compiler_params=_cparams(("arbitrary", "arbitrary")),
        name="outproj",
    )(od, os_, om, w, x)


def _sub_rmsnorm_gate(od, g, post_scale, sg):
    ms = jnp.mean(od * od, axis=-1, keepdims=True)
    return od * lax.rsqrt(ms + EPS) * g * post_scale * sg.astype(f32)


def _diff_prompt_kernel(lam_ref, q_ref, k_ref, v_ref, g_ref, sg_ref, o_ref, *, tq, tk, post_scale):
    i = pl.program_id(2)
    q = q_ref[...]
    lo = _iota(q.shape, 1) < D_DH
    zero = jnp.zeros_like(q)
    qs = jnp.concatenate([jnp.where(lo, q, zero), jnp.where(lo, zero, q)], axis=0)

    def step(j, carry, masked):
        m, l, acc = carry
        kc = k_ref[pl.ds(j * tk, tk), :]
        vc = v_ref[pl.ds(j * tk, tk), :]
        s = _dot_nt(qs, kc)
        if masked:
            row = _iota(s.shape, 0) % tq + i * tq
            col = _iota(s.shape, 1) + j * tk
            s = jnp.where(col <= row, s, NEG_BIG)
        m_new = jnp.maximum(m, jnp.max(s, axis=1, keepdims=True))
        alpha = jnp.exp(m - m_new)
        p = jnp.exp(s - m_new)
        l = alpha * l + jnp.sum(p, axis=1, keepdims=True)
        acc = alpha * acc + _dot(p.astype(MXU_DTYPE), vc)
        return m_new, l, acc

    nd = tq // tk
    carry = (jnp.full((2 * tq, 1), NEG_BIG, f32), jnp.zeros((2 * tq, 1), f32), jnp.zeros((2 * tq, LANES), f32))
    carry = lax.fori_loop(0, i * nd, functools.partial(step, masked=False), carry)
    for d in range(nd):
        carry = step(i * nd + d, carry, True)
    _, l, acc = carry
    od = acc[:tq] / l[:tq] - lam_ref[0] * (acc[tq:] / l[tq:])
    o_ref[...] = _sub_rmsnorm_gate(od, g_ref[...], post_scale, sg_ref[...]).astype(o_ref.dtype)


def _diff_prompt(lam, qd, kd, vd, g_sub, sg, *, post_scale, tq=256, tk=256):
    b, t, w = qd.shape
    nh = w // LANES
    return pl.pallas_call(
        functools.partial(_diff_prompt_kernel, tq=tq, tk=tk, post_scale=post_scale),
        grid=(b, nh, t // tq),
        in_specs=[
            pl.BlockSpec(memory_space=pltpu.SMEM),
            pl.BlockSpec((None, tq, LANES), lambda bb, h, i: (bb, i, h)),
            pl.BlockSpec((None, t, LANES), lambda bb, h, i: (bb, 0, h)),
            pl.BlockSpec((None, None, t, LANES), lambda bb, h, i: (bb, h, 0, 0)),
            pl.BlockSpec((1, LANES), lambda bb, h, i: (0, 0)),
            pl.BlockSpec((None, tq, LANES), lambda bb, h, i: (bb, i, h)),
        ],
        out_specs=pl.BlockSpec((None, tq, LANES), lambda bb, h, i: (bb, i, h)),
        out_shape=jax.ShapeDtypeStruct((b, t, w), MXU_DTYPE),
        compiler_params=_cparams(("arbitrary", "arbitrary", "arbitrary")),
        name="diff_prompt",
    )(lam, qd, kd, vd, g_sub, sg)


def _kth_largest_key_sub(key_scr, nblk, rows, topk):
    def count_ge(cand):
        def body(c, acc):
            k = key_scr[pl.ds(pl.multiple_of(c * rows, rows), rows), :]
            return acc + jnp.sum((k >= cand).astype(i32).reshape(rows // 8, 8, LANES), axis=0)

        acc = lax.fori_loop(0, nblk, body, jnp.zeros((8, LANES), i32))
        return jnp.sum(acc, axis=0, keepdims=True)

    def bit_body(it, prefix):
        cand = prefix + lax.shift_left(jnp.int32(1), 31 - it)
        return jnp.where(count_ge(cand) >= topk, cand, prefix)

    kth = lax.fori_loop(0, 32, bit_body, jnp.full((1, LANES), INT_MIN, i32))
    return kth, count_ge


def _demote_surplus_ties_sub(key_scr, nblk, rows, topk, kth, count_ge):
    tie = (count_ge(kth) > topk) & (kth > KEY_NEG_INF)

    @pl.when(jnp.max(tie.astype(i32)) > 0)
    def _():
        need = (topk - count_ge(kth + 1)).astype(f32)
        tri = (_iota((rows, rows), 0) >= _iota((rows, rows), 1)).astype(f32).astype(MXU_DTYPE)

        def body(c, run):
            sl = pl.ds(pl.multiple_of(c * rows, rows), rows)
            k = key_scr[sl, :]
            eq = k == kth
            incl = _dot(tri, eq.astype(f32).astype(MXU_DTYPE))
            drop = eq & ((run + incl) > need)
            key_scr[sl, :] = jnp.where(drop, KEY_NEG_INF, k)
            return run + incl[rows - 1:rows, :]

        lax.fori_loop(0, nblk, body, jnp.zeros((1, LANES), f32))


def _kth_largest_key_lane(key_scr, nblk, nq, topk):
    def count_ge(cand):
        def body(c, acc):
            return acc + jnp.where(key_scr[c] >= cand, 1.0, 0.0)

        acc = lax.fori_loop(0, nblk, body, jnp.zeros((nq, LANES), f32))
        return jnp.sum(acc, axis=1, keepdims=True)

    def bit_body(it, prefix):
        cand = prefix + lax.shift_left(jnp.int32(1), 31 - it)
        return jnp.where(count_ge(cand) >= topk, cand, prefix)

    kth = lax.fori_loop(0, 32, bit_body, jnp.full((nq, 1), INT_MIN, i32))
    return kth, count_ge


def _demote_surplus_ties_lane(key_scr, nblk, nq, topk, kth, count_ge):
    tie = (count_ge(kth) > topk) & (kth > KEY_NEG_INF)

    @pl.when(jnp.max(jnp.where(tie, 1.0, 0.0)) > 0.0)
    def _():
        need = topk - count_ge(kth + 1)
        triu = (_iota((LANES, LANES), 0) <= _iota((LANES, LANES), 1)).astype(f32).astype(MXU_DTYPE)
        pad = jnp.zeros((16 - nq % 16, LANES), f32) if nq % 16 else None

        def body(c, run):
            k = key_scr[c]
            eq = k == kth
            eqf = jnp.where(eq, 1.0, 0.0)
            if pad is not None:
                eqf = jnp.concatenate([eqf, pad], axis=0)
            incl = _dot(eqf.astype(MXU_DTYPE), triu)[:nq]
            drop = eq & ((run + incl) > need)
            key_scr[c] = jnp.where(drop, KEY_NEG_INF, k)
            return run + incl[:, LANES - 1:LANES]

        lax.fori_loop(0, nblk, body, jnp.zeros((nq, 1), f32))


def _dsa_prompt_kernel(qs_ref, qi_ref, kiwi_ref, ki2_ref, ks_ref, vst_ref, sg_ref, o_ref,
                       key_scr, qm_scr, wt_scr, qst_scr, acc_scr, *, nh, topk, tc):
    qb = pl.program_id(1)
    tq = LANES
    nch = (qb * tq + tq + tc - 1) // tc

    lo = _iota((tq, LANES), 1) < D_IDX
    for h in range(H_IDX):
        chunk = qi_ref[:, (h // 2) * LANES:(h // 2 + 1) * LANES]
        keep = lo if h % 2 == 0 else jnp.logical_not(lo)
        qm_scr[h * tq:(h + 1) * tq, :] = jnp.where(keep, chunk, jnp.zeros_like(chunk))
    wt_scr[...] = kiwi_ref[...].T
    for h in range(nh):
        qst_scr[h * tq:(h + 1) * tq, :] = qs_ref[:, h * LANES:(h + 1) * LANES]

    tpos = qb * tq + _iota((1, LANES), 1)

    def idx_body(c, carry):
        sl = pl.ds(pl.multiple_of(c * tc, tc), tc)
        kc = ki2_ref[sl, :]
        acc = jnp.zeros((tc, LANES), f32)
        for h in range(H_IDX):
            sc = _dot_nt(kc, qm_scr[h * tq:(h + 1) * tq, :])
            acc = acc + jnp.maximum(sc, 0.0) * wt_scr[D_IDX + h:D_IDX + h + 1, :]
        acc = jnp.where(acc == 0.0, 0.0, acc)
        kpos = c * tc + _iota((tc, LANES), 0)
        acc = jnp.where(kpos <= tpos, acc, -jnp.inf)
        key_scr[sl, :] = _f2key(acc)
        return carry

    lax.fori_loop(0, nch, idx_body, 0)

    kth, count_ge = _kth_largest_key_sub(key_scr, nch, tc, topk)
    _demote_surplus_ties_sub(key_scr, nch, tc, topk, kth, count_ge)
    thr = jnp.maximum(kth, KEY_NEG_INF + 1)

    acc_scr[...] = jnp.zeros_like(acc_scr)

    def att_body(c, carry):
        m, l = carry
        sl = pl.ds(pl.multiple_of(c * tc, tc), tc)
        s = _dot_nt(ks_ref[sl, :], qst_scr[...])
        msk = key_scr[sl, :] >= thr
        s = jnp.where(jnp.concatenate([msk] * nh, axis=1), s, NEG_BIG)
        m_new = jnp.maximum(m, jnp.max(s, axis=0, keepdims=True))
        alpha = jnp.exp(m - m_new)
        p = jnp.exp(s - m_new)
        l = alpha * l + jnp.sum(p, axis=0, keepdims=True)
        acc_scr[...] = alpha * acc_scr[...] + _dot(vst_ref[c], p.astype(MXU_DTYPE))
        return m_new, l

    init = (jnp.full((1, nh * tq), NEG_BIG, f32), jnp.zeros((1, nh * tq), f32))
    _, l = lax.fori_loop(0, nch, att_body, init)
    out_t = acc_scr[...] / l
    for h in range(nh):
        sl = slice(h * LANES, (h + 1) * LANES)
        o_ref[:, sl] = (out_t[:, sl].T * sg_ref[:, sl].astype(f32)).astype(o_ref.dtype)


def _dsa_prompt(qs, qi, kiwi, ki2, ks, vst, sg, *, topk, tc=256):
    b, t, w = qs.shape
    nh = w // LANES
    tq = LANES
    assert t % tc == 0 and tc % tq == 0
    return pl.pallas_call(
        functools.partial(_dsa_prompt_kernel, nh=nh, topk=topk, tc=tc),
        grid=(b, t // tq),
        in_specs=[
            pl.BlockSpec((None, tq, w), lambda bb, i: (bb, i, 0)),
            pl.BlockSpec((None, tq, H_IDX * D_IDX), lambda bb, i: (bb, i, 0)),
            pl.BlockSpec((None, tq, LANES), lambda bb, i: (bb, i, 0)),
            pl.BlockSpec((None, t, LANES), lambda bb, i: (bb, 0, 0)),
            pl.BlockSpec((None, t, LANES), lambda bb, i: (bb, 0, 0)),
            pl.BlockSpec((None, t // tc, LANES, tc), lambda bb, i: (bb, 0, 0, 0)),
            pl.BlockSpec((None, tq, w), lambda bb, i: (bb, i, 1)),
        ],
        out_specs=pl.BlockSpec((None, tq, w), lambda bb, i: (bb, i, 0)),
        out_shape=jax.ShapeDtypeStruct((b, t, w), MXU_DTYPE),
        scratch_shapes=[
            pltpu.VMEM((t, LANES), i32),
            pltpu.VMEM((H_IDX * tq, LANES), MXU_DTYPE),
            pltpu.VMEM((LANES, LANES), f32),
            pltpu.VMEM((nh * tq, LANES), MXU_DTYPE),
            pltpu.VMEM((LANES, nh * tq), f32),
        ],
        compiler_params=_cparams(("arbitrary", "arbitrary")),
        name="dsa_prompt",
    )(qs, qi, kiwi, ki2, ks, vst, sg)


def _mem_attn_kernel(q_ref, k_ref, v_ref, sg_ref, o_ref):
    s = _dot_nt(q_ref[...], k_ref[...].astype(MXU_DTYPE))
    m = jnp.max(s, axis=1, keepdims=True)
    p = jnp.exp(s - m)
    l = jnp.sum(p, axis=1, keepdims=True)
    o = _dot(p.astype(MXU_DTYPE), v_ref[...].astype(MXU_DTYPE)) / l
    o_ref[...] = (o * sg_ref[...].astype(f32)).astype(o_ref.dtype)


def _mem_attn(q, k, v, sg, *, layer, sg_col0, tq):
    b, t, w = q.shape
    nh = w // D_MEM
    nm = k.shape[2]
    return pl.pallas_call(
        _mem_attn_kernel,
        grid=(b, t // tq, nh),
        in_specs=[
            pl.BlockSpec((None, tq, D_MEM), lambda bb, i, h: (bb, i, h)),
            pl.BlockSpec((None, None, nm, D_MEM), lambda bb, i, h: (layer, bb, 0, h)),
            pl.BlockSpec((None, None, nm, D_MEM), lambda bb, i, h: (layer, bb, 0, h)),
            pl.BlockSpec((None, tq, D_MEM), lambda bb, i, h: (bb, i, sg_col0 + h)),
        ],
        out_specs=pl.BlockSpec((None, tq, D_MEM), lambda bb, i, h: (bb, i, h)),
        out_shape=jax.ShapeDtypeStruct((b, t, w), MXU_DTYPE),
        compiler_params=_cparams(("arbitrary", "arbitrary", "arbitrary")),
        name="mem_attn",
    )(q, k, v, sg)


def _diff_sample_kernel(pt_ref, lam_ref, qm_ref, *refs, nh, nt, pps, nsteps, post_scale):
    kt_refs = refs[:pps]
    v_refs = refs[pps:2 * pps]
    knewt_ref, vnew_ref, g_ref, sg_ref, o_ref, m_scr, l_scr, acc_scr = refs[2 * pps:]
    s_id = pl.program_id(1)
    nr = 2 * nt

    @pl.when(s_id == 0)
    def _():
        m_scr[...] = jnp.full_like(m_scr, NEG_BIG)
        l_scr[...] = jnp.zeros_like(l_scr)
        acc_scr[...] = jnp.zeros_like(acc_scr)

    def update(kt_ref, v_ref, masked):
        s = jnp.concatenate(
            [_dot(qm_ref[h * nr:(h + 1) * nr, :], kt_ref[h].astype(MXU_DTYPE)) for h in range(nh)], axis=0)
        if masked:
            tpos = _iota(s.shape, 0) % nt
            s = jnp.where(_iota(s.shape, 1) <= tpos, s, NEG_BIG)
        m = m_scr[...]
        m_new = jnp.maximum(m, jnp.max(s, axis=1, keepdims=True))
        alpha = jnp.exp(m - m_new)
        p = jnp.exp(s - m_new)
        l_scr[...] = alpha * l_scr[...] + jnp.sum(p, axis=1, keepdims=True)
        pb = p.astype(MXU_DTYPE)
        pv = jnp.concatenate(
            [_dot(pb[h * nr:(h + 1) * nr, :], v_ref[h].astype(MXU_DTYPE)) for h in range(nh)], axis=0)
        acc_scr[...] = alpha * acc_scr[...] + pv
        m_scr[...] = m_new

    @pl.when(s_id < nsteps)
    def _():
        for r in range(pps):
            update(kt_refs[r], v_refs[r], False)

    @pl.when(s_id == nsteps)
    def _():
        update(knewt_ref, vnew_ref, True)
        o = acc_scr[...] / l_scr[...]
        for h in range(nh):
            od = o[h * nr:h * nr + nt] - lam_ref[0] * o[h * nr + nt:(h + 1) * nr]
            sl = slice(h * LANES, (h + 1) * LANES)
            o_ref[:, sl] = _sub_rmsnorm_gate(od, g_ref[...], post_scale, sg_ref[:, sl]).astype(o_ref.dtype)


def _diff_sample(page_table, lam, qm, cache_kt, cache_v, knewt, vnew, g_sub, sg, *, layer, nt, post_scale):
    db, rows, _ = qm.shape
    nh = cache_v.shape[2]
    page = cache_v.shape[3]
    npages = page_table.shape[1]
    pps = PAGES_PER_STEP
    nsteps = npages // pps
    w = nh * LANES

    def page_map(r):
        def f(bb, s, pt):
            return (layer, pt[bb, jnp.minimum(s * pps + r, npages - 1)], 0, 0, 0)
        return f

    kt_specs = [pl.BlockSpec((None, None, nh, LANES, page), page_map(r)) for r in range(pps)]
    v_specs = [pl.BlockSpec((None, None, nh, page, LANES), page_map(r)) for r in range(pps)]
    grid_spec = pltpu.PrefetchScalarGridSpec(
        num_scalar_prefetch=1,
        grid=(db, nsteps + 1),
        in_specs=[
            pl.BlockSpec(memory_space=pltpu.SMEM),
            pl.BlockSpec((None, rows, LANES), lambda bb, s, pt: (bb, 0, 0)),
            *kt_specs, *v_specs,
            pl.BlockSpec((None, nh, LANES, page), lambda bb, s, pt: (bb, 0, 0, 0)),
            pl.BlockSpec((None, nh, page, LANES), lambda bb, s, pt: (bb, 0, 0, 0)),
            pl.BlockSpec((1, LANES), lambda bb, s, pt: (0, 0)),
            pl.BlockSpec((None, nt, w), lambda bb, s, pt: (bb, 0, 0)),
        ],
        out_specs=pl.BlockSpec((None, nt, w), lambda bb, s, pt: (bb, 0, 0)),
        scratch_shapes=[
            pltpu.VMEM((rows, 1), f32),
            pltpu.VMEM((rows, 1), f32),
            pltpu.VMEM((rows, LANES), f32),
        ],
    )
    return pl.pallas_call(
        functools.partial(_diff_sample_kernel, nh=nh, nt=nt, pps=pps, nsteps=nsteps, post_scale=post_scale),
        grid_spec=grid_spec,
        out_shape=jax.ShapeDtypeStruct((db, nt, w), MXU_DTYPE),
        compiler_params=_cparams(("arbitrary", "arbitrary")),
        name="diff_sample",
    )(page_table, lam, qm, *([cache_kt] * pps), *([cache_v] * pps), knewt, vnew, g_sub, sg)


def _dsa_sample_kernel(pt_ref, qs_ref, qi_ref, w_ref, *refs, nt, pps, nsteps, topk, page):
    kit_refs = refs[:pps]
    ks_refs = refs[pps:2 * pps]
    vs_refs = refs[2 * pps:3 * pps]
    kint_ref, ksn_ref, vsn_ref, sg_ref, o_ref, key_scr, s_scr, v_scr = refs[3 * pps:]
    s_id = pl.program_id(1)
    gsz = LANES // nt

    def process(blk, kit, ks, vs, causal_new):
        sc = _dot(qi_ref[...], kit.astype(MXU_DTYPE))
        val = jnp.maximum(sc, 0.0) * w_ref[...]
        acc = jnp.sum(val.reshape(nt, gsz, page), axis=1)
        acc = jnp.where(acc == 0.0, 0.0, acc)
        if causal_new:
            acc = jnp.where(_iota(acc.shape, 1) <= _iota(acc.shape, 0), acc, -jnp.inf)
        key_scr[blk] = _f2key(acc)
        s_scr[blk] = _dot_nt(qs_ref[...], ks.astype(MXU_DTYPE))
        v_scr[blk] = vs.astype(MXU_DTYPE)

    @pl.when(s_id < nsteps)
    def _():
        for r in range(pps):
            process(s_id * pps + r, kit_refs[r][...], ks_refs[r][...], vs_refs[r][...], False)

    @pl.when(s_id == nsteps)
    def _():
        nblk = nsteps * pps + 1
        process(nblk - 1, kint_ref[...], ksn_ref[...], vsn_ref[...], True)

        kth, count_ge = _kth_largest_key_lane(key_scr, nblk, nt, topk)
        _demote_surplus_ties_lane(key_scr, nblk, nt, topk, kth, count_ge)
        thr = jnp.maximum(kth, KEY_NEG_INF + 1)

        def rowmask(c):
            mf = jnp.where(key_scr[c] >= thr, 1.0, 0.0)
            return jnp.broadcast_to(mf[:, None, :], (nt, gsz, page)).reshape(nt * gsz, page) > 0.5

        def max_body(c, mx):
            return jnp.maximum(mx, jnp.where(rowmask(c), s_scr[c], NEG_BIG))

        mx = lax.fori_loop(0, nblk, max_body, jnp.full((LANES, page), NEG_BIG, f32))
        m = jnp.max(mx, axis=1, keepdims=True)

        def att_body(c, carry):
            lacc, acc = carry
            p = jnp.where(rowmask(c), jnp.exp(s_scr[c] - m), 0.0)
            return lacc + p, acc + _dot(p.astype(MXU_DTYPE), v_scr[c])

        lacc, acc = lax.fori_loop(0, nblk, att_body, (jnp.zeros((LANES, page), f32), jnp.zeros((LANES, D_DSA), f32)))
        l = jnp.sum(lacc, axis=1, keepdims=True)
        o_ref[...] = (acc / l * sg_ref[...].astype(f32)).astype(o_ref.dtype)


def _dsa_sample(page_table, qs_rows, qi_rows, wcol, cache_kit, cache_ks, cache_vs, kit_new, ks_new, vs_new, sg_perm,
                *, layer, nt, topk):
    db = qs_rows.shape[0]
    page = cache_ks.shape[2]
    npages = page_table.shape[1]
    pps = PAGES_PER_STEP
    nsteps = npages // pps
    nblk = npages + 1
    assert page == LANES

    def page_map(r):
        def f(bb, s, pt):
            return (layer, pt[bb, jnp.minimum(s * pps + r, npages - 1)], 0, 0)
        return f

    def pspecs(shape):
        return [pl.BlockSpec((None, None) + shape, page_map(r)) for r in range(pps)]

    per_b = lambda shape: pl.BlockSpec((None,) + shape, lambda bb, s, pt: (bb, 0, 0))
    grid_spec = pltpu.PrefetchScalarGridSpec(
        num_scalar_prefetch=1,
        grid=(db, nsteps + 1),
        in_specs=[
            per_b((LANES, D_DSA)), per_b((LANES, D_IDX)), per_b((LANES, 1)),
            *pspecs((D_IDX, page)), *pspecs((page, D_DSA)), *pspecs((page, D_DSA)),
            per_b((D_IDX, page)), per_b((page, D_DSA)), per_b((page, D_DSA)),
            per_b((LANES, D_DSA)),
        ],
        out_specs=per_b((LANES, D_DSA)),
        scratch_shapes=[
            pltpu.VMEM((nblk, nt, page), i32),
            pltpu.VMEM((nblk, LANES, page), f32),
            pltpu.VMEM((nblk, page, D_DSA), MXU_DTYPE),
        ],
    )
    return pl.pallas_call(
        functools.partial(_dsa_sample_kernel, nt=nt, pps=pps, nsteps=nsteps, topk=topk, page=page),
        grid_spec=grid_spec,
        out_shape=jax.ShapeDtypeStruct((db, LANES, D_DSA), MXU_DTYPE),
        compiler_params=_cparams(("arbitrary", "arbitrary")),
        name="dsa_sample",
    )(page_table, qs_rows, qi_rows, wcol, *([cache_kit] * pps), *([cache_ks] * pps), *([cache_vs] * pps),
      kit_new, ks_new, vs_new, sg_perm)


def _rope_tables(pos, head_dim):
    half = head_dim // 2
    lane = jnp.arange(LANES)
    inv = ROPE_THETA ** (-(lane % half).astype(f32) / half)
    ang = pos.astype(f32)[:, None] * inv[None, :]
    sign = jnp.where((lane % head_dim) < half, -1.0, 1.0).astype(f32)
    return jnp.cos(ang), jnp.sin(ang) * sign[None, :]


def _tile_gain(g, n):
    return jnp.tile(g, n // g.shape[0]).reshape(1, n).astype(f32)


def _mixer_inputs(x2d, pos_tab, wl, gl, *, tm, stacked=None):
    cos64, sin64, cos128, sin128 = pos_tab
    m = x2d.shape[0]
    h = _rmsnorm(x2d, gl["g_in"], min(tm, 256))
    tn = 512
    proj = functools.partial(_proj, h, tm=tm, tn=tn, nt=True)
    rows = lambda n, dt: _rows_out(m, n, dt, tm, tn)
    rope64 = [("row", cos64), ("row", sin64)]
    rope128 = [("row", cos128), ("row", sin128)]
    nqd = wl["qd"].shape[0]
    nqs = wl["qs"].shape[0]
    nqm = wl["qm"].shape[0]
    norm_rope64 = functools.partial(_chunks_norm_rope, gsize=D_DH, half=D_DH // 2)
    norm_rope128 = functools.partial(_chunks_norm_rope, gsize=D_DSA, half=D_DSA // 2)
    gq64 = [("col", _tile_gain(gl["g_q_diff"], nqd))] + rope64
    gk64 = [("col", _tile_gain(gl["g_k_diff"], nqd))] + rope64
    gq128 = [("col", _tile_gain(gl["g_q_dsa"], nqs))] + rope128
    gk128 = [("col", gl["g_k_dsa"].reshape(1, D_DSA))] + rope128
    g_kiwi = [("col", jnp.concatenate([gl["g_k_idx"], jnp.ones((LANES - D_IDX,), f32)]).reshape(1, LANES))] + rope64
    out = {}
    (out["qd"],) = proj(wl["qd"], functools.partial(norm_rope64, scale=D_DH ** -0.5), _store_rows,
                        [rows(nqd, MXU_DTYPE)], gq64, name="proj_qd")
    (out["qs"],) = proj(wl["qs"], functools.partial(norm_rope128, scale=D_DSA ** -0.5), _store_rows,
                        [rows(nqs, MXU_DTYPE)], gq128, name="proj_qs")
    (out["qi"],) = proj(wl["qi"], functools.partial(_chunks_rope, half=D_IDX // 2, scale=D_IDX ** -0.5 * H_IDX ** -0.5),
                        _store_rows, [rows(H_IDX * D_IDX, MXU_DTYPE)], rope64, name="proj_qi")
    (out["qm"],) = proj(wl["qm"], functools.partial(_chunks_norm256, scale=D_MEM ** -0.5), _store_rows,
                        [rows(nqm, MXU_DTYPE)], [("col", _tile_gain(gl["g_q_mem"], nqm))], name="proj_qm")
    (out["sg"],) = proj(wl["gate"], _chunks_silu, _store_rows, [rows(wl["gate"].shape[0], MXU_DTYPE)], name="proj_gate")
    if stacked is None:
        out["kd"], out["kd_c"] = proj(wl["kd"], functools.partial(norm_rope64, scale=1.0), _store_rows,
                                      [rows(nqd, f32), rows(nqd, MXU_DTYPE)], gk64, name="proj_kd")
        out["vd"], out["vd_c"] = proj(wl["vd"], _chunks_raw, _store_rows,
                                      [rows(nqd, f32), rows(nqd, MXU_DTYPE)], name="proj_vd")
        out["ks"], out["ks_c"] = proj(wl["ks"], functools.partial(norm_rope128, scale=1.0), _store_rows,
                                      [rows(D_DSA, f32), rows(D_DSA, MXU_DTYPE)], gk128, name="proj_ks")
        out["vs"], out["vs_c"] = proj(wl["vs"], _chunks_raw, _store_rows,
                                      [rows(D_DSA, f32), rows(D_DSA, MXU_DTYPE)], name="proj_vs")
        out["kiwi"], out["ki2"] = proj(wl["kiwi"], _chunks_kiwi, _store_kiwi_rows,
                                       [rows(LANES, f32), rows(LANES, MXU_DTYPE)], g_kiwi, name="proj_kiwi")
        return out

    l = stacked["layer"]
    b, t = stacked["b"], stacked["t"]
    nrb = t // tm
    nh = nqd // LANES
    hb = min(tn, nqd) // LANES
    depth = stacked["pdk"].shape[0]
    out["pdk"], out["kd_c"] = proj(
        wl["kd"], functools.partial(norm_rope64, scale=1.0), _store_kd_prompt,
        [dict(shape=(depth, b, nh, LANES, t), dtype=f32, block=(None, None, hb, LANES, tm),
              index=lambda i, j: (l, i // nrb, j, 0, i % nrb), alias=stacked["pdk"]),
         rows(nqd, MXU_DTYPE)], gk64, name="proj_kd")
    out["pdv"], out["vd_c"] = proj(
        wl["vd"], _chunks_raw, _store_heads,
        [dict(shape=(depth, b, nh, t, LANES), dtype=f32, block=(None, None, hb, tm, LANES),
              index=lambda i, j: (l, i // nrb, j, i % nrb, 0), alias=stacked["pdv"]),
         dict(shape=(b, nh, t, LANES), dtype=MXU_DTYPE, block=(None, hb, tm, LANES),
              index=lambda i, j: (i // nrb, j, i % nrb, 0), alias=None)], name="proj_vd")
    tok_major = lambda key: dict(shape=(depth, b, t, D_DSA), dtype=f32, block=(None, None, tm, D_DSA),
                                 index=lambda i, j: (l, i // nrb, i % nrb, 0), alias=stacked[key])
    out["psk"], out["ks_c"] = proj(wl["ks"], functools.partial(norm_rope128, scale=1.0), _store_rows,
                                   [tok_major("psk"), rows(D_DSA, MXU_DTYPE)], gk128, name="proj_ks")
    out["psv"], out["vs_c"] = proj(wl["vs"], _chunks_raw, _store_rows,
                                   [tok_major("psv"), rows(D_DSA, MXU_DTYPE)], name="proj_vs")
    out["kiwi"], out["ki2"], out["pik"] = proj(
        wl["kiwi"], _chunks_kiwi, _store_kiwi_prompt,
        [rows(LANES, f32), rows(LANES, MXU_DTYPE),
         dict(shape=(depth, b, D_IDX, t), dtype=f32, block=(None, None, D_IDX, tm),
              index=lambda i, j: (l, i // nrb, 0, i % nrb), alias=stacked["pik"])], g_kiwi, name="proj_kiwi")
    return out


def kernel(x_prompt, x_sample, mem_prompt, cache_diff_k, cache_diff_v, cache_dsa_k, cache_dsa_v, cache_idx_k,
           cache_mem_k, cache_mem_v, page_table, w_in, w_out, w_mem_kv, g_in, g_mem, g_q_diff, g_k_diff,
           g_sub_diff, lam_q1, lam_k1, lam_q2, lam_k2, g_q_dsa, g_k_dsa, g_k_idx, g_q_mem, g_k_mem):
    depth = w_in.shape[0]
    b, t, d = x_prompt.shape
    db, nt, _ = x_sample.shape
    n_mem = mem_prompt.shape[1]
    n_phys, page = cache_dsa_k.shape[1], cache_dsa_k.shape[2]
    npages = page_table.shape[1]
    past = npages * page
    h_diff = cache_diff_k.shape[3]
    w_diff = h_diff * 2 * D_DH
    h_dsa = (3 * d // 8) // D_DSA
    w_dsa = h_dsa * D_DSA
    h_mem = cache_mem_k.shape[3]
    w_mem = h_mem * D_MEM
    gsz = LANES // nt
    assert w_diff + w_dsa + w_mem == d and w_diff == w_dsa and LANES % nt == 0 and h_dsa <= gsz and gsz == H_IDX
    topk_p = min(TOPK_MAX, t // 4)
    topk_s = min(TOPK_MAX, (past + nt) // 4)
    m_p = b * t
    m_s = db * nt
    tm_p = min(1024, t)
    assert t % tm_p == 0

    widths = (w_diff, w_diff, w_diff, w_dsa, D_DSA, D_DSA, H_IDX * D_IDX, D_IDX, H_IDX, w_mem, d)
    offs = [0]
    for wdt in widths:
        offs.append(offs[-1] + wdt)
    names = ("qd", "kd", "vd", "qs", "ks", "vs", "qi", "ki", "wi", "qm", "gate")
    seg = {n: (offs[k], offs[k + 1]) for k, n in enumerate(names)}

    pos_p = jnp.arange(t, dtype=i32)
    pos_s = jnp.tile(past + jnp.arange(nt, dtype=i32), db)
    tab_p = _rope_tables(pos_p, D_DH) + _rope_tables(pos_p, D_DSA)
    tab_s = _rope_tables(pos_s, D_DH) + _rope_tables(pos_s, D_DSA)

    cdkt = cache_diff_k.transpose(0, 1, 3, 4, 5, 2).reshape(depth, n_phys, h_diff, 2 * D_DH, page)
    cdv = cache_diff_v.transpose(0, 1, 3, 2, 4)
    ckit = cache_idx_k.transpose(0, 1, 3, 2)
    cmk = cache_mem_k.reshape(depth, db, n_mem, w_mem)
    cmv = cache_mem_v.reshape(depth, db, n_mem, w_mem)
    w_in_t = jnp.swapaxes(w_in, 1, 2)

    x_p = x_prompt.reshape(m_p, d)
    x_s = x_sample.reshape(m_s, d)
    mem2d = mem_prompt.reshape(b * n_mem, d)
    stk = dict(pdk=jnp.zeros((depth, b, h_diff, 2 * D_DH, t), f32), pdv=jnp.zeros((depth, b, h_diff, t, 2 * D_DH), f32),
               psk=jnp.zeros((depth, b, t, D_DSA), f32), psv=jnp.zeros((depth, b, t, D_DSA), f32),
               pik=jnp.zeros((depth, b, D_IDX, t), f32))
    outs = {k: [] for k in ("pmk", "pmv", "sdk", "sdv", "ssk", "ssv", "sik")}

    for l in range(depth):
        wt = w_in_t[l]
        wl = {n: wt[seg[n][0]:seg[n][1]].astype(MXU_DTYPE)
              for n in ("qd", "kd", "vd", "qs", "ks", "vs", "qi", "qm", "gate")}
        wl["kiwi"] = jnp.pad(wt[seg["ki"][0]:seg["wi"][1]], ((0, LANES - D_IDX - H_IDX), (0, 0))).astype(MXU_DTYPE)
        gl = dict(g_in=g_in[l], g_q_diff=g_q_diff[l], g_k_diff=g_k_diff[l], g_q_dsa=g_q_dsa[l],
                  g_k_dsa=g_k_dsa[l], g_k_idx=g_k_idx[l], g_q_mem=g_q_mem[l])
        w_out_l = w_out[l].astype(MXU_DTYPE)
        w_mkv = w_mem_kv[l].astype(MXU_DTYPE)
        lam_init = 0.8 - 0.6 * math.exp(-0.3 * l)
        lam = (jnp.exp(jnp.sum(lam_q1[l] * lam_k1[l])) - jnp.exp(jnp.sum(lam_q2[l] * lam_k2[l])) + lam_init)
        lam = lam.astype(f32).reshape(1)
        g_sub = g_sub_diff[l].reshape(1, 2 * D_DH)
        post = 1.0 - lam_init

        tp = _mixer_inputs(x_p, tab_p, wl, gl, tm=tm_p, stacked=dict(stk, layer=l, b=b, t=t))
        for key in ("pdk", "pdv", "psk", "psv", "pik"):
            stk[key] = tp[key]
        hm = _rmsnorm(mem2d, g_mem[l], 256)
        mrows = lambda dt: [_rows_out(b * n_mem, w_mem, dt, b * n_mem, 512)]
        mk, = _proj(hm, w_mkv[:, :w_mem], functools.partial(_chunks_norm256, scale=1.0), _store_rows, mrows(f32),
                    [("col", _tile_gain(g_k_mem[l], w_mem))], tm=b * n_mem, tn=512, nt=False, name="proj_mk")
        mv, = _proj(hm, w_mkv[:, w_mem:], _chunks_raw, _store_rows, mrows(f32),
                    tm=b * n_mem, tn=512, nt=False, name="proj_mv")
        r3 = lambda a: a.reshape(b, t, a.shape[-1])
        sg = r3(tp["sg"])
        od = _diff_prompt(lam, r3(tp["qd"]), r3(tp["kd_c"]), tp["vd_c"], g_sub, sg, post_scale=post)
        tc = 256
        vst = tp["vs_c"].reshape(b, t // tc, tc, D_DSA).transpose(0, 1, 3, 2)
        os_ = _dsa_prompt(r3(tp["qs"]), r3(tp["qi"]), r3(tp["kiwi"]), r3(tp["ki2"]), r3(tp["ks_c"]), vst, sg,
                          topk=topk_p, tc=tc)
        om = _mem_attn(r3(tp["qm"]), mk.reshape(1, b, n_mem, w_mem), mv.reshape(1, b, n_mem, w_mem), sg,
                       layer=0, sg_col0=(w_diff + w_dsa) // D_MEM, tq=min(512, t))
        x_p = _outproj(od.reshape(m_p, w_diff), os_.reshape(m_p, w_dsa), om.reshape(m_p, w_mem), w_out_l, x_p,
                       tm=tm_p, tn=512)
        outs["pmk"].append(mk.reshape(b, n_mem, h_mem, D_MEM))
        outs["pmv"].append(mv.reshape(b, n_mem, h_mem, D_MEM))

        ts = _mixer_inputs(x_s, tab_s, wl, gl, tm=m_s)
        s3 = lambda a: a.reshape(db, nt, a.shape[-1])
        sg_s = s3(ts["sg"])
        tokpad = lambda a, axis: jnp.pad(a, [(0, page - nt) if ax == axis else (0, 0) for ax in range(a.ndim)])
        q5 = ts["qd"].reshape(db, nt, h_diff, 2, D_DH).transpose(0, 2, 3, 1, 4)
        eye_c = jnp.eye(2, dtype=MXU_DTYPE)
        qm = (q5[:, :, :, :, None, :] * eye_c[None, None, :, None, :, None]).reshape(db, h_diff * 2 * nt, 2 * D_DH)
        kd4 = ts["kd"].reshape(db, nt, h_diff, 2 * D_DH)
        vd4 = ts["vd"].reshape(db, nt, h_diff, 2 * D_DH)
        od_s = _diff_sample(page_table, lam, qm, cdkt, cdv, tokpad(kd4.transpose(0, 2, 3, 1), 3),
                            tokpad(vd4.transpose(0, 2, 1, 3), 2), g_sub, sg_s, layer=l, nt=nt, post_scale=post)
        qs4 = ts["qs"].reshape(db, nt, h_dsa, D_DSA)
        qs_rows = jnp.pad(qs4, ((0, 0), (0, 0), (0, gsz - h_dsa), (0, 0))).reshape(db, LANES, D_DSA)
        qi_rows = ts["qi"].reshape(db, nt * H_IDX, D_IDX)
        wcol = ts["kiwi"][:, D_IDX:D_IDX + H_IDX].reshape(db, nt * H_IDX, 1)
        sg_dsa = sg_s[:, :, w_diff:w_diff + w_dsa].reshape(db, nt, h_dsa, D_DSA)
        sg_perm = jnp.pad(sg_dsa, ((0, 0), (0, 0), (0, gsz - h_dsa), (0, 0))).reshape(db, LANES, D_DSA)
        ki_new = s3(ts["kiwi"][:, :D_IDX])
        os_s = _dsa_sample(page_table, qs_rows, qi_rows, wcol, ckit, cache_dsa_k, cache_dsa_v,
                           tokpad(ki_new.transpose(0, 2, 1), 2), tokpad(s3(ts["ks"]), 1), tokpad(s3(ts["vs"]), 1),
                           sg_perm, layer=l, nt=nt, topk=topk_s)
        os_s = os_s.reshape(db, nt, gsz, D_DSA)[:, :, :h_dsa].reshape(m_s, w_dsa)
        om_s = _mem_attn(s3(ts["qm"]), cmk, cmv, sg_s, layer=l, sg_col0=(w_diff + w_dsa) // D_MEM, tq=nt)
        x_s = _outproj(od_s.reshape(m_s, w_diff), os_s, om_s.reshape(m_s, w_mem), w_out_l, x_s, tm=m_s, tn=512)
        outs["sdk"].append(ts["kd"].reshape(db, nt, h_diff, 2, D_DH))
        outs["sdv"].append(ts["vd"].reshape(db, nt, h_diff, 2 * D_DH))
        outs["ssk"].append(ts["ks"].reshape(db, nt, D_DSA))
        outs["ssv"].append(ts["vs"].reshape(db, nt, D_DSA))
        outs["sik"].append(ki_new)

    st = lambda k: jnp.stack(outs[k])
    p_diff_k = stk["pdk"].reshape(depth, b, h_diff, 2, D_DH, t).transpose(0, 1, 5, 2, 3, 4)
    p_diff_v = stk["pdv"].transpose(0, 1, 3, 2, 4)
    p_idx_k = stk["pik"].transpose(0, 1, 3, 2)
    return (x_p.reshape(b, t, d), x_s.reshape(db, nt, d),
            p_diff_k, p_diff_v, stk["psk"], stk["psv"], p_idx_k, st("pmk"), st("pmv"),
            st("sdk"), st("sdv"), st("ssk"), st("ssv"), st("sik"))
```

```python
import functools
import math

import jax
import jax.numpy as jnp
from jax import lax
from jax.experimental import pallas as pl
from jax.experimental.pallas import tpu as pltpu

EPS = 1e-6
ROPE_THETA = 10000.0
TOPK_MAX = 256
LANES = 128
D_DH = 64
D_DSA = 128
D_IDX = 64
H_IDX = 16
D_MEM = 256
NEG_BIG = -1e30
KEY_NEG_INF = -2139095041
INT_MIN = -2147483648
MXU_DTYPE = jnp.bfloat16
VMEM_LIMIT_BYTES = 52 * 1024 * 1024
PAGES_PER_STEP = 4
DIFF_TQ = 512
DIFF_TK = 512
LOG2E = math.log2(math.e)

f32 = jnp.float32
i32 = jnp.int32


def _cparams(sem):
    return pltpu.CompilerParams(dimension_semantics=sem, vmem_limit_bytes=VMEM_LIMIT_BYTES)


def _dot(a, b):
    return jnp.dot(a, b, preferred_element_type=f32)


def _dot_nt(a, b):
    return lax.dot_general(a, b, (((1,), (1,)), ((), ())), preferred_element_type=f32)


def _iota(shape, dim):
    return lax.broadcasted_iota(i32, shape, dim)


def _group_mat(gsize):
    r = _iota((LANES, LANES), 0) // gsize
    c = _iota((LANES, LANES), 1) // gsize
    return (r == c).astype(f32).astype(MXU_DTYPE)


def _group_sum(x, gmat):
    hi = x.astype(MXU_DTYPE)
    lo = (x - hi.astype(f32)).astype(MXU_DTYPE)
    return _dot(hi, gmat) + _dot(lo, gmat)


def _rope_chunk(n, cos, sin_signed, half):
    if 2 * half == LANES:
        rot = pltpu.roll(n, half, 1)
    else:
        first = (_iota(n.shape, 1) % (2 * half)) < half
        rot = jnp.where(first, pltpu.roll(n, LANES - half, 1), pltpu.roll(n, half, 1))
    return n * cos + rot * sin_signed


def _f2key(x):
    b = pltpu.bitcast(x, i32)
    return b ^ (lax.shift_right_arithmetic(b, 31) & 0x7FFFFFFF)


def _rmsnorm_kernel(x_ref, g_ref, o_ref):
    x = x_ref[...]
    ms = jnp.mean(x * x, axis=-1, keepdims=True)
    o_ref[...] = (x * lax.rsqrt(ms + EPS) * g_ref[...]).astype(o_ref.dtype)


def _rmsnorm(x, g, tm):
    m, d = x.shape
    return pl.pallas_call(
        _rmsnorm_kernel,
        grid=(m // tm,),
        in_specs=[pl.BlockSpec((tm, d), lambda i: (i, 0)), pl.BlockSpec((1, d), lambda i: (0, 0))],
        out_specs=pl.BlockSpec((tm, d), lambda i: (i, 0)),
        out_shape=jax.ShapeDtypeStruct((m, d), MXU_DTYPE),
        compiler_params=_cparams(("arbitrary",)),
        name="rmsnorm",
    )(x, g.reshape(1, d))


def _chunks_raw(z, aux, store):
    for c in range(z.shape[1] // LANES):
        store(c, z[:, c * LANES:(c + 1) * LANES])


def _chunks_silu(z, aux, store):
    for c in range(z.shape[1] // LANES):
        zc = z[:, c * LANES:(c + 1) * LANES]
        store(c, zc / (1.0 + jnp.exp(-zc)))


def _chunks_norm_rope(z, aux, store, *, gsize, half, scale):
    gain, cos, sin = aux
    gmat = _group_mat(gsize)
    for c in range(z.shape[1] // LANES):
        sl = slice(c * LANES, (c + 1) * LANES)
        zc = z[:, sl]
        ss = _group_sum(zc * zc, gmat)
        n = zc * lax.rsqrt(ss * (1.0 / gsize) + EPS) * gain[:, sl]
        r = _rope_chunk(n, cos, sin, half)
        store(c, r * scale if scale != 1.0 else r)


def _chunks_rope(z, aux, store, *, half, scale):
    cos, sin = aux
    for c in range(z.shape[1] // LANES):
        store(c, _rope_chunk(z[:, c * LANES:(c + 1) * LANES], cos, sin, half) * scale)


def _chunks_norm256(z, aux, store, *, scale):
    (gain,) = aux
    gmat = _group_mat(LANES)
    for c in range(z.shape[1] // D_MEM):
        a = z[:, c * D_MEM:c * D_MEM + LANES]
        b = z[:, c * D_MEM + LANES:(c + 1) * D_MEM]
        ss = _group_sum(a * a, gmat) + _group_sum(b * b, gmat)
        inv = lax.rsqrt(ss * (1.0 / D_MEM) + EPS)
        for k, v in enumerate((a, b)):
            sl = slice(c * D_MEM + k * LANES, c * D_MEM + (k + 1) * LANES)
            r = v * inv * gain[:, sl]
            store(2 * c + k, r * scale if scale != 1.0 else r)


def _chunks_kiwi(z, aux, store):
    gain, cos, sin = aux
    gmat = _group_mat(D_IDX)
    ss = _group_sum(z * z, gmat)
    n = z * lax.rsqrt(ss * (1.0 / D_IDX) + EPS) * gain
    r = _rope_chunk(n, cos, sin, D_IDX // 2)
    lo = _iota(z.shape, 1) < D_IDX
    store(0, jnp.where(lo, r, z))
    r_lo = jnp.where(lo, r, 0.0)
    store(1, r_lo + pltpu.roll(r_lo, D_IDX, 1))


def _store_rows(outs, c, r):
    for o in outs:
        o[:, c * LANES:(c + 1) * LANES] = r.astype(o.dtype)


def _store_kd_prompt(outs, c, r):
    outs[0][c] = r.T
    outs[1][:, c * LANES:(c + 1) * LANES] = r.astype(outs[1].dtype)


def _store_vd_prompt(outs, c, r):
    outs[0][c] = r
    rt = r.T.astype(outs[1].dtype)
    kb = outs[1].shape[-1]
    for kk in range(r.shape[0] // kb):
        outs[1][c, kk] = rt[:, kk * kb:(kk + 1) * kb]


def _store_kiwi_rows(outs, c, r):
    outs[c][...] = r.astype(outs[c].dtype)


def _store_kiwi_prompt(outs, c, r):
    if c == 0:
        outs[0][...] = r
        outs[2][...] = r.T[:D_IDX, :]
    else:
        outs[1][...] = r.astype(outs[1].dtype)


def _proj_kernel(*refs, chunk_fn, store_fn, n_aux, n_alias, nt):
    h_ref, w_ref = refs[:2]
    aux = [r[...] for r in refs[2:2 + n_aux]]
    outs = refs[2 + n_aux + n_alias:]
    z = _dot_nt(h_ref[...], w_ref[...]) if nt else _dot(h_ref[...], w_ref[...])
    chunk_fn(z, aux, functools.partial(store_fn, outs))


def _col_tile(n, tn):
    return math.gcd(n, tn)


def _rows_out(m, n, dtype, tm, tn):
    tn = _col_tile(n, tn)
    return dict(shape=(m, n), dtype=dtype, block=(tm, tn), index=lambda i, j: (i, j), alias=None)


def _proj(h, w, chunk_fn, store_fn, outs, aux=(), *, tm, tn, nt, name):
    m, k = h.shape
    n = w.shape[0] if nt else w.shape[1]
    tn = _col_tile(n, tn)
    assert m % tm == 0 and tn % LANES == 0
    w_spec = pl.BlockSpec((tn, k), lambda i, j: (j, 0)) if nt else pl.BlockSpec((k, tn), lambda i, j: (0, j))
    in_specs = [pl.BlockSpec((tm, k), lambda i, j: (i, 0)), w_spec]
    args = [h, w]
    for kind, a in aux:
        if kind == "col":
            in_specs.append(pl.BlockSpec((1, tn), lambda i, j: (0, j)))
        else:
            nrb = a.shape[0] // tm
            in_specs.append(pl.BlockSpec((tm, LANES), lambda i, j, nrb=nrb: (i % nrb, 0)))
        args.append(a)
    aliases = {}
    for k_out, o in enumerate(outs):
        if o["alias"] is not None:
            aliases[len(args)] = k_out
            in_specs.append(pl.BlockSpec(memory_space=pl.ANY))
            args.append(o["alias"])
    return pl.pallas_call(
        functools.partial(_proj_kernel, chunk_fn=chunk_fn, store_fn=store_fn, n_aux=len(aux),
                          n_alias=len(aliases), nt=nt),
        grid=(m // tm, n // tn),
        in_specs=in_specs,
        out_specs=[pl.BlockSpec(o["block"], o["index"]) for o in outs],
        out_shape=[jax.ShapeDtypeStruct(o["shape"], o["dtype"]) for o in outs],
        input_output_aliases=aliases,
        compiler_params=_cparams(("arbitrary", "arbitrary")),
        name=name,
    )(*args)


def _outproj_kernel(od_ref, os_ref, om_ref, w_ref, x_ref, o_ref, *, wd, ws):
    acc = _dot(od_ref[...], w_ref[0:wd, :])
    acc += _dot(os_ref[...], w_ref[wd:wd + ws, :])
    acc += _dot(om_ref[...], w_ref[wd + ws:, :])
    o_ref[...] = x_ref[...] + acc


def _outproj(od, os_, om, w, x, *, tm, tn):
    m, d = x.shape
    wd, ws, wm = od.shape[1], os_.shape[1], om.shape[1]
    return pl.pallas_call(
        functools.partial(_outproj_kernel, wd=wd, ws=ws),
        grid=(m // tm, d // tn),
        in_specs=[
            pl.BlockSpec((tm, wd), lambda i, j: (i, 0)),
            pl.BlockSpec((tm, ws), lambda i, j: (i, 0)),
            pl.BlockSpec((tm, wm), lambda i, j: (i, 0)),
            pl.BlockSpec((wd + ws + wm, tn), lambda i, j: (0, j)),
            pl.BlockSpec((tm, tn), lambda i, j: (i, j)),
        ],
        out_specs=pl.BlockSpec((tm, tn), lambda i, j: (i, j)),
        out_shape=jax.ShapeDtypeStruct((m, d), f32),
        compiler_params=_cparams(("arbitrary", "arbitrary")),
        name="outproj",
    )(od, os_, om, w, x)


def _sub_rmsnorm_gate(od, g, post_scale, sg):
    ms = jnp.mean(od * od, axis=-1, keepdims=True)
    return od * lax.rsqrt(ms + EPS) * g * post_scale * sg.astype(f32)


def _diff_prompt_kernel(pi_ref, pj_ref, lam_ref, q_ref, k_ref, vt_ref, g_ref, sg_ref, o_ref,
                        m_scr, l_scr, acc_scr, *, tq, tk, hb, post_scale):
    p_id = pl.program_id(2)
    i = pi_ref[p_id]
    j = pj_ref[p_id]

    @pl.when(j == 0)
    def _():
        m_scr[...] = jnp.full_like(m_scr, NEG_BIG)
        l_scr[...] = jnp.zeros_like(l_scr)
        acc_scr[...] = jnp.zeros_like(acc_scr)

    def body(masked):
        for hh in range(hb):
            hsl = slice(hh * LANES, (hh + 1) * LANES)
            q = q_ref[:, hsl]
            kc = k_ref[:, hsl]
            lo = _iota(q.shape, 1) < D_DH
            zero = jnp.zeros_like(q)
            for c in range(2):
                qc = jnp.where(lo, q, zero) if c == 0 else jnp.where(lo, zero, q)
                s = _dot_nt(kc, qc)
                if masked:
                    s = jnp.where(j * tk + _iota(s.shape, 0) <= i * tq + _iota(s.shape, 1), s, NEG_BIG)
                r = 2 * hh + c
                m = m_scr[r]
                m_new = jnp.maximum(m, jnp.max(s, axis=0, keepdims=True))
                alpha = jnp.exp2(m - m_new)
                p = jnp.exp2(s - m_new)
                l_scr[r] = alpha * l_scr[r] + jnp.sum(p, axis=0, keepdims=True)
                acc_scr[r] = alpha * acc_scr[r] + _dot(vt_ref[hh], p.astype(MXU_DTYPE))
                m_scr[r] = m_new

    crosses_diagonal = (j + 1) * tk - 1 > i * tq
    pl.when(crosses_diagonal)(functools.partial(body, True))
    pl.when(jnp.logical_not(crosses_diagonal))(functools.partial(body, False))

    @pl.when(j == ((i + 1) * tq - 1) // tk)
    def _():
        for hh in range(hb):
            hsl = slice(hh * LANES, (hh + 1) * LANES)
            od_t = acc_scr[2 * hh] / l_scr[2 * hh] - lam_ref[0] * (acc_scr[2 * hh + 1] / l_scr[2 * hh + 1])
            o_ref[:, hsl] = _sub_rmsnorm_gate(od_t.T, g_ref[...], post_scale, sg_ref[:, hsl]).astype(o_ref.dtype)


def _diff_prompt(lam, qd, kd, vdt, g_sub, sg, *, post_scale, tq, tk):
    b, t, w = qd.shape
    nh = w // LANES
    hb = 2 if nh % 2 == 0 else 1
    pairs = [(i, j) for i in range(t // tq) for j in range(((i + 1) * tq - 1) // tk + 1)]
    pi = jnp.asarray([p[0] for p in pairs], i32)
    pj = jnp.asarray([p[1] for p in pairs], i32)
    grid_spec = pltpu.PrefetchScalarGridSpec(
        num_scalar_prefetch=2,
        grid=(b, nh // hb, len(pairs)),
        in_specs=[
            pl.BlockSpec(memory_space=pltpu.SMEM),
            pl.BlockSpec((None, tq, hb * LANES), lambda bb, h, p, pi, pj: (bb, pi[p], h)),
            pl.BlockSpec((None, tk, hb * LANES), lambda bb, h, p, pi, pj: (bb, pj[p], h)),
            pl.BlockSpec((None, hb, None, LANES, tk), lambda bb, h, p, pi, pj: (bb, h, pj[p], 0, 0)),
            pl.BlockSpec((1, LANES), lambda bb, h, p, pi, pj: (0, 0)),
            pl.BlockSpec((None, tq, hb * LANES), lambda bb, h, p, pi, pj: (bb, pi[p], h)),
        ],
        out_specs=pl.BlockSpec((None, tq, hb * LANES), lambda bb, h, p, pi, pj: (bb, pi[p], h)),
        scratch_shapes=[
            pltpu.VMEM((2 * hb, 1, tq), f32),
            pltpu.VMEM((2 * hb, 1, tq), f32),
            pltpu.VMEM((2 * hb, LANES, tq), f32),
        ],
    )
    return pl.pallas_call(
        functools.partial(_diff_prompt_kernel, tq=tq, tk=tk, hb=hb, post_scale=post_scale),
        grid_spec=grid_spec,
        out_shape=jax.ShapeDtypeStruct((b, t, w), MXU_DTYPE),
        compiler_params=_cparams(("arbitrary", "arbitrary", "arbitrary")),
        name="diff_prompt",
    )(pi, pj, lam, qd, kd, vdt, g_sub, sg)


def _kth_largest_key_sub(key_scr, nblk, rows, topk):
    def count_ge(cand):
        def body(c, acc):
            k = key_scr[pl.ds(pl.multiple_of(c * rows, rows), rows), :]
            return acc + jnp.sum((k >= cand).astype(i32).reshape(rows // 8, 8, LANES), axis=0)

        acc = lax.fori_loop(0, nblk, body, jnp.zeros((8, LANES), i32))
        return jnp.sum(acc, axis=0, keepdims=True)

    def bit_body(it, prefix):
        cand = prefix + lax.shift_left(jnp.int32(1), 31 - it)
        return jnp.where(count_ge(cand) >= topk, cand, prefix)

    kth = lax.fori_loop(0, 32, bit_body, jnp.full((1, LANES), INT_MIN, i32))
    return kth, count_ge


def _demote_surplus_ties_sub(key_scr, nblk, rows, topk, kth, count_ge):
    tie = (count_ge(kth) > topk) & (kth > KEY_NEG_INF)

    @pl.when(jnp.max(tie.astype(i32)) > 0)
    def _():
        need = (topk - count_ge(kth + 1)).astype(f32)
        tri = (_iota((rows, rows), 0) >= _iota((rows, rows), 1)).astype(f32).astype(MXU_DTYPE)

        def body(c, run):
            sl = pl.ds(pl.multiple_of(c * rows, rows), rows)
            k = key_scr[sl, :]
            eq = k == kth
            incl = _dot(tri, eq.astype(f32).astype(MXU_DTYPE))
            drop = eq & ((run + incl) > need)
            key_scr[sl, :] = jnp.where(drop, KEY_NEG_INF, k)
            return run + incl[rows - 1:rows, :]

        lax.fori_loop(0, nblk, body, jnp.zeros((1, LANES), f32))


def _kth_largest_key_lane(key_scr, nblk, nq, topk):
    def count_ge(cand):
        def body(c, acc):
            return acc + jnp.where(key_scr[c] >= cand, 1.0, 0.0)

        acc = lax.fori_loop(0, nblk, body, jnp.zeros((nq, LANES), f32))
        return jnp.sum(acc, axis=1, keepdims=True)

    def bit_body(it, prefix):
        cand = prefix + lax.shift_left(jnp.int32(1), 31 - it)
        return jnp.where(count_ge(cand) >= topk, cand, prefix)

    kth = lax.fori_loop(0, 32, bit_body, jnp.full((nq, 1), INT_MIN, i32))
    return kth, count_ge


def _demote_surplus_ties_lane(key_scr, nblk, nq, topk, kth, count_ge):
    tie = (count_ge(kth) > topk) & (kth > KEY_NEG_INF)

    @pl.when(jnp.max(jnp.where(tie, 1.0, 0.0)) > 0.0)
    def _():
        need = topk - count_ge(kth + 1)
        triu = (_iota((LANES, LANES), 0) <= _iota((LANES, LANES), 1)).astype(f32).astype(MXU_DTYPE)
        pad = jnp.zeros((16 - nq % 16, LANES), f32) if nq % 16 else None

        def body(c, run):
            k = key_scr[c]
            eq = k == kth
            eqf = jnp.where(eq, 1.0, 0.0)
            if pad is not None:
                eqf = jnp.concatenate([eqf, pad], axis=0)
            incl = _dot(eqf.astype(MXU_DTYPE), triu)[:nq]
            drop = eq & ((run + incl) > need)
            key_scr[c] = jnp.where(drop, KEY_NEG_INF, k)
            return run + incl[:, LANES - 1:LANES]

        lax.fori_loop(0, nblk, body, jnp.zeros((nq, 1), f32))


def _dsa_prompt_kernel(qs_ref, qi_ref, kiwi_ref, ki2_ref, ks_ref, vst_ref, sg_ref, o_ref,
                       key_scr, qm_scr, wt_scr, qst_scr, acc_scr, *, nh, topk, tc):
    qb = pl.program_id(1)
    tq = LANES
    nch = (qb * tq + tq + tc - 1) // tc

    lo = _iota((tq, LANES), 1) < D_IDX
    for h in range(H_IDX):
        chunk = qi_ref[:, (h // 2) * LANES:(h // 2 + 1) * LANES]
        keep = lo if h % 2 == 0 else jnp.logical_not(lo)
        qm_scr[h * tq:(h + 1) * tq, :] = jnp.where(keep, chunk, jnp.zeros_like(chunk))
    wt_scr[...] = kiwi_ref[...].T
    for h in range(nh):
        qst_scr[h * tq:(h + 1) * tq, :] = qs_ref[:, h * LANES:(h + 1) * LANES]

    tpos = qb * tq + _iota((1, LANES), 1)

    def idx_body(c, carry):
        sl = pl.ds(pl.multiple_of(c * tc, tc), tc)
        kc = ki2_ref[sl, :]
        acc = jnp.zeros((tc, LANES), f32)
        for hp in range(H_IDX // 2):
            sc = _dot_nt(kc, qm_scr[2 * hp * tq:(2 * hp + 2) * tq, :])
            for k in range(2):
                h = 2 * hp + k
                acc = acc + jnp.maximum(sc[:, k * tq:(k + 1) * tq], 0.0) * wt_scr[D_IDX + h:D_IDX + h + 1, :]
        acc = jnp.where(acc == 0.0, 0.0, acc)
        kpos = c * tc + _iota((tc, LANES), 0)
        acc = jnp.where(kpos <= tpos, acc, -jnp.inf)
        key_scr[sl, :] = _f2key(acc)
        return carry

    lax.fori_loop(0, nch, idx_body, 0)

    kth, count_ge = _kth_largest_key_sub(key_scr, nch, tc, topk)
    _demote_surplus_ties_sub(key_scr, nch, tc, topk, kth, count_ge)
    thr = jnp.maximum(kth, KEY_NEG_INF + 1)

    acc_scr[...] = jnp.zeros_like(acc_scr)

    def att_body(c, carry):
        m, l = carry
        sl = pl.ds(pl.multiple_of(c * tc, tc), tc)
        s = _dot_nt(ks_ref[sl, :], qst_scr[...])
        msk = key_scr[sl, :] >= thr
        s = jnp.where(jnp.concatenate([msk] * nh, axis=1), s, NEG_BIG)
        m_new = jnp.maximum(m, jnp.max(s, axis=0, keepdims=True))
        alpha = jnp.exp2(m - m_new)
        p = jnp.exp2(s - m_new)
        l = alpha * l + jnp.sum(p, axis=0, keepdims=True)
        acc_scr[...] = alpha * acc_scr[...] + _dot(vst_ref[c], p.astype(MXU_DTYPE))
        return m_new, l

    init = (jnp.full((1, nh * tq), NEG_BIG, f32), jnp.zeros((1, nh * tq), f32))
    _, l = lax.fori_loop(0, nch, att_body, init)
    out_t = acc_scr[...] / l
    for h in range(nh):
        sl = slice(h * LANES, (h + 1) * LANES)
        o_ref[:, sl] = (out_t[:, sl].T * sg_ref[:, sl].astype(f32)).astype(o_ref.dtype)


def _dsa_prompt(qs, qi, kiwi, ki2, ks, vst, sg, *, topk, tc=256):
    b, t, w = qs.shape
    nh = w // LANES
    tq = LANES
    assert t % tc == 0 and tc % tq == 0
    return pl.pallas_call(
        functools.partial(_dsa_prompt_kernel, nh=nh, topk=topk, tc=tc),
        grid=(b, t // tq),
        in_specs=[
            pl.BlockSpec((None, tq, w), lambda bb, i: (bb, i, 0)),
            pl.BlockSpec((None, tq, H_IDX * D_IDX), lambda bb, i: (bb, i, 0)),
            pl.BlockSpec((None, tq, LANES), lambda bb, i: (bb, i, 0)),
            pl.BlockSpec((None, t, LANES), lambda bb, i: (bb, 0, 0)),
            pl.BlockSpec((None, t, LANES), lambda bb, i: (bb, 0, 0)),
            pl.BlockSpec((None, t // tc, LANES, tc), lambda bb, i: (bb, 0, 0, 0)),
            pl.BlockSpec((None, tq, w), lambda bb, i: (bb, i, 1)),
        ],
        out_specs=pl.BlockSpec((None, tq, w), lambda bb, i: (bb, i, 0)),
        out_shape=jax.ShapeDtypeStruct((b, t, w), MXU_DTYPE),
        scratch_shapes=[
            pltpu.VMEM((t, LANES), i32),
            pltpu.VMEM((H_IDX * tq, LANES), MXU_DTYPE),
            pltpu.VMEM((LANES, LANES), f32),
            pltpu.VMEM((nh * tq, LANES), MXU_DTYPE),
            pltpu.VMEM((LANES, nh * tq), f32),
        ],
        compiler_params=_cparams(("arbitrary", "arbitrary")),
        name="dsa_prompt",
    )(qs, qi, kiwi, ki2, ks, vst, sg)


def _mem_attn_kernel(q_ref, k_ref, v_ref, sg_ref, o_ref):
    s = _dot_nt(q_ref[...], k_ref[...].astype(MXU_DTYPE))
    m = jnp.max(s, axis=1, keepdims=True)
    p = jnp.exp2(s - m)
    l = jnp.sum(p, axis=1, keepdims=True)
    o = _dot(p.astype(MXU_DTYPE), v_ref[...].astype(MXU_DTYPE)) / l
    o_ref[...] = (o * sg_ref[...].astype(f32)).astype(o_ref.dtype)


def _mem_attn(q, k, v, sg, *, layer, sg_col0, tq):
    b, t, w = q.shape
    nh = w // D_MEM
    nm = k.shape[2]
    return pl.pallas_call(
        _mem_attn_kernel,
        grid=(b, t // tq, nh),
        in_specs=[
            pl.BlockSpec((None, tq, D_MEM), lambda bb, i, h: (bb, i, h)),
            pl.BlockSpec((None, None, nm, D_MEM), lambda bb, i, h: (layer, bb, 0, h)),
            pl.BlockSpec((None, None, nm, D_MEM), lambda bb, i, h: (layer, bb, 0, h)),
            pl.BlockSpec((None, tq, D_MEM), lambda bb, i, h: (bb, i, sg_col0 + h)),
        ],
        out_specs=pl.BlockSpec((None, tq, D_MEM), lambda bb, i, h: (bb, i, h)),
        out_shape=jax.ShapeDtypeStruct((b, t, w), MXU_DTYPE),
        compiler_params=_cparams(("arbitrary", "arbitrary", "arbitrary")),
        name="mem_attn",
    )(q, k, v, sg)


def _diff_sample_kernel(pt_ref, lam_ref, qm_ref, *refs, nh, nt, pps, nsteps, post_scale):
    kt_refs = refs[:pps]
    v_refs = refs[pps:2 * pps]
    knewt_ref, vnew_ref, g_ref, sg_ref, o_ref, m_scr, l_scr, acc_scr = refs[2 * pps:]
    s_id = pl.program_id(1)
    nr = 2 * nt

    @pl.when(s_id == 0)
    def _():
        m_scr[...] = jnp.full_like(m_scr, NEG_BIG)
        l_scr[...] = jnp.zeros_like(l_scr)
        acc_scr[...] = jnp.zeros_like(acc_scr)

    def update(kt_ref, v_ref, masked):
        s = jnp.concatenate(
            [_dot(qm_ref[h * nr:(h + 1) * nr, :], kt_ref[h].astype(MXU_DTYPE)) for h in range(nh)], axis=0)
        if masked:
            tpos = _iota(s.shape, 0) % nt
            s = jnp.where(_iota(s.shape, 1) <= tpos, s, NEG_BIG)
        m = m_scr[...]
        m_new = jnp.maximum(m, jnp.max(s, axis=1, keepdims=True))
        alpha = jnp.exp2(m - m_new)
        p = jnp.exp2(s - m_new)
        l_scr[...] = alpha * l_scr[...] + jnp.sum(p, axis=1, keepdims=True)
        pb = p.astype(MXU_DTYPE)
        pv = jnp.concatenate(
            [_dot(pb[h * nr:(h + 1) * nr, :], v_ref[h].astype(MXU_DTYPE)) for h in range(nh)], axis=0)
        acc_scr[...] = alpha * acc_scr[...] + pv
        m_scr[...] = m_new

    @pl.when(s_id < nsteps)
    def _():
        for r in range(pps):
            update(kt_refs[r], v_refs[r], False)

    @pl.when(s_id == nsteps)
    def _():
        update(knewt_ref, vnew_ref, True)
        o = acc_scr[...] / l_scr[...]
        for h in range(nh):
            od = o[h * nr:h * nr + nt] - lam_ref[0] * o[h * nr + nt:(h + 1) * nr]
            sl = slice(h * LANES, (h + 1) * LANES)
            o_ref[:, sl] = _sub_rmsnorm_gate(od, g_ref[...], post_scale, sg_ref[:, sl]).astype(o_ref.dtype)


def _diff_sample(page_table, lam, qm, cache_kt, cache_v, knewt, vnew, g_sub, sg, *, layer, nt, post_scale):
    db, rows, _ = qm.shape
    nh = cache_v.shape[2]
    page = cache_v.shape[3]
    npages = page_table.shape[1]
    pps = PAGES_PER_STEP
    nsteps = npages // pps
    w = nh * LANES

    def page_map(r):
        def f(bb, s, pt):
            return (layer, pt[bb, jnp.minimum(s * pps + r, npages - 1)], 0, 0, 0)
        return f

    kt_specs = [pl.BlockSpec((None, None, nh, LANES, page), page_map(r)) for r in range(pps)]
    v_specs = [pl.BlockSpec((None, None, nh, page, LANES), page_map(r)) for r in range(pps)]
    grid_spec = pltpu.PrefetchScalarGridSpec(
        num_scalar_prefetch=1,
        grid=(db, nsteps + 1),
        in_specs=[
            pl.BlockSpec(memory_space=pltpu.SMEM),
            pl.BlockSpec((None, rows, LANES), lambda bb, s, pt: (bb, 0, 0)),
            *kt_specs, *v_specs,
            pl.BlockSpec((None, nh, LANES, page), lambda bb, s, pt: (bb, 0, 0, 0)),
            pl.BlockSpec((None, nh, page, LANES), lambda bb, s, pt: (bb, 0, 0, 0)),
            pl.BlockSpec((1, LANES), lambda bb, s, pt: (0, 0)),
            pl.BlockSpec((None, nt, w), lambda bb, s, pt: (bb, 0, 0)),
        ],
        out_specs=pl.BlockSpec((None, nt, w), lambda bb, s, pt: (bb, 0, 0)),
        scratch_shapes=[
            pltpu.VMEM((rows, 1), f32),
            pltpu.VMEM((rows, 1), f32),
            pltpu.VMEM((rows, LANES), f32),
        ],
    )
    return pl.pallas_call(
        functools.partial(_diff_sample_kernel, nh=nh, nt=nt, pps=pps, nsteps=nsteps, post_scale=post_scale),
        grid_spec=grid_spec,
        out_shape=jax.ShapeDtypeStruct((db, nt, w), MXU_DTYPE),
        compiler_params=_cparams(("arbitrary", "arbitrary")),
        name="diff_sample",
    )(page_table, lam, qm, *([cache_kt] * pps), *([cache_v] * pps), knewt, vnew, g_sub, sg)


def _dsa_sample_kernel(pt_ref, qs_ref, qi_ref, w_ref, *refs, nt, pps, nsteps, topk, page):
    kit_refs = refs[:pps]
    ks_refs = refs[pps:2 * pps]
    vs_refs = refs[2 * pps:3 * pps]
    kint_ref, ksn_ref, vsn_ref, sg_ref, o_ref, key_scr, s_scr, v_scr = refs[3 * pps:]
    s_id = pl.program_id(1)
    gsz = LANES // nt

    def process(blk, kit, ks, vs, causal_new):
        sc = _dot(qi_ref[...], kit.astype(MXU_DTYPE))
        val = jnp.maximum(sc, 0.0) * w_ref[...]
        acc = jnp.sum(val.reshape(nt, gsz, page), axis=1)
        acc = jnp.where(acc == 0.0, 0.0, acc)
        if causal_new:
            acc = jnp.where(_iota(acc.shape, 1) <= _iota(acc.shape, 0), acc, -jnp.inf)
        key_scr[blk] = _f2key(acc)
        s_scr[blk] = _dot_nt(qs_ref[...], ks.astype(MXU_DTYPE))
        v_scr[blk] = vs.astype(MXU_DTYPE)

    @pl.when(s_id < nsteps)
    def _():
        for r in range(pps):
            process(s_id * pps + r, kit_refs[r][...], ks_refs[r][...], vs_refs[r][...], False)

    @pl.when(s_id == nsteps)
    def _():
        nblk = nsteps * pps + 1
        process(nblk - 1, kint_ref[...], ksn_ref[...], vsn_ref[...], True)

        kth, count_ge = _kth_largest_key_lane(key_scr, nblk, nt, topk)
        _demote_surplus_ties_lane(key_scr, nblk, nt, topk, kth, count_ge)
        thr = jnp.maximum(kth, KEY_NEG_INF + 1)

        def rowmask(c):
            mf = jnp.where(key_scr[c] >= thr, 1.0, 0.0)
            return jnp.broadcast_to(mf[:, None, :], (nt, gsz, page)).reshape(nt * gsz, page) > 0.5

        def max_body(c, mx):
            return jnp.maximum(mx, jnp.where(rowmask(c), s_scr[c], NEG_BIG))

        mx = lax.fori_loop(0, nblk, max_body, jnp.full((LANES, page), NEG_BIG, f32))
        m = jnp.max(mx, axis=1, keepdims=True)

        def att_body(c, carry):
            lacc, acc = carry
            p = jnp.where(rowmask(c), jnp.exp2(s_scr[c] - m), 0.0)
            return lacc + p, acc + _dot(p.astype(MXU_DTYPE), v_scr[c])

        lacc, acc = lax.fori_loop(0, nblk, att_body, (jnp.zeros((LANES, page), f32), jnp.zeros((LANES, D_DSA), f32)))
        l = jnp.sum(lacc, axis=1, keepdims=True)
        o_ref[...] = (acc / l * sg_ref[...].astype(f32)).astype(o_ref.dtype)


def _dsa_sample(page_table, qs_rows, qi_rows, wcol, cache_kit, cache_ks, cache_vs, kit_new, ks_new, vs_new, sg_perm,
                *, layer, nt, topk):
    db = qs_rows.shape[0]
    page = cache_ks.shape[2]
    npages = page_table.shape[1]
    pps = PAGES_PER_STEP
    nsteps = npages // pps
    nblk = npages + 1
    assert page == LANES

    def page_map(r):
        def f(bb, s, pt):
            return (layer, pt[bb, jnp.minimum(s * pps + r, npages - 1)], 0, 0)
        return f

    def pspecs(shape):
        return [pl.BlockSpec((None, None) + shape, page_map(r)) for r in range(pps)]

    per_b = lambda shape: pl.BlockSpec((None,) + shape, lambda bb, s, pt: (bb, 0, 0))
    grid_spec = pltpu.PrefetchScalarGridSpec(
        num_scalar_prefetch=1,
        grid=(db, nsteps + 1),
        in_specs=[
            per_b((LANES, D_DSA)), per_b((LANES, D_IDX)), per_b((LANES, 1)),
            *pspecs((D_IDX, page)), *pspecs((page, D_DSA)), *pspecs((page, D_DSA)),
            per_b((D_IDX, page)), per_b((page, D_DSA)), per_b((page, D_DSA)),
            per_b((LANES, D_DSA)),
        ],
        out_specs=per_b((LANES, D_DSA)),
        scratch_shapes=[
            pltpu.VMEM((nblk, nt, page), i32),
            pltpu.VMEM((nblk, LANES, page), f32),
            pltpu.VMEM((nblk, page, D_DSA), MXU_DTYPE),
        ],
    )
    return pl.pallas_call(
        functools.partial(_dsa_sample_kernel, nt=nt, pps=pps, nsteps=nsteps, topk=topk, page=page),
        grid_spec=grid_spec,
        out_shape=jax.ShapeDtypeStruct((db, LANES, D_DSA), MXU_DTYPE),
        compiler_params=_cparams(("arbitrary", "arbitrary")),
        name="dsa_sample",
    )(page_table, qs_rows, qi_rows, wcol, *([cache_kit] * pps), *([cache_ks] * pps), *([cache_vs] * pps),
      kit_new, ks_new, vs_new, sg_perm)


def _rope_tables(pos, head_dim):
    half = head_dim // 2
    lane = jnp.arange(LANES)
    inv = ROPE_THETA ** (-(lane % half).astype(f32) / half)
    ang = pos.astype(f32)[:, None] * inv[None, :]
    sign = jnp.where((lane % head_dim) < half, -1.0, 1.0).astype(f32)
    return jnp.cos(ang), jnp.sin(ang) * sign[None, :]


def _tile_gain(g, n):
    return jnp.tile(g, n // g.shape[0]).reshape(1, n).astype(f32)


def _mixer_inputs(x2d, pos_tab, wl, gl, *, tm, stacked=None):
    cos64, sin64, cos128, sin128 = pos_tab
    m = x2d.shape[0]
    h = _rmsnorm(x2d, gl["g_in"], min(tm, 256))
    tn = 512
    proj = functools.partial(_proj, h, tm=tm, tn=tn, nt=True)
    rows = lambda n, dt: _rows_out(m, n, dt, tm, tn)
    rope64 = [("row", cos64), ("row", sin64)]
    rope128 = [("row", cos128), ("row", sin128)]
    nqd = wl["qd"].shape[0]
    nqs = wl["qs"].shape[0]
    nqm = wl["qm"].shape[0]
    norm_rope64 = functools.partial(_chunks_norm_rope, gsize=D_DH, half=D_DH // 2)
    norm_rope128 = functools.partial(_chunks_norm_rope, gsize=D_DSA, half=D_DSA // 2)
    gq64 = [("col", _tile_gain(gl["g_q_diff"], nqd))] + rope64
    gk64 = [("col", _tile_gain(gl["g_k_diff"], nqd))] + rope64
    gq128 = [("col", _tile_gain(gl["g_q_dsa"], nqs))] + rope128
    gk128 = [("col", gl["g_k_dsa"].reshape(1, D_DSA))] + rope128
    g_kiwi = [("col", jnp.concatenate([gl["g_k_idx"], jnp.ones((LANES - D_IDX,), f32)]).reshape(1, LANES))] + rope64
    out = {}
    (out["qd"],) = proj(wl["qd"], functools.partial(norm_rope64, scale=D_DH ** -0.5 * LOG2E), _store_rows,
                        [rows(nqd, MXU_DTYPE)], gq64, name="proj_qd")
    (out["qs"],) = proj(wl["qs"], functools.partial(norm_rope128, scale=D_DSA ** -0.5 * LOG2E), _store_rows,
                        [rows(nqs, MXU_DTYPE)], gq128, name="proj_qs")
    (out["qi"],) = proj(wl["qi"], functools.partial(_chunks_rope, half=D_IDX // 2, scale=D_IDX ** -0.5 * H_IDX ** -0.5),
                        _store_rows, [rows(H_IDX * D_IDX, MXU_DTYPE)], rope64, name="proj_qi")
    (out["qm"],) = proj(wl["qm"], functools.partial(_chunks_norm256, scale=D_MEM ** -0.5 * LOG2E), _store_rows,
                        [rows(nqm, MXU_DTYPE)], [("col", _tile_gain(gl["g_q_mem"], nqm))], name="proj_qm")
    (out["sg"],) = proj(wl["gate"], _chunks_silu, _store_rows, [rows(wl["gate"].shape[0], MXU_DTYPE)], name="proj_gate")
    if stacked is None:
        out["kd"], out["kd_c"] = proj(wl["kd"], functools.partial(norm_rope64, scale=1.0), _store_rows,
                                      [rows(nqd, f32), rows(nqd, MXU_DTYPE)], gk64, name="proj_kd")
        out["vd"], out["vd_c"] = proj(wl["vd"], _chunks_raw, _store_rows,
                                      [rows(nqd, f32), rows(nqd, MXU_DTYPE)], name="proj_vd")
        out["ks"], out["ks_c"] = proj(wl["ks"], functools.partial(norm_rope128, scale=1.0), _store_rows,
                                      [rows(D_DSA, f32), rows(D_DSA, MXU_DTYPE)], gk128, name="proj_ks")
        out["vs"], out["vs_c"] = proj(wl["vs"], _chunks_raw, _store_rows,
                                      [rows(D_DSA, f32), rows(D_DSA, MXU_DTYPE)], name="proj_vs")
        out["kiwi"], out["ki2"] = proj(wl["kiwi"], _chunks_kiwi, _store_kiwi_rows,
                                       [rows(LANES, f32), rows(LANES, MXU_DTYPE)], g_kiwi, name="proj_kiwi")
        return out

    l = stacked["layer"]
    b, t = stacked["b"], stacked["t"]
    nrb = t // tm
    nh = nqd // LANES
    hb = _col_tile(nqd, tn) // LANES
    vkb = stacked["diff_tk"]
    assert tm % vkb == 0
    depth = stacked["pdk"].shape[0]
    out["pdk"], out["kd_c"] = proj(
        wl["kd"], functools.partial(norm_rope64, scale=1.0), _store_kd_prompt,
        [dict(shape=(depth, b, nh, LANES, t), dtype=f32, block=(None, None, hb, LANES, tm),
              index=lambda i, j: (l, i // nrb, j, 0, i % nrb), alias=stacked["pdk"]),
         rows(nqd, MXU_DTYPE)], gk64, name="proj_kd")
    out["pdv"], out["vd_c"] = proj(
        wl["vd"], _chunks_raw, _store_vd_prompt,
        [dict(shape=(depth, b, nh, t, LANES), dtype=f32, block=(None, None, hb, tm, LANES),
              index=lambda i, j: (l, i // nrb, j, i % nrb, 0), alias=stacked["pdv"]),
         dict(shape=(b, nh, t // vkb, LANES, vkb), dtype=MXU_DTYPE, block=(None, hb, tm // vkb, LANES, vkb),
              index=lambda i, j: (i // nrb, j, i % nrb, 0, 0), alias=None)], name="proj_vd")
    tok_major = lambda key: dict(shape=(depth, b, t, D_DSA), dtype=f32, block=(None, None, tm, D_DSA),
                                 index=lambda i, j: (l, i // nrb, i % nrb, 0), alias=stacked[key])
    out["psk"], out["ks_c"] = proj(wl["ks"], functools.partial(norm_rope128, scale=1.0), _store_rows,
                                   [tok_major("psk"), rows(D_DSA, MXU_DTYPE)], gk128, name="proj_ks")
    out["psv"], out["vs_c"] = proj(wl["vs"], _chunks_raw, _store_rows,
                                   [tok_major("psv"), rows(D_DSA, MXU_DTYPE)], name="proj_vs")
    out["kiwi"], out["ki2"], out["pik"] = proj(
        wl["kiwi"], _chunks_kiwi, _store_kiwi_prompt,
        [rows(LANES, f32), rows(LANES, MXU_DTYPE),
         dict(shape=(depth, b, D_IDX, t), dtype=f32, block=(None, None, D_IDX, tm),
              index=lambda i, j: (l, i // nrb, 0, i % nrb), alias=stacked["pik"])], g_kiwi, name="proj_kiwi")
    return out


def kernel(x_prompt, x_sample, mem_prompt, cache_diff_k, cache_diff_v, cache_dsa_k, cache_dsa_v, cache_idx_k,
           cache_mem_k, cache_mem_v, page_table, w_in, w_out, w_mem_kv, g_in, g_mem, g_q_diff, g_k_diff,
           g_sub_diff, lam_q1, lam_k1, lam_q2, lam_k2, g_q_dsa, g_k_dsa, g_k_idx, g_q_mem, g_k_mem):
    depth = w_in.shape[0]
    b, t, d = x_prompt.shape
    db, nt, _ = x_sample.shape
    n_mem = mem_prompt.shape[1]
    n_phys, page = cache_dsa_k.shape[1], cache_dsa_k.shape[2]
    npages = page_table.shape[1]
    past = npages * page
    h_diff = cache_diff_k.shape[3]
    w_diff = h_diff * 2 * D_DH
    h_dsa = (3 * d // 8) // D_DSA
    w_dsa = h_dsa * D_DSA
    h_mem = cache_mem_k.shape[3]
    w_mem = h_mem * D_MEM
    gsz = LANES // nt
    assert w_diff + w_dsa + w_mem == d and w_diff == w_dsa and LANES % nt == 0 and h_dsa <= gsz and gsz == H_IDX
    topk_p = min(TOPK_MAX, t // 4)
    topk_s = min(TOPK_MAX, (past + nt) // 4)
    m_p = b * t
    m_s = db * nt
    tm_p = min(1024, t)
    assert t % tm_p == 0

    widths = (w_diff, w_diff, w_diff, w_dsa, D_DSA, D_DSA, H_IDX * D_IDX, D_IDX, H_IDX, w_mem, d)
    offs = [0]
    for wdt in widths:
        offs.append(offs[-1] + wdt)
    names = ("qd", "kd", "vd", "qs", "ks", "vs", "qi", "ki", "wi", "qm", "gate")
    seg = {n: (offs[k], offs[k + 1]) for k, n in enumerate(names)}

    pos_p = jnp.arange(t, dtype=i32)
    pos_s = jnp.tile(past + jnp.arange(nt, dtype=i32), db)
    tab_p = _rope_tables(pos_p, D_DH) + _rope_tables(pos_p, D_DSA)
    tab_s = _rope_tables(pos_s, D_DH) + _rope_tables(pos_s, D_DSA)

    cdkt = cache_diff_k.transpose(0, 1, 3, 4, 5, 2).reshape(depth, n_phys, h_diff, 2 * D_DH, page)
    cdv = cache_diff_v.transpose(0, 1, 3, 2, 4)
    ckit = cache_idx_k.transpose(0, 1, 3, 2)
    cmk = cache_mem_k.reshape(depth, db, n_mem, w_mem)
    cmv = cache_mem_v.reshape(depth, db, n_mem, w_mem)
    w_in_t = jnp.swapaxes(w_in, 1, 2)

    x_p = x_prompt.reshape(m_p, d)
    x_s = x_sample.reshape(m_s, d)
    mem2d = mem_prompt.reshape(b * n_mem, d)
    stk = dict(pdk=jnp.zeros((depth, b, h_diff, 2 * D_DH, t), f32), pdv=jnp.zeros((depth, b, h_diff, t, 2 * D_DH), f32),
               psk=jnp.zeros((depth, b, t, D_DSA), f32), psv=jnp.zeros((depth, b, t, D_DSA), f32),
               pik=jnp.zeros((depth, b, D_IDX, t), f32))
    outs = {k: [] for k in ("pmk", "pmv", "sdk", "sdv", "ssk", "ssv", "sik")}

    for l in range(depth):
        wt = w_in_t[l]
        wl = {n: wt[seg[n][0]:seg[n][1]].astype(MXU_DTYPE)
              for n in ("qd", "kd", "vd", "qs", "ks", "vs", "qi", "qm", "gate")}
        wl["kiwi"] = jnp.pad(wt[seg["ki"][0]:seg["wi"][1]], ((0, LANES - D_IDX - H_IDX), (0, 0))).astype(MXU_DTYPE)
        gl = dict(g_in=g_in[l], g_q_diff=g_q_diff[l], g_k_diff=g_k_diff[l], g_q_dsa=g_q_dsa[l],
                  g_k_dsa=g_k_dsa[l], g_k_idx=g_k_idx[l], g_q_mem=g_q_mem[l])
        w_out_l = w_out[l].astype(MXU_DTYPE)
        w_mkv = w_mem_kv[l].astype(MXU_DTYPE)
        lam_init = 0.8 - 0.6 * math.exp(-0.3 * l)
        lam = (jnp.exp(jnp.sum(lam_q1[l] * lam_k1[l])) - jnp.exp(jnp.sum(lam_q2[l] * lam_k2[l])) + lam_init)
        lam = lam.astype(f32).reshape(1)
        g_sub = g_sub_diff[l].reshape(1, 2 * D_DH)
        post = 1.0 - lam_init

        diff_tq, diff_tk = min(DIFF_TQ, t), min(DIFF_TK, t)
        tp = _mixer_inputs(x_p, tab_p, wl, gl, tm=tm_p, stacked=dict(stk, layer=l, b=b, t=t, diff_tk=diff_tk))
        for key in ("pdk", "pdv", "psk", "psv", "pik"):
            stk[key] = tp[key]
        hm = _rmsnorm(mem2d, g_mem[l], 256)
        mrows = lambda dt: [_rows_out(b * n_mem, w_mem, dt, b * n_mem, 512)]
        mk, = _proj(hm, w_mkv[:, :w_mem], functools.partial(_chunks_norm256, scale=1.0), _store_rows, mrows(f32),
                    [("col", _tile_gain(g_k_mem[l], w_mem))], tm=b * n_mem, tn=512, nt=False, name="proj_mk")
        mv, = _proj(hm, w_mkv[:, w_mem:], _chunks_raw, _store_rows, mrows(f32),
                    tm=b * n_mem, tn=512, nt=False, name="proj_mv")
        r3 = lambda a: a.reshape(b, t, a.shape[-1])
        sg = r3(tp["sg"])
        od = _diff_prompt(lam, r3(tp["qd"]), r3(tp["kd_c"]), tp["vd_c"], g_sub, sg, post_scale=post,
                          tq=diff_tq, tk=diff_tk)
        tc = 256
        vst = tp["vs_c"].reshape(b, t // tc, tc, D_DSA).transpose(0, 1, 3, 2)
        os_ = _dsa_prompt(r3(tp["qs"]), r3(tp["qi"]), r3(tp["kiwi"]), r3(tp["ki2"]), r3(tp["ks_c"]), vst, sg,
                          topk=topk_p, tc=tc)
        om = _mem_attn(r3(tp["qm"]), mk.reshape(1, b, n_mem, w_mem), mv.reshape(1, b, n_mem, w_mem), sg,
                       layer=0, sg_col0=(w_diff + w_dsa) // D_MEM, tq=min(512, t))
        x_p = _outproj(od.reshape(m_p, w_diff), os_.reshape(m_p, w_dsa), om.reshape(m_p, w_mem), w_out_l, x_p,
                       tm=tm_p, tn=512)
        outs["pmk"].append(mk.reshape(b, n_mem, h_mem, D_MEM))
        outs["pmv"].append(mv.reshape(b, n_mem, h_mem, D_MEM))

        ts = _mixer_inputs(x_s, tab_s, wl, gl, tm=m_s)
        s3 = lambda a: a.reshape(db, nt, a.shape[-1])
        sg_s = s3(ts["sg"])
        tokpad = lambda a, axis: jnp.pad(a, [(0, page - nt) if ax == axis else (0, 0) for ax in range(a.ndim)])
        q5 = ts["qd"].reshape(db, nt, h_diff, 2, D_DH).transpose(0, 2, 3, 1, 4)
        eye_c = jnp.eye(2, dtype=MXU_DTYPE)
        qm = (q5[:, :, :, :, None, :] * eye_c[None, None, :, None, :, None]).reshape(db, h_diff * 2 * nt, 2 * D_DH)
        kd4 = ts["kd"].reshape(db, nt, h_diff, 2 * D_DH)
        vd4 = ts["vd"].reshape(db, nt, h_diff, 2 * D_DH)
        od_s = _diff_sample(page_table, lam, qm, cdkt, cdv, tokpad(kd4.transpose(0, 2, 3, 1), 3),
                            tokpad(vd4.transpose(0, 2, 1, 3), 2), g_sub, sg_s, layer=l, nt=nt, post_scale=post)
        qs4 = ts["qs"].reshape(db, nt, h_dsa, D_DSA)
        qs_rows = jnp.pad(qs4, ((0, 0), (0, 0), (0, gsz - h_dsa), (0, 0))).reshape(db, LANES, D_DSA)
        qi_rows = ts["qi"].reshape(db, nt * H_IDX, D_IDX)
        wcol = ts["kiwi"][:, D_IDX:D_IDX + H_IDX].reshape(db, nt * H_IDX, 1)
        sg_dsa = sg_s[:, :, w_diff:w_diff + w_dsa].reshape(db, nt, h_dsa, D_DSA)
        sg_perm = jnp.pad(sg_dsa, ((0, 0), (0, 0), (0, gsz - h_dsa), (0, 0))).reshape(db, LANES, D_DSA)
        ki_new = s3(ts["kiwi"][:, :D_IDX])
        os_s = _dsa_sample(page_table, qs_rows, qi_rows, wcol, ckit, cache_dsa_k, cache_dsa_v,
                           tokpad(ki_new.transpose(0, 2, 1), 2), tokpad(s3(ts["ks"]), 1), tokpad(s3(ts["vs"]), 1),
                           sg_perm, layer=l, nt=nt, topk=topk_s)
        os_s = os_s.reshape(db, nt, gsz, D_DSA)[:, :, :h_dsa].reshape(m_s, w_dsa)
        om_s = _mem_attn(s3(ts["qm"]), cmk, cmv, sg_s, layer=l, sg_col0=(w_diff + w_dsa) // D_MEM, tq=nt)
        x_s = _outproj(od_s.reshape(m_s, w_diff), os_s, om_s.reshape(m_s, w_mem), w_out_l, x_s, tm=m_s, tn=512)
        outs["sdk"].append(ts["kd"].reshape(db, nt, h_diff, 2, D_DH))
        outs["sdv"].append(ts["vd"].reshape(db, nt, h_diff, 2 * D_DH))
        outs["ssk"].append(ts["ks"].reshape(db, nt, D_DSA))
        outs["ssv"].append(ts["vs"].reshape(db, nt, D_DSA))
        outs["sik"].append(ki_new)

    st = lambda k: jnp.stack(outs[k])
    p_diff_k = stk["pdk"].reshape(depth, b, h_diff, 2, D_DH, t).transpose(0, 1, 5, 2, 3, 4)
    p_diff_v = stk["pdv"].transpose(0, 1, 3, 2, 4)
    p_idx_k = stk["pik"].transpose(0, 1, 3, 2)
    return (x_p.reshape(b, t, d), x_s.reshape(db, nt, d),
            p_diff_k, p_diff_v, stk["psk"], stk["psv"], p_idx_k, st("pmk"), st("pmv"),
            st("sdk"), st("sdv"), st("ssk"), st("ssv"), st("sik"))
```

```python
import functools
import math

import jax
import jax.numpy as jnp
from jax import lax
from jax.experimental import pallas as pl
from jax.experimental.pallas import tpu as pltpu

EPS = 1e-6
ROPE_THETA = 10000.0
TOPK_MAX = 256
LANES = 128
D_DH = 64
D_DSA = 128
D_IDX = 64
H_IDX = 16
D_MEM = 256
NEG_BIG = -1e30
KEY_NEG_INF = -2139095041
INT_MIN = -2147483648
MXU_DTYPE = jnp.bfloat16
VMEM_LIMIT_BYTES = 52 * 1024 * 1024
PAGES_PER_STEP = 8
DIFF_TQ = 512
DIFF_TK = 512
LOG2E = math.log2(math.e)

f32 = jnp.float32
i32 = jnp.int32


def _cparams(sem):
    return pltpu.CompilerParams(dimension_semantics=sem, vmem_limit_bytes=VMEM_LIMIT_BYTES)


def _dot(a, b):
    return jnp.dot(a, b, preferred_element_type=f32)


def _dot_nt(a, b):
    return lax.dot_general(a, b, (((1,), (1,)), ((), ())), preferred_element_type=f32)


def _iota(shape, dim):
    return lax.broadcasted_iota(i32, shape, dim)


def _group_mat(gsize):
    r = _iota((LANES, LANES), 0) // gsize
    c = _iota((LANES, LANES), 1) // gsize
    return (r == c).astype(f32).astype(MXU_DTYPE)


def _group_sum(x, gmat):
    hi = x.astype(MXU_DTYPE)
    lo = (x - hi.astype(f32)).astype(MXU_DTYPE)
    return _dot(hi, gmat) + _dot(lo, gmat)


def _rope_chunk(n, cos, sin_signed, half):
    if 2 * half == LANES:
        rot = pltpu.roll(n, half, 1)
    else:
        first = (_iota(n.shape, 1) % (2 * half)) < half
        rot = jnp.where(first, pltpu.roll(n, LANES - half, 1), pltpu.roll(n, half, 1))
    return n * cos + rot * sin_signed


def _f2key(x):
    b = pltpu.bitcast(x, i32)
    return b ^ (lax.shift_right_arithmetic(b, 31) & 0x7FFFFFFF)


def _rmsnorm_kernel(x_ref, g_ref, o_ref):
    x = x_ref[...]
    ms = jnp.mean(x * x, axis=-1, keepdims=True)
    o_ref[...] = (x * lax.rsqrt(ms + EPS) * g_ref[...]).astype(o_ref.dtype)


def _rmsnorm(x, g, tm):
    m, d = x.shape
    return pl.pallas_call(
        _rmsnorm_kernel,
        grid=(m // tm,),
        in_specs=[pl.BlockSpec((tm, d), lambda i: (i, 0)), pl.BlockSpec((1, d), lambda i: (0, 0))],
        out_specs=pl.BlockSpec((tm, d), lambda i: (i, 0)),
        out_shape=jax.ShapeDtypeStruct((m, d), MXU_DTYPE),
        compiler_params=_cparams(("arbitrary",)),
        name="rmsnorm",
    )(x, g.reshape(1, d))


def _chunks_raw(z, aux, store):
    for c in range(z.shape[1] // LANES):
        store(c, z[:, c * LANES:(c + 1) * LANES])


def _chunks_silu(z, aux, store):
    for c in range(z.shape[1] // LANES):
        zc = z[:, c * LANES:(c + 1) * LANES]
        store(c, zc / (1.0 + jnp.exp(-zc)))


def _chunks_norm_rope(z, aux, store, *, gsize, half, scale):
    gain, cos, sin = aux
    gmat = _group_mat(gsize)
    for c in range(z.shape[1] // LANES):
        sl = slice(c * LANES, (c + 1) * LANES)
        zc = z[:, sl]
        ss = _group_sum(zc * zc, gmat)
        n = zc * lax.rsqrt(ss * (1.0 / gsize) + EPS) * gain[:, sl]
        r = _rope_chunk(n, cos, sin, half)
        store(c, r * scale if scale != 1.0 else r)


def _chunks_rope(z, aux, store, *, half, scale):
    cos, sin = aux
    for c in range(z.shape[1] // LANES):
        store(c, _rope_chunk(z[:, c * LANES:(c + 1) * LANES], cos, sin, half) * scale)


def _chunks_norm256(z, aux, store, *, scale):
    (gain,) = aux
    gmat = _group_mat(LANES)
    for c in range(z.shape[1] // D_MEM):
        a = z[:, c * D_MEM:c * D_MEM + LANES]
        b = z[:, c * D_MEM + LANES:(c + 1) * D_MEM]
        ss = _group_sum(a * a, gmat) + _group_sum(b * b, gmat)
        inv = lax.rsqrt(ss * (1.0 / D_MEM) + EPS)
        for k, v in enumerate((a, b)):
            sl = slice(c * D_MEM + k * LANES, c * D_MEM + (k + 1) * LANES)
            r = v * inv * gain[:, sl]
            store(2 * c + k, r * scale if scale != 1.0 else r)


def _chunks_kiwi(z, aux, store):
    gain, cos, sin = aux
    gmat = _group_mat(D_IDX)
    ss = _group_sum(z * z, gmat)
    n = z * lax.rsqrt(ss * (1.0 / D_IDX) + EPS) * gain
    r = _rope_chunk(n, cos, sin, D_IDX // 2)
    lo = _iota(z.shape, 1) < D_IDX
    store(0, jnp.where(lo, r, z))
    r_lo = jnp.where(lo, r, 0.0)
    store(1, r_lo + pltpu.roll(r_lo, D_IDX, 1))


def _store_rows(outs, c, r):
    for o in outs:
        o[:, c * LANES:(c + 1) * LANES] = r.astype(o.dtype)


def _store_kd_prompt(outs, c, r):
    outs[0][c] = r.T
    outs[1][:, c * LANES:(c + 1) * LANES] = r.astype(outs[1].dtype)


def _store_vd_prompt(outs, c, r):
    outs[0][c] = r
    rt = r.T.astype(outs[1].dtype)
    kb = outs[1].shape[-1]
    for kk in range(r.shape[0] // kb):
        outs[1][c, kk] = rt[:, kk * kb:(kk + 1) * kb]


def _store_kiwi_rows(outs, c, r):
    outs[c][...] = r.astype(outs[c].dtype)


def _store_kiwi_prompt(outs, c, r):
    if c == 0:
        outs[0][...] = r
        outs[2][...] = r.T[:D_IDX, :]
    else:
        outs[1][...] = r.astype(outs[1].dtype)


def _proj_kernel(*refs, chunk_fn, store_fn, n_aux, n_alias, nt):
    h_ref, w_ref = refs[:2]
    aux = [r[...] for r in refs[2:2 + n_aux]]
    outs = refs[2 + n_aux + n_alias:]
    z = _dot_nt(h_ref[...], w_ref[...]) if nt else _dot(h_ref[...], w_ref[...])
    chunk_fn(z, aux, functools.partial(store_fn, outs))


def _col_tile(n, tn):
    return math.gcd(n, tn)


def _rows_out(m, n, dtype, tm, tn):
    tn = _col_tile(n, tn)
    return dict(shape=(m, n), dtype=dtype, block=(tm, tn), index=lambda i, j: (i, j), alias=None)


def _proj(h, w, chunk_fn, store_fn, outs, aux=(), *, tm, tn, nt, name):
    m, k = h.shape
    n = w.shape[0] if nt else w.shape[1]
    tn = _col_tile(n, tn)
    assert m % tm == 0 and tn % LANES == 0
    w_spec = pl.BlockSpec((tn, k), lambda i, j: (j, 0)) if nt else pl.BlockSpec((k, tn), lambda i, j: (0, j))
    in_specs = [pl.BlockSpec((tm, k), lambda i, j: (i, 0)), w_spec]
    args = [h, w]
    for kind, a in aux:
        if kind == "col":
            in_specs.append(pl.BlockSpec((1, tn), lambda i, j: (0, j)))
        else:
            nrb = a.shape[0] // tm
            in_specs.append(pl.BlockSpec((tm, LANES), lambda i, j, nrb=nrb: (i % nrb, 0)))
        args.append(a)
    aliases = {}
    for k_out, o in enumerate(outs):
        if o["alias"] is not None:
            aliases[len(args)] = k_out
            in_specs.append(pl.BlockSpec(memory_space=pl.ANY))
            args.append(o["alias"])
    return pl.pallas_call(
        functools.partial(_proj_kernel, chunk_fn=chunk_fn, store_fn=store_fn, n_aux=len(aux),
                          n_alias=len(aliases), nt=nt),
        grid=(m // tm, n // tn),
        in_specs=in_specs,
        out_specs=[pl.BlockSpec(o["block"], o["index"]) for o in outs],
        out_shape=[jax.ShapeDtypeStruct(o["shape"], o["dtype"]) for o in outs],
        input_output_aliases=aliases,
        compiler_params=_cparams(("arbitrary", "arbitrary")),
        name=name,
    )(*args)


def _outproj_kernel(od_ref, os_ref, om_ref, w_ref, x_ref, o_ref, *, wd, ws):
    acc = _dot(od_ref[...], w_ref[0:wd, :])
    acc += _dot(os_ref[...], w_ref[wd:wd + ws, :])
    acc += _dot(om_ref[...], w_ref[wd + ws:, :])
    o_ref[...] = x_ref[...] + acc


def _outproj(od, os_, om, w, x, *, tm, tn):
    m, d = x.shape
    wd, ws, wm = od.shape[1], os_.shape[1], om.shape[1]
    return pl.pallas_call(
        functools.partial(_outproj_kernel, wd=wd, ws=ws),
        grid=(m // tm, d // tn),
        in_specs=[
            pl.BlockSpec((tm, wd), lambda i, j: (i, 0)),
            pl.BlockSpec((tm, ws), lambda i, j: (i, 0)),
            pl.BlockSpec((tm, wm), lambda i, j: (i, 0)),
            pl.BlockSpec((wd + ws + wm, tn), lambda i, j: (0, j)),
            pl.BlockSpec((tm, tn), lambda i, j: (i, j)),
        ],
        out_specs=pl.BlockSpec((tm, tn), lambda i, j: (i, j)),
        out_shape=jax.ShapeDtypeStruct((m, d), f32),
        compiler_params=_cparams(("arbitrary", "arbitrary")),
        name="outproj",
    )(od, os_, om, w, x)


def _sub_rmsnorm_gate(od, g, post_scale, sg):
    ms = jnp.mean(od * od, axis=-1, keepdims=True)
    return od * lax.rsqrt(ms + EPS) * g * post_scale * sg.astype(f32)


def _diff_prompt_kernel(pi_ref, pj_ref, lam_ref, q_ref, k_ref, vt_ref, g_ref, sg_ref, o_ref,
                        m_scr, l_scr, acc_scr, *, tq, tk, hb, post_scale):
    p_id = pl.program_id(2)
    i = pi_ref[p_id]
    j = pj_ref[p_id]

    @pl.when(j == 0)
    def _():
        m_scr[...] = jnp.full_like(m_scr, NEG_BIG)
        l_scr[...] = jnp.zeros_like(l_scr)
        acc_scr[...] = jnp.zeros_like(acc_scr)

    def body(masked):
        for hh in range(hb):
            hsl = slice(hh * LANES, (hh + 1) * LANES)
            q = q_ref[:, hsl]
            kc = k_ref[:, hsl]
            lo = _iota(q.shape, 1) < D_DH
            zero = jnp.zeros_like(q)
            for c in range(2):
                qc = jnp.where(lo, q, zero) if c == 0 else jnp.where(lo, zero, q)
                s = _dot_nt(kc, qc)
                if masked:
                    s = jnp.where(j * tk + _iota(s.shape, 0) <= i * tq + _iota(s.shape, 1), s, NEG_BIG)
                r = 2 * hh + c
                m = m_scr[r]
                m_new = jnp.maximum(m, jnp.max(s, axis=0, keepdims=True))
                alpha = jnp.exp2(m - m_new)
                p = jnp.exp2(s - m_new)
                l_scr[r] = alpha * l_scr[r] + jnp.sum(p, axis=0, keepdims=True)
                acc_scr[r] = alpha * acc_scr[r] + _dot(vt_ref[hh], p.astype(MXU_DTYPE))
                m_scr[r] = m_new

    crosses_diagonal = (j + 1) * tk - 1 > i * tq
    pl.when(crosses_diagonal)(functools.partial(body, True))
    pl.when(jnp.logical_not(crosses_diagonal))(functools.partial(body, False))

    @pl.when(j == ((i + 1) * tq - 1) // tk)
    def _():
        for hh in range(hb):
            hsl = slice(hh * LANES, (hh + 1) * LANES)
            od_t = acc_scr[2 * hh] / l_scr[2 * hh] - lam_ref[0] * (acc_scr[2 * hh + 1] / l_scr[2 * hh + 1])
            o_ref[:, hsl] = _sub_rmsnorm_gate(od_t.T, g_ref[...], post_scale, sg_ref[:, hsl]).astype(o_ref.dtype)


def _diff_prompt(lam, qd, kd, vdt, g_sub, sg, *, post_scale, tq, tk):
    b, t, w = qd.shape
    nh = w // LANES
    hb = 2 if nh % 2 == 0 else 1
    pairs = [(i, j) for i in range(t // tq) for j in range(((i + 1) * tq - 1) // tk + 1)]
    pi = jnp.asarray([p[0] for p in pairs], i32)
    pj = jnp.asarray([p[1] for p in pairs], i32)
    grid_spec = pltpu.PrefetchScalarGridSpec(
        num_scalar_prefetch=2,
        grid=(b, nh // hb, len(pairs)),
        in_specs=[
            pl.BlockSpec(memory_space=pltpu.SMEM),
            pl.BlockSpec((None, tq, hb * LANES), lambda bb, h, p, pi, pj: (bb, pi[p], h)),
            pl.BlockSpec((None, tk, hb * LANES), lambda bb, h, p, pi, pj: (bb, pj[p], h)),
            pl.BlockSpec((None, hb, None, LANES, tk), lambda bb, h, p, pi, pj: (bb, h, pj[p], 0, 0)),
            pl.BlockSpec((1, LANES), lambda bb, h, p, pi, pj: (0, 0)),
            pl.BlockSpec((None, tq, hb * LANES), lambda bb, h, p, pi, pj: (bb, pi[p], h)),
        ],
        out_specs=pl.BlockSpec((None, tq, hb * LANES), lambda bb, h, p, pi, pj: (bb, pi[p], h)),
        scratch_shapes=[
            pltpu.VMEM((2 * hb, 1, tq), f32),
            pltpu.VMEM((2 * hb, 1, tq), f32),
            pltpu.VMEM((2 * hb, LANES, tq), f32),
        ],
    )
    return pl.pallas_call(
        functools.partial(_diff_prompt_kernel, tq=tq, tk=tk, hb=hb, post_scale=post_scale),
        grid_spec=grid_spec,
        out_shape=jax.ShapeDtypeStruct((b, t, w), MXU_DTYPE),
        compiler_params=_cparams(("arbitrary", "arbitrary", "arbitrary")),
        name="diff_prompt",
    )(pi, pj, lam, qd, kd, vdt, g_sub, sg)


def _kth_largest_key_sub(key_scr, nblk, rows, topk):
    def count_ge(cand):
        def body(c, acc):
            k = key_scr[pl.ds(pl.multiple_of(c * rows, rows), rows), :]
            return acc + jnp.sum((k >= cand).astype(i32).reshape(rows // 8, 8, LANES), axis=0)

        acc = lax.fori_loop(0, nblk, body, jnp.zeros((8, LANES), i32))
        return jnp.sum(acc, axis=0, keepdims=True)

    def bit_body(it, prefix):
        cand = prefix + lax.shift_left(jnp.int32(1), 31 - it)
        return jnp.where(count_ge(cand) >= topk, cand, prefix)

    kth = lax.fori_loop(0, 32, bit_body, jnp.full((1, LANES), INT_MIN, i32))
    return kth, count_ge


def _demote_surplus_ties_sub(key_scr, nblk, rows, topk, kth, count_ge):
    tie = (count_ge(kth) > topk) & (kth > KEY_NEG_INF)

    @pl.when(jnp.max(tie.astype(i32)) > 0)
    def _():
        need = (topk - count_ge(kth + 1)).astype(f32)
        tri = (_iota((rows, rows), 0) >= _iota((rows, rows), 1)).astype(f32).astype(MXU_DTYPE)

        def body(c, run):
            sl = pl.ds(pl.multiple_of(c * rows, rows), rows)
            k = key_scr[sl, :]
            eq = k == kth
            incl = _dot(tri, eq.astype(f32).astype(MXU_DTYPE))
            drop = eq & ((run + incl) > need)
            key_scr[sl, :] = jnp.where(drop, KEY_NEG_INF, k)
            return run + incl[rows - 1:rows, :]

        lax.fori_loop(0, nblk, body, jnp.zeros((1, LANES), f32))


def _demote_surplus_ties_lane(key_scr, nblk, nq, topk, kth, count_ge):
    tie = (count_ge(kth) > topk) & (kth > KEY_NEG_INF)

    @pl.when(jnp.max(jnp.where(tie, 1.0, 0.0)) > 0.0)
    def _():
        need = topk - count_ge(kth + 1)
        triu = (_iota((LANES, LANES), 0) <= _iota((LANES, LANES), 1)).astype(f32).astype(MXU_DTYPE)
        pad = jnp.zeros((16 - nq % 16, LANES), f32) if nq % 16 else None

        def body(c, run):
            k = key_scr[c]
            eq = k == kth
            eqf = jnp.where(eq, 1.0, 0.0)
            if pad is not None:
                eqf = jnp.concatenate([eqf, pad], axis=0)
            incl = _dot(eqf.astype(MXU_DTYPE), triu)[:nq]
            drop = eq & ((run + incl) > need)
            key_scr[c] = jnp.where(drop, KEY_NEG_INF, k)
            return run + incl[:, LANES - 1:LANES]

        lax.fori_loop(0, nblk, body, jnp.zeros((nq, 1), f32))


def _dsa_prompt_kernel(qs_ref, qi_ref, kiwi_ref, ki2_ref, ks_ref, vst_ref, sg_ref, o_ref,
                       key_scr, qm_scr, wt_scr, qst_scr, acc_scr, *, nh, topk, tc):
    qb = pl.program_id(1)
    tq = LANES
    nch = (qb * tq + tq + tc - 1) // tc

    lo = _iota((tq, LANES), 1) < D_IDX
    for h in range(H_IDX):
        chunk = qi_ref[:, (h // 2) * LANES:(h // 2 + 1) * LANES]
        keep = lo if h % 2 == 0 else jnp.logical_not(lo)
        qm_scr[h * tq:(h + 1) * tq, :] = jnp.where(keep, chunk, jnp.zeros_like(chunk))
    wt_scr[...] = kiwi_ref[...].T
    for h in range(nh):
        qst_scr[h * tq:(h + 1) * tq, :] = qs_ref[:, h * LANES:(h + 1) * LANES]

    tpos = qb * tq + _iota((1, LANES), 1)

    def idx_body(c, carry):
        sl = pl.ds(pl.multiple_of(c * tc, tc), tc)
        kc = ki2_ref[sl, :]
        acc = jnp.zeros((tc, LANES), f32)
        for hp in range(H_IDX // 2):
            sc = _dot_nt(kc, qm_scr[2 * hp * tq:(2 * hp + 2) * tq, :])
            for k in range(2):
                h = 2 * hp + k
                acc = acc + jnp.maximum(sc[:, k * tq:(k + 1) * tq], 0.0) * wt_scr[D_IDX + h:D_IDX + h + 1, :]
        acc = jnp.where(acc == 0.0, 0.0, acc)
        kpos = c * tc + _iota((tc, LANES), 0)
        acc = jnp.where(kpos <= tpos, acc, -jnp.inf)
        key_scr[sl, :] = _f2key(acc)
        return carry

    lax.fori_loop(0, nch, idx_body, 0)

    kth, count_ge = _kth_largest_key_sub(key_scr, nch, tc, topk)
    _demote_surplus_ties_sub(key_scr, nch, tc, topk, kth, count_ge)
    thr = jnp.maximum(kth, KEY_NEG_INF + 1)

    acc_scr[...] = jnp.zeros_like(acc_scr)

    def att_body(c, carry):
        m, l = carry
        sl = pl.ds(pl.multiple_of(c * tc, tc), tc)
        s = _dot_nt(ks_ref[sl, :], qst_scr[...])
        msk = key_scr[sl, :] >= thr
        s = jnp.where(jnp.concatenate([msk] * nh, axis=1), s, NEG_BIG)
        m_new = jnp.maximum(m, jnp.max(s, axis=0, keepdims=True))
        alpha = jnp.exp2(m - m_new)
        p = jnp.exp2(s - m_new)
        l = alpha * l + jnp.sum(p, axis=0, keepdims=True)
        acc_scr[...] = alpha * acc_scr[...] + _dot(vst_ref[c], p.astype(MXU_DTYPE))
        return m_new, l

    init = (jnp.full((1, nh * tq), NEG_BIG, f32), jnp.zeros((1, nh * tq), f32))
    _, l = lax.fori_loop(0, nch, att_body, init)
    out_t = acc_scr[...] / l
    for h in range(nh):
        sl = slice(h * LANES, (h + 1) * LANES)
        o_ref[:, sl] = (out_t[:, sl].T * sg_ref[:, sl].astype(f32)).astype(o_ref.dtype)


def _dsa_prompt(qs, qi, kiwi, ki2, ks, vst, sg, *, topk, tc=256):
    b, t, w = qs.shape
    nh = w // LANES
    tq = LANES
    assert t % tc == 0 and tc % tq == 0
    return pl.pallas_call(
        functools.partial(_dsa_prompt_kernel, nh=nh, topk=topk, tc=tc),
        grid=(b, t // tq),
        in_specs=[
            pl.BlockSpec((None, tq, w), lambda bb, i: (bb, i, 0)),
            pl.BlockSpec((None, tq, H_IDX * D_IDX), lambda bb, i: (bb, i, 0)),
            pl.BlockSpec((None, tq, LANES), lambda bb, i: (bb, i, 0)),
            pl.BlockSpec((None, t, LANES), lambda bb, i: (bb, 0, 0)),
            pl.BlockSpec((None, t, LANES), lambda bb, i: (bb, 0, 0)),
            pl.BlockSpec((None, t // tc, LANES, tc), lambda bb, i: (bb, 0, 0, 0)),
            pl.BlockSpec((None, tq, w), lambda bb, i: (bb, i, 1)),
        ],
        out_specs=pl.BlockSpec((None, tq, w), lambda bb, i: (bb, i, 0)),
        out_shape=jax.ShapeDtypeStruct((b, t, w), MXU_DTYPE),
        scratch_shapes=[
            pltpu.VMEM((t, LANES), i32),
            pltpu.VMEM((H_IDX * tq, LANES), MXU_DTYPE),
            pltpu.VMEM((LANES, LANES), f32),
            pltpu.VMEM((nh * tq, LANES), MXU_DTYPE),
            pltpu.VMEM((LANES, nh * tq), f32),
        ],
        compiler_params=_cparams(("arbitrary", "arbitrary")),
        name="dsa_prompt",
    )(qs, qi, kiwi, ki2, ks, vst, sg)


def _mem_attn_kernel(q_ref, k_ref, v_ref, sg_ref, o_ref):
    s = _dot_nt(q_ref[...], k_ref[...].astype(MXU_DTYPE))
    m = jnp.max(s, axis=1, keepdims=True)
    p = jnp.exp2(s - m)
    l = jnp.sum(p, axis=1, keepdims=True)
    o = _dot(p.astype(MXU_DTYPE), v_ref[...].astype(MXU_DTYPE)) / l
    o_ref[...] = (o * sg_ref[...].astype(f32)).astype(o_ref.dtype)


def _mem_attn(q, k, v, sg, *, layer, sg_col0, tq):
    b, t, w = q.shape
    nh = w // D_MEM
    nm = k.shape[2]
    return pl.pallas_call(
        _mem_attn_kernel,
        grid=(b, t // tq, nh),
        in_specs=[
            pl.BlockSpec((None, tq, D_MEM), lambda bb, i, h: (bb, i, h)),
            pl.BlockSpec((None, None, nm, D_MEM), lambda bb, i, h: (layer, bb, 0, h)),
            pl.BlockSpec((None, None, nm, D_MEM), lambda bb, i, h: (layer, bb, 0, h)),
            pl.BlockSpec((None, tq, D_MEM), lambda bb, i, h: (bb, i, sg_col0 + h)),
        ],
        out_specs=pl.BlockSpec((None, tq, D_MEM), lambda bb, i, h: (bb, i, h)),
        out_shape=jax.ShapeDtypeStruct((b, t, w), MXU_DTYPE),
        compiler_params=_cparams(("arbitrary", "arbitrary", "arbitrary")),
        name="mem_attn",
    )(q, k, v, sg)


def _diff_sample_kernel(pt_ref, lam_ref, qm_ref, *refs, nh, nt, pps, nsteps, post_scale):
    kt_refs = refs[:pps]
    v_refs = refs[pps:2 * pps]
    knewt_ref, vnew_ref, g_ref, sg_ref, o_ref, m_scr, l_scr, acc_scr = refs[2 * pps:]
    s_id = pl.program_id(1)
    nr = 2 * nt

    @pl.when(s_id == 0)
    def _():
        m_scr[...] = jnp.full_like(m_scr, NEG_BIG)
        l_scr[...] = jnp.zeros_like(l_scr)
        acc_scr[...] = jnp.zeros_like(acc_scr)

    def update(kts, vs, masked):
        s = jnp.concatenate(
            [jnp.concatenate([_dot(qm_ref[h * nr:(h + 1) * nr, :], kt[h].astype(MXU_DTYPE)) for kt in kts], axis=1)
             for h in range(nh)], axis=0)
        if masked:
            tpos = _iota(s.shape, 0) % nt
            s = jnp.where(_iota(s.shape, 1) <= tpos, s, NEG_BIG)
        m = m_scr[...]
        m_new = jnp.maximum(m, jnp.max(s, axis=1, keepdims=True))
        alpha = jnp.exp2(m - m_new)
        p = jnp.exp2(s - m_new)
        l_scr[...] = alpha * l_scr[...] + jnp.sum(p, axis=1, keepdims=True)
        pb = p.astype(MXU_DTYPE)
        kw = s.shape[1] // len(kts)
        pv = []
        for h in range(nh):
            acc = None
            for r, v in enumerate(vs):
                d = _dot(pb[h * nr:(h + 1) * nr, r * kw:(r + 1) * kw], v[h].astype(MXU_DTYPE))
                acc = d if acc is None else acc + d
            pv.append(acc)
        acc_scr[...] = alpha * acc_scr[...] + jnp.concatenate(pv, axis=0)
        m_scr[...] = m_new

    @pl.when(s_id < nsteps)
    def _():
        update(kt_refs, v_refs, False)

    @pl.when(s_id == nsteps)
    def _():
        update([knewt_ref], [vnew_ref], True)
        o = acc_scr[...] / l_scr[...]
        for h in range(nh):
            od = o[h * nr:h * nr + nt] - lam_ref[0] * o[h * nr + nt:(h + 1) * nr]
            sl = slice(h * LANES, (h + 1) * LANES)
            o_ref[:, sl] = _sub_rmsnorm_gate(od, g_ref[...], post_scale, sg_ref[:, sl]).astype(o_ref.dtype)


def _diff_sample(page_table, lam, qm, cache_kt, cache_v, knewt, vnew, g_sub, sg, *, layer, nt, post_scale):
    db, rows, _ = qm.shape
    nh = cache_v.shape[2]
    page = cache_v.shape[3]
    npages = page_table.shape[1]
    pps = math.gcd(PAGES_PER_STEP, npages)
    nsteps = npages // pps
    w = nh * LANES

    def page_map(r):
        def f(bb, s, pt):
            return (layer, pt[bb, jnp.minimum(s * pps + r, npages - 1)], 0, 0, 0)
        return f

    kt_specs = [pl.BlockSpec((None, None, nh, LANES, page), page_map(r)) for r in range(pps)]
    v_specs = [pl.BlockSpec((None, None, nh, page, LANES), page_map(r)) for r in range(pps)]
    grid_spec = pltpu.PrefetchScalarGridSpec(
        num_scalar_prefetch=1,
        grid=(db, nsteps + 1),
        in_specs=[
            pl.BlockSpec(memory_space=pltpu.SMEM),
            pl.BlockSpec((None, rows, LANES), lambda bb, s, pt: (bb, 0, 0)),
            *kt_specs, *v_specs,
            pl.BlockSpec((None, nh, LANES, page), lambda bb, s, pt: (bb, 0, 0, 0)),
            pl.BlockSpec((None, nh, page, LANES), lambda bb, s, pt: (bb, 0, 0, 0)),
            pl.BlockSpec((1, LANES), lambda bb, s, pt: (0, 0)),
            pl.BlockSpec((None, nt, w), lambda bb, s, pt: (bb, 0, 0)),
        ],
        out_specs=pl.BlockSpec((None, nt, w), lambda bb, s, pt: (bb, 0, 0)),
        scratch_shapes=[
            pltpu.VMEM((rows, 1), f32),
            pltpu.VMEM((rows, 1), f32),
            pltpu.VMEM((rows, LANES), f32),
        ],
    )
    return pl.pallas_call(
        functools.partial(_diff_sample_kernel, nh=nh, nt=nt, pps=pps, nsteps=nsteps, post_scale=post_scale),
        grid_spec=grid_spec,
        out_shape=jax.ShapeDtypeStruct((db, nt, w), MXU_DTYPE),
        compiler_params=_cparams(("arbitrary", "arbitrary")),
        name="diff_sample",
    )(page_table, lam, qm, *([cache_kt] * pps), *([cache_v] * pps), knewt, vnew, g_sub, sg)


def _dsa_sample_kernel(pt_ref, qs_ref, qi_ref, w_ref, *refs, nt, pps, nsteps, topk, page):
    kit_refs = refs[:pps]
    ks_refs = refs[pps:2 * pps]
    vs_refs = refs[2 * pps:3 * pps]
    kint_ref, ksn_ref, vsn_ref, sg_ref, o_ref, key_scr, s_scr, v_scr = refs[3 * pps:]
    s_id = pl.program_id(1)
    gsz = LANES // nt

    def process(blk, kit, ks, vs, causal_new):
        sc = _dot(qi_ref[...], kit.astype(MXU_DTYPE))
        val = jnp.maximum(sc, 0.0) * w_ref[...]
        acc = jnp.sum(val.reshape(gsz, nt, page), axis=0)
        acc = jnp.where(acc == 0.0, 0.0, acc)
        if causal_new:
            acc = jnp.where(_iota(acc.shape, 1) <= _iota(acc.shape, 0), acc, -jnp.inf)
        key_scr[blk] = _f2key(acc)
        s_scr[blk] = _dot_nt(qs_ref[...], ks.astype(MXU_DTYPE))
        v_scr[blk] = vs.astype(MXU_DTYPE)

    @pl.when(s_id < nsteps)
    def _():
        for r in range(pps):
            process(s_id * pps + r, kit_refs[r][...], ks_refs[r][...], vs_refs[r][...], False)

    @pl.when(s_id == nsteps)
    def _():
        nblk = nsteps * pps + 1
        process(nblk - 1, kint_ref[...], ksn_ref[...], vsn_ref[...], True)

        keys = jnp.concatenate([key_scr[c] for c in range(nblk)], axis=1)

        def count_ge_all(cand):
            return jnp.sum(jnp.where(keys >= cand, 1.0, 0.0), axis=1, keepdims=True)

        def bit_body(it, prefix):
            cand = prefix + lax.shift_left(jnp.int32(1), 31 - it)
            return jnp.where(count_ge_all(cand) >= topk, cand, prefix)

        kth = lax.fori_loop(0, 32, bit_body, jnp.full((nt, 1), INT_MIN, i32))
        _demote_surplus_ties_lane(key_scr, nblk, nt, topk, kth, count_ge_all)
        thr = jnp.maximum(kth, KEY_NEG_INF + 1)

        unroll = max(u for u in (5, 4, 3, 2, 1) if nblk % u == 0)

        def sel_scores(c):
            msk = key_scr[c] >= thr
            return msk[None], s_scr[c].reshape(gsz, nt, page)

        def max_body(it, mx):
            for u in range(unroll):
                msk, s3 = sel_scores(it * unroll + u)
                mx = jnp.maximum(mx, jnp.where(msk, s3, NEG_BIG))
            return mx

        mx = lax.fori_loop(0, nblk // unroll, max_body, jnp.full((gsz, nt, page), NEG_BIG, f32))
        m = jnp.max(mx, axis=2, keepdims=True)

        def att_body(it, carry):
            lacc, acc = carry
            for u in range(unroll):
                c = it * unroll + u
                msk, s3 = sel_scores(c)
                p = jnp.where(msk, jnp.exp2(s3 - m), 0.0)
                lacc = lacc + p
                acc = acc + _dot(p.reshape(gsz * nt, page).astype(MXU_DTYPE), v_scr[c])
            return lacc, acc

        lacc, acc = lax.fori_loop(0, nblk // unroll, att_body,
                                  (jnp.zeros((gsz, nt, page), f32), jnp.zeros((LANES, D_DSA), f32)))
        l = jnp.sum(lacc, axis=2, keepdims=True).reshape(gsz * nt, 1)
        o_ref[...] = (acc / l * sg_ref[...].astype(f32)).astype(o_ref.dtype)


def _dsa_sample(page_table, qs_rows, qi_rows, wcol, cache_kit, cache_ks, cache_vs, kit_new, ks_new, vs_new, sg_perm,
                *, layer, nt, topk):
    db = qs_rows.shape[0]
    page = cache_ks.shape[2]
    npages = page_table.shape[1]
    pps = math.gcd(PAGES_PER_STEP, npages)
    nsteps = npages // pps
    nblk = npages + 1
    assert page == LANES

    def page_map(r):
        def f(bb, s, pt):
            return (layer, pt[bb, jnp.minimum(s * pps + r, npages - 1)], 0, 0)
        return f

    def pspecs(shape):
        return [pl.BlockSpec((None, None) + shape, page_map(r)) for r in range(pps)]

    per_b = lambda shape: pl.BlockSpec((None,) + shape, lambda bb, s, pt: (bb, 0, 0))
    grid_spec = pltpu.PrefetchScalarGridSpec(
        num_scalar_prefetch=1,
        grid=(db, nsteps + 1),
        in_specs=[
            per_b((LANES, D_DSA)), per_b((LANES, D_IDX)), per_b((LANES, 1)),
            *pspecs((D_IDX, page)), *pspecs((page, D_DSA)), *pspecs((page, D_DSA)),
            per_b((D_IDX, page)), per_b((page, D_DSA)), per_b((page, D_DSA)),
            per_b((LANES, D_DSA)),
        ],
        out_specs=per_b((LANES, D_DSA)),
        scratch_shapes=[
            pltpu.VMEM((nblk, nt, page), i32),
            pltpu.VMEM((nblk, LANES, page), f32),
            pltpu.VMEM((nblk, page, D_DSA), MXU_DTYPE),
        ],
    )
    return pl.pallas_call(
        functools.partial(_dsa_sample_kernel, nt=nt, pps=pps, nsteps=nsteps, topk=topk, page=page),
        grid_spec=grid_spec,
        out_shape=jax.ShapeDtypeStruct((db, LANES, D_DSA), MXU_DTYPE),
        compiler_params=_cparams(("arbitrary", "arbitrary")),
        name="dsa_sample",
    )(page_table, qs_rows, qi_rows, wcol, *([cache_kit] * pps), *([cache_ks] * pps), *([cache_vs] * pps),
      kit_new, ks_new, vs_new, sg_perm)


def _rope_tables(pos, head_dim):
    half = head_dim // 2
    lane = jnp.arange(LANES)
    inv = ROPE_THETA ** (-(lane % half).astype(f32) / half)
    ang = pos.astype(f32)[:, None] * inv[None, :]
    sign = jnp.where((lane % head_dim) < half, -1.0, 1.0).astype(f32)
    return jnp.cos(ang), jnp.sin(ang) * sign[None, :]


def _tile_gain(g, n):
    return jnp.tile(g, n // g.shape[0]).reshape(1, n).astype(f32)


def _mixer_inputs(x2d, pos_tab, wl, gl, *, tm, stacked=None):
    cos64, sin64, cos128, sin128 = pos_tab
    m = x2d.shape[0]
    h = _rmsnorm(x2d, gl["g_in"], min(tm, 256))
    tn = 512
    proj = functools.partial(_proj, h, tm=tm, tn=tn, nt=True)
    rows = lambda n, dt: _rows_out(m, n, dt, tm, tn)
    rope64 = [("row", cos64), ("row", sin64)]
    rope128 = [("row", cos128), ("row", sin128)]
    nqd = wl["qd"].shape[0]
    nqs = wl["qs"].shape[0]
    nqm = wl["qm"].shape[0]
    norm_rope64 = functools.partial(_chunks_norm_rope, gsize=D_DH, half=D_DH // 2)
    norm_rope128 = functools.partial(_chunks_norm_rope, gsize=D_DSA, half=D_DSA // 2)
    gq64 = [("col", _tile_gain(gl["g_q_diff"], nqd))] + rope64
    gk64 = [("col", _tile_gain(gl["g_k_diff"], nqd))] + rope64
    gq128 = [("col", _tile_gain(gl["g_q_dsa"], nqs))] + rope128
    gk128 = [("col", gl["g_k_dsa"].reshape(1, D_DSA))] + rope128
    g_kiwi = [("col", jnp.concatenate([gl["g_k_idx"], jnp.ones((LANES - D_IDX,), f32)]).reshape(1, LANES))] + rope64
    out = {}
    (out["qd"],) = proj(wl["qd"], functools.partial(norm_rope64, scale=D_DH ** -0.5 * LOG2E), _store_rows,
                        [rows(nqd, MXU_DTYPE)], gq64, name="proj_qd")
    (out["qs"],) = proj(wl["qs"], functools.partial(norm_rope128, scale=D_DSA ** -0.5 * LOG2E), _store_rows,
                        [rows(nqs, MXU_DTYPE)], gq128, name="proj_qs")
    (out["qi"],) = proj(wl["qi"], functools.partial(_chunks_rope, half=D_IDX // 2, scale=D_IDX ** -0.5 * H_IDX ** -0.5),
                        _store_rows, [rows(H_IDX * D_IDX, MXU_DTYPE)], rope64, name="proj_qi")
    (out["qm"],) = proj(wl["qm"], functools.partial(_chunks_norm256, scale=D_MEM ** -0.5 * LOG2E), _store_rows,
                        [rows(nqm, MXU_DTYPE)], [("col", _tile_gain(gl["g_q_mem"], nqm))], name="proj_qm")
    (out["sg"],) = proj(wl["gate"], _chunks_silu, _store_rows, [rows(wl["gate"].shape[0], MXU_DTYPE)], name="proj_gate")
    if stacked is None:
        out["kd"], out["kd_c"] = proj(wl["kd"], functools.partial(norm_rope64, scale=1.0), _store_rows,
                                      [rows(nqd, f32), rows(nqd, MXU_DTYPE)], gk64, name="proj_kd")
        out["vd"], out["vd_c"] = proj(wl["vd"], _chunks_raw, _store_rows,
                                      [rows(nqd, f32), rows(nqd, MXU_DTYPE)], name="proj_vd")
        out["ks"], out["ks_c"] = proj(wl["ks"], functools.partial(norm_rope128, scale=1.0), _store_rows,
                                      [rows(D_DSA, f32), rows(D_DSA, MXU_DTYPE)], gk128, name="proj_ks")
        out["vs"], out["vs_c"] = proj(wl["vs"], _chunks_raw, _store_rows,
                                      [rows(D_DSA, f32), rows(D_DSA, MXU_DTYPE)], name="proj_vs")
        out["kiwi"], out["ki2"] = proj(wl["kiwi"], _chunks_kiwi, _store_kiwi_rows,
                                       [rows(LANES, f32), rows(LANES, MXU_DTYPE)], g_kiwi, name="proj_kiwi")
        return out

    l = stacked["layer"]
    b, t = stacked["b"], stacked["t"]
    nrb = t // tm
    nh = nqd // LANES
    hb = _col_tile(nqd, tn) // LANES
    vkb = stacked["diff_tk"]
    assert tm % vkb == 0
    depth = stacked["pdk"].shape[0]
    out["pdk"], out["kd_c"] = proj(
        wl["kd"], functools.partial(norm_rope64, scale=1.0), _store_kd_prompt,
        [dict(shape=(depth, b, nh, LANES, t), dtype=f32, block=(None, None, hb, LANES, tm),
              index=lambda i, j: (l, i // nrb, j, 0, i % nrb), alias=stacked["pdk"]),
         rows(nqd, MXU_DTYPE)], gk64, name="proj_kd")
    out["pdv"], out["vd_c"] = proj(
        wl["vd"], _chunks_raw, _store_vd_prompt,
        [dict(shape=(depth, b, nh, t, LANES), dtype=f32, block=(None, None, hb, tm, LANES),
              index=lambda i, j: (l, i // nrb, j, i % nrb, 0), alias=stacked["pdv"]),
         dict(shape=(b, nh, t // vkb, LANES, vkb), dtype=MXU_DTYPE, block=(None, hb, tm // vkb, LANES, vkb),
              index=lambda i, j: (i // nrb, j, i % nrb, 0, 0), alias=None)], name="proj_vd")
    tok_major = lambda key: dict(shape=(depth, b, t, D_DSA), dtype=f32, block=(None, None, tm, D_DSA),
                                 index=lambda i, j: (l, i // nrb, i % nrb, 0), alias=stacked[key])
    out["psk"], out["ks_c"] = proj(wl["ks"], functools.partial(norm_rope128, scale=1.0), _store_rows,
                                   [tok_major("psk"), rows(D_DSA, MXU_DTYPE)], gk128, name="proj_ks")
    out["psv"], out["vs_c"] = proj(wl["vs"], _chunks_raw, _store_rows,
                                   [tok_major("psv"), rows(D_DSA, MXU_DTYPE)], name="proj_vs")
    out["kiwi"], out["ki2"], out["pik"] = proj(
        wl["kiwi"], _chunks_kiwi, _store_kiwi_prompt,
        [rows(LANES, f32), rows(LANES, MXU_DTYPE),
         dict(shape=(depth, b, D_IDX, t), dtype=f32, block=(None, None, D_IDX, tm),
              index=lambda i, j: (l, i // nrb, 0, i % nrb), alias=stacked["pik"])], g_kiwi, name="proj_kiwi")
    return out


def kernel(x_prompt, x_sample, mem_prompt, cache_diff_k, cache_diff_v, cache_dsa_k, cache_dsa_v, cache_idx_k,
           cache_mem_k, cache_mem_v, page_table, w_in, w_out, w_mem_kv, g_in, g_mem, g_q_diff, g_k_diff,
           g_sub_diff, lam_q1, lam_k1, lam_q2, lam_k2, g_q_dsa, g_k_dsa, g_k_idx, g_q_mem, g_k_mem):
    depth = w_in.shape[0]
    b, t, d = x_prompt.shape
    db, nt, _ = x_sample.shape
    n_mem = mem_prompt.shape[1]
    n_phys, page = cache_dsa_k.shape[1], cache_dsa_k.shape[2]
    npages = page_table.shape[1]
    past = npages * page
    h_diff = cache_diff_k.shape[3]
    w_diff = h_diff * 2 * D_DH
    h_dsa = (3 * d // 8) // D_DSA
    w_dsa = h_dsa * D_DSA
    h_mem = cache_mem_k.shape[3]
    w_mem = h_mem * D_MEM
    gsz = LANES // nt
    assert w_diff + w_dsa + w_mem == d and w_diff == w_dsa and LANES % nt == 0 and h_dsa <= gsz and gsz == H_IDX
    topk_p = min(TOPK_MAX, t // 4)
    topk_s = min(TOPK_MAX, (past + nt) // 4)
    m_p = b * t
    m_s = db * nt
    tm_p = min(1024, t)
    assert t % tm_p == 0

    widths = (w_diff, w_diff, w_diff, w_dsa, D_DSA, D_DSA, H_IDX * D_IDX, D_IDX, H_IDX, w_mem, d)
    offs = [0]
    for wdt in widths:
        offs.append(offs[-1] + wdt)
    names = ("qd", "kd", "vd", "qs", "ks", "vs", "qi", "ki", "wi", "qm", "gate")
    seg = {n: (offs[k], offs[k + 1]) for k, n in enumerate(names)}

    pos_p = jnp.arange(t, dtype=i32)
    pos_s = jnp.tile(past + jnp.arange(nt, dtype=i32), db)
    tab_p = _rope_tables(pos_p, D_DH) + _rope_tables(pos_p, D_DSA)
    tab_s = _rope_tables(pos_s, D_DH) + _rope_tables(pos_s, D_DSA)

    cdkt = cache_diff_k.transpose(0, 1, 3, 4, 5, 2).reshape(depth, n_phys, h_diff, 2 * D_DH, page)
    cdv = cache_diff_v.transpose(0, 1, 3, 2, 4)
    ckit = cache_idx_k.transpose(0, 1, 3, 2)
    cmk = cache_mem_k.reshape(depth, db, n_mem, w_mem)
    cmv = cache_mem_v.reshape(depth, db, n_mem, w_mem)
    w_in_t = jnp.swapaxes(w_in, 1, 2)

    x_p = x_prompt.reshape(m_p, d)
    x_s = x_sample.reshape(m_s, d)
    mem2d = mem_prompt.reshape(b * n_mem, d)
    stk = dict(pdk=jnp.zeros((depth, b, h_diff, 2 * D_DH, t), f32), pdv=jnp.zeros((depth, b, h_diff, t, 2 * D_DH), f32),
               psk=jnp.zeros((depth, b, t, D_DSA), f32), psv=jnp.zeros((depth, b, t, D_DSA), f32),
               pik=jnp.zeros((depth, b, D_IDX, t), f32))
    outs = {k: [] for k in ("pmk", "pmv", "sdk", "sdv", "ssk", "ssv", "sik")}

    for l in range(depth):
        wt = w_in_t[l]
        wl = {n: wt[seg[n][0]:seg[n][1]].astype(MXU_DTYPE)
              for n in ("qd", "kd", "vd", "qs", "ks", "vs", "qi", "qm", "gate")}
        wl["kiwi"] = jnp.pad(wt[seg["ki"][0]:seg["wi"][1]], ((0, LANES - D_IDX - H_IDX), (0, 0))).astype(MXU_DTYPE)
        gl = dict(g_in=g_in[l], g_q_diff=g_q_diff[l], g_k_diff=g_k_diff[l], g_q_dsa=g_q_dsa[l],
                  g_k_dsa=g_k_dsa[l], g_k_idx=g_k_idx[l], g_q_mem=g_q_mem[l])
        w_out_l = w_out[l].astype(MXU_DTYPE)
        w_mkv = w_mem_kv[l].astype(MXU_DTYPE)
        lam_init = 0.8 - 0.6 * math.exp(-0.3 * l)
        lam = (jnp.exp(jnp.sum(lam_q1[l] * lam_k1[l])) - jnp.exp(jnp.sum(lam_q2[l] * lam_k2[l])) + lam_init)
        lam = lam.astype(f32).reshape(1)
        g_sub = g_sub_diff[l].reshape(1, 2 * D_DH)
        post = 1.0 - lam_init

        diff_tq, diff_tk = min(DIFF_TQ, t), min(DIFF_TK, t)
        tp = _mixer_inputs(x_p, tab_p, wl, gl, tm=tm_p, stacked=dict(stk, layer=l, b=b, t=t, diff_tk=diff_tk))
        for key in ("pdk", "pdv", "psk", "psv", "pik"):
            stk[key] = tp[key]
        hm = _rmsnorm(mem2d, g_mem[l], 256)
        mrows = lambda dt: [_rows_out(b * n_mem, w_mem, dt, b * n_mem, 512)]
        mk, = _proj(hm, w_mkv[:, :w_mem], functools.partial(_chunks_norm256, scale=1.0), _store_rows, mrows(f32),
                    [("col", _tile_gain(g_k_mem[l], w_mem))], tm=b * n_mem, tn=512, nt=False, name="proj_mk")
        mv, = _proj(hm, w_mkv[:, w_mem:], _chunks_raw, _store_rows, mrows(f32),
                    tm=b * n_mem, tn=512, nt=False, name="proj_mv")
        r3 = lambda a: a.reshape(b, t, a.shape[-1])
        sg = r3(tp["sg"])
        od = _diff_prompt(lam, r3(tp["qd"]), r3(tp["kd_c"]), tp["vd_c"], g_sub, sg, post_scale=post,
                          tq=diff_tq, tk=diff_tk)
        tc = 256
        vst = tp["vs_c"].reshape(b, t // tc, tc, D_DSA).transpose(0, 1, 3, 2)
        os_ = _dsa_prompt(r3(tp["qs"]), r3(tp["qi"]), r3(tp["kiwi"]), r3(tp["ki2"]), r3(tp["ks_c"]), vst, sg,
                          topk=topk_p, tc=tc)
        om = _mem_attn(r3(tp["qm"]), mk.reshape(1, b, n_mem, w_mem), mv.reshape(1, b, n_mem, w_mem), sg,
                       layer=0, sg_col0=(w_diff + w_dsa) // D_MEM, tq=min(512, t))
        x_p = _outproj(od.reshape(m_p, w_diff), os_.reshape(m_p, w_dsa), om.reshape(m_p, w_mem), w_out_l, x_p,
                       tm=tm_p, tn=512)
        outs["pmk"].append(mk.reshape(b, n_mem, h_mem, D_MEM))
        outs["pmv"].append(mv.reshape(b, n_mem, h_mem, D_MEM))

        ts = _mixer_inputs(x_s, tab_s, wl, gl, tm=m_s)
        s3 = lambda a: a.reshape(db, nt, a.shape[-1])
        sg_s = s3(ts["sg"])
        tokpad = lambda a, axis: jnp.pad(a, [(0, page - nt) if ax == axis else (0, 0) for ax in range(a.ndim)])
        q5 = ts["qd"].reshape(db, nt, h_diff, 2, D_DH).transpose(0, 2, 3, 1, 4)
        eye_c = jnp.eye(2, dtype=MXU_DTYPE)
        qm = (q5[:, :, :, :, None, :] * eye_c[None, None, :, None, :, None]).reshape(db, h_diff * 2 * nt, 2 * D_DH)
        kd4 = ts["kd"].reshape(db, nt, h_diff, 2 * D_DH)
        vd4 = ts["vd"].reshape(db, nt, h_diff, 2 * D_DH)
        od_s = _diff_sample(page_table, lam, qm, cdkt, cdv, tokpad(kd4.transpose(0, 2, 3, 1), 3),
                            tokpad(vd4.transpose(0, 2, 1, 3), 2), g_sub, sg_s, layer=l, nt=nt, post_scale=post)
        qs4 = ts["qs"].reshape(db, nt, h_dsa, D_DSA)
        slot_major = lambda a: a.transpose(0, 2, 1, 3).reshape(db, LANES, a.shape[-1])
        qs_rows = slot_major(jnp.pad(qs4, ((0, 0), (0, 0), (0, gsz - h_dsa), (0, 0))))
        qi_rows = slot_major(ts["qi"].reshape(db, nt, H_IDX, D_IDX))
        wcol = slot_major(ts["kiwi"][:, D_IDX:D_IDX + H_IDX].reshape(db, nt, H_IDX, 1))
        sg_dsa = sg_s[:, :, w_diff:w_diff + w_dsa].reshape(db, nt, h_dsa, D_DSA)
        sg_perm = slot_major(jnp.pad(sg_dsa, ((0, 0), (0, 0), (0, gsz - h_dsa), (0, 0))))
        ki_new = s3(ts["kiwi"][:, :D_IDX])
        os_s = _dsa_sample(page_table, qs_rows, qi_rows, wcol, ckit, cache_dsa_k, cache_dsa_v,
                           tokpad(ki_new.transpose(0, 2, 1), 2), tokpad(s3(ts["ks"]), 1), tokpad(s3(ts["vs"]), 1),
                           sg_perm, layer=l, nt=nt, topk=topk_s)
        os_s = os_s.reshape(db, gsz, nt, D_DSA)[:, :h_dsa].transpose(0, 2, 1, 3).reshape(m_s, w_dsa)
        om_s = _mem_attn(s3(ts["qm"]), cmk, cmv, sg_s, layer=l, sg_col0=(w_diff + w_dsa) // D_MEM, tq=nt)
        x_s = _outproj(od_s.reshape(m_s, w_diff), os_s, om_s.reshape(m_s, w_mem), w_out_l, x_s, tm=m_s, tn=512)
        outs["sdk"].append(ts["kd"].reshape(db, nt, h_diff, 2, D_DH))
        outs["sdv"].append(ts["vd"].reshape(db, nt, h_diff, 2 * D_DH))
        outs["ssk"].append(ts["ks"].reshape(db, nt, D_DSA))
        outs["ssv"].append(ts["vs"].reshape(db, nt, D_DSA))
        outs["sik"].append(ki_new)

    st = lambda k: jnp.stack(outs[k])
    p_diff_k = stk["pdk"].reshape(depth, b, h_diff, 2, D_DH, t).transpose(0, 1, 5, 2, 3, 4)
    p_diff_v = stk["pdv"].transpose(0, 1, 3, 2, 4)
    p_idx_k = stk["pik"].transpose(0, 1, 3, 2)
    return (x_p.reshape(b, t, d), x_s.reshape(db, nt, d),
            p_diff_k, p_diff_v, stk["psk"], stk["psv"], p_idx_k, st("pmk"), st("pmv"),
            st("sdk"), st("sdv"), st("ssk"), st("ssv"), st("sik"))
```

```python
import functools
import math

import jax
import jax.numpy as jnp
from jax import lax
from jax.experimental import pallas as pl
from jax.experimental.pallas import tpu as pltpu

EPS = 1e-6
ROPE_THETA = 10000.0
TOPK_MAX = 256
LANES = 128
D_DH = 64
D_DSA = 128
D_IDX = 64
H_IDX = 16
D_MEM = 256
NEG_BIG = -1e30
KEY_NEG_INF = -2139095041
INT_MIN = -2147483648
MXU_DTYPE = jnp.bfloat16
VMEM_LIMIT_BYTES = 52 * 1024 * 1024
PAGES_PER_STEP = 8
DIFF_TQ = 512
DIFF_TK = 512
LOG2E = math.log2(math.e)

f32 = jnp.float32
i32 = jnp.int32


def _cparams(sem):
    return pltpu.CompilerParams(dimension_semantics=sem, vmem_limit_bytes=VMEM_LIMIT_BYTES)


def _dot(a, b):
    return jnp.dot(a, b, preferred_element_type=f32)


def _dot_nt(a, b):
    return lax.dot_general(a, b, (((1,), (1,)), ((), ())), preferred_element_type=f32)


def _iota(shape, dim):
    return lax.broadcasted_iota(i32, shape, dim)


def _group_mat(gsize):
    r = _iota((LANES, LANES), 0) // gsize
    c = _iota((LANES, LANES), 1) // gsize
    return (r == c).astype(f32).astype(MXU_DTYPE)


def _group_sum(x, gmat):
    hi = x.astype(MXU_DTYPE)
    lo = (x - hi.astype(f32)).astype(MXU_DTYPE)
    return _dot(hi, gmat) + _dot(lo, gmat)


def _rope_chunk(n, cos, sin_signed, half):
    if 2 * half == LANES:
        rot = pltpu.roll(n, half, 1)
    else:
        first = (_iota(n.shape, 1) % (2 * half)) < half
        rot = jnp.where(first, pltpu.roll(n, LANES - half, 1), pltpu.roll(n, half, 1))
    return n * cos + rot * sin_signed


def _f2key(x):
    b = pltpu.bitcast(x, i32)
    return b ^ (lax.shift_right_arithmetic(b, 31) & 0x7FFFFFFF)


def _rmsnorm_kernel(x_ref, g_ref, o_ref):
    x = x_ref[...]
    ms = jnp.mean(x * x, axis=-1, keepdims=True)
    o_ref[...] = (x * lax.rsqrt(ms + EPS) * g_ref[...]).astype(o_ref.dtype)


def _rmsnorm(x, g, tm):
    m, d = x.shape
    return pl.pallas_call(
        _rmsnorm_kernel,
        grid=(m // tm,),
        in_specs=[pl.BlockSpec((tm, d), lambda i: (i, 0)), pl.BlockSpec((1, d), lambda i: (0, 0))],
        out_specs=pl.BlockSpec((tm, d), lambda i: (i, 0)),
        out_shape=jax.ShapeDtypeStruct((m, d), MXU_DTYPE),
        compiler_params=_cparams(("arbitrary",)),
        name="rmsnorm",
    )(x, g.reshape(1, d))


def _chunks_raw(z, aux, store):
    for c in range(z.shape[1] // LANES):
        store(c, z[:, c * LANES:(c + 1) * LANES])


def _chunks_silu(z, aux, store):
    for c in range(z.shape[1] // LANES):
        zc = z[:, c * LANES:(c + 1) * LANES]
        store(c, zc / (1.0 + jnp.exp(-zc)))


def _chunks_norm_rope(z, aux, store, *, gsize, half, scale):
    gain, cos, sin = aux
    gmat = _group_mat(gsize)
    for c in range(z.shape[1] // LANES):
        sl = slice(c * LANES, (c + 1) * LANES)
        zc = z[:, sl]
        ss = _group_sum(zc * zc, gmat)
        n = zc * lax.rsqrt(ss * (1.0 / gsize) + EPS) * gain[:, sl]
        r = _rope_chunk(n, cos, sin, half)
        store(c, r * scale if scale != 1.0 else r)


def _chunks_rope(z, aux, store, *, half, scale):
    cos, sin = aux
    for c in range(z.shape[1] // LANES):
        store(c, _rope_chunk(z[:, c * LANES:(c + 1) * LANES], cos, sin, half) * scale)


def _chunks_norm256(z, aux, store, *, scale):
    (gain,) = aux
    gmat = _group_mat(LANES)
    for c in range(z.shape[1] // D_MEM):
        a = z[:, c * D_MEM:c * D_MEM + LANES]
        b = z[:, c * D_MEM + LANES:(c + 1) * D_MEM]
        ss = _group_sum(a * a, gmat) + _group_sum(b * b, gmat)
        inv = lax.rsqrt(ss * (1.0 / D_MEM) + EPS)
        for k, v in enumerate((a, b)):
            sl = slice(c * D_MEM + k * LANES, c * D_MEM + (k + 1) * LANES)
            r = v * inv * gain[:, sl]
            store(2 * c + k, r * scale if scale != 1.0 else r)


def _chunks_kiwi(z, aux, store):
    gain, cos, sin = aux
    gmat = _group_mat(D_IDX)
    ss = _group_sum(z * z, gmat)
    n = z * lax.rsqrt(ss * (1.0 / D_IDX) + EPS) * gain
    r = _rope_chunk(n, cos, sin, D_IDX // 2)
    lo = _iota(z.shape, 1) < D_IDX
    store(0, jnp.where(lo, r, z))
    r_lo = jnp.where(lo, r, 0.0)
    store(1, r_lo + pltpu.roll(r_lo, D_IDX, 1))


def _store_rows(outs, c, r):
    for o in outs:
        o[:, c * LANES:(c + 1) * LANES] = r.astype(o.dtype)


def _store_kd_prompt(outs, c, r):
    outs[0][c] = r.T
    outs[1][:, c * LANES:(c + 1) * LANES] = r.astype(outs[1].dtype)


def _store_vd_prompt(outs, c, r):
    outs[0][c] = r
    rt = r.T.astype(outs[1].dtype)
    kb = outs[1].shape[-1]
    for kk in range(r.shape[0] // kb):
        outs[1][c, kk] = rt[:, kk * kb:(kk + 1) * kb]


def _chunks_ks_vs_kiwi(z, aux, store):
    gain_k, cos128, sin128, gain_kiwi, cos64, sin64 = aux
    j = pl.program_id(1)

    @pl.when(j == 0)
    def _():
        _chunks_norm_rope(z, (gain_k, cos128, sin128), functools.partial(store, 0), gsize=D_DSA, half=D_DSA // 2,
                          scale=1.0)

    @pl.when(j == 1)
    def _():
        store(1, 0, z)

    @pl.when(j == 2)
    def _():
        _chunks_kiwi(z, (gain_kiwi, cos64, sin64), functools.partial(store, 2))


def _store_ks_vs_kiwi(outs, which, c, r):
    if which < 2:
        outs[2 * which][...] = r
        outs[2 * which + 1][...] = r.astype(outs[2 * which + 1].dtype)
    elif c == 0:
        outs[4][...] = r
        if len(outs) > 6:
            outs[6][...] = r.T[:D_IDX, :]
    else:
        outs[5][...] = r.astype(outs[5].dtype)


def _proj_kernel(*refs, chunk_fn, store_fn, n_aux, n_alias, nt):
    h_ref, w_ref = refs[:2]
    aux = [r[...] for r in refs[2:2 + n_aux]]
    outs = refs[2 + n_aux + n_alias:]
    w = w_ref[...].astype(MXU_DTYPE)
    z = _dot_nt(h_ref[...], w) if nt else _dot(h_ref[...], w)
    chunk_fn(z, aux, functools.partial(store_fn, outs))


def _win_desc(w_in_t, layer, off, n, tn):
    tn = _col_tile(n, tn)
    k = w_in_t.shape[2]
    if off % tn == 0:
        return dict(array=w_in_t, spec=pl.BlockSpec((None, tn, k), lambda i, j: (layer, off // tn + j, 0)), n=n, tn=tn)
    row0 = layer * w_in_t.shape[1] + off
    assert row0 % 8 == 0
    spec = pl.BlockSpec((pl.Element(tn), pl.Element(k)), lambda i, j: (pl.multiple_of(row0 + j * tn, 8), 0))
    return dict(array=w_in_t.reshape(-1, k), spec=spec, n=n, tn=tn)


def _wcols_desc(w, layer, off, n, tn):
    tn = _col_tile(n, tn)
    assert off % tn == 0
    return dict(array=w, spec=pl.BlockSpec((None, w.shape[1], tn), lambda i, j: (layer, 0, off // tn + j)), n=n, tn=tn)


def _col_tile(n, tn):
    return math.gcd(n, tn)


def _rows_out(m, n, dtype, tm, tn):
    tn = _col_tile(n, tn)
    return dict(shape=(m, n), dtype=dtype, block=(tm, tn), index=lambda i, j: (i, j), alias=None)


def _proj(h, wd, chunk_fn, store_fn, outs, aux=(), *, tm, nt, name):
    m, k = h.shape
    n, tn = wd["n"], wd["tn"]
    assert m % tm == 0 and tn % LANES == 0 and n % tn == 0
    in_specs = [pl.BlockSpec((tm, k), lambda i, j: (i, 0)), wd["spec"]]
    args = [h, wd["array"]]
    for kind, a in aux:
        if kind == "col":
            in_specs.append(pl.BlockSpec((1, tn), lambda i, j: (0, j)))
        elif kind == "const":
            in_specs.append(pl.BlockSpec((1, LANES), lambda i, j: (0, 0)))
        else:
            nrb = a.shape[0] // tm
            in_specs.append(pl.BlockSpec((tm, LANES), lambda i, j, nrb=nrb: (i % nrb, 0)))
        args.append(a)
    aliases = {}
    for k_out, o in enumerate(outs):
        if o["alias"] is not None:
            aliases[len(args)] = k_out
            in_specs.append(pl.BlockSpec(memory_space=pl.ANY))
            args.append(o["alias"])
    return pl.pallas_call(
        functools.partial(_proj_kernel, chunk_fn=chunk_fn, store_fn=store_fn, n_aux=len(aux),
                          n_alias=len(aliases), nt=nt),
        grid=(m // tm, n // tn),
        in_specs=in_specs,
        out_specs=[pl.BlockSpec(o["block"], o["index"]) for o in outs],
        out_shape=[jax.ShapeDtypeStruct(o["shape"], o["dtype"]) for o in outs],
        input_output_aliases=aliases,
        compiler_params=_cparams(("arbitrary", "arbitrary")),
        name=name,
    )(*args)


def _outproj_kernel(od_ref, os_ref, om_ref, w_ref, x_ref, o_ref, *, wd, ws):
    acc = _dot(od_ref[...], w_ref[0:wd, :].astype(MXU_DTYPE))
    acc += _dot(os_ref[...], w_ref[wd:wd + ws, :].astype(MXU_DTYPE))
    acc += _dot(om_ref[...], w_ref[wd + ws:, :].astype(MXU_DTYPE))
    o_ref[...] = x_ref[...] + acc


def _outproj(od, os_, om, w, x, *, layer, tm, tn):
    m, d = x.shape
    wd, ws, wm = od.shape[1], os_.shape[1], om.shape[1]
    return pl.pallas_call(
        functools.partial(_outproj_kernel, wd=wd, ws=ws),
        grid=(m // tm, d // tn),
        in_specs=[
            pl.BlockSpec((tm, wd), lambda i, j: (i, 0)),
            pl.BlockSpec((tm, ws), lambda i, j: (i, 0)),
            pl.BlockSpec((tm, wm), lambda i, j: (i, 0)),
            pl.BlockSpec((None, wd + ws + wm, tn), lambda i, j: (layer, 0, j)),
            pl.BlockSpec((tm, tn), lambda i, j: (i, j)),
        ],
        out_specs=pl.BlockSpec((tm, tn), lambda i, j: (i, j)),
        out_shape=jax.ShapeDtypeStruct((m, d), f32),
        compiler_params=_cparams(("arbitrary", "arbitrary")),
        name="outproj",
    )(od, os_, om, w, x)


def _sub_rmsnorm_gate(od, g, post_scale, sg):
    ms = jnp.mean(od * od, axis=-1, keepdims=True)
    return od * lax.rsqrt(ms + EPS) * g * post_scale * sg.astype(f32)


def _diff_prompt_kernel(pi_ref, pj_ref, lam_ref, q_ref, k_ref, vt_ref, g_ref, sg_ref, o_ref,
                        m_scr, l_scr, acc_scr, *, tq, tk, hb, post_scale):
    p_id = pl.program_id(2)
    i = pi_ref[p_id]
    j = pj_ref[p_id]

    @pl.when(j == 0)
    def _():
        m_scr[...] = jnp.full_like(m_scr, NEG_BIG)
        l_scr[...] = jnp.zeros_like(l_scr)
        acc_scr[...] = jnp.zeros_like(acc_scr)

    def body(masked):
        for hh in range(hb):
            hsl = slice(hh * LANES, (hh + 1) * LANES)
            q = q_ref[:, hsl]
            kc = k_ref[:, hsl]
            lo = _iota(q.shape, 1) < D_DH
            zero = jnp.zeros_like(q)
            for c in range(2):
                qc = jnp.where(lo, q, zero) if c == 0 else jnp.where(lo, zero, q)
                s = _dot_nt(kc, qc)
                if masked:
                    s = jnp.where(j * tk + _iota(s.shape, 0) <= i * tq + _iota(s.shape, 1), s, NEG_BIG)
                r = 2 * hh + c
                m = m_scr[r]
                m_new = jnp.maximum(m, jnp.max(s, axis=0, keepdims=True))
                alpha = jnp.exp2(m - m_new)
                p = jnp.exp2(s - m_new)
                l_scr[r] = alpha * l_scr[r] + jnp.sum(p, axis=0, keepdims=True)
                acc_scr[r] = alpha * acc_scr[r] + _dot(vt_ref[hh], p.astype(MXU_DTYPE))
                m_scr[r] = m_new

    crosses_diagonal = (j + 1) * tk - 1 > i * tq
    pl.when(crosses_diagonal)(functools.partial(body, True))
    pl.when(jnp.logical_not(crosses_diagonal))(functools.partial(body, False))

    @pl.when(j == ((i + 1) * tq - 1) // tk)
    def _():
        for hh in range(hb):
            hsl = slice(hh * LANES, (hh + 1) * LANES)
            od_t = acc_scr[2 * hh] / l_scr[2 * hh] - lam_ref[0] * (acc_scr[2 * hh + 1] / l_scr[2 * hh + 1])
            o_ref[:, hsl] = _sub_rmsnorm_gate(od_t.T, g_ref[...], post_scale, sg_ref[:, hsl]).astype(o_ref.dtype)


def _diff_prompt(lam, qd, kd, vdt, g_sub, sg, *, post_scale, tq, tk):
    b, t, w = qd.shape
    nh = w // LANES
    hb = 2 if nh % 2 == 0 else 1
    pairs = [(i, j) for i in range(t // tq) for j in range(((i + 1) * tq - 1) // tk + 1)]
    pi = jnp.asarray([p[0] for p in pairs], i32)
    pj = jnp.asarray([p[1] for p in pairs], i32)
    grid_spec = pltpu.PrefetchScalarGridSpec(
        num_scalar_prefetch=2,
        grid=(b, nh // hb, len(pairs)),
        in_specs=[
            pl.BlockSpec(memory_space=pltpu.SMEM),
            pl.BlockSpec((None, tq, hb * LANES), lambda bb, h, p, pi, pj: (bb, pi[p], h)),
            pl.BlockSpec((None, tk, hb * LANES), lambda bb, h, p, pi, pj: (bb, pj[p], h)),
            pl.BlockSpec((None, hb, None, LANES, tk), lambda bb, h, p, pi, pj: (bb, h, pj[p], 0, 0)),
            pl.BlockSpec((1, LANES), lambda bb, h, p, pi, pj: (0, 0)),
            pl.BlockSpec((None, tq, hb * LANES), lambda bb, h, p, pi, pj: (bb, pi[p], h)),
        ],
        out_specs=pl.BlockSpec((None, tq, hb * LANES), lambda bb, h, p, pi, pj: (bb, pi[p], h)),
        scratch_shapes=[
            pltpu.VMEM((2 * hb, 1, tq), f32),
            pltpu.VMEM((2 * hb, 1, tq), f32),
            pltpu.VMEM((2 * hb, LANES, tq), f32),
        ],
    )
    return pl.pallas_call(
        functools.partial(_diff_prompt_kernel, tq=tq, tk=tk, hb=hb, post_scale=post_scale),
        grid_spec=grid_spec,
        out_shape=jax.ShapeDtypeStruct((b, t, w), MXU_DTYPE),
        compiler_params=_cparams(("arbitrary", "arbitrary", "arbitrary")),
        name="diff_prompt",
    )(pi, pj, lam, qd, kd, vdt, g_sub, sg)


def _kth_largest_key_sub(key_scr, nblk, rows, topk):
    def count_ge(cand):
        def body(c, acc):
            k = key_scr[pl.ds(pl.multiple_of(c * rows, rows), rows), :]
            return acc + jnp.sum((k >= cand).astype(i32).reshape(rows // 8, 8, LANES), axis=0)

        acc = lax.fori_loop(0, nblk, body, jnp.zeros((8, LANES), i32))
        return jnp.sum(acc, axis=0, keepdims=True)

    def bit_body(it, prefix):
        cand = prefix + lax.shift_left(jnp.int32(1), 31 - it)
        return jnp.where(count_ge(cand) >= topk, cand, prefix)

    kth = lax.fori_loop(0, 32, bit_body, jnp.full((1, LANES), INT_MIN, i32))
    return kth, count_ge


def _demote_surplus_ties_sub(key_scr, nblk, rows, topk, kth, count_ge):
    tie = (count_ge(kth) > topk) & (kth > KEY_NEG_INF)

    @pl.when(jnp.max(tie.astype(i32)) > 0)
    def _():
        need = (topk - count_ge(kth + 1)).astype(f32)
        tri = (_iota((rows, rows), 0) >= _iota((rows, rows), 1)).astype(f32).astype(MXU_DTYPE)

        def body(c, run):
            sl = pl.ds(pl.multiple_of(c * rows, rows), rows)
            k = key_scr[sl, :]
            eq = k == kth
            incl = _dot(tri, eq.astype(f32).astype(MXU_DTYPE))
            drop = eq & ((run + incl) > need)
            key_scr[sl, :] = jnp.where(drop, KEY_NEG_INF, k)
            return run + incl[rows - 1:rows, :]

        lax.fori_loop(0, nblk, body, jnp.zeros((1, LANES), f32))


def _demote_surplus_ties_lane(key_scr, nblk, nq, topk, kth, count_ge):
    tie = (count_ge(kth) > topk) & (kth > KEY_NEG_INF)

    @pl.when(jnp.max(jnp.where(tie, 1.0, 0.0)) > 0.0)
    def _():
        need = topk - count_ge(kth + 1)
        triu = (_iota((LANES, LANES), 0) <= _iota((LANES, LANES), 1)).astype(f32).astype(MXU_DTYPE)
        pad = jnp.zeros((16 - nq % 16, LANES), f32) if nq % 16 else None

        def body(c, run):
            k = key_scr[c]
            eq = k == kth
            eqf = jnp.where(eq, 1.0, 0.0)
            if pad is not None:
                eqf = jnp.concatenate([eqf, pad], axis=0)
            incl = _dot(eqf.astype(MXU_DTYPE), triu)[:nq]
            drop = eq & ((run + incl) > need)
            key_scr[c] = jnp.where(drop, KEY_NEG_INF, k)
            return run + incl[:, LANES - 1:LANES]

        lax.fori_loop(0, nblk, body, jnp.zeros((nq, 1), f32))


def _dsa_prompt_kernel(qs_ref, qi_ref, kiwi_ref, ki2_ref, ks_ref, vst_ref, sg_ref, o_ref,
                       key_scr, qm_scr, wt_scr, qst_scr, acc_scr, *, nh, topk, tc):
    qb = pl.program_id(1)
    tq = LANES
    nch = (qb * tq + tq + tc - 1) // tc

    lo = _iota((tq, LANES), 1) < D_IDX
    for h in range(H_IDX):
        chunk = qi_ref[:, (h // 2) * LANES:(h // 2 + 1) * LANES]
        keep = lo if h % 2 == 0 else jnp.logical_not(lo)
        qm_scr[h * tq:(h + 1) * tq, :] = jnp.where(keep, chunk, jnp.zeros_like(chunk))
    wt_scr[...] = kiwi_ref[...].T
    for h in range(nh):
        qst_scr[h * tq:(h + 1) * tq, :] = qs_ref[:, h * LANES:(h + 1) * LANES]

    tpos = qb * tq + _iota((1, LANES), 1)

    def idx_body(c, carry):
        sl = pl.ds(pl.multiple_of(c * tc, tc), tc)
        kc = ki2_ref[sl, :]
        acc = jnp.zeros((tc, LANES), f32)
        for hp in range(H_IDX // 2):
            sc = _dot_nt(kc, qm_scr[2 * hp * tq:(2 * hp + 2) * tq, :])
            for k in range(2):
                h = 2 * hp + k
                acc = acc + jnp.maximum(sc[:, k * tq:(k + 1) * tq], 0.0) * wt_scr[D_IDX + h:D_IDX + h + 1, :]
        acc = jnp.where(acc == 0.0, 0.0, acc)
        kpos = c * tc + _iota((tc, LANES), 0)
        acc = jnp.where(kpos <= tpos, acc, -jnp.inf)
        key_scr[sl, :] = _f2key(acc)
        return carry

    lax.fori_loop(0, nch, idx_body, 0)

    kth, count_ge = _kth_largest_key_sub(key_scr, nch, tc, topk)
    _demote_surplus_ties_sub(key_scr, nch, tc, topk, kth, count_ge)
    thr = jnp.maximum(kth, KEY_NEG_INF + 1)

    acc_scr[...] = jnp.zeros_like(acc_scr)

    def att_body(c, carry):
        m, l = carry
        sl = pl.ds(pl.multiple_of(c * tc, tc), tc)
        s = _dot_nt(ks_ref[sl, :], qst_scr[...])
        msk = key_scr[sl, :] >= thr
        s = jnp.where(jnp.concatenate([msk] * nh, axis=1), s, NEG_BIG)
        m_new = jnp.maximum(m, jnp.max(s, axis=0, keepdims=True))
        alpha = jnp.exp2(m - m_new)
        p = jnp.exp2(s - m_new)
        l = alpha * l + jnp.sum(p, axis=0, keepdims=True)
        acc_scr[...] = alpha * acc_scr[...] + _dot(vst_ref[c], p.astype(MXU_DTYPE))
        return m_new, l

    init = (jnp.full((1, nh * tq), NEG_BIG, f32), jnp.zeros((1, nh * tq), f32))
    _, l = lax.fori_loop(0, nch, att_body, init)
    out_t = acc_scr[...] / l
    for h in range(nh):
        sl = slice(h * LANES, (h + 1) * LANES)
        o_ref[:, sl] = (out_t[:, sl].T * sg_ref[:, sl].astype(f32)).astype(o_ref.dtype)


def _dsa_prompt(qs, qi, kiwi, ki2, ks, vst, sg, *, topk, tc=256):
    b, t, w = qs.shape
    nh = w // LANES
    tq = LANES
    assert t % tc == 0 and tc % tq == 0
    return pl.pallas_call(
        functools.partial(_dsa_prompt_kernel, nh=nh, topk=topk, tc=tc),
        grid=(b, t // tq),
        in_specs=[
            pl.BlockSpec((None, tq, w), lambda bb, i: (bb, i, 0)),
            pl.BlockSpec((None, tq, H_IDX * D_IDX), lambda bb, i: (bb, i, 0)),
            pl.BlockSpec((None, tq, LANES), lambda bb, i: (bb, i, 0)),
            pl.BlockSpec((None, t, LANES), lambda bb, i: (bb, 0, 0)),
            pl.BlockSpec((None, t, LANES), lambda bb, i: (bb, 0, 0)),
            pl.BlockSpec((None, t // tc, LANES, tc), lambda bb, i: (bb, 0, 0, 0)),
            pl.BlockSpec((None, tq, w), lambda bb, i: (bb, i, 1)),
        ],
        out_specs=pl.BlockSpec((None, tq, w), lambda bb, i: (bb, i, 0)),
        out_shape=jax.ShapeDtypeStruct((b, t, w), MXU_DTYPE),
        scratch_shapes=[
            pltpu.VMEM((t, LANES), i32),
            pltpu.VMEM((H_IDX * tq, LANES), MXU_DTYPE),
            pltpu.VMEM((LANES, LANES), f32),
            pltpu.VMEM((nh * tq, LANES), MXU_DTYPE),
            pltpu.VMEM((LANES, nh * tq), f32),
        ],
        compiler_params=_cparams(("arbitrary", "arbitrary")),
        name="dsa_prompt",
    )(qs, qi, kiwi, ki2, ks, vst, sg)


def _mem_attn_kernel(q_ref, k_ref, v_ref, sg_ref, o_ref):
    s = _dot_nt(q_ref[...], k_ref[...].astype(MXU_DTYPE))
    m = jnp.max(s, axis=1, keepdims=True)
    p = jnp.exp2(s - m)
    l = jnp.sum(p, axis=1, keepdims=True)
    o = _dot(p.astype(MXU_DTYPE), v_ref[...].astype(MXU_DTYPE)) / l
    o_ref[...] = (o * sg_ref[...].astype(f32)).astype(o_ref.dtype)


def _mem_attn(q, k, v, sg, *, layer, sg_col0, tq):
    b, t, w = q.shape
    nh = w // D_MEM
    nm = k.shape[2]
    return pl.pallas_call(
        _mem_attn_kernel,
        grid=(b, t // tq, nh),
        in_specs=[
            pl.BlockSpec((None, tq, D_MEM), lambda bb, i, h: (bb, i, h)),
            pl.BlockSpec((None, None, nm, D_MEM), lambda bb, i, h: (layer, bb, 0, h)),
            pl.BlockSpec((None, None, nm, D_MEM), lambda bb, i, h: (layer, bb, 0, h)),
            pl.BlockSpec((None, tq, D_MEM), lambda bb, i, h: (bb, i, sg_col0 + h)),
        ],
        out_specs=pl.BlockSpec((None, tq, D_MEM), lambda bb, i, h: (bb, i, h)),
        out_shape=jax.ShapeDtypeStruct((b, t, w), MXU_DTYPE),
        compiler_params=_cparams(("arbitrary", "arbitrary", "arbitrary")),
        name="mem_attn",
    )(q, k, v, sg)


def _diff_sample_kernel(pt_ref, lam_ref, qm_ref, *refs, nh, nt, pps, nsteps, post_scale):
    kt_refs = refs[:pps]
    v_refs = refs[pps:2 * pps]
    knewt_ref, vnew_ref, g_ref, sg_ref, o_ref, m_scr, l_scr, acc_scr = refs[2 * pps:]
    s_id = pl.program_id(1)
    nr = 2 * nt

    @pl.when(s_id == 0)
    def _():
        m_scr[...] = jnp.full_like(m_scr, NEG_BIG)
        l_scr[...] = jnp.zeros_like(l_scr)
        acc_scr[...] = jnp.zeros_like(acc_scr)

    def update(kts, vs, masked):
        s = jnp.concatenate(
            [jnp.concatenate([_dot(qm_ref[h * nr:(h + 1) * nr, :], kt[h].astype(MXU_DTYPE)) for kt in kts], axis=1)
             for h in range(nh)], axis=0)
        if masked:
            tpos = _iota(s.shape, 0) % nt
            s = jnp.where(_iota(s.shape, 1) <= tpos, s, NEG_BIG)
        m = m_scr[...]
        m_new = jnp.maximum(m, jnp.max(s, axis=1, keepdims=True))
        alpha = jnp.exp2(m - m_new)
        p = jnp.exp2(s - m_new)
        l_scr[...] = alpha * l_scr[...] + jnp.sum(p, axis=1, keepdims=True)
        pb = p.astype(MXU_DTYPE)
        kw = s.shape[1] // len(kts)
        pv = []
        for h in range(nh):
            acc = None
            for r, v in enumerate(vs):
                d = _dot(pb[h * nr:(h + 1) * nr, r * kw:(r + 1) * kw], v[h].astype(MXU_DTYPE))
                acc = d if acc is None else acc + d
            pv.append(acc)
        acc_scr[...] = alpha * acc_scr[...] + jnp.concatenate(pv, axis=0)
        m_scr[...] = m_new

    @pl.when(s_id < nsteps)
    def _():
        update(kt_refs, v_refs, False)

    @pl.when(s_id == nsteps)
    def _():
        update([knewt_ref], [vnew_ref], True)
        o = acc_scr[...] / l_scr[...]
        for h in range(nh):
            od = o[h * nr:h * nr + nt] - lam_ref[0] * o[h * nr + nt:(h + 1) * nr]
            sl = slice(h * LANES, (h + 1) * LANES)
            o_ref[:, sl] = _sub_rmsnorm_gate(od, g_ref[...], post_scale, sg_ref[:, sl]).astype(o_ref.dtype)


def _diff_sample(page_table, lam, qm, cache_kt, cache_v, knewt, vnew, g_sub, sg, *, layer, nt, post_scale):
    db, rows, _ = qm.shape
    nh = cache_v.shape[2]
    page = cache_v.shape[3]
    npages = page_table.shape[1]
    pps = math.gcd(PAGES_PER_STEP, npages)
    nsteps = npages // pps
    w = nh * LANES

    def page_map(r):
        def f(bb, s, pt):
            return (layer, pt[bb, jnp.minimum(s * pps + r, npages - 1)], 0, 0, 0)
        return f

    kt_specs = [pl.BlockSpec((None, None, nh, LANES, page), page_map(r)) for r in range(pps)]
    v_specs = [pl.BlockSpec((None, None, nh, page, LANES), page_map(r)) for r in range(pps)]
    grid_spec = pltpu.PrefetchScalarGridSpec(
        num_scalar_prefetch=1,
        grid=(db, nsteps + 1),
        in_specs=[
            pl.BlockSpec(memory_space=pltpu.SMEM),
            pl.BlockSpec((None, rows, LANES), lambda bb, s, pt: (bb, 0, 0)),
            *kt_specs, *v_specs,
            pl.BlockSpec((None, nh, LANES, page), lambda bb, s, pt: (bb, 0, 0, 0)),
            pl.BlockSpec((None, nh, page, LANES), lambda bb, s, pt: (bb, 0, 0, 0)),
            pl.BlockSpec((1, LANES), lambda bb, s, pt: (0, 0)),
            pl.BlockSpec((None, nt, w), lambda bb, s, pt: (bb, 0, 0)),
        ],
        out_specs=pl.BlockSpec((None, nt, w), lambda bb, s, pt: (bb, 0, 0)),
        scratch_shapes=[
            pltpu.VMEM((rows, 1), f32),
            pltpu.VMEM((rows, 1), f32),
            pltpu.VMEM((rows, LANES), f32),
        ],
    )
    return pl.pallas_call(
        functools.partial(_diff_sample_kernel, nh=nh, nt=nt, pps=pps, nsteps=nsteps, post_scale=post_scale),
        grid_spec=grid_spec,
        out_shape=jax.ShapeDtypeStruct((db, nt, w), MXU_DTYPE),
        compiler_params=_cparams(("arbitrary", "arbitrary")),
        name="diff_sample",
    )(page_table, lam, qm, *([cache_kt] * pps), *([cache_v] * pps), knewt, vnew, g_sub, sg)


def _dsa_sample_kernel(pt_ref, qs_ref, qi_ref, w_ref, *refs, nt, pps, nsteps, topk, page):
    kit_refs = refs[:pps]
    ks_refs = refs[pps:2 * pps]
    vs_refs = refs[2 * pps:3 * pps]
    kint_ref, ksn_ref, vsn_ref, sg_ref, o_ref, key_scr, s_scr, v_scr = refs[3 * pps:]
    s_id = pl.program_id(1)
    gsz = LANES // nt

    def process(blk, kit, ks, vs, causal_new):
        sc = _dot(qi_ref[...], kit.astype(MXU_DTYPE))
        val = jnp.maximum(sc, 0.0) * w_ref[...]
        acc = jnp.sum(val.reshape(gsz, nt, page), axis=0)
        acc = jnp.where(acc == 0.0, 0.0, acc)
        if causal_new:
            acc = jnp.where(_iota(acc.shape, 1) <= _iota(acc.shape, 0), acc, -jnp.inf)
        key_scr[blk] = _f2key(acc)
        s_scr[blk] = _dot_nt(qs_ref[...], ks.astype(MXU_DTYPE))
        v_scr[blk] = vs.astype(MXU_DTYPE)

    @pl.when(s_id < nsteps)
    def _():
        for r in range(pps):
            process(s_id * pps + r, kit_refs[r][...], ks_refs[r][...], vs_refs[r][...], False)

    @pl.when(s_id == nsteps)
    def _():
        nblk = nsteps * pps + 1
        process(nblk - 1, kint_ref[...], ksn_ref[...], vsn_ref[...], True)

        keys = jnp.concatenate([key_scr[c] for c in range(nblk)], axis=1)

        def count_ge_all(cand):
            return jnp.sum(jnp.where(keys >= cand, 1.0, 0.0), axis=1, keepdims=True)

        def bit_body(it, prefix):
            cand = prefix + lax.shift_left(jnp.int32(1), 31 - it)
            return jnp.where(count_ge_all(cand) >= topk, cand, prefix)

        kth = lax.fori_loop(0, 32, bit_body, jnp.full((nt, 1), INT_MIN, i32))
        _demote_surplus_ties_lane(key_scr, nblk, nt, topk, kth, count_ge_all)
        thr = jnp.maximum(kth, KEY_NEG_INF + 1)

        unroll = max(u for u in (5, 4, 3, 2, 1) if nblk % u == 0)

        def sel_scores(c):
            msk = key_scr[c] >= thr
            return msk[None], s_scr[c].reshape(gsz, nt, page)

        def max_body(it, mx):
            for u in range(unroll):
                msk, s3 = sel_scores(it * unroll + u)
                mx = jnp.maximum(mx, jnp.where(msk, s3, NEG_BIG))
            return mx

        mx = lax.fori_loop(0, nblk // unroll, max_body, jnp.full((gsz, nt, page), NEG_BIG, f32))
        m = jnp.max(mx, axis=2, keepdims=True)

        def att_body(it, carry):
            lacc, acc = carry
            for u in range(unroll):
                c = it * unroll + u
                msk, s3 = sel_scores(c)
                p = jnp.where(msk, jnp.exp2(s3 - m), 0.0)
                lacc = lacc + p
                acc = acc + _dot(p.reshape(gsz * nt, page).astype(MXU_DTYPE), v_scr[c])
            return lacc, acc

        lacc, acc = lax.fori_loop(0, nblk // unroll, att_body,
                                  (jnp.zeros((gsz, nt, page), f32), jnp.zeros((LANES, D_DSA), f32)))
        l = jnp.sum(lacc, axis=2, keepdims=True).reshape(gsz * nt, 1)
        o_ref[...] = (acc / l * sg_ref[...].astype(f32)).astype(o_ref.dtype)


def _dsa_sample(page_table, qs_rows, qi_rows, wcol, cache_kit, cache_ks, cache_vs, kit_new, ks_new, vs_new, sg_perm,
                *, layer, nt, topk):
    db = qs_rows.shape[0]
    page = cache_ks.shape[2]
    npages = page_table.shape[1]
    pps = math.gcd(PAGES_PER_STEP, npages)
    nsteps = npages // pps
    nblk = npages + 1
    assert page == LANES

    def page_map(r):
        def f(bb, s, pt):
            return (layer, pt[bb, jnp.minimum(s * pps + r, npages - 1)], 0, 0)
        return f

    def pspecs(shape):
        return [pl.BlockSpec((None, None) + shape, page_map(r)) for r in range(pps)]

    per_b = lambda shape: pl.BlockSpec((None,) + shape, lambda bb, s, pt: (bb, 0, 0))
    grid_spec = pltpu.PrefetchScalarGridSpec(
        num_scalar_prefetch=1,
        grid=(db, nsteps + 1),
        in_specs=[
            per_b((LANES, D_DSA)), per_b((LANES, D_IDX)), per_b((LANES, 1)),
            *pspecs((D_IDX, page)), *pspecs((page, D_DSA)), *pspecs((page, D_DSA)),
            per_b((D_IDX, page)), per_b((page, D_DSA)), per_b((page, D_DSA)),
            per_b((LANES, D_DSA)),
        ],
        out_specs=per_b((LANES, D_DSA)),
        scratch_shapes=[
            pltpu.VMEM((nblk, nt, page), i32),
            pltpu.VMEM((nblk, LANES, page), f32),
            pltpu.VMEM((nblk, page, D_DSA), MXU_DTYPE),
        ],
    )
    return pl.pallas_call(
        functools.partial(_dsa_sample_kernel, nt=nt, pps=pps, nsteps=nsteps, topk=topk, page=page),
        grid_spec=grid_spec,
        out_shape=jax.ShapeDtypeStruct((db, LANES, D_DSA), MXU_DTYPE),
        compiler_params=_cparams(("arbitrary", "arbitrary")),
        name="dsa_sample",
    )(page_table, qs_rows, qi_rows, wcol, *([cache_kit] * pps), *([cache_ks] * pps), *([cache_vs] * pps),
      kit_new, ks_new, vs_new, sg_perm)


def _rope_tables(pos, head_dim):
    half = head_dim // 2
    lane = jnp.arange(LANES)
    inv = ROPE_THETA ** (-(lane % half).astype(f32) / half)
    ang = pos.astype(f32)[:, None] * inv[None, :]
    sign = jnp.where((lane % head_dim) < half, -1.0, 1.0).astype(f32)
    return jnp.cos(ang), jnp.sin(ang) * sign[None, :]


def _tile_gain(g, n):
    return jnp.tile(g, n // g.shape[0]).reshape(1, n).astype(f32)


def _mixer_inputs(x2d, pos_tab, w_in_t, layer, seg, gl, *, tm, stacked=None):
    cos64, sin64, cos128, sin128 = pos_tab
    m = x2d.shape[0]
    h = _rmsnorm(x2d, gl["g_in"], min(tm, 256))
    tn = 512
    proj = functools.partial(_proj, h, tm=tm, nt=True)
    wseg = lambda name: _win_desc(w_in_t, layer, seg[name][0], seg[name][1] - seg[name][0], tn)
    rows = lambda n, dt: _rows_out(m, n, dt, tm, tn)
    rope64 = [("row", cos64), ("row", sin64)]
    rope128 = [("row", cos128), ("row", sin128)]
    nqd = seg["qd"][1] - seg["qd"][0]
    nqs = seg["qs"][1] - seg["qs"][0]
    nqm = seg["qm"][1] - seg["qm"][0]
    ngate = seg["gate"][1] - seg["gate"][0]
    norm_rope64 = functools.partial(_chunks_norm_rope, gsize=D_DH, half=D_DH // 2)
    gq64 = [("col", _tile_gain(gl["g_q_diff"], nqd))] + rope64
    gk64 = [("col", _tile_gain(gl["g_k_diff"], nqd))] + rope64
    gq128 = [("col", _tile_gain(gl["g_q_dsa"], nqs))] + rope128
    out = {}
    (out["qd"],) = proj(wseg("qd"), functools.partial(norm_rope64, scale=D_DH ** -0.5 * LOG2E), _store_rows,
                        [rows(nqd, MXU_DTYPE)], gq64, name="proj_qd")
    (out["qs"],) = proj(wseg("qs"), functools.partial(_chunks_norm_rope, gsize=D_DSA, half=D_DSA // 2,
                                                      scale=D_DSA ** -0.5 * LOG2E), _store_rows,
                        [rows(nqs, MXU_DTYPE)], gq128, name="proj_qs")
    (out["qi"],) = proj(wseg("qi"), functools.partial(_chunks_rope, half=D_IDX // 2, scale=D_IDX ** -0.5 * H_IDX ** -0.5),
                        _store_rows, [rows(H_IDX * D_IDX, MXU_DTYPE)], rope64, name="proj_qi")
    (out["qm"],) = proj(wseg("qm"), functools.partial(_chunks_norm256, scale=D_MEM ** -0.5 * LOG2E), _store_rows,
                        [rows(nqm, MXU_DTYPE)], [("col", _tile_gain(gl["g_q_mem"], nqm))], name="proj_qm")
    (out["sg"],) = proj(wseg("gate"), _chunks_silu, _store_rows, [rows(ngate, MXU_DTYPE)], name="proj_gate")

    ks0, vs0, ki0 = seg["ks"][0], seg["vs"][0], seg["ki"][0]
    assert vs0 - ks0 == LANES and ks0 % LANES == 0 and ki0 % LANES == 0
    skip = (ki0 - vs0) // LANES - 1
    w_ksv = dict(array=w_in_t, n=3 * LANES, tn=LANES,
                 spec=pl.BlockSpec((None, LANES, w_in_t.shape[2]),
                                   lambda i, j: (layer, ks0 // LANES + j + jnp.where(j == 2, skip, 0), 0)))
    g_kiwi = jnp.concatenate([gl["g_k_idx"], jnp.ones((LANES - D_IDX,), f32)]).reshape(1, LANES)
    aux_ksv = [("const", gl["g_k_dsa"].reshape(1, D_DSA))] + rope128 + [("const", g_kiwi)] + rope64
    blk = lambda dt: dict(shape=(m, LANES), dtype=dt, block=(tm, LANES), index=lambda i, j: (i, 0), alias=None)
    if stacked is None:
        out["kd"], out["kd_c"] = proj(wseg("kd"), functools.partial(norm_rope64, scale=1.0), _store_rows,
                                      [rows(nqd, f32), rows(nqd, MXU_DTYPE)], gk64, name="proj_kd")
        out["vd"], out["vd_c"] = proj(wseg("vd"), _chunks_raw, _store_rows,
                                      [rows(nqd, f32), rows(nqd, MXU_DTYPE)], name="proj_vd")
        out["ks"], out["ks_c"], out["vs"], out["vs_c"], out["kiwi"], out["ki2"] = proj(
            w_ksv, _chunks_ks_vs_kiwi, _store_ks_vs_kiwi,
            [blk(f32), blk(MXU_DTYPE), blk(f32), blk(MXU_DTYPE), blk(f32), blk(MXU_DTYPE)], aux_ksv,
            name="proj_ks_vs_kiwi")
        return out

    l = layer
    b, t = stacked["b"], stacked["t"]
    nrb = t // tm
    nh = nqd // LANES
    hb = _col_tile(nqd, tn) // LANES
    vkb = stacked["diff_tk"]
    assert tm % vkb == 0
    depth = stacked["pdk"].shape[0]
    out["pdk"], out["kd_c"] = proj(
        wseg("kd"), functools.partial(norm_rope64, scale=1.0), _store_kd_prompt,
        [dict(shape=(depth, b, nh, LANES, t), dtype=f32, block=(None, None, hb, LANES, tm),
              index=lambda i, j: (l, i // nrb, j, 0, i % nrb), alias=stacked["pdk"]),
         rows(nqd, MXU_DTYPE)], gk64, name="proj_kd")
    out["pdv"], out["vd_c"] = proj(
        wseg("vd"), _chunks_raw, _store_vd_prompt,
        [dict(shape=(depth, b, nh, t, LANES), dtype=f32, block=(None, None, hb, tm, LANES),
              index=lambda i, j: (l, i // nrb, j, i % nrb, 0), alias=stacked["pdv"]),
         dict(shape=(b, nh, t // vkb, LANES, vkb), dtype=MXU_DTYPE, block=(None, hb, tm // vkb, LANES, vkb),
              index=lambda i, j: (i // nrb, j, i % nrb, 0, 0), alias=None)], name="proj_vd")
    tok_major = lambda key: dict(shape=(depth, b, t, D_DSA), dtype=f32, block=(None, None, tm, D_DSA),
                                 index=lambda i, j: (l, i // nrb, i % nrb, 0), alias=stacked[key])
    out["psk"], out["ks_c"], out["psv"], out["vs_c"], out["kiwi"], out["ki2"], out["pik"] = proj(
        w_ksv, _chunks_ks_vs_kiwi, _store_ks_vs_kiwi,
        [tok_major("psk"), blk(MXU_DTYPE), tok_major("psv"), blk(MXU_DTYPE), blk(f32), blk(MXU_DTYPE),
         dict(shape=(depth, b, D_IDX, t), dtype=f32, block=(None, None, D_IDX, tm),
              index=lambda i, j: (l, i // nrb, 0, i % nrb), alias=stacked["pik"])], aux_ksv, name="proj_ks_vs_kiwi")
    return out


def kernel(x_prompt, x_sample, mem_prompt, cache_diff_k, cache_diff_v, cache_dsa_k, cache_dsa_v, cache_idx_k,
           cache_mem_k, cache_mem_v, page_table, w_in, w_out, w_mem_kv, g_in, g_mem, g_q_diff, g_k_diff,
           g_sub_diff, lam_q1, lam_k1, lam_q2, lam_k2, g_q_dsa, g_k_dsa, g_k_idx, g_q_mem, g_k_mem):
    depth = w_in.shape[0]
    b, t, d = x_prompt.shape
    db, nt, _ = x_sample.shape
    n_mem = mem_prompt.shape[1]
    n_phys, page = cache_dsa_k.shape[1], cache_dsa_k.shape[2]
    npages = page_table.shape[1]
    past = npages * page
    h_diff = cache_diff_k.shape[3]
    w_diff = h_diff * 2 * D_DH
    h_dsa = (3 * d // 8) // D_DSA
    w_dsa = h_dsa * D_DSA
    h_mem = cache_mem_k.shape[3]
    w_mem = h_mem * D_MEM
    gsz = LANES // nt
    assert w_diff + w_dsa + w_mem == d and w_diff == w_dsa and LANES % nt == 0 and h_dsa <= gsz and gsz == H_IDX
    topk_p = min(TOPK_MAX, t // 4)
    topk_s = min(TOPK_MAX, (past + nt) // 4)
    m_p = b * t
    m_s = db * nt
    tm_p = min(1024, t)
    assert t % tm_p == 0

    widths = (w_diff, w_diff, w_diff, w_dsa, D_DSA, D_DSA, H_IDX * D_IDX, D_IDX, H_IDX, w_mem, d)
    offs = [0]
    for wdt in widths:
        offs.append(offs[-1] + wdt)
    names = ("qd", "kd", "vd", "qs", "ks", "vs", "qi", "ki", "wi", "qm", "gate")
    seg = {n: (offs[k], offs[k + 1]) for k, n in enumerate(names)}

    pos_p = jnp.arange(t, dtype=i32)
    pos_s = jnp.tile(past + jnp.arange(nt, dtype=i32), db)
    tab_p = _rope_tables(pos_p, D_DH) + _rope_tables(pos_p, D_DSA)
    tab_s = _rope_tables(pos_s, D_DH) + _rope_tables(pos_s, D_DSA)

    cdkt = cache_diff_k.transpose(0, 1, 3, 4, 5, 2).reshape(depth, n_phys, h_diff, 2 * D_DH, page)
    cdv = cache_diff_v.transpose(0, 1, 3, 2, 4)
    ckit = cache_idx_k.transpose(0, 1, 3, 2)
    cmk = cache_mem_k.reshape(depth, db, n_mem, w_mem)
    cmv = cache_mem_v.reshape(depth, db, n_mem, w_mem)
    w_in_t = jnp.swapaxes(w_in, 1, 2)

    x_p = x_prompt.reshape(m_p, d)
    x_s = x_sample.reshape(m_s, d)
    mem2d = mem_prompt.reshape(b * n_mem, d)
    stk = dict(pdk=jnp.zeros((depth, b, h_diff, 2 * D_DH, t), f32), pdv=jnp.zeros((depth, b, h_diff, t, 2 * D_DH), f32),
               psk=jnp.zeros((depth, b, t, D_DSA), f32), psv=jnp.zeros((depth, b, t, D_DSA), f32),
               pik=jnp.zeros((depth, b, D_IDX, t), f32))
    outs = {k: [] for k in ("pmk", "pmv", "sdk", "sdv", "ssk", "ssv", "sik")}

    for l in range(depth):
        gl = dict(g_in=g_in[l], g_q_diff=g_q_diff[l], g_k_diff=g_k_diff[l], g_q_dsa=g_q_dsa[l],
                  g_k_dsa=g_k_dsa[l], g_k_idx=g_k_idx[l], g_q_mem=g_q_mem[l])
        lam_init = 0.8 - 0.6 * math.exp(-0.3 * l)
        lam = (jnp.exp(jnp.sum(lam_q1[l] * lam_k1[l])) - jnp.exp(jnp.sum(lam_q2[l] * lam_k2[l])) + lam_init)
        lam = lam.astype(f32).reshape(1)
        g_sub = g_sub_diff[l].reshape(1, 2 * D_DH)
        post = 1.0 - lam_init

        diff_tq, diff_tk = min(DIFF_TQ, t), min(DIFF_TK, t)
        tp = _mixer_inputs(x_p, tab_p, w_in_t, l, seg, gl, tm=tm_p, stacked=dict(stk, b=b, t=t, diff_tk=diff_tk))
        for key in ("pdk", "pdv", "psk", "psv", "pik"):
            stk[key] = tp[key]
        hm = _rmsnorm(mem2d, g_mem[l], 256)
        mrows = lambda dt: [_rows_out(b * n_mem, w_mem, dt, b * n_mem, 512)]
        mk, = _proj(hm, _wcols_desc(w_mem_kv, l, 0, w_mem, 512), functools.partial(_chunks_norm256, scale=1.0),
                    _store_rows, mrows(f32), [("col", _tile_gain(g_k_mem[l], w_mem))], tm=b * n_mem, nt=False,
                    name="proj_mk")
        mv, = _proj(hm, _wcols_desc(w_mem_kv, l, w_mem, w_mem, 512), _chunks_raw, _store_rows, mrows(f32),
                    tm=b * n_mem, nt=False, name="proj_mv")
        r3 = lambda a: a.reshape(b, t, a.shape[-1])
        sg = r3(tp["sg"])
        od = _diff_prompt(lam, r3(tp["qd"]), r3(tp["kd_c"]), tp["vd_c"], g_sub, sg, post_scale=post,
                          tq=diff_tq, tk=diff_tk)
        tc = 256
        vst = tp["vs_c"].reshape(b, t // tc, tc, D_DSA).transpose(0, 1, 3, 2)
        os_ = _dsa_prompt(r3(tp["qs"]), r3(tp["qi"]), r3(tp["kiwi"]), r3(tp["ki2"]), r3(tp["ks_c"]), vst, sg,
                          topk=topk_p, tc=tc)
        om = _mem_attn(r3(tp["qm"]), mk.reshape(1, b, n_mem, w_mem), mv.reshape(1, b, n_mem, w_mem), sg,
                       layer=0, sg_col0=(w_diff + w_dsa) // D_MEM, tq=min(512, t))
        x_p = _outproj(od.reshape(m_p, w_diff), os_.reshape(m_p, w_dsa), om.reshape(m_p, w_mem), w_out, x_p,
                       layer=l, tm=tm_p, tn=512)
        outs["pmk"].append(mk.reshape(b, n_mem, h_mem, D_MEM))
        outs["pmv"].append(mv.reshape(b, n_mem, h_mem, D_MEM))

        ts = _mixer_inputs(x_s, tab_s, w_in_t, l, seg, gl, tm=m_s)
        s3 = lambda a: a.reshape(db, nt, a.shape[-1])
        sg_s = s3(ts["sg"])
        tokpad = lambda a, axis: jnp.pad(a, [(0, page - nt) if ax == axis else (0, 0) for ax in range(a.ndim)])
        q5 = ts["qd"].reshape(db, nt, h_diff, 2, D_DH).transpose(0, 2, 3, 1, 4)
        eye_c = jnp.eye(2, dtype=MXU_DTYPE)
        qm = (q5[:, :, :, :, None, :] * eye_c[None, None, :, None, :, None]).reshape(db, h_diff * 2 * nt, 2 * D_DH)
        kd4 = ts["kd"].reshape(db, nt, h_diff, 2 * D_DH)
        vd4 = ts["vd"].reshape(db, nt, h_diff, 2 * D_DH)
        od_s = _diff_sample(page_table, lam, qm, cdkt, cdv, tokpad(kd4.transpose(0, 2, 3, 1), 3),
                            tokpad(vd4.transpose(0, 2, 1, 3), 2), g_sub, sg_s, layer=l, nt=nt, post_scale=post)
        qs4 = ts["qs"].reshape(db, nt, h_dsa, D_DSA)
        slot_major = lambda a: a.transpose(0, 2, 1, 3).reshape(db, LANES, a.shape[-1])
        qs_rows = slot_major(jnp.pad(qs4, ((0, 0), (0, 0), (0, gsz - h_dsa), (0, 0))))
        qi_rows = slot_major(ts["qi"].reshape(db, nt, H_IDX, D_IDX))
        wcol = slot_major(ts["kiwi"][:, D_IDX:D_IDX + H_IDX].reshape(db, nt, H_IDX, 1))
        sg_dsa = sg_s[:, :, w_diff:w_diff + w_dsa].reshape(db, nt, h_dsa, D_DSA)
        sg_perm = slot_major(jnp.pad(sg_dsa, ((0, 0), (0, 0), (0, gsz - h_dsa), (0, 0))))
        ki_new = s3(ts["kiwi"][:, :D_IDX])
        os_s = _dsa_sample(page_table, qs_rows, qi_rows, wcol, ckit, cache_dsa_k, cache_dsa_v,
                           tokpad(ki_new.transpose(0, 2, 1), 2), tokpad(s3(ts["ks"]), 1), tokpad(s3(ts["vs"]), 1),
                           sg_perm, layer=l, nt=nt, topk=topk_s)
        os_s = os_s.reshape(db, gsz, nt, D_DSA)[:, :h_dsa].transpose(0, 2, 1, 3).reshape(m_s, w_dsa)
        om_s = _mem_attn(s3(ts["qm"]), cmk, cmv, sg_s, layer=l, sg_col0=(w_diff + w_dsa) // D_MEM, tq=nt)
        x_s = _outproj(od_s.reshape(m_s, w_diff), os_s, om_s.reshape(m_s, w_mem), w_out, x_s, layer=l, tm=m_s,
                       tn=512)
        outs["sdk"].append(ts["kd"].reshape(db, nt, h_diff, 2, D_DH))
        outs["sdv"].append(ts["vd"].reshape(db, nt, h_diff, 2 * D_DH))
        outs["ssk"].append(ts["ks"].reshape(db, nt, D_DSA))
        outs["ssv"].append(ts["vs"].reshape(db, nt, D_DSA))
        outs["sik"].append(ki_new)

    st = lambda k: jnp.stack(outs[k])
    p_diff_k = stk["pdk"].reshape(depth, b, h_diff, 2, D_DH, t).transpose(0, 1, 5, 2, 3, 4)
    p_diff_v = stk["pdv"].transpose(0, 1, 3, 2, 4)
    p_idx_k = stk["pik"].transpose(0, 1, 3, 2)
    return (x_p.reshape(b, t, d), x_s.reshape(db, nt, d),
            p_diff_k, p_diff_v, stk["psk"], stk["psv"], p_idx_k, st("pmk"), st("pmv"),
            st("sdk"), st("sdv"), st("ssk"), st("ssv"), st("sik"))
```

```python
import functools
import math

import jax
import jax.numpy as jnp
from jax import lax
from jax.experimental import pallas as pl
from jax.experimental.pallas import tpu as pltpu

EPS = 1e-6
ROPE_THETA = 10000.0
TOPK_MAX = 256
LANES = 128
D_DH = 64
D_DSA = 128
D_IDX = 64
H_IDX = 16
D_MEM = 256
NEG_BIG = -1e30
KEY_NEG_INF = -2139095041
INT_MIN = -2147483648
MXU_DTYPE = jnp.bfloat16
VMEM_LIMIT_BYTES = 52 * 1024 * 1024
PAGES_PER_STEP = 8
DIFF_TQ = 512
DIFF_TK = 512
LOG2E = math.log2(math.e)

f32 = jnp.float32
i32 = jnp.int32


def _cparams(sem):
    return pltpu.CompilerParams(dimension_semantics=sem, vmem_limit_bytes=VMEM_LIMIT_BYTES)


def _dot(a, b):
    return jnp.dot(a, b, preferred_element_type=f32)


def _dot_nt(a, b):
    return lax.dot_general(a, b, (((1,), (1,)), ((), ())), preferred_element_type=f32)


def _iota(shape, dim):
    return lax.broadcasted_iota(i32, shape, dim)


def _group_mat(gsize):
    r = _iota((LANES, LANES), 0) // gsize
    c = _iota((LANES, LANES), 1) // gsize
    return (r == c).astype(f32).astype(MXU_DTYPE)


def _group_sum(x, gmat):
    hi = x.astype(MXU_DTYPE)
    lo = (x - hi.astype(f32)).astype(MXU_DTYPE)
    return _dot(hi, gmat) + _dot(lo, gmat)


def _rope_chunk(n, cos, sin_signed, half):
    if 2 * half == LANES:
        rot = pltpu.roll(n, half, 1)
    else:
        first = (_iota(n.shape, 1) % (2 * half)) < half
        rot = jnp.where(first, pltpu.roll(n, LANES - half, 1), pltpu.roll(n, half, 1))
    return n * cos + rot * sin_signed


def _f2key(x):
    b = pltpu.bitcast(x, i32)
    return b ^ (lax.shift_right_arithmetic(b, 31) & 0x7FFFFFFF)


def _rmsnorm_kernel(x_ref, g_ref, o_ref):
    x = x_ref[...]
    ms = jnp.mean(x * x, axis=-1, keepdims=True)
    o_ref[...] = (x * lax.rsqrt(ms + EPS) * g_ref[...]).astype(o_ref.dtype)


def _rmsnorm(x, g, tm):
    m, d = x.shape
    return pl.pallas_call(
        _rmsnorm_kernel,
        grid=(m // tm,),
        in_specs=[pl.BlockSpec((tm, d), lambda i: (i, 0)), pl.BlockSpec((1, d), lambda i: (0, 0))],
        out_specs=pl.BlockSpec((tm, d), lambda i: (i, 0)),
        out_shape=jax.ShapeDtypeStruct((m, d), MXU_DTYPE),
        compiler_params=_cparams(("arbitrary",)),
        name="rmsnorm",
    )(x, g.reshape(1, d))


def _chunks_raw(z, aux, store):
    for c in range(z.shape[1] // LANES):
        store(c, z[:, c * LANES:(c + 1) * LANES])


def _chunks_silu(z, aux, store):
    for c in range(z.shape[1] // LANES):
        zc = z[:, c * LANES:(c + 1) * LANES]
        store(c, zc / (1.0 + jnp.exp(-zc)))


def _chunks_norm_rope(z, aux, store, *, gsize, half, scale):
    gain, cos, sin = aux
    gmat = _group_mat(gsize)
    for c in range(z.shape[1] // LANES):
        sl = slice(c * LANES, (c + 1) * LANES)
        zc = z[:, sl]
        ss = _group_sum(zc * zc, gmat)
        n = zc * lax.rsqrt(ss * (1.0 / gsize) + EPS) * gain[:, sl]
        r = _rope_chunk(n, cos, sin, half)
        store(c, r * scale if scale != 1.0 else r)


def _chunks_rope(z, aux, store, *, half, scale):
    cos, sin = aux
    for c in range(z.shape[1] // LANES):
        store(c, _rope_chunk(z[:, c * LANES:(c + 1) * LANES], cos, sin, half) * scale)


def _chunks_norm256(z, aux, store, *, scale):
    (gain,) = aux
    gmat = _group_mat(LANES)
    for c in range(z.shape[1] // D_MEM):
        a = z[:, c * D_MEM:c * D_MEM + LANES]
        b = z[:, c * D_MEM + LANES:(c + 1) * D_MEM]
        ss = _group_sum(a * a, gmat) + _group_sum(b * b, gmat)
        inv = lax.rsqrt(ss * (1.0 / D_MEM) + EPS)
        for k, v in enumerate((a, b)):
            sl = slice(c * D_MEM + k * LANES, c * D_MEM + (k + 1) * LANES)
            r = v * inv * gain[:, sl]
            store(2 * c + k, r * scale if scale != 1.0 else r)


def _chunks_kiwi(z, aux, store):
    gain, cos, sin = aux
    gmat = _group_mat(D_IDX)
    ss = _group_sum(z * z, gmat)
    n = z * lax.rsqrt(ss * (1.0 / D_IDX) + EPS) * gain
    r = _rope_chunk(n, cos, sin, D_IDX // 2)
    lo = _iota(z.shape, 1) < D_IDX
    store(0, jnp.where(lo, r, z))
    r_lo = jnp.where(lo, r, 0.0)
    store(1, r_lo + pltpu.roll(r_lo, D_IDX, 1))


def _store_rows(outs, c, r):
    for o in outs:
        o[:, c * LANES:(c + 1) * LANES] = r.astype(o.dtype)


def _store_kd_prompt(outs, c, r):
    outs[0][c] = r.T
    outs[1][:, c * LANES:(c + 1) * LANES] = r.astype(outs[1].dtype)


def _store_vd_prompt(outs, c, r):
    outs[0][c] = r
    rt = r.T.astype(outs[1].dtype)
    kb = outs[1].shape[-1]
    for kk in range(r.shape[0] // kb):
        outs[1][c, kk] = rt[:, kk * kb:(kk + 1) * kb]


def _chunks_ks_vs_kiwi(z, aux, store):
    gain_k, cos128, sin128, gain_kiwi, cos64, sin64 = aux
    j = pl.program_id(1)

    @pl.when(j == 0)
    def _():
        _chunks_norm_rope(z, (gain_k, cos128, sin128), functools.partial(store, 0), gsize=D_DSA, half=D_DSA // 2,
                          scale=1.0)

    @pl.when(j == 1)
    def _():
        store(1, 0, z)

    @pl.when(j == 2)
    def _():
        _chunks_kiwi(z, (gain_kiwi, cos64, sin64), functools.partial(store, 2))


def _store_ks_vs_kiwi(outs, which, c, r):
    if which < 2:
        outs[2 * which][...] = r
        outs[2 * which + 1][...] = r.astype(outs[2 * which + 1].dtype)
    elif c == 0:
        outs[4][...] = r
        if len(outs) > 6:
            outs[6][...] = r.T[:D_IDX, :]
    else:
        outs[5][...] = r.astype(outs[5].dtype)


def _proj_kernel(*refs, chunk_fn, store_fn, n_aux, n_alias, nt):
    h_ref, w_ref = refs[:2]
    aux = [r[...] for r in refs[2:2 + n_aux]]
    outs = refs[2 + n_aux + n_alias:]
    w = w_ref[...].astype(MXU_DTYPE)
    z = _dot_nt(h_ref[...], w) if nt else _dot(h_ref[...], w)
    chunk_fn(z, aux, functools.partial(store_fn, outs))


def _win_desc(w_in_t, layer, off, n, tn):
    tn = _col_tile(n, tn)
    k = w_in_t.shape[2]
    if off % tn == 0:
        return dict(array=w_in_t, spec=pl.BlockSpec((None, tn, k), lambda i, j: (layer, off // tn + j, 0)), n=n, tn=tn)
    row0 = layer * w_in_t.shape[1] + off
    assert row0 % 8 == 0
    spec = pl.BlockSpec((pl.Element(tn), pl.Element(k)), lambda i, j: (pl.multiple_of(row0 + j * tn, 8), 0))
    return dict(array=w_in_t.reshape(-1, k), spec=spec, n=n, tn=tn)


def _wcols_desc(w, layer, off, n, tn):
    tn = _col_tile(n, tn)
    assert off % tn == 0
    return dict(array=w, spec=pl.BlockSpec((None, w.shape[1], tn), lambda i, j: (layer, 0, off // tn + j)), n=n, tn=tn)


def _col_tile(n, tn):
    return math.gcd(n, tn)


def _rows_out(m, n, dtype, tm, tn):
    tn = _col_tile(n, tn)
    return dict(shape=(m, n), dtype=dtype, block=(tm, tn), index=lambda i, j: (i, j), alias=None)


def _proj(h, wd, chunk_fn, store_fn, outs, aux=(), *, tm, nt, name):
    m, k = h.shape
    n, tn = wd["n"], wd["tn"]
    assert m % tm == 0 and tn % LANES == 0 and n % tn == 0
    in_specs = [pl.BlockSpec((tm, k), lambda i, j: (i, 0)), wd["spec"]]
    args = [h, wd["array"]]
    for kind, a in aux:
        if kind == "col":
            in_specs.append(pl.BlockSpec((1, tn), lambda i, j: (0, j)))
        elif kind == "const":
            in_specs.append(pl.BlockSpec((1, LANES), lambda i, j: (0, 0)))
        else:
            nrb = a.shape[0] // tm
            in_specs.append(pl.BlockSpec((tm, LANES), lambda i, j, nrb=nrb: (i % nrb, 0)))
        args.append(a)
    aliases = {}
    for k_out, o in enumerate(outs):
        if o["alias"] is not None:
            aliases[len(args)] = k_out
            in_specs.append(pl.BlockSpec(memory_space=pl.ANY))
            args.append(o["alias"])
    return pl.pallas_call(
        functools.partial(_proj_kernel, chunk_fn=chunk_fn, store_fn=store_fn, n_aux=len(aux),
                          n_alias=len(aliases), nt=nt),
        grid=(m // tm, n // tn),
        in_specs=in_specs,
        out_specs=[pl.BlockSpec(o["block"], o["index"]) for o in outs],
        out_shape=[jax.ShapeDtypeStruct(o["shape"], o["dtype"]) for o in outs],
        input_output_aliases=aliases,
        compiler_params=_cparams(("arbitrary", "arbitrary")),
        name=name,
    )(*args)


def _outproj_kernel(od_ref, os_ref, om_ref, w_ref, x_ref, o_ref, *, wd, ws):
    acc = _dot(od_ref[...], w_ref[0:wd, :].astype(MXU_DTYPE))
    acc += _dot(os_ref[...], w_ref[wd:wd + ws, :].astype(MXU_DTYPE))
    acc += _dot(om_ref[...], w_ref[wd + ws:, :].astype(MXU_DTYPE))
    o_ref[...] = x_ref[...] + acc


def _outproj(od, os_, om, w, x, *, layer, tm, tn):
    m, d = x.shape
    wd, ws, wm = od.shape[1], os_.shape[1], om.shape[1]
    return pl.pallas_call(
        functools.partial(_outproj_kernel, wd=wd, ws=ws),
        grid=(m // tm, d // tn),
        in_specs=[
            pl.BlockSpec((tm, wd), lambda i, j: (i, 0)),
            pl.BlockSpec((tm, ws), lambda i, j: (i, 0)),
            pl.BlockSpec((tm, wm), lambda i, j: (i, 0)),
            pl.BlockSpec((None, wd + ws + wm, tn), lambda i, j: (layer, 0, j)),
            pl.BlockSpec((tm, tn), lambda i, j: (i, j)),
        ],
        out_specs=pl.BlockSpec((tm, tn), lambda i, j: (i, j)),
        out_shape=jax.ShapeDtypeStruct((m, d), f32),
        compiler_params=_cparams(("arbitrary", "arbitrary")),
        name="outproj",
    )(od, os_, om, w, x)


def _sub_rmsnorm_gate(od, g, post_scale, sg):
    ms = jnp.mean(od * od, axis=-1, keepdims=True)
    return od * lax.rsqrt(ms + EPS) * g * post_scale * sg.astype(f32)


def _diff_prompt_kernel(pi_ref, pj_ref, lam_ref, q_ref, k_ref, vt_ref, g_ref, sg_ref, o_ref,
                        m_scr, l_scr, acc_scr, *, tq, tk, hb, post_scale):
    p_id = pl.program_id(2)
    i = pi_ref[p_id]
    j = pj_ref[p_id]

    @pl.when(j == 0)
    def _():
        m_scr[...] = jnp.full_like(m_scr, NEG_BIG)
        l_scr[...] = jnp.zeros_like(l_scr)
        acc_scr[...] = jnp.zeros_like(acc_scr)

    nchain = 2 * hb
    h2 = tq // 2
    split_diag = tq == tk and h2 % LANES == 0

    def masked_q(r):
        q = q_ref[:, (r // 2) * LANES:(r // 2 + 1) * LANES]
        lo = _iota(q.shape, 1) < D_DH
        zero = jnp.zeros_like(q)
        return jnp.where(lo, q, zero) if r % 2 == 0 else jnp.where(lo, zero, q)

    def keys(r):
        return k_ref[:, (r // 2) * LANES:(r // 2 + 1) * LANES]

    def scores(r):
        return _dot_nt(keys(r), masked_q(r))

    def scores_diag(r):
        qc, kc = masked_q(r), keys(r)
        return _dot_nt(kc[:h2], qc), _dot_nt(kc[h2:], qc[h2:])

    def causal(s, key0, qry0):
        return jnp.where(j * tk + key0 + _iota(s.shape, 0) <= i * tq + qry0 + _iota(s.shape, 1), s, NEG_BIG)

    def body(masked):
        s_next = scores(0)
        for r in range(nchain):
            s = s_next
            if r + 1 < nchain:
                s_next = scores(r + 1)
            if masked:
                s = causal(s, 0, 0)
            m = m_scr[r]
            m_new = jnp.maximum(m, jnp.max(s, axis=0, keepdims=True))
            alpha = jnp.exp2(m - m_new)
            p = jnp.exp2(s - m_new)
            l_scr[r] = alpha * l_scr[r] + jnp.sum(p, axis=0, keepdims=True)
            acc_scr[r] = alpha * acc_scr[r] + _dot(vt_ref[r // 2], p.astype(MXU_DTYPE))
            m_scr[r] = m_new

    def body_diag():
        s_next = scores_diag(0)
        for r in range(nchain):
            sa, sb = s_next
            if r + 1 < nchain:
                s_next = scores_diag(r + 1)
            sa = jnp.concatenate([causal(sa[:, :h2], 0, 0), sa[:, h2:]], axis=1)
            sb = causal(sb, h2, h2)
            m = m_scr[r]
            mb = jnp.concatenate([jnp.full((1, h2), NEG_BIG, f32), jnp.max(sb, axis=0, keepdims=True)], axis=1)
            m_new = jnp.maximum(m, jnp.maximum(jnp.max(sa, axis=0, keepdims=True), mb))
            alpha = jnp.exp2(m - m_new)
            pa = jnp.exp2(sa - m_new)
            pb = jnp.exp2(sb - m_new[:, h2:])
            lb = jnp.concatenate([jnp.zeros((1, h2), f32), jnp.sum(pb, axis=0, keepdims=True)], axis=1)
            l_scr[r] = alpha * l_scr[r] + jnp.sum(pa, axis=0, keepdims=True) + lb
            vt = vt_ref[r // 2]
            pva = _dot(vt[:, :h2], pa.astype(MXU_DTYPE))
            pvb = _dot(vt[:, h2:], pb.astype(MXU_DTYPE))
            acc_scr[r] = alpha * acc_scr[r] + pva + jnp.concatenate([jnp.zeros((LANES, h2), f32), pvb], axis=1)
            m_scr[r] = m_new

    crosses_diagonal = (j + 1) * tk - 1 > i * tq
    pl.when(crosses_diagonal)(body_diag if split_diag else functools.partial(body, True))
    pl.when(jnp.logical_not(crosses_diagonal))(functools.partial(body, False))

    @pl.when(j == ((i + 1) * tq - 1) // tk)
    def _():
        for hh in range(hb):
            hsl = slice(hh * LANES, (hh + 1) * LANES)
            od_t = acc_scr[2 * hh] / l_scr[2 * hh] - lam_ref[0] * (acc_scr[2 * hh + 1] / l_scr[2 * hh + 1])
            o_ref[:, hsl] = _sub_rmsnorm_gate(od_t.T, g_ref[...], post_scale, sg_ref[:, hsl]).astype(o_ref.dtype)


def _diff_prompt(lam, qd, kd, vdt, g_sub, sg, *, post_scale, tq, tk):
    b, t, w = qd.shape
    nh = w // LANES
    hb = 2 if nh % 2 == 0 else 1
    pairs = [(i, j) for i in range(t // tq) for j in range(((i + 1) * tq - 1) // tk + 1)]
    pi = jnp.asarray([p[0] for p in pairs], i32)
    pj = jnp.asarray([p[1] for p in pairs], i32)
    grid_spec = pltpu.PrefetchScalarGridSpec(
        num_scalar_prefetch=2,
        grid=(b, nh // hb, len(pairs)),
        in_specs=[
            pl.BlockSpec(memory_space=pltpu.SMEM),
            pl.BlockSpec((None, tq, hb * LANES), lambda bb, h, p, pi, pj: (bb, pi[p], h)),
            pl.BlockSpec((None, tk, hb * LANES), lambda bb, h, p, pi, pj: (bb, pj[p], h)),
            pl.BlockSpec((None, hb, None, LANES, tk), lambda bb, h, p, pi, pj: (bb, h, pj[p], 0, 0)),
            pl.BlockSpec((1, LANES), lambda bb, h, p, pi, pj: (0, 0)),
            pl.BlockSpec((None, tq, hb * LANES), lambda bb, h, p, pi, pj: (bb, pi[p], h)),
        ],
        out_specs=pl.BlockSpec((None, tq, hb * LANES), lambda bb, h, p, pi, pj: (bb, pi[p], h)),
        scratch_shapes=[
            pltpu.VMEM((2 * hb, 1, tq), f32),
            pltpu.VMEM((2 * hb, 1, tq), f32),
            pltpu.VMEM((2 * hb, LANES, tq), f32),
        ],
    )
    return pl.pallas_call(
        functools.partial(_diff_prompt_kernel, tq=tq, tk=tk, hb=hb, post_scale=post_scale),
        grid_spec=grid_spec,
        out_shape=jax.ShapeDtypeStruct((b, t, w), MXU_DTYPE),
        compiler_params=_cparams(("arbitrary", "arbitrary", "arbitrary")),
        name="diff_prompt",
    )(pi, pj, lam, qd, kd, vdt, g_sub, sg)


def _kth_largest_key_sub(key_scr, nblk, rows, topk):
    def count_ge(cand):
        def body(c, acc):
            k = key_scr[pl.ds(pl.multiple_of(c * rows, rows), rows), :]
            return acc + jnp.sum((k >= cand).astype(i32).reshape(rows // 8, 8, LANES), axis=0)

        acc = lax.fori_loop(0, nblk, body, jnp.zeros((8, LANES), i32))
        return jnp.sum(acc, axis=0, keepdims=True)

    def bit_body(it, prefix):
        cand = prefix + lax.shift_left(jnp.int32(1), 31 - it)
        return jnp.where(count_ge(cand) >= topk, cand, prefix)

    kth = lax.fori_loop(0, 32, bit_body, jnp.full((1, LANES), INT_MIN, i32))
    return kth, count_ge


def _demote_surplus_ties_sub(key_scr, nblk, rows, topk, kth, count_ge):
    tie = (count_ge(kth) > topk) & (kth > KEY_NEG_INF)

    @pl.when(jnp.max(tie.astype(i32)) > 0)
    def _():
        need = (topk - count_ge(kth + 1)).astype(f32)
        tri = (_iota((rows, rows), 0) >= _iota((rows, rows), 1)).astype(f32).astype(MXU_DTYPE)

        def body(c, run):
            sl = pl.ds(pl.multiple_of(c * rows, rows), rows)
            k = key_scr[sl, :]
            eq = k == kth
            incl = _dot(tri, eq.astype(f32).astype(MXU_DTYPE))
            drop = eq & ((run + incl) > need)
            key_scr[sl, :] = jnp.where(drop, KEY_NEG_INF, k)
            return run + incl[rows - 1:rows, :]

        lax.fori_loop(0, nblk, body, jnp.zeros((1, LANES), f32))


def _demote_surplus_ties_lane(key_scr, nblk, nq, topk, kth, count_ge):
    tie = (count_ge(kth) > topk) & (kth > KEY_NEG_INF)

    @pl.when(jnp.max(jnp.where(tie, 1.0, 0.0)) > 0.0)
    def _():
        need = topk - count_ge(kth + 1)
        triu = (_iota((LANES, LANES), 0) <= _iota((LANES, LANES), 1)).astype(f32).astype(MXU_DTYPE)
        pad = jnp.zeros((16 - nq % 16, LANES), f32) if nq % 16 else None

        def body(c, run):
            k = key_scr[c]
            eq = k == kth
            eqf = jnp.where(eq, 1.0, 0.0)
            if pad is not None:
                eqf = jnp.concatenate([eqf, pad], axis=0)
            incl = _dot(eqf.astype(MXU_DTYPE), triu)[:nq]
            drop = eq & ((run + incl) > need)
            key_scr[c] = jnp.where(drop, KEY_NEG_INF, k)
            return run + incl[:, LANES - 1:LANES]

        lax.fori_loop(0, nblk, body, jnp.zeros((nq, 1), f32))


def _dsa_prompt_kernel(qs_ref, qi_ref, kiwi_ref, ki2_ref, ks_ref, vst_ref, sg_ref, o_ref,
                       key_scr, qm_scr, wt_scr, qst_scr, acc_scr, *, nh, topk, tc):
    qb = pl.program_id(1)
    tq = LANES
    nch = (qb * tq + tq + tc - 1) // tc

    lo = _iota((tq, LANES), 1) < D_IDX
    for h in range(H_IDX):
        chunk = qi_ref[:, (h // 2) * LANES:(h // 2 + 1) * LANES]
        keep = lo if h % 2 == 0 else jnp.logical_not(lo)
        qm_scr[h * tq:(h + 1) * tq, :] = jnp.where(keep, chunk, jnp.zeros_like(chunk))
    wt_scr[...] = kiwi_ref[...].T
    for h in range(nh):
        qst_scr[h * tq:(h + 1) * tq, :] = qs_ref[:, h * LANES:(h + 1) * LANES]

    tpos = qb * tq + _iota((1, LANES), 1)

    def idx_body(c, carry):
        sl = pl.ds(pl.multiple_of(c * tc, tc), tc)
        kc = ki2_ref[sl, :]
        acc = jnp.zeros((tc, LANES), f32)
        for hp in range(H_IDX // 2):
            sc = _dot_nt(kc, qm_scr[2 * hp * tq:(2 * hp + 2) * tq, :])
            for k in range(2):
                h = 2 * hp + k
                acc = acc + jnp.maximum(sc[:, k * tq:(k + 1) * tq], 0.0) * wt_scr[D_IDX + h:D_IDX + h + 1, :]
        acc = jnp.where(acc == 0.0, 0.0, acc)
        kpos = c * tc + _iota((tc, LANES), 0)
        acc = jnp.where(kpos <= tpos, acc, -jnp.inf)
        key_scr[sl, :] = _f2key(acc)
        return carry

    lax.fori_loop(0, nch, idx_body, 0)

    kth, count_ge = _kth_largest_key_sub(key_scr, nch, tc, topk)
    _demote_surplus_ties_sub(key_scr, nch, tc, topk, kth, count_ge)
    thr = jnp.maximum(kth, KEY_NEG_INF + 1)

    acc_scr[...] = jnp.zeros_like(acc_scr)

    def att_body(c, carry):
        m, l = carry
        sl = pl.ds(pl.multiple_of(c * tc, tc), tc)
        s = _dot_nt(ks_ref[sl, :], qst_scr[...])
        msk = key_scr[sl, :] >= thr
        s = jnp.where(jnp.concatenate([msk] * nh, axis=1), s, NEG_BIG)
        m_new = jnp.maximum(m, jnp.max(s, axis=0, keepdims=True))
        alpha = jnp.exp2(m - m_new)
        p = jnp.exp2(s - m_new)
        l = alpha * l + jnp.sum(p, axis=0, keepdims=True)
        acc_scr[...] = alpha * acc_scr[...] + _dot(vst_ref[c], p.astype(MXU_DTYPE))
        return m_new, l

    init = (jnp.full((1, nh * tq), NEG_BIG, f32), jnp.zeros((1, nh * tq), f32))
    _, l = lax.fori_loop(0, nch, att_body, init)
    out_t = acc_scr[...] / l
    for h in range(nh):
        sl = slice(h * LANES, (h + 1) * LANES)
        o_ref[:, sl] = (out_t[:, sl].T * sg_ref[:, sl].astype(f32)).astype(o_ref.dtype)


def _dsa_prompt(qs, qi, kiwi, ki2, ks, vst, sg, *, topk, tc=256):
    b, t, w = qs.shape
    nh = w // LANES
    tq = LANES
    assert t % tc == 0 and tc % tq == 0
    return pl.pallas_call(
        functools.partial(_dsa_prompt_kernel, nh=nh, topk=topk, tc=tc),
        grid=(b, t // tq),
        in_specs=[
            pl.BlockSpec((None, tq, w), lambda bb, i: (bb, i, 0)),
            pl.BlockSpec((None, tq, H_IDX * D_IDX), lambda bb, i: (bb, i, 0)),
            pl.BlockSpec((None, tq, LANES), lambda bb, i: (bb, i, 0)),
            pl.BlockSpec((None, t, LANES), lambda bb, i: (bb, 0, 0)),
            pl.BlockSpec((None, t, LANES), lambda bb, i: (bb, 0, 0)),
            pl.BlockSpec((None, t // tc, LANES, tc), lambda bb, i: (bb, 0, 0, 0)),
            pl.BlockSpec((None, tq, w), lambda bb, i: (bb, i, 1)),
        ],
        out_specs=pl.BlockSpec((None, tq, w), lambda bb, i: (bb, i, 0)),
        out_shape=jax.ShapeDtypeStruct((b, t, w), MXU_DTYPE),
        scratch_shapes=[
            pltpu.VMEM((t, LANES), i32),
            pltpu.VMEM((H_IDX * tq, LANES), MXU_DTYPE),
            pltpu.VMEM((LANES, LANES), f32),
            pltpu.VMEM((nh * tq, LANES), MXU_DTYPE),
            pltpu.VMEM((LANES, nh * tq), f32),
        ],
        compiler_params=_cparams(("arbitrary", "arbitrary")),
        name="dsa_prompt",
    )(qs, qi, kiwi, ki2, ks, vst, sg)


def _mem_attn_kernel(q_ref, k_ref, v_ref, sg_ref, o_ref):
    s = _dot_nt(q_ref[...], k_ref[...].astype(MXU_DTYPE))
    m = jnp.max(s, axis=1, keepdims=True)
    p = jnp.exp2(s - m)
    l = jnp.sum(p, axis=1, keepdims=True)
    o = _dot(p.astype(MXU_DTYPE), v_ref[...].astype(MXU_DTYPE)) / l
    o_ref[...] = (o * sg_ref[...].astype(f32)).astype(o_ref.dtype)


def _mem_attn(q, k, v, sg, *, layer, sg_col0, tq):
    b, t, w = q.shape
    nh = w // D_MEM
    nm = k.shape[2]
    return pl.pallas_call(
        _mem_attn_kernel,
        grid=(b, t // tq, nh),
        in_specs=[
            pl.BlockSpec((None, tq, D_MEM), lambda bb, i, h: (bb, i, h)),
            pl.BlockSpec((None, None, nm, D_MEM), lambda bb, i, h: (layer, bb, 0, h)),
            pl.BlockSpec((None, None, nm, D_MEM), lambda bb, i, h: (layer, bb, 0, h)),
            pl.BlockSpec((None, tq, D_MEM), lambda bb, i, h: (bb, i, sg_col0 + h)),
        ],
        out_specs=pl.BlockSpec((None, tq, D_MEM), lambda bb, i, h: (bb, i, h)),
        out_shape=jax.ShapeDtypeStruct((b, t, w), MXU_DTYPE),
        compiler_params=_cparams(("arbitrary", "arbitrary", "arbitrary")),
        name="mem_attn",
    )(q, k, v, sg)


def _diff_sample_kernel(pt_ref, lam_ref, qm_ref, *refs, nh, nt, pps, nsteps, post_scale):
    kt_refs = refs[:pps]
    v_refs = refs[pps:2 * pps]
    knewt_ref, vnew_ref, g_ref, sg_ref, o_ref, m_scr, l_scr, acc_scr = refs[2 * pps:]
    s_id = pl.program_id(1)
    nr = 2 * nt

    @pl.when(s_id == 0)
    def _():
        m_scr[...] = jnp.full_like(m_scr, NEG_BIG)
        l_scr[...] = jnp.zeros_like(l_scr)
        acc_scr[...] = jnp.zeros_like(acc_scr)

    def update(kts, vs, masked):
        s = jnp.concatenate(
            [jnp.concatenate([_dot(qm_ref[h * nr:(h + 1) * nr, :], kt[h].astype(MXU_DTYPE)) for kt in kts], axis=1)
             for h in range(nh)], axis=0)
        if masked:
            tpos = _iota(s.shape, 0) % nt
            s = jnp.where(_iota(s.shape, 1) <= tpos, s, NEG_BIG)
        m = m_scr[...]
        m_new = jnp.maximum(m, jnp.max(s, axis=1, keepdims=True))
        alpha = jnp.exp2(m - m_new)
        p = jnp.exp2(s - m_new)
        l_scr[...] = alpha * l_scr[...] + jnp.sum(p, axis=1, keepdims=True)
        pb = p.astype(MXU_DTYPE)
        kw = s.shape[1] // len(kts)
        pv = []
        for h in range(nh):
            acc = None
            for r, v in enumerate(vs):
                d = _dot(pb[h * nr:(h + 1) * nr, r * kw:(r + 1) * kw], v[h].astype(MXU_DTYPE))
                acc = d if acc is None else acc + d
            pv.append(acc)
        acc_scr[...] = alpha * acc_scr[...] + jnp.concatenate(pv, axis=0)
        m_scr[...] = m_new

    @pl.when(s_id < nsteps)
    def _():
        update(kt_refs, v_refs, False)

    @pl.when(s_id == nsteps)
    def _():
        update([knewt_ref], [vnew_ref], True)
        o = acc_scr[...] / l_scr[...]
        for h in range(nh):
            od = o[h * nr:h * nr + nt] - lam_ref[0] * o[h * nr + nt:(h + 1) * nr]
            sl = slice(h * LANES, (h + 1) * LANES)
            o_ref[:, sl] = _sub_rmsnorm_gate(od, g_ref[...], post_scale, sg_ref[:, sl]).astype(o_ref.dtype)


def _diff_sample(page_table, lam, qm, cache_kt, cache_v, knewt, vnew, g_sub, sg, *, layer, nt, post_scale):
    db, rows, _ = qm.shape
    nh = cache_v.shape[2]
    page = cache_v.shape[3]
    npages = page_table.shape[1]
    pps = math.gcd(PAGES_PER_STEP, npages)
    nsteps = npages // pps
    w = nh * LANES

    def page_map(r):
        def f(bb, s, pt):
            return (layer, pt[bb, jnp.minimum(s * pps + r, npages - 1)], 0, 0, 0)
        return f

    kt_specs = [pl.BlockSpec((None, None, nh, LANES, page), page_map(r)) for r in range(pps)]
    v_specs = [pl.BlockSpec((None, None, nh, page, LANES), page_map(r)) for r in range(pps)]
    grid_spec = pltpu.PrefetchScalarGridSpec(
        num_scalar_prefetch=1,
        grid=(db, nsteps + 1),
        in_specs=[
            pl.BlockSpec(memory_space=pltpu.SMEM),
            pl.BlockSpec((None, rows, LANES), lambda bb, s, pt: (bb, 0, 0)),
            *kt_specs, *v_specs,
            pl.BlockSpec((None, nh, LANES, page), lambda bb, s, pt: (bb, 0, 0, 0)),
            pl.BlockSpec((None, nh, page, LANES), lambda bb, s, pt: (bb, 0, 0, 0)),
            pl.BlockSpec((1, LANES), lambda bb, s, pt: (0, 0)),
            pl.BlockSpec((None, nt, w), lambda bb, s, pt: (bb, 0, 0)),
        ],
        out_specs=pl.BlockSpec((None, nt, w), lambda bb, s, pt: (bb, 0, 0)),
        scratch_shapes=[
            pltpu.VMEM((rows, 1), f32),
            pltpu.VMEM((rows, 1), f32),
            pltpu.VMEM((rows, LANES), f32),
        ],
    )
    return pl.pallas_call(
        functools.partial(_diff_sample_kernel, nh=nh, nt=nt, pps=pps, nsteps=nsteps, post_scale=post_scale),
        grid_spec=grid_spec,
        out_shape=jax.ShapeDtypeStruct((db, nt, w), MXU_DTYPE),
        compiler_params=_cparams(("arbitrary", "arbitrary")),
        name="diff_sample",
    )(page_table, lam, qm, *([cache_kt] * pps), *([cache_v] * pps), knewt, vnew, g_sub, sg)


def _dsa_sample_kernel(pt_ref, qs_ref, qi_ref, w_ref, *refs, nt, pps, nsteps, topk, page):
    kit_refs = refs[:pps]
    ks_refs = refs[pps:2 * pps]
    vs_refs = refs[2 * pps:3 * pps]
    kint_ref, ksn_ref, vsn_ref, sg_ref, o_ref, key_scr, s_scr, v_scr = refs[3 * pps:]
    s_id = pl.program_id(1)
    gsz = LANES // nt

    def process(blk, kit, ks, vs, causal_new):
        sc = _dot(qi_ref[...], kit.astype(MXU_DTYPE))
        val = jnp.maximum(sc, 0.0) * w_ref[...]
        acc = jnp.sum(val.reshape(gsz, nt, page), axis=0)
        acc = jnp.where(acc == 0.0, 0.0, acc)
        if causal_new:
            acc = jnp.where(_iota(acc.shape, 1) <= _iota(acc.shape, 0), acc, -jnp.inf)
        key_scr[blk] = _f2key(acc)
        s_scr[blk] = _dot_nt(qs_ref[...], ks.astype(MXU_DTYPE))
        v_scr[blk] = vs.astype(MXU_DTYPE)

    @pl.when(s_id < nsteps)
    def _():
        for r in range(pps):
            process(s_id * pps + r, kit_refs[r][...], ks_refs[r][...], vs_refs[r][...], False)

    @pl.when(s_id == nsteps)
    def _():
        nblk = nsteps * pps + 1
        process(nblk - 1, kint_ref[...], ksn_ref[...], vsn_ref[...], True)

        keys = jnp.concatenate([key_scr[c] for c in range(nblk)], axis=1)

        def count_ge_all(cand):
            return jnp.sum(jnp.where(keys >= cand, 1.0, 0.0), axis=1, keepdims=True)

        def bit_body(it, prefix):
            cand = prefix + lax.shift_left(jnp.int32(1), 31 - it)
            return jnp.where(count_ge_all(cand) >= topk, cand, prefix)

        kth = lax.fori_loop(0, 32, bit_body, jnp.full((nt, 1), INT_MIN, i32))
        _demote_surplus_ties_lane(key_scr, nblk, nt, topk, kth, count_ge_all)
        thr = jnp.maximum(kth, KEY_NEG_INF + 1)

        unroll = max(u for u in (5, 4, 3, 2, 1) if nblk % u == 0)

        def sel_scores(c):
            msk = key_scr[c] >= thr
            return msk[None], s_scr[c].reshape(gsz, nt, page)

        def max_body(it, mx):
            for u in range(unroll):
                msk, s3 = sel_scores(it * unroll + u)
                mx = jnp.maximum(mx, jnp.where(msk, s3, NEG_BIG))
            return mx

        mx = lax.fori_loop(0, nblk // unroll, max_body, jnp.full((gsz, nt, page), NEG_BIG, f32))
        m = jnp.max(mx, axis=2, keepdims=True)

        def att_body(it, carry):
            lacc, acc = carry
            for u in range(unroll):
                c = it * unroll + u
                msk, s3 = sel_scores(c)
                p = jnp.where(msk, jnp.exp2(s3 - m), 0.0)
                lacc = lacc + p
                acc = acc + _dot(p.reshape(gsz * nt, page).astype(MXU_DTYPE), v_scr[c])
            return lacc, acc

        lacc, acc = lax.fori_loop(0, nblk // unroll, att_body,
                                  (jnp.zeros((gsz, nt, page), f32), jnp.zeros((LANES, D_DSA), f32)))
        l = jnp.sum(lacc, axis=2, keepdims=True).reshape(gsz * nt, 1)
        o_ref[...] = (acc / l * sg_ref[...].astype(f32)).astype(o_ref.dtype)


def _dsa_sample(page_table, qs_rows, qi_rows, wcol, cache_kit, cache_ks, cache_vs, kit_new, ks_new, vs_new, sg_perm,
                *, layer, nt, topk):
    db = qs_rows.shape[0]
    page = cache_ks.shape[2]
    npages = page_table.shape[1]
    pps = math.gcd(PAGES_PER_STEP, npages)
    nsteps = npages // pps
    nblk = npages + 1
    assert page == LANES

    def page_map(r):
        def f(bb, s, pt):
            return (layer, pt[bb, jnp.minimum(s * pps + r, npages - 1)], 0, 0)
        return f

    def pspecs(shape):
        return [pl.BlockSpec((None, None) + shape, page_map(r)) for r in range(pps)]

    per_b = lambda shape: pl.BlockSpec((None,) + shape, lambda bb, s, pt: (bb, 0, 0))
    grid_spec = pltpu.PrefetchScalarGridSpec(
        num_scalar_prefetch=1,
        grid=(db, nsteps + 1),
        in_specs=[
            per_b((LANES, D_DSA)), per_b((LANES, D_IDX)), per_b((LANES, 1)),
            *pspecs((D_IDX, page)), *pspecs((page, D_DSA)), *pspecs((page, D_DSA)),
            per_b((D_IDX, page)), per_b((page, D_DSA)), per_b((page, D_DSA)),
            per_b((LANES, D_DSA)),
        ],
        out_specs=per_b((LANES, D_DSA)),
        scratch_shapes=[
            pltpu.VMEM((nblk, nt, page), i32),
            pltpu.VMEM((nblk, LANES, page), f32),
            pltpu.VMEM((nblk, page, D_DSA), MXU_DTYPE),
        ],
    )
    return pl.pallas_call(
        functools.partial(_dsa_sample_kernel, nt=nt, pps=pps, nsteps=nsteps, topk=topk, page=page),
        grid_spec=grid_spec,
        out_shape=jax.ShapeDtypeStruct((db, LANES, D_DSA), MXU_DTYPE),
        compiler_params=_cparams(("arbitrary", "arbitrary")),
        name="dsa_sample",
    )(page_table, qs_rows, qi_rows, wcol, *([cache_kit] * pps), *([cache_ks] * pps), *([cache_vs] * pps),
      kit_new, ks_new, vs_new, sg_perm)


def _rope_tables(pos, head_dim):
    half = head_dim // 2
    lane = jnp.arange(LANES)
    inv = ROPE_THETA ** (-(lane % half).astype(f32) / half)
    ang = pos.astype(f32)[:, None] * inv[None, :]
    sign = jnp.where((lane % head_dim) < half, -1.0, 1.0).astype(f32)
    return jnp.cos(ang), jnp.sin(ang) * sign[None, :]


def _tile_gain(g, n):
    return jnp.tile(g, n // g.shape[0]).reshape(1, n).astype(f32)


def _mixer_inputs(x2d, pos_tab, w_in_t, layer, seg, gl, *, tm, stacked=None):
    cos64, sin64, cos128, sin128 = pos_tab
    m = x2d.shape[0]
    h = _rmsnorm(x2d, gl["g_in"], min(tm, 256))
    tn = 512
    proj = functools.partial(_proj, h, tm=tm, nt=True)
    wseg = lambda name: _win_desc(w_in_t, layer, seg[name][0], seg[name][1] - seg[name][0], tn)
    rows = lambda n, dt: _rows_out(m, n, dt, tm, tn)
    rope64 = [("row", cos64), ("row", sin64)]
    rope128 = [("row", cos128), ("row", sin128)]
    nqd = seg["qd"][1] - seg["qd"][0]
    nqs = seg["qs"][1] - seg["qs"][0]
    nqm = seg["qm"][1] - seg["qm"][0]
    ngate = seg["gate"][1] - seg["gate"][0]
    norm_rope64 = functools.partial(_chunks_norm_rope, gsize=D_DH, half=D_DH // 2)
    gq64 = [("col", _tile_gain(gl["g_q_diff"], nqd))] + rope64
    gk64 = [("col", _tile_gain(gl["g_k_diff"], nqd))] + rope64
    gq128 = [("col", _tile_gain(gl["g_q_dsa"], nqs))] + rope128
    out = {}
    (out["qd"],) = proj(wseg("qd"), functools.partial(norm_rope64, scale=D_DH ** -0.5 * LOG2E), _store_rows,
                        [rows(nqd, MXU_DTYPE)], gq64, name="proj_qd")
    (out["qs"],) = proj(wseg("qs"), functools.partial(_chunks_norm_rope, gsize=D_DSA, half=D_DSA // 2,
                                                      scale=D_DSA ** -0.5 * LOG2E), _store_rows,
                        [rows(nqs, MXU_DTYPE)], gq128, name="proj_qs")
    (out["qi"],) = proj(wseg("qi"), functools.partial(_chunks_rope, half=D_IDX // 2, scale=D_IDX ** -0.5 * H_IDX ** -0.5),
                        _store_rows, [rows(H_IDX * D_IDX, MXU_DTYPE)], rope64, name="proj_qi")
    (out["qm"],) = proj(wseg("qm"), functools.partial(_chunks_norm256, scale=D_MEM ** -0.5 * LOG2E), _store_rows,
                        [rows(nqm, MXU_DTYPE)], [("col", _tile_gain(gl["g_q_mem"], nqm))], name="proj_qm")
    (out["sg"],) = proj(wseg("gate"), _chunks_silu, _store_rows, [rows(ngate, MXU_DTYPE)], name="proj_gate")

    ks0, vs0, ki0 = seg["ks"][0], seg["vs"][0], seg["ki"][0]
    assert vs0 - ks0 == LANES and ks0 % LANES == 0 and ki0 % LANES == 0
    skip = (ki0 - vs0) // LANES - 1
    w_ksv = dict(array=w_in_t, n=3 * LANES, tn=LANES,
                 spec=pl.BlockSpec((None, LANES, w_in_t.shape[2]),
                                   lambda i, j: (layer, ks0 // LANES + j + jnp.where(j == 2, skip, 0), 0)))
    g_kiwi = jnp.concatenate([gl["g_k_idx"], jnp.ones((LANES - D_IDX,), f32)]).reshape(1, LANES)
    aux_ksv = [("const", gl["g_k_dsa"].reshape(1, D_DSA))] + rope128 + [("const", g_kiwi)] + rope64
    blk = lambda dt: dict(shape=(m, LANES), dtype=dt, block=(tm, LANES), index=lambda i, j: (i, 0), alias=None)
    if stacked is None:
        out["kd"], out["kd_c"] = proj(wseg("kd"), functools.partial(norm_rope64, scale=1.0), _store_rows,
                                      [rows(nqd, f32), rows(nqd, MXU_DTYPE)], gk64, name="proj_kd")
        out["vd"], out["vd_c"] = proj(wseg("vd"), _chunks_raw, _store_rows,
                                      [rows(nqd, f32), rows(nqd, MXU_DTYPE)], name="proj_vd")
        out["ks"], out["ks_c"], out["vs"], out["vs_c"], out["kiwi"], out["ki2"] = proj(
            w_ksv, _chunks_ks_vs_kiwi, _store_ks_vs_kiwi,
            [blk(f32), blk(MXU_DTYPE), blk(f32), blk(MXU_DTYPE), blk(f32), blk(MXU_DTYPE)], aux_ksv,
            name="proj_ks_vs_kiwi")
        return out

    l = layer
    b, t = stacked["b"], stacked["t"]
    nrb = t // tm
    nh = nqd // LANES
    hb = _col_tile(nqd, tn) // LANES
    vkb = stacked["diff_tk"]
    assert tm % vkb == 0
    depth = stacked["pdk"].shape[0]
    out["pdk"], out["kd_c"] = proj(
        wseg("kd"), functools.partial(norm_rope64, scale=1.0), _store_kd_prompt,
        [dict(shape=(depth, b, nh, LANES, t), dtype=f32, block=(None, None, hb, LANES, tm),
              index=lambda i, j: (l, i // nrb, j, 0, i % nrb), alias=stacked["pdk"]),
         rows(nqd, MXU_DTYPE)], gk64, name="proj_kd")
    out["pdv"], out["vd_c"] = proj(
        wseg("vd"), _chunks_raw, _store_vd_prompt,
        [dict(shape=(depth, b, nh, t, LANES), dtype=f32, block=(None, None, hb, tm, LANES),
              index=lambda i, j: (l, i // nrb, j, i % nrb, 0), alias=stacked["pdv"]),
         dict(shape=(b, nh, t // vkb, LANES, vkb), dtype=MXU_DTYPE, block=(None, hb, tm // vkb, LANES, vkb),
              index=lambda i, j: (i // nrb, j, i % nrb, 0, 0), alias=None)], name="proj_vd")
    tok_major = lambda key: dict(shape=(depth, b, t, D_DSA), dtype=f32, block=(None, None, tm, D_DSA),
                                 index=lambda i, j: (l, i // nrb, i % nrb, 0), alias=stacked[key])
    out["psk"], out["ks_c"], out["psv"], out["vs_c"], out["kiwi"], out["ki2"], out["pik"] = proj(
        w_ksv, _chunks_ks_vs_kiwi, _store_ks_vs_kiwi,
        [tok_major("psk"), blk(MXU_DTYPE), tok_major("psv"), blk(MXU_DTYPE), blk(f32), blk(MXU_DTYPE),
         dict(shape=(depth, b, D_IDX, t), dtype=f32, block=(None, None, D_IDX, tm),
              index=lambda i, j: (l, i // nrb, 0, i % nrb), alias=stacked["pik"])], aux_ksv, name="proj_ks_vs_kiwi")
    return out


def kernel(x_prompt, x_sample, mem_prompt, cache_diff_k, cache_diff_v, cache_dsa_k, cache_dsa_v, cache_idx_k,
           cache_mem_k, cache_mem_v, page_table, w_in, w_out, w_mem_kv, g_in, g_mem, g_q_diff, g_k_diff,
           g_sub_diff, lam_q1, lam_k1, lam_q2, lam_k2, g_q_dsa, g_k_dsa, g_k_idx, g_q_mem, g_k_mem):
    depth = w_in.shape[0]
    b, t, d = x_prompt.shape
    db, nt, _ = x_sample.shape
    n_mem = mem_prompt.shape[1]
    n_phys, page = cache_dsa_k.shape[1], cache_dsa_k.shape[2]
    npages = page_table.shape[1]
    past = npages * page
    h_diff = cache_diff_k.shape[3]
    w_diff = h_diff * 2 * D_DH
    h_dsa = (3 * d // 8) // D_DSA
    w_dsa = h_dsa * D_DSA
    h_mem = cache_mem_k.shape[3]
    w_mem = h_mem * D_MEM
    gsz = LANES // nt
    assert w_diff + w_dsa + w_mem == d and w_diff == w_dsa and LANES % nt == 0 and h_dsa <= gsz and gsz == H_IDX
    topk_p = min(TOPK_MAX, t // 4)
    topk_s = min(TOPK_MAX, (past + nt) // 4)
    m_p = b * t
    m_s = db * nt
    tm_p = min(1024, t)
    assert t % tm_p == 0

    widths = (w_diff, w_diff, w_diff, w_dsa, D_DSA, D_DSA, H_IDX * D_IDX, D_IDX, H_IDX, w_mem, d)
    offs = [0]
    for wdt in widths:
        offs.append(offs[-1] + wdt)
    names = ("qd", "kd", "vd", "qs", "ks", "vs", "qi", "ki", "wi", "qm", "gate")
    seg = {n: (offs[k], offs[k + 1]) for k, n in enumerate(names)}

    pos_p = jnp.arange(t, dtype=i32)
    pos_s = jnp.tile(past + jnp.arange(nt, dtype=i32), db)
    tab_p = _rope_tables(pos_p, D_DH) + _rope_tables(pos_p, D_DSA)
    tab_s = _rope_tables(pos_s, D_DH) + _rope_tables(pos_s, D_DSA)

    cdkt = cache_diff_k.transpose(0, 1, 3, 4, 5, 2).reshape(depth, n_phys, h_diff, 2 * D_DH, page)
    cdv = cache_diff_v.transpose(0, 1, 3, 2, 4)
    ckit = cache_idx_k.transpose(0, 1, 3, 2)
    cmk = cache_mem_k.reshape(depth, db, n_mem, w_mem)
    cmv = cache_mem_v.reshape(depth, db, n_mem, w_mem)
    w_in_t = jnp.swapaxes(w_in, 1, 2)

    x_p = x_prompt.reshape(m_p, d)
    x_s = x_sample.reshape(m_s, d)
    mem2d = mem_prompt.reshape(b * n_mem, d)
    stk = dict(pdk=jnp.zeros((depth, b, h_diff, 2 * D_DH, t), f32), pdv=jnp.zeros((depth, b, h_diff, t, 2 * D_DH), f32),
               psk=jnp.zeros((depth, b, t, D_DSA), f32), psv=jnp.zeros((depth, b, t, D_DSA), f32),
               pik=jnp.zeros((depth, b, D_IDX, t), f32))
    outs = {k: [] for k in ("pmk", "pmv", "sdk", "sdv", "ssk", "ssv", "sik")}

    for l in range(depth):
        gl = dict(g_in=g_in[l], g_q_diff=g_q_diff[l], g_k_diff=g_k_diff[l], g_q_dsa=g_q_dsa[l],
                  g_k_dsa=g_k_dsa[l], g_k_idx=g_k_idx[l], g_q_mem=g_q_mem[l])
        lam_init = 0.8 - 0.6 * math.exp(-0.3 * l)
        lam = (jnp.exp(jnp.sum(lam_q1[l] * lam_k1[l])) - jnp.exp(jnp.sum(lam_q2[l] * lam_k2[l])) + lam_init)
        lam = lam.astype(f32).reshape(1)
        g_sub = g_sub_diff[l].reshape(1, 2 * D_DH)
        post = 1.0 - lam_init

        diff_tq, diff_tk = min(DIFF_TQ, t), min(DIFF_TK, t)
        tp = _mixer_inputs(x_p, tab_p, w_in_t, l, seg, gl, tm=tm_p, stacked=dict(stk, b=b, t=t, diff_tk=diff_tk))
        for key in ("pdk", "pdv", "psk", "psv", "pik"):
            stk[key] = tp[key]
        hm = _rmsnorm(mem2d, g_mem[l], 256)
        mrows = lambda dt: [_rows_out(b * n_mem, w_mem, dt, b * n_mem, 512)]
        mk, = _proj(hm, _wcols_desc(w_mem_kv, l, 0, w_mem, 512), functools.partial(_chunks_norm256, scale=1.0),
                    _store_rows, mrows(f32), [("col", _tile_gain(g_k_mem[l], w_mem))], tm=b * n_mem, nt=False,
                    name="proj_mk")
        mv, = _proj(hm, _wcols_desc(w_mem_kv, l, w_mem, w_mem, 512), _chunks_raw, _store_rows, mrows(f32),
                    tm=b * n_mem, nt=False, name="proj_mv")
        r3 = lambda a: a.reshape(b, t, a.shape[-1])
        sg = r3(tp["sg"])
        od = _diff_prompt(lam, r3(tp["qd"]), r3(tp["kd_c"]), tp["vd_c"], g_sub, sg, post_scale=post,
                          tq=diff_tq, tk=diff_tk)
        tc = 256
        vst = tp["vs_c"].reshape(b, t // tc, tc, D_DSA).transpose(0, 1, 3, 2)
        os_ = _dsa_prompt(r3(tp["qs"]), r3(tp["qi"]), r3(tp["kiwi"]), r3(tp["ki2"]), r3(tp["ks_c"]), vst, sg,
                          topk=topk_p, tc=tc)
        om = _mem_attn(r3(tp["qm"]), mk.reshape(1, b, n_mem, w_mem), mv.reshape(1, b, n_mem, w_mem), sg,
                       layer=0, sg_col0=(w_diff + w_dsa) // D_MEM, tq=min(512, t))
        x_p = _outproj(od.reshape(m_p, w_diff), os_.reshape(m_p, w_dsa), om.reshape(m_p, w_mem), w_out, x_p,
                       layer=l, tm=tm_p, tn=512)
        outs["pmk"].append(mk.reshape(b, n_mem, h_mem, D_MEM))
        outs["pmv"].append(mv.reshape(b, n_mem, h_mem, D_MEM))

        ts = _mixer_inputs(x_s, tab_s, w_in_t, l, seg, gl, tm=m_s)
        s3 = lambda a: a.reshape(db, nt, a.shape[-1])
        sg_s = s3(ts["sg"])
        tokpad = lambda a, axis: jnp.pad(a, [(0, page - nt) if ax == axis else (0, 0) for ax in range(a.ndim)])
        q5 = ts["qd"].reshape(db, nt, h_diff, 2, D_DH).transpose(0, 2, 3, 1, 4)
        eye_c = jnp.eye(2, dtype=MXU_DTYPE)
        qm = (q5[:, :, :, :, None, :] * eye_c[None, None, :, None, :, None]).reshape(db, h_diff * 2 * nt, 2 * D_DH)
        kd4 = ts["kd"].reshape(db, nt, h_diff, 2 * D_DH)
        vd4 = ts["vd"].reshape(db, nt, h_diff, 2 * D_DH)
        od_s = _diff_sample(page_table, lam, qm, cdkt, cdv, tokpad(kd4.transpose(0, 2, 3, 1), 3),
                            tokpad(vd4.transpose(0, 2, 1, 3), 2), g_sub, sg_s, layer=l, nt=nt, post_scale=post)
        qs4 = ts["qs"].reshape(db, nt, h_dsa, D_DSA)
        slot_major = lambda a: a.transpose(0, 2, 1, 3).reshape(db, LANES, a.shape[-1])
        qs_rows = slot_major(jnp.pad(qs4, ((0, 0), (0, 0), (0, gsz - h_dsa), (0, 0))))
        qi_rows = slot_major(ts["qi"].reshape(db, nt, H_IDX, D_IDX))
        wcol = slot_major(ts["kiwi"][:, D_IDX:D_IDX + H_IDX].reshape(db, nt, H_IDX, 1))
        sg_dsa = sg_s[:, :, w_diff:w_diff + w_dsa].reshape(db, nt, h_dsa, D_DSA)
        sg_perm = slot_major(jnp.pad(sg_dsa, ((0, 0), (0, 0), (0, gsz - h_dsa), (0, 0))))
        ki_new = s3(ts["kiwi"][:, :D_IDX])
        os_s = _dsa_sample(page_table, qs_rows, qi_rows, wcol, ckit, cache_dsa_k, cache_dsa_v,
                           tokpad(ki_new.transpose(0, 2, 1), 2), tokpad(s3(ts["ks"]), 1), tokpad(s3(ts["vs"]), 1),
                           sg_perm, layer=l, nt=nt, topk=topk_s)
        os_s = os_s.reshape(db, gsz, nt, D_DSA)[:, :h_dsa].transpose(0, 2, 1, 3).reshape(m_s, w_dsa)
        om_s = _mem_attn(s3(ts["qm"]), cmk, cmv, sg_s, layer=l, sg_col0=(w_diff + w_dsa) // D_MEM, tq=nt)
        x_s = _outproj(od_s.reshape(m_s, w_diff), os_s, om_s.reshape(m_s, w_mem), w_out, x_s, layer=l, tm=m_s,
                       tn=512)
        outs["sdk"].append(ts["kd"].reshape(db, nt, h_diff, 2, D_DH))
        outs["sdv"].append(ts["vd"].reshape(db, nt, h_diff, 2 * D_DH))
        outs["ssk"].append(ts["ks"].reshape(db, nt, D_DSA))
        outs["ssv"].append(ts["vs"].reshape(db, nt, D_DSA))
        outs["sik"].append(ki_new)

    st = lambda k: jnp.stack(outs[k])
    p_diff_k = stk["pdk"].reshape(depth, b, h_diff, 2, D_DH, t).transpose(0, 1, 5, 2, 3, 4)
    p_diff_v = stk["pdv"].transpose(0, 1, 3, 2, 4)
    p_idx_k = stk["pik"].transpose(0, 1, 3, 2)
    return (x_p.reshape(b, t, d), x_s.reshape(db, nt, d),
            p_diff_k, p_diff_v, stk["psk"], stk["psv"], p_idx_k, st("pmk"), st("pmv"),
            st("sdk"), st("sdv"), st("ssk"), st("ssv"), st("sik"))
```

```python
import functools
import math

import jax
import jax.numpy as jnp
from jax import lax
from jax.experimental import pallas as pl
from jax.experimental.pallas import tpu as pltpu

EPS = 1e-6
ROPE_THETA = 10000.0
TOPK_MAX = 256
LANES = 128
D_DH = 64
D_DSA = 128
D_IDX = 64
H_IDX = 16
D_MEM = 256
NEG_BIG = -1e30
KEY_NEG_INF = -2139095041
INT_MIN = -2147483648
MXU_DTYPE = jnp.bfloat16
VMEM_LIMIT_BYTES = 52 * 1024 * 1024
PAGES_PER_STEP = 8
DIFF_TQ = 512
DIFF_TK = 512
LOG2E = math.log2(math.e)

f32 = jnp.float32
i32 = jnp.int32


def _cparams(sem):
    return pltpu.CompilerParams(dimension_semantics=sem, vmem_limit_bytes=VMEM_LIMIT_BYTES)


def _dot(a, b):
    return jnp.dot(a, b, preferred_element_type=f32)


def _dot_nt(a, b):
    return lax.dot_general(a, b, (((1,), (1,)), ((), ())), preferred_element_type=f32)


def _iota(shape, dim):
    return lax.broadcasted_iota(i32, shape, dim)


def _group_mat(gsize, value):
    r = _iota((LANES, LANES), 0) // gsize
    c = _iota((LANES, LANES), 1) // gsize
    return jnp.where(r == c, value, 0.0).astype(MXU_DTYPE)


def _group_sum(x, gmat):
    return _dot(x.astype(MXU_DTYPE), gmat)


def _rope_chunk(n, cos, sin_signed, half):
    if 2 * half == LANES:
        rot = pltpu.roll(n, half, 1)
    else:
        first = (_iota(n.shape, 1) % (2 * half)) < half
        rot = jnp.where(first, pltpu.roll(n, LANES - half, 1), pltpu.roll(n, half, 1))
    return n * cos + rot * sin_signed


def _f2key(x):
    b = pltpu.bitcast(x, i32)
    return b ^ (lax.shift_right_arithmetic(b, 31) & 0x7FFFFFFF)


def _rmsnorm_kernel(x_ref, g_ref, o_ref):
    x = x_ref[...]
    ms = jnp.mean(x * x, axis=-1, keepdims=True)
    o_ref[...] = (x * lax.rsqrt(ms + EPS) * g_ref[...]).astype(o_ref.dtype)


def _rmsnorm(x, g, tm):
    m, d = x.shape
    return pl.pallas_call(
        _rmsnorm_kernel,
        grid=(m // tm,),
        in_specs=[pl.BlockSpec((tm, d), lambda i: (i, 0)), pl.BlockSpec((1, d), lambda i: (0, 0))],
        out_specs=pl.BlockSpec((tm, d), lambda i: (i, 0)),
        out_shape=jax.ShapeDtypeStruct((m, d), MXU_DTYPE),
        compiler_params=_cparams(("arbitrary",)),
        name="rmsnorm",
    )(x, g.reshape(1, d))


def _chunks_raw(z, aux, store):
    for c in range(z.shape[1] // LANES):
        store(c, z[:, c * LANES:(c + 1) * LANES])


def _chunks_silu(z, aux, store):
    for c in range(z.shape[1] // LANES):
        zc = z[:, c * LANES:(c + 1) * LANES]
        store(c, zc / (1.0 + jnp.exp(-zc)))


def _chunks_norm_rope(z, aux, store, *, gsize, half):
    gain, cos, sin = aux
    gmat = _group_mat(gsize, 1.0 / gsize)
    for c in range(z.shape[1] // LANES):
        sl = slice(c * LANES, (c + 1) * LANES)
        zc = z[:, sl]
        n = zc * lax.rsqrt(_group_sum(zc * zc, gmat) + EPS) * gain[:, sl]
        store(c, _rope_chunk(n, cos, sin, half))


def _chunks_rope(z, aux, store, *, half):
    cos, sin = aux
    for c in range(z.shape[1] // LANES):
        store(c, _rope_chunk(z[:, c * LANES:(c + 1) * LANES], cos, sin, half))


def _chunks_norm256(z, aux, store):
    (gain,) = aux
    gmat = _group_mat(LANES, 1.0 / D_MEM)
    for c in range(z.shape[1] // D_MEM):
        a = z[:, c * D_MEM:c * D_MEM + LANES]
        b = z[:, c * D_MEM + LANES:(c + 1) * D_MEM]
        inv = lax.rsqrt(_group_sum(a * a, gmat) + _group_sum(b * b, gmat) + EPS)
        for k, v in enumerate((a, b)):
            sl = slice(c * D_MEM + k * LANES, c * D_MEM + (k + 1) * LANES)
            store(2 * c + k, v * inv * gain[:, sl])


def _chunks_kiwi(z, aux, store):
    gain, cos, sin = aux
    gmat = _group_mat(D_IDX, 1.0 / D_IDX)
    n = z * lax.rsqrt(_group_sum(z * z, gmat) + EPS) * gain
    r = _rope_chunk(n, cos, sin, D_IDX // 2)
    lo = _iota(z.shape, 1) < D_IDX
    store(0, jnp.where(lo, r, z))
    r_lo = jnp.where(lo, r, 0.0)
    store(1, r_lo + pltpu.roll(r_lo, D_IDX, 1))


def _store_rows(outs, c, r):
    for o in outs:
        o[:, c * LANES:(c + 1) * LANES] = r.astype(o.dtype)


def _store_kd_prompt(outs, c, r):
    outs[0][c] = r.T
    outs[1][:, c * LANES:(c + 1) * LANES] = r.astype(outs[1].dtype)


def _store_vd_prompt(outs, c, r):
    outs[0][c] = r
    rt = r.T.astype(outs[1].dtype)
    kb = outs[1].shape[-1]
    for kk in range(r.shape[0] // kb):
        outs[1][c, kk] = rt[:, kk * kb:(kk + 1) * kb]


def _chunks_ks_vs_kiwi(z, aux, store):
    gain_k, cos128, sin128, gain_kiwi, cos64, sin64 = aux
    j = pl.program_id(1)

    @pl.when(j == 0)
    def _():
        _chunks_norm_rope(z, (gain_k, cos128, sin128), functools.partial(store, 0), gsize=D_DSA, half=D_DSA // 2)

    @pl.when(j == 1)
    def _():
        store(1, 0, z)

    @pl.when(j == 2)
    def _():
        _chunks_kiwi(z, (gain_kiwi, cos64, sin64), functools.partial(store, 2))


def _store_ks_vs_kiwi(outs, which, c, r):
    if which < 2:
        outs[2 * which][...] = r
        outs[2 * which + 1][...] = r.astype(outs[2 * which + 1].dtype)
    elif c == 0:
        outs[4][...] = r
        if len(outs) > 6:
            outs[6][...] = r.T[:D_IDX, :]
    else:
        outs[5][...] = r.astype(outs[5].dtype)


def _proj_kernel(*refs, chunk_fn, store_fn, n_aux, n_alias, nt):
    h_ref, w_ref = refs[:2]
    aux = [r[...] for r in refs[2:2 + n_aux]]
    outs = refs[2 + n_aux + n_alias:]
    w = w_ref[...].astype(MXU_DTYPE)
    z = _dot_nt(h_ref[...], w) if nt else _dot(h_ref[...], w)
    chunk_fn(z, aux, functools.partial(store_fn, outs))


def _win_desc(w_in_t, layer, off, n, tn):
    tn = _col_tile(n, tn)
    k = w_in_t.shape[2]
    if off % tn == 0:
        return dict(array=w_in_t, spec=pl.BlockSpec((None, tn, k), lambda i, j: (layer, off // tn + j, 0)), n=n, tn=tn)
    row0 = layer * w_in_t.shape[1] + off
    assert row0 % 8 == 0
    spec = pl.BlockSpec((pl.Element(tn), pl.Element(k)), lambda i, j: (pl.multiple_of(row0 + j * tn, 8), 0))
    return dict(array=w_in_t.reshape(-1, k), spec=spec, n=n, tn=tn)


def _wcols_desc(w, layer, off, n, tn):
    tn = _col_tile(n, tn)
    assert off % tn == 0
    return dict(array=w, spec=pl.BlockSpec((None, w.shape[1], tn), lambda i, j: (layer, 0, off // tn + j)), n=n, tn=tn)


def _col_tile(n, tn):
    return math.gcd(n, tn)


def _rows_out(m, n, dtype, tm, tn):
    tn = _col_tile(n, tn)
    return dict(shape=(m, n), dtype=dtype, block=(tm, tn), index=lambda i, j: (i, j), alias=None)


def _proj(h, wd, chunk_fn, store_fn, outs, aux=(), *, tm, nt, name):
    m, k = h.shape
    n, tn = wd["n"], wd["tn"]
    assert m % tm == 0 and tn % LANES == 0 and n % tn == 0
    in_specs = [pl.BlockSpec((tm, k), lambda i, j: (i, 0)), wd["spec"]]
    args = [h, wd["array"]]
    for kind, a in aux:
        if kind == "col":
            in_specs.append(pl.BlockSpec((1, tn), lambda i, j: (0, j)))
        elif kind == "const":
            in_specs.append(pl.BlockSpec((1, LANES), lambda i, j: (0, 0)))
        else:
            nrb = a.shape[0] // tm
            in_specs.append(pl.BlockSpec((tm, LANES), lambda i, j, nrb=nrb: (i % nrb, 0)))
        args.append(a)
    aliases = {}
    for k_out, o in enumerate(outs):
        if o["alias"] is not None:
            aliases[len(args)] = k_out
            in_specs.append(pl.BlockSpec(memory_space=pl.ANY))
            args.append(o["alias"])
    return pl.pallas_call(
        functools.partial(_proj_kernel, chunk_fn=chunk_fn, store_fn=store_fn, n_aux=len(aux),
                          n_alias=len(aliases), nt=nt),
        grid=(m // tm, n // tn),
        in_specs=in_specs,
        out_specs=[pl.BlockSpec(o["block"], o["index"]) for o in outs],
        out_shape=[jax.ShapeDtypeStruct(o["shape"], o["dtype"]) for o in outs],
        input_output_aliases=aliases,
        compiler_params=_cparams(("arbitrary", "arbitrary")),
        name=name,
    )(*args)


def _outproj_kernel(od_ref, os_ref, om_ref, w_ref, x_ref, o_ref, *, wd, ws):
    acc = _dot(od_ref[...], w_ref[0:wd, :].astype(MXU_DTYPE))
    acc += _dot(os_ref[...], w_ref[wd:wd + ws, :].astype(MXU_DTYPE))
    acc += _dot(om_ref[...], w_ref[wd + ws:, :].astype(MXU_DTYPE))
    o_ref[...] = x_ref[...] + acc


def _outproj(od, os_, om, w, x, *, layer, tm, tn):
    m, d = x.shape
    wd, ws, wm = od.shape[1], os_.shape[1], om.shape[1]
    return pl.pallas_call(
        functools.partial(_outproj_kernel, wd=wd, ws=ws),
        grid=(m // tm, d // tn),
        in_specs=[
            pl.BlockSpec((tm, wd), lambda i, j: (i, 0)),
            pl.BlockSpec((tm, ws), lambda i, j: (i, 0)),
            pl.BlockSpec((tm, wm), lambda i, j: (i, 0)),
            pl.BlockSpec((None, wd + ws + wm, tn), lambda i, j: (layer, 0, j)),
            pl.BlockSpec((tm, tn), lambda i, j: (i, j)),
        ],
        out_specs=pl.BlockSpec((tm, tn), lambda i, j: (i, j)),
        out_shape=jax.ShapeDtypeStruct((m, d), f32),
        compiler_params=_cparams(("arbitrary", "arbitrary")),
        name="outproj",
    )(od, os_, om, w, x)


def _sub_rmsnorm_gate(od, g, post_scale, sg):
    ms = jnp.mean(od * od, axis=-1, keepdims=True)
    return od * lax.rsqrt(ms + EPS) * g * post_scale * sg.astype(f32)


def _diff_prompt_kernel(pi_ref, pj_ref, lam_ref, q_ref, k_ref, vt_ref, g_ref, sg_ref, o_ref,
                        m_scr, l_scr, acc_scr, *, tq, tk, hb, post_scale):
    p_id = pl.program_id(2)
    i = pi_ref[p_id]
    j = pj_ref[p_id]

    @pl.when(j == 0)
    def _():
        m_scr[...] = jnp.full_like(m_scr, NEG_BIG)
        l_scr[...] = jnp.zeros_like(l_scr)
        acc_scr[...] = jnp.zeros_like(acc_scr)

    nchain = 2 * hb
    h2 = tq // 2
    split_diag = tq == tk and h2 % LANES == 0

    def masked_q(r):
        q = q_ref[:, (r // 2) * LANES:(r // 2 + 1) * LANES]
        lo = _iota(q.shape, 1) < D_DH
        zero = jnp.zeros_like(q)
        return jnp.where(lo, q, zero) if r % 2 == 0 else jnp.where(lo, zero, q)

    def keys(r):
        return k_ref[:, (r // 2) * LANES:(r // 2 + 1) * LANES]

    def scores(r):
        return _dot_nt(keys(r), masked_q(r))

    def scores_diag(r):
        qc, kc = masked_q(r), keys(r)
        return _dot_nt(kc[:h2], qc), _dot_nt(kc[h2:], qc[h2:])

    def causal(s, key0, qry0):
        return jnp.where(j * tk + key0 + _iota(s.shape, 0) <= i * tq + qry0 + _iota(s.shape, 1), s, NEG_BIG)

    def body(masked):
        s_next = scores(0)
        for r in range(nchain):
            s = s_next
            if r + 1 < nchain:
                s_next = scores(r + 1)
            if masked:
                s = causal(s, 0, 0)
            m = m_scr[r]
            m_new = jnp.maximum(m, jnp.max(s, axis=0, keepdims=True))
            alpha = jnp.exp2(m - m_new)
            p = jnp.exp2(s - m_new)
            l_scr[r] = alpha * l_scr[r] + jnp.sum(p, axis=0, keepdims=True)
            acc_scr[r] = alpha * acc_scr[r] + _dot(vt_ref[r // 2], p.astype(MXU_DTYPE))
            m_scr[r] = m_new

    def body_diag():
        s_next = scores_diag(0)
        for r in range(nchain):
            sa, sb = s_next
            if r + 1 < nchain:
                s_next = scores_diag(r + 1)
            sa = jnp.concatenate([causal(sa[:, :h2], 0, 0), sa[:, h2:]], axis=1)
            sb = causal(sb, h2, h2)
            m = m_scr[r]
            mb = jnp.concatenate([jnp.full((1, h2), NEG_BIG, f32), jnp.max(sb, axis=0, keepdims=True)], axis=1)
            m_new = jnp.maximum(m, jnp.maximum(jnp.max(sa, axis=0, keepdims=True), mb))
            alpha = jnp.exp2(m - m_new)
            pa = jnp.exp2(sa - m_new)
            pb = jnp.exp2(sb - m_new[:, h2:])
            lb = jnp.concatenate([jnp.zeros((1, h2), f32), jnp.sum(pb, axis=0, keepdims=True)], axis=1)
            l_scr[r] = alpha * l_scr[r] + jnp.sum(pa, axis=0, keepdims=True) + lb
            vt = vt_ref[r // 2]
            pva = _dot(vt[:, :h2], pa.astype(MXU_DTYPE))
            pvb = _dot(vt[:, h2:], pb.astype(MXU_DTYPE))
            acc_scr[r] = alpha * acc_scr[r] + pva + jnp.concatenate([jnp.zeros((LANES, h2), f32), pvb], axis=1)
            m_scr[r] = m_new

    crosses_diagonal = (j + 1) * tk - 1 > i * tq
    pl.when(crosses_diagonal)(body_diag if split_diag else functools.partial(body, True))
    pl.when(jnp.logical_not(crosses_diagonal))(functools.partial(body, False))

    @pl.when(j == ((i + 1) * tq - 1) // tk)
    def _():
        for hh in range(hb):
            hsl = slice(hh * LANES, (hh + 1) * LANES)
            od_t = acc_scr[2 * hh] / l_scr[2 * hh] - lam_ref[0] * (acc_scr[2 * hh + 1] / l_scr[2 * hh + 1])
            o_ref[:, hsl] = _sub_rmsnorm_gate(od_t.T, g_ref[...], post_scale, sg_ref[:, hsl]).astype(o_ref.dtype)


def _diff_prompt(lam, qd, kd, vdt, g_sub, sg, *, post_scale, tq, tk):
    b, t, w = qd.shape
    nh = w // LANES
    hb = 2 if nh % 2 == 0 else 1
    pairs = [(i, j) for i in range(t // tq) for j in range(((i + 1) * tq - 1) // tk + 1)]
    pi = jnp.asarray([p[0] for p in pairs], i32)
    pj = jnp.asarray([p[1] for p in pairs], i32)
    grid_spec = pltpu.PrefetchScalarGridSpec(
        num_scalar_prefetch=2,
        grid=(b, nh // hb, len(pairs)),
        in_specs=[
            pl.BlockSpec(memory_space=pltpu.SMEM),
            pl.BlockSpec((None, tq, hb * LANES), lambda bb, h, p, pi, pj: (bb, pi[p], h)),
            pl.BlockSpec((None, tk, hb * LANES), lambda bb, h, p, pi, pj: (bb, pj[p], h)),
            pl.BlockSpec((None, hb, None, LANES, tk), lambda bb, h, p, pi, pj: (bb, h, pj[p], 0, 0)),
            pl.BlockSpec((1, LANES), lambda bb, h, p, pi, pj: (0, 0)),
            pl.BlockSpec((None, tq, hb * LANES), lambda bb, h, p, pi, pj: (bb, pi[p], h)),
        ],
        out_specs=pl.BlockSpec((None, tq, hb * LANES), lambda bb, h, p, pi, pj: (bb, pi[p], h)),
        scratch_shapes=[
            pltpu.VMEM((2 * hb, 1, tq), f32),
            pltpu.VMEM((2 * hb, 1, tq), f32),
            pltpu.VMEM((2 * hb, LANES, tq), f32),
        ],
    )
    return pl.pallas_call(
        functools.partial(_diff_prompt_kernel, tq=tq, tk=tk, hb=hb, post_scale=post_scale),
        grid_spec=grid_spec,
        out_shape=jax.ShapeDtypeStruct((b, t, w), MXU_DTYPE),
        compiler_params=_cparams(("arbitrary", "arbitrary", "arbitrary")),
        name="diff_prompt",
    )(pi, pj, lam, qd, kd, vdt, g_sub, sg)


def _kth_largest_key_sub(key_scr, nblk, rows, topk):
    def count_ge(cand):
        def body(c, acc):
            k = key_scr[pl.ds(pl.multiple_of(c * rows, rows), rows), :]
            return acc + jnp.sum((k >= cand).astype(i32).reshape(rows // 8, 8, LANES), axis=0)

        acc = lax.fori_loop(0, nblk, body, jnp.zeros((8, LANES), i32))
        return jnp.sum(acc, axis=0, keepdims=True)

    def bit_body(it, prefix):
        cand = prefix + lax.shift_left(jnp.int32(1), 31 - it)
        return jnp.where(count_ge(cand) >= topk, cand, prefix)

    kth = lax.fori_loop(0, 32, bit_body, jnp.full((1, LANES), INT_MIN, i32))
    return kth, count_ge


def _demote_surplus_ties_sub(key_scr, nblk, rows, topk, kth, count_ge):
    tie = (count_ge(kth) > topk) & (kth > KEY_NEG_INF)

    @pl.when(jnp.max(tie.astype(i32)) > 0)
    def _():
        need = (topk - count_ge(kth + 1)).astype(f32)
        tri = (_iota((rows, rows), 0) >= _iota((rows, rows), 1)).astype(f32).astype(MXU_DTYPE)

        def body(c, run):
            sl = pl.ds(pl.multiple_of(c * rows, rows), rows)
            k = key_scr[sl, :]
            eq = k == kth
            incl = _dot(tri, eq.astype(f32).astype(MXU_DTYPE))
            drop = eq & ((run + incl) > need)
            key_scr[sl, :] = jnp.where(drop, KEY_NEG_INF, k)
            return run + incl[rows - 1:rows, :]

        lax.fori_loop(0, nblk, body, jnp.zeros((1, LANES), f32))


def _demote_surplus_ties_lane(key_scr, nblk, nq, topk, kth, count_ge):
    tie = (count_ge(kth) > topk) & (kth > KEY_NEG_INF)

    @pl.when(jnp.max(jnp.where(tie, 1.0, 0.0)) > 0.0)
    def _():
        need = topk - count_ge(kth + 1)
        triu = (_iota((LANES, LANES), 0) <= _iota((LANES, LANES), 1)).astype(f32).astype(MXU_DTYPE)
        pad = jnp.zeros((16 - nq % 16, LANES), f32) if nq % 16 else None

        def body(c, run):
            k = key_scr[c]
            eq = k == kth
            eqf = jnp.where(eq, 1.0, 0.0)
            if pad is not None:
                eqf = jnp.concatenate([eqf, pad], axis=0)
            incl = _dot(eqf.astype(MXU_DTYPE), triu)[:nq]
            drop = eq & ((run + incl) > need)
            key_scr[c] = jnp.where(drop, KEY_NEG_INF, k)
            return run + incl[:, LANES - 1:LANES]

        lax.fori_loop(0, nblk, body, jnp.zeros((nq, 1), f32))


def _dsa_prompt_kernel(qs_ref, qi_ref, kiwi_ref, ki2_ref, ks_ref, vst_ref, sg_ref, o_ref,
                       key_scr, qm_scr, wt_scr, qst_scr, acc_scr, *, nh, topk, tc):
    qb = pl.program_id(1)
    tq = LANES
    nch = (qb * tq + tq + tc - 1) // tc

    lo = _iota((tq, LANES), 1) < D_IDX
    for h in range(H_IDX):
        chunk = qi_ref[:, (h // 2) * LANES:(h // 2 + 1) * LANES]
        keep = lo if h % 2 == 0 else jnp.logical_not(lo)
        qm_scr[h * tq:(h + 1) * tq, :] = jnp.where(keep, chunk, jnp.zeros_like(chunk))
    wt_scr[...] = kiwi_ref[...].T
    for h in range(nh):
        qst_scr[h * tq:(h + 1) * tq, :] = qs_ref[:, h * LANES:(h + 1) * LANES]

    tpos = qb * tq + _iota((1, LANES), 1)

    def idx_body(c, carry):
        sl = pl.ds(pl.multiple_of(c * tc, tc), tc)
        kc = ki2_ref[sl, :]
        acc = jnp.zeros((tc, LANES), f32)
        for hp in range(H_IDX // 2):
            sc = _dot_nt(kc, qm_scr[2 * hp * tq:(2 * hp + 2) * tq, :])
            for k in range(2):
                h = 2 * hp + k
                acc = acc + jnp.maximum(sc[:, k * tq:(k + 1) * tq], 0.0) * wt_scr[D_IDX + h:D_IDX + h + 1, :]
        acc = jnp.where(acc == 0.0, 0.0, acc)
        kpos = c * tc + _iota((tc, LANES), 0)
        acc = jnp.where(kpos <= tpos, acc, -jnp.inf)
        key_scr[sl, :] = _f2key(acc)
        return carry

    lax.fori_loop(0, nch, idx_body, 0)

    kth, count_ge = _kth_largest_key_sub(key_scr, nch, tc, topk)
    _demote_surplus_ties_sub(key_scr, nch, tc, topk, kth, count_ge)
    thr = jnp.maximum(kth, KEY_NEG_INF + 1)

    acc_scr[...] = jnp.zeros_like(acc_scr)

    def att_body(c, carry):
        m, l = carry
        sl = pl.ds(pl.multiple_of(c * tc, tc), tc)
        s = _dot_nt(ks_ref[sl, :], qst_scr[...])
        msk = key_scr[sl, :] >= thr
        s = jnp.where(jnp.concatenate([msk] * nh, axis=1), s, NEG_BIG)
        m_new = jnp.maximum(m, jnp.max(s, axis=0, keepdims=True))
        alpha = jnp.exp2(m - m_new)
        p = jnp.exp2(s - m_new)
        l = alpha * l + jnp.sum(p, axis=0, keepdims=True)
        acc_scr[...] = alpha * acc_scr[...] + _dot(vst_ref[c], p.astype(MXU_DTYPE))
        return m_new, l

    init = (jnp.full((1, nh * tq), NEG_BIG, f32), jnp.zeros((1, nh * tq), f32))
    _, l = lax.fori_loop(0, nch, att_body, init)
    out_t = acc_scr[...] / l
    for h in range(nh):
        sl = slice(h * LANES, (h + 1) * LANES)
        o_ref[:, sl] = (out_t[:, sl].T * sg_ref[:, sl].astype(f32)).astype(o_ref.dtype)


def _dsa_prompt(qs, qi, kiwi, ki2, ks, vst, sg, *, topk, tc=256):
    b, t, w = qs.shape
    nh = w // LANES
    tq = LANES
    assert t % tc == 0 and tc % tq == 0
    return pl.pallas_call(
        functools.partial(_dsa_prompt_kernel, nh=nh, topk=topk, tc=tc),
        grid=(b, t // tq),
        in_specs=[
            pl.BlockSpec((None, tq, w), lambda bb, i: (bb, i, 0)),
            pl.BlockSpec((None, tq, H_IDX * D_IDX), lambda bb, i: (bb, i, 0)),
            pl.BlockSpec((None, tq, LANES), lambda bb, i: (bb, i, 0)),
            pl.BlockSpec((None, t, LANES), lambda bb, i: (bb, 0, 0)),
            pl.BlockSpec((None, t, LANES), lambda bb, i: (bb, 0, 0)),
            pl.BlockSpec((None, t // tc, LANES, tc), lambda bb, i: (bb, 0, 0, 0)),
            pl.BlockSpec((None, tq, w), lambda bb, i: (bb, i, 1)),
        ],
        out_specs=pl.BlockSpec((None, tq, w), lambda bb, i: (bb, i, 0)),
        out_shape=jax.ShapeDtypeStruct((b, t, w), MXU_DTYPE),
        scratch_shapes=[
            pltpu.VMEM((t, LANES), i32),
            pltpu.VMEM((H_IDX * tq, LANES), MXU_DTYPE),
            pltpu.VMEM((LANES, LANES), f32),
            pltpu.VMEM((nh * tq, LANES), MXU_DTYPE),
            pltpu.VMEM((LANES, nh * tq), f32),
        ],
        compiler_params=_cparams(("arbitrary", "arbitrary")),
        name="dsa_prompt",
    )(qs, qi, kiwi, ki2, ks, vst, sg)


def _mem_attn_kernel(q_ref, k_ref, v_ref, sg_ref, o_ref, *, nh):
    def scores(h):
        hs = slice(h * D_MEM, (h + 1) * D_MEM)
        return _dot_nt(q_ref[:, hs], k_ref[:, hs].astype(MXU_DTYPE))

    s_next = scores(0)
    for h in range(nh):
        s = s_next
        if h + 1 < nh:
            s_next = scores(h + 1)
        hs = slice(h * D_MEM, (h + 1) * D_MEM)
        m = jnp.max(s, axis=1, keepdims=True)
        p = jnp.exp2(s - m)
        l = jnp.sum(p, axis=1, keepdims=True)
        o = _dot(p.astype(MXU_DTYPE), v_ref[:, hs].astype(MXU_DTYPE)) / l
        o_ref[:, hs] = (o * sg_ref[:, hs].astype(f32)).astype(o_ref.dtype)


def _mem_attn(q, k, v, sg, *, layer, sg_col0, tq):
    b, t, w = q.shape
    nm = k.shape[2]
    assert sg_col0 % w == 0
    return pl.pallas_call(
        functools.partial(_mem_attn_kernel, nh=w // D_MEM),
        grid=(b, t // tq),
        in_specs=[
            pl.BlockSpec((None, tq, w), lambda bb, i: (bb, i, 0)),
            pl.BlockSpec((None, None, nm, w), lambda bb, i: (layer, bb, 0, 0)),
            pl.BlockSpec((None, None, nm, w), lambda bb, i: (layer, bb, 0, 0)),
            pl.BlockSpec((None, tq, w), lambda bb, i: (bb, i, sg_col0 // w)),
        ],
        out_specs=pl.BlockSpec((None, tq, w), lambda bb, i: (bb, i, 0)),
        out_shape=jax.ShapeDtypeStruct((b, t, w), MXU_DTYPE),
        compiler_params=_cparams(("arbitrary", "arbitrary")),
        name="mem_attn",
    )(q, k, v, sg)


def _diff_sample_kernel(pt_ref, lam_ref, qm_ref, *refs, nh, nt, pps, nsteps, post_scale):
    kt_refs = refs[:pps]
    v_refs = refs[pps:2 * pps]
    knewt_ref, vnew_ref, g_ref, sg_ref, o_ref, m_scr, l_scr, acc_scr = refs[2 * pps:]
    s_id = pl.program_id(1)
    nr = 2 * nt

    @pl.when(s_id == 0)
    def _():
        m_scr[...] = jnp.full_like(m_scr, NEG_BIG)
        l_scr[...] = jnp.zeros_like(l_scr)
        acc_scr[...] = jnp.zeros_like(acc_scr)

    def update(kts, vs, masked):
        s = jnp.concatenate(
            [jnp.concatenate([_dot(qm_ref[h * nr:(h + 1) * nr, :], kt[h].astype(MXU_DTYPE)) for kt in kts], axis=1)
             for h in range(nh)], axis=0)
        if masked:
            tpos = _iota(s.shape, 0) % nt
            s = jnp.where(_iota(s.shape, 1) <= tpos, s, NEG_BIG)
        m = m_scr[...]
        m_new = jnp.maximum(m, jnp.max(s, axis=1, keepdims=True))
        alpha = jnp.exp2(m - m_new)
        p = jnp.exp2(s - m_new)
        l_scr[...] = alpha * l_scr[...] + jnp.sum(p, axis=1, keepdims=True)
        pb = p.astype(MXU_DTYPE)
        kw = s.shape[1] // len(kts)
        pv = []
        for h in range(nh):
            acc = None
            for r, v in enumerate(vs):
                d = _dot(pb[h * nr:(h + 1) * nr, r * kw:(r + 1) * kw], v[h].astype(MXU_DTYPE))
                acc = d if acc is None else acc + d
            pv.append(acc)
        acc_scr[...] = alpha * acc_scr[...] + jnp.concatenate(pv, axis=0)
        m_scr[...] = m_new

    @pl.when(s_id < nsteps)
    def _():
        update(kt_refs, v_refs, False)

    @pl.when(s_id == nsteps)
    def _():
        update([knewt_ref], [vnew_ref], True)
        o = acc_scr[...] / l_scr[...]
        for h in range(nh):
            od = o[h * nr:h * nr + nt] - lam_ref[0] * o[h * nr + nt:(h + 1) * nr]
            sl = slice(h * LANES, (h + 1) * LANES)
            o_ref[:, sl] = _sub_rmsnorm_gate(od, g_ref[...], post_scale, sg_ref[:, sl]).astype(o_ref.dtype)


def _diff_sample(page_table, lam, qm, cache_kt, cache_v, knewt, vnew, g_sub, sg, *, layer, nt, post_scale):
    db, rows, _ = qm.shape
    nh = cache_v.shape[2]
    page = cache_v.shape[3]
    npages = page_table.shape[1]
    pps = math.gcd(PAGES_PER_STEP, npages)
    nsteps = npages // pps
    w = nh * LANES

    def page_map(r):
        def f(bb, s, pt):
            return (layer, pt[bb, jnp.minimum(s * pps + r, npages - 1)], 0, 0, 0)
        return f

    kt_specs = [pl.BlockSpec((None, None, nh, LANES, page), page_map(r)) for r in range(pps)]
    v_specs = [pl.BlockSpec((None, None, nh, page, LANES), page_map(r)) for r in range(pps)]
    grid_spec = pltpu.PrefetchScalarGridSpec(
        num_scalar_prefetch=1,
        grid=(db, nsteps + 1),
        in_specs=[
            pl.BlockSpec(memory_space=pltpu.SMEM),
            pl.BlockSpec((None, rows, LANES), lambda bb, s, pt: (bb, 0, 0)),
            *kt_specs, *v_specs,
            pl.BlockSpec((None, nh, LANES, page), lambda bb, s, pt: (bb, 0, 0, 0)),
            pl.BlockSpec((None, nh, page, LANES), lambda bb, s, pt: (bb, 0, 0, 0)),
            pl.BlockSpec((1, LANES), lambda bb, s, pt: (0, 0)),
            pl.BlockSpec((None, nt, w), lambda bb, s, pt: (bb, 0, 0)),
        ],
        out_specs=pl.BlockSpec((None, nt, w), lambda bb, s, pt: (bb, 0, 0)),
        scratch_shapes=[
            pltpu.VMEM((rows, 1), f32),
            pltpu.VMEM((rows, 1), f32),
            pltpu.VMEM((rows, LANES), f32),
        ],
    )
    return pl.pallas_call(
        functools.partial(_diff_sample_kernel, nh=nh, nt=nt, pps=pps, nsteps=nsteps, post_scale=post_scale),
        grid_spec=grid_spec,
        out_shape=jax.ShapeDtypeStruct((db, nt, w), MXU_DTYPE),
        compiler_params=_cparams(("arbitrary", "arbitrary")),
        name="diff_sample",
    )(page_table, lam, qm, *([cache_kt] * pps), *([cache_v] * pps), knewt, vnew, g_sub, sg)


def _dsa_sample_kernel(pt_ref, qs_ref, qi_ref, w_ref, *refs, nt, pps, nsteps, topk, page):
    kit_refs = refs[:pps]
    ks_refs = refs[pps:2 * pps]
    vs_refs = refs[2 * pps:3 * pps]
    kint_ref, ksn_ref, vsn_ref, sg_ref, o_ref, key_scr, s_scr, v_scr = refs[3 * pps:]
    s_id = pl.program_id(1)
    gsz = LANES // nt

    def process(blk, kit, ks, vs, causal_new):
        sc = _dot(qi_ref[...], kit.astype(MXU_DTYPE))
        val = jnp.maximum(sc, 0.0) * w_ref[...]
        acc = jnp.sum(val.reshape(gsz, nt, page), axis=0)
        acc = jnp.where(acc == 0.0, 0.0, acc)
        if causal_new:
            acc = jnp.where(_iota(acc.shape, 1) <= _iota(acc.shape, 0), acc, -jnp.inf)
        key_scr[blk] = _f2key(acc)
        s_scr[blk] = _dot_nt(qs_ref[...], ks.astype(MXU_DTYPE))
        v_scr[blk] = vs.astype(MXU_DTYPE)

    @pl.when(s_id < nsteps)
    def _():
        for r in range(pps):
            process(s_id * pps + r, kit_refs[r][...], ks_refs[r][...], vs_refs[r][...], False)

    @pl.when(s_id == nsteps)
    def _():
        nblk = nsteps * pps + 1
        process(nblk - 1, kint_ref[...], ksn_ref[...], vsn_ref[...], True)

        keys = jnp.concatenate([key_scr[c] for c in range(nblk)], axis=1)

        def count_ge_all(cand):
            return jnp.sum(jnp.where(keys >= cand, 1.0, 0.0), axis=1, keepdims=True)

        def bit_body(it, prefix):
            cand = prefix + lax.shift_left(jnp.int32(1), 31 - it)
            return jnp.where(count_ge_all(cand) >= topk, cand, prefix)

        kth = lax.fori_loop(0, 32, bit_body, jnp.full((nt, 1), INT_MIN, i32))
        _demote_surplus_ties_lane(key_scr, nblk, nt, topk, kth, count_ge_all)
        thr = jnp.maximum(kth, KEY_NEG_INF + 1)

        unroll = max(u for u in (5, 4, 3, 2, 1) if nblk % u == 0)

        def sel_scores(c):
            msk = key_scr[c] >= thr
            return msk[None], s_scr[c].reshape(gsz, nt, page)

        def max_body(it, mx):
            for u in range(unroll):
                msk, s3 = sel_scores(it * unroll + u)
                mx = jnp.maximum(mx, jnp.where(msk, s3, NEG_BIG))
            return mx

        mx = lax.fori_loop(0, nblk // unroll, max_body, jnp.full((gsz, nt, page), NEG_BIG, f32))
        m = jnp.max(mx, axis=2, keepdims=True)

        def att_body(it, carry):
            lacc, acc = carry
            for u in range(unroll):
                c = it * unroll + u
                msk, s3 = sel_scores(c)
                p = jnp.where(msk, jnp.exp2(s3 - m), 0.0)
                lacc = lacc + p
                acc = acc + _dot(p.reshape(gsz * nt, page).astype(MXU_DTYPE), v_scr[c])
            return lacc, acc

        lacc, acc = lax.fori_loop(0, nblk // unroll, att_body,
                                  (jnp.zeros((gsz, nt, page), f32), jnp.zeros((LANES, D_DSA), f32)))
        l = jnp.sum(lacc, axis=2, keepdims=True).reshape(gsz * nt, 1)
        o_ref[...] = (acc / l * sg_ref[...].astype(f32)).astype(o_ref.dtype)


def _dsa_sample(page_table, qs_rows, qi_rows, wcol, cache_kit, cache_ks, cache_vs, kit_new, ks_new, vs_new, sg_perm,
                *, layer, nt, topk):
    db = qs_rows.shape[0]
    page = cache_ks.shape[2]
    npages = page_table.shape[1]
    pps = math.gcd(PAGES_PER_STEP, npages)
    nsteps = npages // pps
    nblk = npages + 1
    assert page == LANES

    def page_map(r):
        def f(bb, s, pt):
            return (layer, pt[bb, jnp.minimum(s * pps + r, npages - 1)], 0, 0)
        return f

    def pspecs(shape):
        return [pl.BlockSpec((None, None) + shape, page_map(r)) for r in range(pps)]

    per_b = lambda shape: pl.BlockSpec((None,) + shape, lambda bb, s, pt: (bb, 0, 0))
    grid_spec = pltpu.PrefetchScalarGridSpec(
        num_scalar_prefetch=1,
        grid=(db, nsteps + 1),
        in_specs=[
            per_b((LANES, D_DSA)), per_b((LANES, D_IDX)), per_b((LANES, 1)),
            *pspecs((D_IDX, page)), *pspecs((page, D_DSA)), *pspecs((page, D_DSA)),
            per_b((D_IDX, page)), per_b((page, D_DSA)), per_b((page, D_DSA)),
            per_b((LANES, D_DSA)),
        ],
        out_specs=per_b((LANES, D_DSA)),
        scratch_shapes=[
            pltpu.VMEM((nblk, nt, page), i32),
            pltpu.VMEM((nblk, LANES, page), f32),
            pltpu.VMEM((nblk, page, D_DSA), MXU_DTYPE),
        ],
    )
    return pl.pallas_call(
        functools.partial(_dsa_sample_kernel, nt=nt, pps=pps, nsteps=nsteps, topk=topk, page=page),
        grid_spec=grid_spec,
        out_shape=jax.ShapeDtypeStruct((db, LANES, D_DSA), MXU_DTYPE),
        compiler_params=_cparams(("arbitrary", "arbitrary")),
        name="dsa_sample",
    )(page_table, qs_rows, qi_rows, wcol, *([cache_kit] * pps), *([cache_ks] * pps), *([cache_vs] * pps),
      kit_new, ks_new, vs_new, sg_perm)


def _rope_tables(pos, head_dim):
    half = head_dim // 2
    lane = jnp.arange(LANES)
    inv = ROPE_THETA ** (-(lane % half).astype(f32) / half)
    ang = pos.astype(f32)[:, None] * inv[None, :]
    sign = jnp.where((lane % head_dim) < half, -1.0, 1.0).astype(f32)
    return jnp.cos(ang), jnp.sin(ang) * sign[None, :]


def _tile_gain(g, n):
    return jnp.tile(g, n // g.shape[0]).reshape(1, n).astype(f32)


def _mixer_inputs(x2d, pos_tab, w_in_t, layer, seg, gl, *, tm, stacked=None):
    cos64, sin64, cos128, sin128 = pos_tab
    m = x2d.shape[0]
    h = _rmsnorm(x2d, gl["g_in"], min(tm, 256))
    tn = 512
    proj = functools.partial(_proj, h, tm=tm, nt=True)
    wseg = lambda name: _win_desc(w_in_t, layer, seg[name][0], seg[name][1] - seg[name][0], tn)
    rows = lambda n, dt: _rows_out(m, n, dt, tm, tn)
    rope64 = [("row", cos64), ("row", sin64)]
    rope128 = [("row", cos128), ("row", sin128)]
    nqd = seg["qd"][1] - seg["qd"][0]
    nqs = seg["qs"][1] - seg["qs"][0]
    nqm = seg["qm"][1] - seg["qm"][0]
    ngate = seg["gate"][1] - seg["gate"][0]
    norm_rope64 = functools.partial(_chunks_norm_rope, gsize=D_DH, half=D_DH // 2)
    gq64 = [("col", _tile_gain(gl["g_q_diff"], nqd) * (D_DH ** -0.5 * LOG2E))] + rope64
    gk64 = [("col", _tile_gain(gl["g_k_diff"], nqd))] + rope64
    gq128 = [("col", _tile_gain(gl["g_q_dsa"], nqs) * (D_DSA ** -0.5 * LOG2E))] + rope128
    qi_scale = D_IDX ** -0.5 * H_IDX ** -0.5
    out = {}
    (out["qd"],) = proj(wseg("qd"), norm_rope64, _store_rows, [rows(nqd, MXU_DTYPE)], gq64, name="proj_qd")
    (out["qs"],) = proj(wseg("qs"), functools.partial(_chunks_norm_rope, gsize=D_DSA, half=D_DSA // 2), _store_rows,
                        [rows(nqs, MXU_DTYPE)], gq128, name="proj_qs")
    (out["qi"],) = proj(wseg("qi"), functools.partial(_chunks_rope, half=D_IDX // 2), _store_rows,
                        [rows(H_IDX * D_IDX, MXU_DTYPE)], [("row", cos64 * qi_scale), ("row", sin64 * qi_scale)],
                        name="proj_qi")
    (out["qm"],) = proj(wseg("qm"), _chunks_norm256, _store_rows, [rows(nqm, MXU_DTYPE)],
                        [("col", _tile_gain(gl["g_q_mem"], nqm) * (D_MEM ** -0.5 * LOG2E))], name="proj_qm")
    (out["sg"],) = proj(wseg("gate"), _chunks_silu, _store_rows, [rows(ngate, MXU_DTYPE)], name="proj_gate")

    ks0, vs0, ki0 = seg["ks"][0], seg["vs"][0], seg["ki"][0]
    assert vs0 - ks0 == LANES and ks0 % LANES == 0 and ki0 % LANES == 0
    skip = (ki0 - vs0) // LANES - 1
    w_ksv = dict(array=w_in_t, n=3 * LANES, tn=LANES,
                 spec=pl.BlockSpec((None, LANES, w_in_t.shape[2]),
                                   lambda i, j: (layer, ks0 // LANES + j + jnp.where(j == 2, skip, 0), 0)))
    g_kiwi = jnp.concatenate([gl["g_k_idx"], jnp.ones((LANES - D_IDX,), f32)]).reshape(1, LANES)
    aux_ksv = [("const", gl["g_k_dsa"].reshape(1, D_DSA))] + rope128 + [("const", g_kiwi)] + rope64
    blk = lambda dt: dict(shape=(m, LANES), dtype=dt, block=(tm, LANES), index=lambda i, j: (i, 0), alias=None)
    if stacked is None:
        out["kd"], out["kd_c"] = proj(wseg("kd"), norm_rope64, _store_rows,
                                      [rows(nqd, f32), rows(nqd, MXU_DTYPE)], gk64, name="proj_kd")
        out["vd"], out["vd_c"] = proj(wseg("vd"), _chunks_raw, _store_rows,
                                      [rows(nqd, f32), rows(nqd, MXU_DTYPE)], name="proj_vd")
        out["ks"], out["ks_c"], out["vs"], out["vs_c"], out["kiwi"], out["ki2"] = proj(
            w_ksv, _chunks_ks_vs_kiwi, _store_ks_vs_kiwi,
            [blk(f32), blk(MXU_DTYPE), blk(f32), blk(MXU_DTYPE), blk(f32), blk(MXU_DTYPE)], aux_ksv,
            name="proj_ks_vs_kiwi")
        return out

    l = layer
    b, t = stacked["b"], stacked["t"]
    nrb = t // tm
    nh = nqd // LANES
    hb = _col_tile(nqd, tn) // LANES
    vkb = stacked["diff_tk"]
    assert tm % vkb == 0
    depth = stacked["pdk"].shape[0]
    out["pdk"], out["kd_c"] = proj(
        wseg("kd"), norm_rope64, _store_kd_prompt,
        [dict(shape=(depth, b, nh, LANES, t), dtype=f32, block=(None, None, hb, LANES, tm),
              index=lambda i, j: (l, i // nrb, j, 0, i % nrb), alias=stacked["pdk"]),
         rows(nqd, MXU_DTYPE)], gk64, name="proj_kd")
    out["pdv"], out["vd_c"] = proj(
        wseg("vd"), _chunks_raw, _store_vd_prompt,
        [dict(shape=(depth, b, nh, t, LANES), dtype=f32, block=(None, None, hb, tm, LANES),
              index=lambda i, j: (l, i // nrb, j, i % nrb, 0), alias=stacked["pdv"]),
         dict(shape=(b, nh, t // vkb, LANES, vkb), dtype=MXU_DTYPE, block=(None, hb, tm // vkb, LANES, vkb),
              index=lambda i, j: (i // nrb, j, i % nrb, 0, 0), alias=None)], name="proj_vd")
    tok_major = lambda key: dict(shape=(depth, b, t, D_DSA), dtype=f32, block=(None, None, tm, D_DSA),
                                 index=lambda i, j: (l, i // nrb, i % nrb, 0), alias=stacked[key])
    out["psk"], out["ks_c"], out["psv"], out["vs_c"], out["kiwi"], out["ki2"], out["pik"] = proj(
        w_ksv, _chunks_ks_vs_kiwi, _store_ks_vs_kiwi,
        [tok_major("psk"), blk(MXU_DTYPE), tok_major("psv"), blk(MXU_DTYPE), blk(f32), blk(MXU_DTYPE),
         dict(shape=(depth, b, D_IDX, t), dtype=f32, block=(None, None, D_IDX, tm),
              index=lambda i, j: (l, i // nrb, 0, i % nrb), alias=stacked["pik"])], aux_ksv, name="proj_ks_vs_kiwi")
    return out


def kernel(x_prompt, x_sample, mem_prompt, cache_diff_k, cache_diff_v, cache_dsa_k, cache_dsa_v, cache_idx_k,
           cache_mem_k, cache_mem_v, page_table, w_in, w_out, w_mem_kv, g_in, g_mem, g_q_diff, g_k_diff,
           g_sub_diff, lam_q1, lam_k1, lam_q2, lam_k2, g_q_dsa, g_k_dsa, g_k_idx, g_q_mem, g_k_mem):
    depth = w_in.shape[0]
    b, t, d = x_prompt.shape
    db, nt, _ = x_sample.shape
    n_mem = mem_prompt.shape[1]
    n_phys, page = cache_dsa_k.shape[1], cache_dsa_k.shape[2]
    npages = page_table.shape[1]
    past = npages * page
    h_diff = cache_diff_k.shape[3]
    w_diff = h_diff * 2 * D_DH
    h_dsa = (3 * d // 8) // D_DSA
    w_dsa = h_dsa * D_DSA
    h_mem = cache_mem_k.shape[3]
    w_mem = h_mem * D_MEM
    gsz = LANES // nt
    assert w_diff + w_dsa + w_mem == d and w_diff == w_dsa and LANES % nt == 0 and h_dsa <= gsz and gsz == H_IDX
    topk_p = min(TOPK_MAX, t // 4)
    topk_s = min(TOPK_MAX, (past + nt) // 4)
    m_p = b * t
    m_s = db * nt
    tm_p = min(1024, t)
    assert t % tm_p == 0

    widths = (w_diff, w_diff, w_diff, w_dsa, D_DSA, D_DSA, H_IDX * D_IDX, D_IDX, H_IDX, w_mem, d)
    offs = [0]
    for wdt in widths:
        offs.append(offs[-1] + wdt)
    names = ("qd", "kd", "vd", "qs", "ks", "vs", "qi", "ki", "wi", "qm", "gate")
    seg = {n: (offs[k], offs[k + 1]) for k, n in enumerate(names)}

    pos_p = jnp.arange(t, dtype=i32)
    pos_s = jnp.tile(past + jnp.arange(nt, dtype=i32), db)
    tab_p = _rope_tables(pos_p, D_DH) + _rope_tables(pos_p, D_DSA)
    tab_s = _rope_tables(pos_s, D_DH) + _rope_tables(pos_s, D_DSA)

    cdkt = cache_diff_k.transpose(0, 1, 3, 4, 5, 2).reshape(depth, n_phys, h_diff, 2 * D_DH, page)
    cdv = cache_diff_v.transpose(0, 1, 3, 2, 4)
    ckit = cache_idx_k.transpose(0, 1, 3, 2)
    cmk = cache_mem_k.reshape(depth, db, n_mem, w_mem)
    cmv = cache_mem_v.reshape(depth, db, n_mem, w_mem)
    w_in_t = jnp.swapaxes(w_in, 1, 2)

    x_p = x_prompt.reshape(m_p, d)
    x_s = x_sample.reshape(m_s, d)
    mem2d = mem_prompt.reshape(b * n_mem, d)
    stk = dict(pdk=jnp.zeros((depth, b, h_diff, 2 * D_DH, t), f32), pdv=jnp.zeros((depth, b, h_diff, t, 2 * D_DH), f32),
               psk=jnp.zeros((depth, b, t, D_DSA), f32), psv=jnp.zeros((depth, b, t, D_DSA), f32),
               pik=jnp.zeros((depth, b, D_IDX, t), f32))
    outs = {k: [] for k in ("pmk", "pmv", "sdk", "sdv", "ssk", "ssv", "sik")}

    for l in range(depth):
        gl = dict(g_in=g_in[l], g_q_diff=g_q_diff[l], g_k_diff=g_k_diff[l], g_q_dsa=g_q_dsa[l],
                  g_k_dsa=g_k_dsa[l], g_k_idx=g_k_idx[l], g_q_mem=g_q_mem[l])
        lam_init = 0.8 - 0.6 * math.exp(-0.3 * l)
        lam = (jnp.exp(jnp.sum(lam_q1[l] * lam_k1[l])) - jnp.exp(jnp.sum(lam_q2[l] * lam_k2[l])) + lam_init)
        lam = lam.astype(f32).reshape(1)
        g_sub = g_sub_diff[l].reshape(1, 2 * D_DH)
        post = 1.0 - lam_init

        diff_tq, diff_tk = min(DIFF_TQ, t), min(DIFF_TK, t)
        tp = _mixer_inputs(x_p, tab_p, w_in_t, l, seg, gl, tm=tm_p, stacked=dict(stk, b=b, t=t, diff_tk=diff_tk))
        for key in ("pdk", "pdv", "psk", "psv", "pik"):
            stk[key] = tp[key]
        hm = _rmsnorm(mem2d, g_mem[l], 256)
        mrows = lambda dt: [_rows_out(b * n_mem, w_mem, dt, b * n_mem, 512)]
        mk, = _proj(hm, _wcols_desc(w_mem_kv, l, 0, w_mem, 512), _chunks_norm256,
                    _store_rows, mrows(f32), [("col", _tile_gain(g_k_mem[l], w_mem))], tm=b * n_mem, nt=False,
                    name="proj_mk")
        mv, = _proj(hm, _wcols_desc(w_mem_kv, l, w_mem, w_mem, 512), _chunks_raw, _store_rows, mrows(f32),
                    tm=b * n_mem, nt=False, name="proj_mv")
        r3 = lambda a: a.reshape(b, t, a.shape[-1])
        sg = r3(tp["sg"])
        od = _diff_prompt(lam, r3(tp["qd"]), r3(tp["kd_c"]), tp["vd_c"], g_sub, sg, post_scale=post,
                          tq=diff_tq, tk=diff_tk)
        tc = 256
        vst = tp["vs_c"].reshape(b, t // tc, tc, D_DSA).transpose(0, 1, 3, 2)
        os_ = _dsa_prompt(r3(tp["qs"]), r3(tp["qi"]), r3(tp["kiwi"]), r3(tp["ki2"]), r3(tp["ks_c"]), vst, sg,
                          topk=topk_p, tc=tc)
        om = _mem_attn(r3(tp["qm"]), mk.reshape(1, b, n_mem, w_mem), mv.reshape(1, b, n_mem, w_mem), sg,
                       layer=0, sg_col0=w_diff + w_dsa, tq=min(512, t))
        x_p = _outproj(od.reshape(m_p, w_diff), os_.reshape(m_p, w_dsa), om.reshape(m_p, w_mem), w_out, x_p,
                       layer=l, tm=tm_p, tn=512)
        outs["pmk"].append(mk.reshape(b, n_mem, h_mem, D_MEM))
        outs["pmv"].append(mv.reshape(b, n_mem, h_mem, D_MEM))

        ts = _mixer_inputs(x_s, tab_s, w_in_t, l, seg, gl, tm=m_s)
        s3 = lambda a: a.reshape(db, nt, a.shape[-1])
        sg_s = s3(ts["sg"])
        tokpad = lambda a, axis: jnp.pad(a, [(0, page - nt) if ax == axis else (0, 0) for ax in range(a.ndim)])
        q5 = ts["qd"].reshape(db, nt, h_diff, 2, D_DH).transpose(0, 2, 3, 1, 4)
        eye_c = jnp.eye(2, dtype=MXU_DTYPE)
        qm = (q5[:, :, :, :, None, :] * eye_c[None, None, :, None, :, None]).reshape(db, h_diff * 2 * nt, 2 * D_DH)
        kd4 = ts["kd"].reshape(db, nt, h_diff, 2 * D_DH)
        vd4 = ts["vd"].reshape(db, nt, h_diff, 2 * D_DH)
        od_s = _diff_sample(page_table, lam, qm, cdkt, cdv, tokpad(kd4.transpose(0, 2, 3, 1), 3),
                            tokpad(vd4.transpose(0, 2, 1, 3), 2), g_sub, sg_s, layer=l, nt=nt, post_scale=post)
        qs4 = ts["qs"].reshape(db, nt, h_dsa, D_DSA)
        slot_major = lambda a: a.transpose(0, 2, 1, 3).reshape(db, LANES, a.shape[-1])
        qs_rows = slot_major(jnp.pad(qs4, ((0, 0), (0, 0), (0, gsz - h_dsa), (0, 0))))
        qi_rows = slot_major(ts["qi"].reshape(db, nt, H_IDX, D_IDX))
        wcol = slot_major(ts["kiwi"][:, D_IDX:D_IDX + H_IDX].reshape(db, nt, H_IDX, 1))
        sg_dsa = sg_s[:, :, w_diff:w_diff + w_dsa].reshape(db, nt, h_dsa, D_DSA)
        sg_perm = slot_major(jnp.pad(sg_dsa, ((0, 0), (0, 0), (0, gsz - h_dsa), (0, 0))))
        ki_new = s3(ts["kiwi"][:, :D_IDX])
        os_s = _dsa_sample(page_table, qs_rows, qi_rows, wcol, ckit, cache_dsa_k, cache_dsa_v,
                           tokpad(ki_new.transpose(0, 2, 1), 2), tokpad(s3(ts["ks"]), 1), tokpad(s3(ts["vs"]), 1),
                           sg_perm, layer=l, nt=nt, topk=topk_s)
        os_s = os_s.reshape(db, gsz, nt, D_DSA)[:, :h_dsa].transpose(0, 2, 1, 3).reshape(m_s, w_dsa)
        om_s = _mem_attn(s3(ts["qm"]), cmk, cmv, sg_s, layer=l, sg_col0=w_diff + w_dsa, tq=nt)
        x_s = _outproj(od_s.reshape(m_s, w_diff), os_s, om_s.reshape(m_s, w_mem), w_out, x_s, layer=l, tm=m_s,
                       tn=512)
        outs["sdk"].append(ts["kd"].reshape(db, nt, h_diff, 2, D_DH))
        outs["sdv"].append(ts["vd"].reshape(db, nt, h_diff, 2 * D_DH))
        outs["ssk"].append(ts["ks"].reshape(db, nt, D_DSA))
        outs["ssv"].append(ts["vs"].reshape(db, nt, D_DSA))
        outs["sik"].append(ki_new)

    st = lambda k: jnp.stack(outs[k])
    p_diff_k = stk["pdk"].reshape(depth, b, h_diff, 2, D_DH, t).transpose(0, 1, 5, 2, 3, 4)
    p_diff_v = stk["pdv"].transpose(0, 1, 3, 2, 4)
    p_idx_k = stk["pik"].transpose(0, 1, 3, 2)
    return (x_p.reshape(b, t, d), x_s.reshape(db, nt, d),
            p_diff_k, p_diff_v, stk["psk"], stk["psv"], p_idx_k, st("pmk"), st("pmv"),
            st("sdk"), st("sdv"), st("ssk"), st("ssv"), st("sik"))
```

```python
import functools
import math

import jax
import jax.numpy as jnp
from jax import lax
from jax.experimental import pallas as pl
from jax.experimental.pallas import tpu as pltpu

EPS = 1e-6
ROPE_THETA = 10000.0
TOPK_MAX = 256
LANES = 128
D_DH = 64
D_DSA = 128
D_IDX = 64
H_IDX = 16
D_MEM = 256
NEG_BIG = -1e30
KEY_NEG_INF = -2139095041
INT_MIN = -2147483648
MXU_DTYPE = jnp.bfloat16
VMEM_LIMIT_BYTES = 52 * 1024 * 1024
PAGES_PER_STEP = 8
DIFF_TQ = 512
DIFF_TK = 512
LOG2E = math.log2(math.e)

f32 = jnp.float32
i32 = jnp.int32


def _cparams(sem):
    return pltpu.CompilerParams(dimension_semantics=sem, vmem_limit_bytes=VMEM_LIMIT_BYTES)


def _dot(a, b):
    return jnp.dot(a, b, preferred_element_type=f32)


def _dot_nt(a, b):
    return lax.dot_general(a, b, (((1,), (1,)), ((), ())), preferred_element_type=f32)


def _iota(shape, dim):
    return lax.broadcasted_iota(i32, shape, dim)


def _group_mat(gsize, value):
    r = _iota((LANES, LANES), 0) // gsize
    c = _iota((LANES, LANES), 1) // gsize
    return jnp.where(r == c, value, 0.0).astype(MXU_DTYPE)


def _group_sum(x, gmat):
    return _dot(x.astype(MXU_DTYPE), gmat)


def _rope_chunk(n, cos, sin_signed, half):
    if 2 * half == LANES:
        rot = pltpu.roll(n, half, 1)
    else:
        first = (_iota(n.shape, 1) % (2 * half)) < half
        rot = jnp.where(first, pltpu.roll(n, LANES - half, 1), pltpu.roll(n, half, 1))
    return n * cos + rot * sin_signed


def _f2key(x):
    b = pltpu.bitcast(x, i32)
    return b ^ (lax.shift_right_arithmetic(b, 31) & 0x7FFFFFFF)


def _rmsnorm_kernel(x_ref, g_ref, o_ref):
    x = x_ref[...]
    ms = jnp.mean(x * x, axis=-1, keepdims=True)
    o_ref[...] = (x * lax.rsqrt(ms + EPS) * g_ref[...]).astype(o_ref.dtype)


def _rmsnorm(x, g, tm):
    m, d = x.shape
    return pl.pallas_call(
        _rmsnorm_kernel,
        grid=(m // tm,),
        in_specs=[pl.BlockSpec((tm, d), lambda i: (i, 0)), pl.BlockSpec((1, d), lambda i: (0, 0))],
        out_specs=pl.BlockSpec((tm, d), lambda i: (i, 0)),
        out_shape=jax.ShapeDtypeStruct((m, d), MXU_DTYPE),
        compiler_params=_cparams(("arbitrary",)),
        name="rmsnorm",
    )(x, g.reshape(1, d))


def _chunks_raw(z, aux, store):
    for c in range(z.shape[1] // LANES):
        store(c, z[:, c * LANES:(c + 1) * LANES])


def _chunks_silu(z, aux, store):
    for c in range(z.shape[1] // LANES):
        zc = z[:, c * LANES:(c + 1) * LANES]
        store(c, zc / (1.0 + jnp.exp(-zc)))


def _chunks_norm_rope(z, aux, store, *, gsize, half):
    gain, cos, sin = aux
    gmat = _group_mat(gsize, 1.0 / gsize)
    for c in range(z.shape[1] // LANES):
        sl = slice(c * LANES, (c + 1) * LANES)
        zc = z[:, sl]
        n = zc * lax.rsqrt(_group_sum(zc * zc, gmat) + EPS) * gain[:, sl]
        store(c, _rope_chunk(n, cos, sin, half))


def _chunks_rope(z, aux, store, *, half):
    cos, sin = aux
    for c in range(z.shape[1] // LANES):
        store(c, _rope_chunk(z[:, c * LANES:(c + 1) * LANES], cos, sin, half))


def _chunks_norm256(z, aux, store):
    (gain,) = aux
    gmat = _group_mat(LANES, 1.0 / D_MEM)
    for c in range(z.shape[1] // D_MEM):
        a = z[:, c * D_MEM:c * D_MEM + LANES]
        b = z[:, c * D_MEM + LANES:(c + 1) * D_MEM]
        inv = lax.rsqrt(_group_sum(a * a, gmat) + _group_sum(b * b, gmat) + EPS)
        for k, v in enumerate((a, b)):
            sl = slice(c * D_MEM + k * LANES, c * D_MEM + (k + 1) * LANES)
            store(2 * c + k, v * inv * gain[:, sl])


def _chunks_kiwi(z, aux, store):
    gain, cos, sin = aux
    z = z[:, :]
    gmat = _group_mat(D_IDX, 1.0 / D_IDX)
    n = z * lax.rsqrt(_group_sum(z * z, gmat) + EPS) * gain
    r = _rope_chunk(n, cos, sin, D_IDX // 2)
    lo = _iota(z.shape, 1) < D_IDX
    store(0, jnp.where(lo, r, z))
    r_lo = jnp.where(lo, r, 0.0)
    store(1, r_lo + pltpu.roll(r_lo, D_IDX, 1))


def _store_rows(outs, c, r):
    for o in outs:
        o[:, c * LANES:(c + 1) * LANES] = r.astype(o.dtype)


def _store_kd_prompt(outs, c, r):
    outs[0][c] = r.T
    outs[1][:, c * LANES:(c + 1) * LANES] = r.astype(outs[1].dtype)


def _store_vd_prompt(outs, c, r):
    outs[0][c] = r
    rt = r.T.astype(outs[1].dtype)
    kb = outs[1].shape[-1]
    for kk in range(r.shape[0] // kb):
        outs[1][c, kk] = rt[:, kk * kb:(kk + 1) * kb]


def _chunks_ks_vs_kiwi(z, aux, store, *, col):
    gain_k, cos128, sin128, gain_kiwi, cos64, sin64 = aux
    j = col

    @pl.when(j == 0)
    def _():
        _chunks_norm_rope(z, (gain_k, cos128, sin128), functools.partial(store, 0), gsize=D_DSA, half=D_DSA // 2)

    @pl.when(j == 1)
    def _():
        store(1, 0, z[:, :])

    @pl.when(j == 2)
    def _():
        _chunks_kiwi(z, (gain_kiwi, cos64, sin64), functools.partial(store, 2))


def _store_ks_vs_kiwi(outs, which, c, r):
    if which < 2:
        outs[2 * which][...] = r
        outs[2 * which + 1][...] = r.astype(outs[2 * which + 1].dtype)
    elif c == 0:
        outs[4][...] = r
        if len(outs) > 6:
            outs[6][...] = r.T[:D_IDX, :]
    else:
        outs[5][...] = r.astype(outs[5].dtype)


class _LazyTile:
    def __init__(self, ref):
        self.ref = ref
        self.shape = ref.shape

    def __getitem__(self, idx):
        return self.ref[idx]


def _proj_kernel(*refs, chunk_fn, store_fn, n_aux, n_alias, nt, nj, ntiles, pass_col):
    h_ref, w_ref = refs[:2]
    aux_refs = refs[2:2 + n_aux]
    outs = refs[2 + n_aux + n_alias:-1]
    z_scr = refs[-1]
    s = pl.program_id(0)

    def epilogue():
        kw = dict(col=(s - 1) % nj) if pass_col else {}
        chunk_fn(_LazyTile(z_scr), [r[...] for r in aux_refs], functools.partial(store_fn, outs), **kw)

    def matmul():
        w = w_ref[...].astype(MXU_DTYPE)
        z_scr[...] = _dot_nt(h_ref[...], w) if nt else _dot(h_ref[...], w)

    @pl.when(s == 0)
    def _():
        matmul()

    @pl.when(jnp.logical_and(s > 0, s < ntiles))
    def _():
        epilogue()
        matmul()

    @pl.when(s == ntiles)
    def _():
        epilogue()


def _win_desc(w_in_t, layer, off, n, tn):
    tn = _col_tile(n, tn)
    k = w_in_t.shape[2]
    if off % tn == 0:
        return dict(array=w_in_t, spec=pl.BlockSpec((None, tn, k), lambda i, j: (layer, off // tn + j, 0)), n=n, tn=tn)
    row0 = layer * w_in_t.shape[1] + off
    assert row0 % 8 == 0
    spec = pl.BlockSpec((pl.Element(tn), pl.Element(k)), lambda i, j: (pl.multiple_of(row0 + j * tn, 8), 0))
    return dict(array=w_in_t.reshape(-1, k), spec=spec, n=n, tn=tn)


def _wcols_desc(w, layer, off, n, tn):
    tn = _col_tile(n, tn)
    assert off % tn == 0
    return dict(array=w, spec=pl.BlockSpec((None, w.shape[1], tn), lambda i, j: (layer, 0, off // tn + j)), n=n, tn=tn)


def _col_tile(n, tn):
    return math.gcd(n, tn)


def _rows_out(m, n, dtype, tm, tn):
    tn = _col_tile(n, tn)
    return dict(shape=(m, n), dtype=dtype, block=(tm, tn), index=lambda i, j: (i, j), alias=None)


def _proj(h, wd, chunk_fn, store_fn, outs, aux=(), *, tm, nt, name, pass_col=False):
    m, k = h.shape
    n, tn = wd["n"], wd["tn"]
    assert m % tm == 0 and tn % LANES == 0 and n % tn == 0
    nj = n // tn
    ntiles = (m // tm) * nj
    cur = lambda f: (lambda s: f(jnp.minimum(s, ntiles - 1) // nj, jnp.minimum(s, ntiles - 1) % nj))
    prev = lambda f: (lambda s: f(jnp.maximum(s - 1, 0) // nj, jnp.maximum(s - 1, 0) % nj))
    in_specs = [pl.BlockSpec((tm, k), cur(lambda i, j: (i, 0))),
                pl.BlockSpec(wd["spec"].block_shape, cur(wd["spec"].index_map))]
    args = [h, wd["array"]]
    for kind, a in aux:
        if kind == "col":
            in_specs.append(pl.BlockSpec((1, tn), prev(lambda i, j: (0, j))))
        elif kind == "const":
            in_specs.append(pl.BlockSpec((1, LANES), prev(lambda i, j: (0, 0))))
        else:
            nrb = a.shape[0] // tm
            in_specs.append(pl.BlockSpec((tm, LANES), prev(lambda i, j, nrb=nrb: (i % nrb, 0))))
        args.append(a)
    aliases = {}
    for k_out, o in enumerate(outs):
        if o["alias"] is not None:
            aliases[len(args)] = k_out
            in_specs.append(pl.BlockSpec(memory_space=pl.ANY))
            args.append(o["alias"])
    return pl.pallas_call(
        functools.partial(_proj_kernel, chunk_fn=chunk_fn, store_fn=store_fn, n_aux=len(aux),
                          n_alias=len(aliases), nt=nt, nj=nj, ntiles=ntiles, pass_col=pass_col),
        grid=(ntiles + 1,),
        in_specs=in_specs,
        out_specs=[pl.BlockSpec(o["block"], prev(o["index"])) for o in outs],
        out_shape=[jax.ShapeDtypeStruct(o["shape"], o["dtype"]) for o in outs],
        scratch_shapes=[pltpu.VMEM((tm, tn), f32)],
        input_output_aliases=aliases,
        compiler_params=_cparams(("arbitrary",)),
        name=name,
    )(*args)


def _outproj_kernel(od_ref, os_ref, om_ref, w_ref, x_ref, o_ref, *, wd, ws):
    acc = _dot(od_ref[...], w_ref[0:wd, :].astype(MXU_DTYPE))
    acc += _dot(os_ref[...], w_ref[wd:wd + ws, :].astype(MXU_DTYPE))
    acc += _dot(om_ref[...], w_ref[wd + ws:, :].astype(MXU_DTYPE))
    o_ref[...] = x_ref[...] + acc


def _outproj(od, os_, om, w, x, *, layer, tm, tn):
    m, d = x.shape
    wd, ws, wm = od.shape[1], os_.shape[1], om.shape[1]
    return pl.pallas_call(
        functools.partial(_outproj_kernel, wd=wd, ws=ws),
        grid=(m // tm, d // tn),
        in_specs=[
            pl.BlockSpec((tm, wd), lambda i, j: (i, 0)),
            pl.BlockSpec((tm, ws), lambda i, j: (i, 0)),
            pl.BlockSpec((tm, wm), lambda i, j: (i, 0)),
            pl.BlockSpec((None, wd + ws + wm, tn), lambda i, j: (layer, 0, j)),
            pl.BlockSpec((tm, tn), lambda i, j: (i, j)),
        ],
        out_specs=pl.BlockSpec((tm, tn), lambda i, j: (i, j)),
        out_shape=jax.ShapeDtypeStruct((m, d), f32),
        compiler_params=_cparams(("arbitrary", "arbitrary")),
        name="outproj",
    )(od, os_, om, w, x)


def _sub_rmsnorm_gate(od, g, post_scale, sg):
    ms = jnp.mean(od * od, axis=-1, keepdims=True)
    return od * lax.rsqrt(ms + EPS) * g * post_scale * sg.astype(f32)


def _diff_prompt_kernel(pi_ref, pj_ref, lam_ref, q_ref, k_ref, vt_ref, g_ref, sg_ref, o_ref,
                        m_scr, l_scr, acc_scr, *, tq, tk, hb, post_scale):
    p_id = pl.program_id(2)
    i = pi_ref[p_id]
    j = pj_ref[p_id]

    @pl.when(j == 0)
    def _():
        m_scr[...] = jnp.full_like(m_scr, NEG_BIG)
        l_scr[...] = jnp.zeros_like(l_scr)
        acc_scr[...] = jnp.zeros_like(acc_scr)

    nchain = 2 * hb
    h2 = tq // 2
    split_diag = tq == tk and h2 % LANES == 0

    def masked_q(r):
        q = q_ref[:, (r // 2) * LANES:(r // 2 + 1) * LANES]
        lo = _iota(q.shape, 1) < D_DH
        zero = jnp.zeros_like(q)
        return jnp.where(lo, q, zero) if r % 2 == 0 else jnp.where(lo, zero, q)

    def keys(r):
        return k_ref[:, (r // 2) * LANES:(r // 2 + 1) * LANES]

    def scores(r):
        return _dot_nt(keys(r), masked_q(r))

    def scores_diag(r):
        qc, kc = masked_q(r), keys(r)
        return _dot_nt(kc[:h2], qc), _dot_nt(kc[h2:], qc[h2:])

    def causal(s, key0, qry0):
        return jnp.where(j * tk + key0 + _iota(s.shape, 0) <= i * tq + qry0 + _iota(s.shape, 1), s, NEG_BIG)

    def body(masked):
        s_next = scores(0)
        for r in range(nchain):
            s = s_next
            if r + 1 < nchain:
                s_next = scores(r + 1)
            if masked:
                s = causal(s, 0, 0)
            m = m_scr[r]
            m_new = jnp.maximum(m, jnp.max(s, axis=0, keepdims=True))
            alpha = jnp.exp2(m - m_new)
            p = jnp.exp2(s - m_new)
            l_scr[r] = alpha * l_scr[r] + jnp.sum(p, axis=0, keepdims=True)
            acc_scr[r] = alpha * acc_scr[r] + _dot(vt_ref[r // 2], p.astype(MXU_DTYPE))
            m_scr[r] = m_new

    def body_diag():
        s_next = scores_diag(0)
        for r in range(nchain):
            sa, sb = s_next
            if r + 1 < nchain:
                s_next = scores_diag(r + 1)
            sa = jnp.concatenate([causal(sa[:, :h2], 0, 0), sa[:, h2:]], axis=1)
            sb = causal(sb, h2, h2)
            m = m_scr[r]
            mb = jnp.concatenate([jnp.full((1, h2), NEG_BIG, f32), jnp.max(sb, axis=0, keepdims=True)], axis=1)
            m_new = jnp.maximum(m, jnp.maximum(jnp.max(sa, axis=0, keepdims=True), mb))
            alpha = jnp.exp2(m - m_new)
            pa = jnp.exp2(sa - m_new)
            pb = jnp.exp2(sb - m_new[:, h2:])
            lb = jnp.concatenate([jnp.zeros((1, h2), f32), jnp.sum(pb, axis=0, keepdims=True)], axis=1)
            l_scr[r] = alpha * l_scr[r] + jnp.sum(pa, axis=0, keepdims=True) + lb
            vt = vt_ref[r // 2]
            pva = _dot(vt[:, :h2], pa.astype(MXU_DTYPE))
            pvb = _dot(vt[:, h2:], pb.astype(MXU_DTYPE))
            acc_scr[r] = alpha * acc_scr[r] + pva + jnp.concatenate([jnp.zeros((LANES, h2), f32), pvb], axis=1)
            m_scr[r] = m_new

    crosses_diagonal = (j + 1) * tk - 1 > i * tq
    pl.when(crosses_diagonal)(body_diag if split_diag else functools.partial(body, True))
    pl.when(jnp.logical_not(crosses_diagonal))(functools.partial(body, False))

    @pl.when(j == ((i + 1) * tq - 1) // tk)
    def _():
        for hh in range(hb):
            hsl = slice(hh * LANES, (hh + 1) * LANES)
            od_t = acc_scr[2 * hh] / l_scr[2 * hh] - lam_ref[0] * (acc_scr[2 * hh + 1] / l_scr[2 * hh + 1])
            o_ref[:, hsl] = _sub_rmsnorm_gate(od_t.T, g_ref[...], post_scale, sg_ref[:, hsl]).astype(o_ref.dtype)


def _diff_prompt(lam, qd, kd, vdt, g_sub, sg, *, post_scale, tq, tk):
    b, t, w = qd.shape
    nh = w // LANES
    hb = 2 if nh % 2 == 0 else 1
    pairs = [(i, j) for i in range(t // tq) for j in range(((i + 1) * tq - 1) // tk + 1)]
    pi = jnp.asarray([p[0] for p in pairs], i32)
    pj = jnp.asarray([p[1] for p in pairs], i32)
    grid_spec = pltpu.PrefetchScalarGridSpec(
        num_scalar_prefetch=2,
        grid=(b, nh // hb, len(pairs)),
        in_specs=[
            pl.BlockSpec(memory_space=pltpu.SMEM),
            pl.BlockSpec((None, tq, hb * LANES), lambda bb, h, p, pi, pj: (bb, pi[p], h)),
            pl.BlockSpec((None, tk, hb * LANES), lambda bb, h, p, pi, pj: (bb, pj[p], h)),
            pl.BlockSpec((None, hb, None, LANES, tk), lambda bb, h, p, pi, pj: (bb, h, pj[p], 0, 0)),
            pl.BlockSpec((1, LANES), lambda bb, h, p, pi, pj: (0, 0)),
            pl.BlockSpec((None, tq, hb * LANES), lambda bb, h, p, pi, pj: (bb, pi[p], h)),
        ],
        out_specs=pl.BlockSpec((None, tq, hb * LANES), lambda bb, h, p, pi, pj: (bb, pi[p], h)),
        scratch_shapes=[
            pltpu.VMEM((2 * hb, 1, tq), f32),
            pltpu.VMEM((2 * hb, 1, tq), f32),
            pltpu.VMEM((2 * hb, LANES, tq), f32),
        ],
    )
    return pl.pallas_call(
        functools.partial(_diff_prompt_kernel, tq=tq, tk=tk, hb=hb, post_scale=post_scale),
        grid_spec=grid_spec,
        out_shape=jax.ShapeDtypeStruct((b, t, w), MXU_DTYPE),
        compiler_params=_cparams(("arbitrary", "arbitrary", "arbitrary")),
        name="diff_prompt",
    )(pi, pj, lam, qd, kd, vdt, g_sub, sg)


def _kth_largest_key_sub(key_scr, nblk, rows, topk):
    def count_ge(cand):
        def body(c, acc):
            k = key_scr[pl.ds(pl.multiple_of(c * rows, rows), rows), :]
            return acc + jnp.sum((k >= cand).astype(i32).reshape(rows // 8, 8, LANES), axis=0)

        acc = lax.fori_loop(0, nblk, body, jnp.zeros((8, LANES), i32))
        return jnp.sum(acc, axis=0, keepdims=True)

    def bit_body(it, prefix):
        cand = prefix + lax.shift_left(jnp.int32(1), 31 - it)
        return jnp.where(count_ge(cand) >= topk, cand, prefix)

    kth = lax.fori_loop(0, 32, bit_body, jnp.full((1, LANES), INT_MIN, i32))
    return kth, count_ge


def _demote_surplus_ties_sub(key_scr, nblk, rows, topk, kth, count_ge):
    tie = (count_ge(kth) > topk) & (kth > KEY_NEG_INF)

    @pl.when(jnp.max(tie.astype(i32)) > 0)
    def _():
        need = (topk - count_ge(kth + 1)).astype(f32)
        tri = (_iota((rows, rows), 0) >= _iota((rows, rows), 1)).astype(f32).astype(MXU_DTYPE)

        def body(c, run):
            sl = pl.ds(pl.multiple_of(c * rows, rows), rows)
            k = key_scr[sl, :]
            eq = k == kth
            incl = _dot(tri, eq.astype(f32).astype(MXU_DTYPE))
            drop = eq & ((run + incl) > need)
            key_scr[sl, :] = jnp.where(drop, KEY_NEG_INF, k)
            return run + incl[rows - 1:rows, :]

        lax.fori_loop(0, nblk, body, jnp.zeros((1, LANES), f32))


def _demote_surplus_ties_lane(key_scr, nblk, nq, topk, kth, count_ge):
    tie = (count_ge(kth) > topk) & (kth > KEY_NEG_INF)

    @pl.when(jnp.max(jnp.where(tie, 1.0, 0.0)) > 0.0)
    def _():
        need = topk - count_ge(kth + 1)
        triu = (_iota((LANES, LANES), 0) <= _iota((LANES, LANES), 1)).astype(f32).astype(MXU_DTYPE)
        pad = jnp.zeros((16 - nq % 16, LANES), f32) if nq % 16 else None

        def body(c, run):
            k = key_scr[c]
            eq = k == kth
            eqf = jnp.where(eq, 1.0, 0.0)
            if pad is not None:
                eqf = jnp.concatenate([eqf, pad], axis=0)
            incl = _dot(eqf.astype(MXU_DTYPE), triu)[:nq]
            drop = eq & ((run + incl) > need)
            key_scr[c] = jnp.where(drop, KEY_NEG_INF, k)
            return run + incl[:, LANES - 1:LANES]

        lax.fori_loop(0, nblk, body, jnp.zeros((nq, 1), f32))


def _dsa_prompt_kernel(qs_ref, qi_ref, kiwi_ref, ki2_ref, ks_ref, vst_ref, sg_ref, o_ref,
                       key_scr, qm_scr, wt_scr, qst_scr, acc_scr, *, nh, topk, tc):
    qb = pl.program_id(1)
    tq = LANES
    nch = (qb * tq + tq + tc - 1) // tc

    lo = _iota((tq, LANES), 1) < D_IDX
    for h in range(H_IDX):
        chunk = qi_ref[:, (h // 2) * LANES:(h // 2 + 1) * LANES]
        keep = lo if h % 2 == 0 else jnp.logical_not(lo)
        qm_scr[h * tq:(h + 1) * tq, :] = jnp.where(keep, chunk, jnp.zeros_like(chunk))
    wt_scr[...] = kiwi_ref[...].T
    for h in range(nh):
        qst_scr[h * tq:(h + 1) * tq, :] = qs_ref[:, h * LANES:(h + 1) * LANES]

    tpos = qb * tq + _iota((1, LANES), 1)

    def idx_body(c, carry):
        sl = pl.ds(pl.multiple_of(c * tc, tc), tc)
        kc = ki2_ref[sl, :]
        acc = jnp.zeros((tc, LANES), f32)
        for hp in range(H_IDX // 2):
            sc = _dot_nt(kc, qm_scr[2 * hp * tq:(2 * hp + 2) * tq, :])
            for k in range(2):
                h = 2 * hp + k
                acc = acc + jnp.maximum(sc[:, k * tq:(k + 1) * tq], 0.0) * wt_scr[D_IDX + h:D_IDX + h + 1, :]
        acc = jnp.where(acc == 0.0, 0.0, acc)
        kpos = c * tc + _iota((tc, LANES), 0)
        acc = jnp.where(kpos <= tpos, acc, -jnp.inf)
        key_scr[sl, :] = _f2key(acc)
        return carry

    lax.fori_loop(0, nch, idx_body, 0)

    kth, count_ge = _kth_largest_key_sub(key_scr, nch, tc, topk)
    _demote_surplus_ties_sub(key_scr, nch, tc, topk, kth, count_ge)
    thr = jnp.maximum(kth, KEY_NEG_INF + 1)

    acc_scr[...] = jnp.zeros_like(acc_scr)

    def att_body(c, carry):
        m, l = carry
        sl = pl.ds(pl.multiple_of(c * tc, tc), tc)
        s = _dot_nt(ks_ref[sl, :], qst_scr[...])
        msk = key_scr[sl, :] >= thr
        s = jnp.where(jnp.concatenate([msk] * nh, axis=1), s, NEG_BIG)
        m_new = jnp.maximum(m, jnp.max(s, axis=0, keepdims=True))
        alpha = jnp.exp2(m - m_new)
        p = jnp.exp2(s - m_new)
        l = alpha * l + jnp.sum(p, axis=0, keepdims=True)
        acc_scr[...] = alpha * acc_scr[...] + _dot(vst_ref[c], p.astype(MXU_DTYPE))
        return m_new, l

    init = (jnp.full((1, nh * tq), NEG_BIG, f32), jnp.zeros((1, nh * tq), f32))
    _, l = lax.fori_loop(0, nch, att_body, init)
    out_t = acc_scr[...] / l
    for h in range(nh):
        sl = slice(h * LANES, (h + 1) * LANES)
        o_ref[:, sl] = (out_t[:, sl].T * sg_ref[:, sl].astype(f32)).astype(o_ref.dtype)


def _dsa_prompt(qs, qi, kiwi, ki2, ks, vst, sg, *, topk, tc=256):
    b, t, w = qs.shape
    nh = w // LANES
    tq = LANES
    assert t % tc == 0 and tc % tq == 0
    return pl.pallas_call(
        functools.partial(_dsa_prompt_kernel, nh=nh, topk=topk, tc=tc),
        grid=(b, t // tq),
        in_specs=[
            pl.BlockSpec((None, tq, w), lambda bb, i: (bb, i, 0)),
            pl.BlockSpec((None, tq, H_IDX * D_IDX), lambda bb, i: (bb, i, 0)),
            pl.BlockSpec((None, tq, LANES), lambda bb, i: (bb, i, 0)),
            pl.BlockSpec((None, t, LANES), lambda bb, i: (bb, 0, 0)),
            pl.BlockSpec((None, t, LANES), lambda bb, i: (bb, 0, 0)),
            pl.BlockSpec((None, t // tc, LANES, tc), lambda bb, i: (bb, 0, 0, 0)),
            pl.BlockSpec((None, tq, w), lambda bb, i: (bb, i, 1)),
        ],
        out_specs=pl.BlockSpec((None, tq, w), lambda bb, i: (bb, i, 0)),
        out_shape=jax.ShapeDtypeStruct((b, t, w), MXU_DTYPE),
        scratch_shapes=[
            pltpu.VMEM((t, LANES), i32),
            pltpu.VMEM((H_IDX * tq, LANES), MXU_DTYPE),
            pltpu.VMEM((LANES, LANES), f32),
            pltpu.VMEM((nh * tq, LANES), MXU_DTYPE),
            pltpu.VMEM((LANES, nh * tq), f32),
        ],
        compiler_params=_cparams(("arbitrary", "arbitrary")),
        name="dsa_prompt",
    )(qs, qi, kiwi, ki2, ks, vst, sg)


def _mem_attn_kernel(q_ref, k_ref, v_ref, sg_ref, o_ref, *, nh):
    def scores(h):
        hs = slice(h * D_MEM, (h + 1) * D_MEM)
        return _dot_nt(q_ref[:, hs], k_ref[:, hs].astype(MXU_DTYPE))

    s_next = scores(0)
    for h in range(nh):
        s = s_next
        if h + 1 < nh:
            s_next = scores(h + 1)
        hs = slice(h * D_MEM, (h + 1) * D_MEM)
        m = jnp.max(s, axis=1, keepdims=True)
        p = jnp.exp2(s - m)
        l = jnp.sum(p, axis=1, keepdims=True)
        o = _dot(p.astype(MXU_DTYPE), v_ref[:, hs].astype(MXU_DTYPE)) / l
        o_ref[:, hs] = (o * sg_ref[:, hs].astype(f32)).astype(o_ref.dtype)


def _mem_attn(q, k, v, sg, *, layer, sg_col0, tq):
    b, t, w = q.shape
    nm = k.shape[2]
    assert sg_col0 % w == 0
    return pl.pallas_call(
        functools.partial(_mem_attn_kernel, nh=w // D_MEM),
        grid=(b, t // tq),
        in_specs=[
            pl.BlockSpec((None, tq, w), lambda bb, i: (bb, i, 0)),
            pl.BlockSpec((None, None, nm, w), lambda bb, i: (layer, bb, 0, 0)),
            pl.BlockSpec((None, None, nm, w), lambda bb, i: (layer, bb, 0, 0)),
            pl.BlockSpec((None, tq, w), lambda bb, i: (bb, i, sg_col0 // w)),
        ],
        out_specs=pl.BlockSpec((None, tq, w), lambda bb, i: (bb, i, 0)),
        out_shape=jax.ShapeDtypeStruct((b, t, w), MXU_DTYPE),
        compiler_params=_cparams(("arbitrary", "arbitrary")),
        name="mem_attn",
    )(q, k, v, sg)


def _diff_sample_kernel(pt_ref, lam_ref, qm_ref, *refs, nh, nt, pps, nsteps, post_scale):
    kt_refs = refs[:pps]
    v_refs = refs[pps:2 * pps]
    knewt_ref, vnew_ref, g_ref, sg_ref, o_ref, m_scr, l_scr, acc_scr = refs[2 * pps:]
    s_id = pl.program_id(1)
    nr = 2 * nt

    @pl.when(s_id == 0)
    def _():
        m_scr[...] = jnp.full_like(m_scr, NEG_BIG)
        l_scr[...] = jnp.zeros_like(l_scr)
        acc_scr[...] = jnp.zeros_like(acc_scr)

    def update(kts, vs, masked):
        s = jnp.concatenate(
            [jnp.concatenate([_dot(qm_ref[h * nr:(h + 1) * nr, :], kt[h].astype(MXU_DTYPE)) for kt in kts], axis=1)
             for h in range(nh)], axis=0)
        if masked:
            tpos = _iota(s.shape, 0) % nt
            s = jnp.where(_iota(s.shape, 1) <= tpos, s, NEG_BIG)
        m = m_scr[...]
        m_new = jnp.maximum(m, jnp.max(s, axis=1, keepdims=True))
        alpha = jnp.exp2(m - m_new)
        p = jnp.exp2(s - m_new)
        l_scr[...] = alpha * l_scr[...] + jnp.sum(p, axis=1, keepdims=True)
        pb = p.astype(MXU_DTYPE)
        kw = s.shape[1] // len(kts)
        pv = []
        for h in range(nh):
            acc = None
            for r, v in enumerate(vs):
                d = _dot(pb[h * nr:(h + 1) * nr, r * kw:(r + 1) * kw], v[h].astype(MXU_DTYPE))
                acc = d if acc is None else acc + d
            pv.append(acc)
        acc_scr[...] = alpha * acc_scr[...] + jnp.concatenate(pv, axis=0)
        m_scr[...] = m_new

    @pl.when(s_id < nsteps)
    def _():
        update(kt_refs, v_refs, False)

    @pl.when(s_id == nsteps)
    def _():
        update([knewt_ref], [vnew_ref], True)
        o = acc_scr[...] / l_scr[...]
        for h in range(nh):
            od = o[h * nr:h * nr + nt] - lam_ref[0] * o[h * nr + nt:(h + 1) * nr]
            sl = slice(h * LANES, (h + 1) * LANES)
            o_ref[:, sl] = _sub_rmsnorm_gate(od, g_ref[...], post_scale, sg_ref[:, sl]).astype(o_ref.dtype)


def _diff_sample(page_table, lam, qm, cache_kt, cache_v, knewt, vnew, g_sub, sg, *, layer, nt, post_scale):
    db, rows, _ = qm.shape
    nh = cache_v.shape[2]
    page = cache_v.shape[3]
    npages = page_table.shape[1]
    pps = math.gcd(PAGES_PER_STEP, npages)
    nsteps = npages // pps
    w = nh * LANES

    def page_map(r):
        def f(bb, s, pt):
            return (layer, pt[bb, jnp.minimum(s * pps + r, npages - 1)], 0, 0, 0)
        return f

    kt_specs = [pl.BlockSpec((None, None, nh, LANES, page), page_map(r)) for r in range(pps)]
    v_specs = [pl.BlockSpec((None, None, nh, page, LANES), page_map(r)) for r in range(pps)]
    grid_spec = pltpu.PrefetchScalarGridSpec(
        num_scalar_prefetch=1,
        grid=(db, nsteps + 1),
        in_specs=[
            pl.BlockSpec(memory_space=pltpu.SMEM),
            pl.BlockSpec((None, rows, LANES), lambda bb, s, pt: (bb, 0, 0)),
            *kt_specs, *v_specs,
            pl.BlockSpec((None, nh, LANES, page), lambda bb, s, pt: (bb, 0, 0, 0)),
            pl.BlockSpec((None, nh, page, LANES), lambda bb, s, pt: (bb, 0, 0, 0)),
            pl.BlockSpec((1, LANES), lambda bb, s, pt: (0, 0)),
            pl.BlockSpec((None, nt, w), lambda bb, s, pt: (bb, 0, 0)),
        ],
        out_specs=pl.BlockSpec((None, nt, w), lambda bb, s, pt: (bb, 0, 0)),
        scratch_shapes=[
            pltpu.VMEM((rows, 1), f32),
            pltpu.VMEM((rows, 1), f32),
            pltpu.VMEM((rows, LANES), f32),
        ],
    )
    return pl.pallas_call(
        functools.partial(_diff_sample_kernel, nh=nh, nt=nt, pps=pps, nsteps=nsteps, post_scale=post_scale),
        grid_spec=grid_spec,
        out_shape=jax.ShapeDtypeStruct((db, nt, w), MXU_DTYPE),
        compiler_params=_cparams(("arbitrary", "arbitrary")),
        name="diff_sample",
    )(page_table, lam, qm, *([cache_kt] * pps), *([cache_v] * pps), knewt, vnew, g_sub, sg)


def _dsa_sample_kernel(pt_ref, qs_ref, qi_ref, w_ref, *refs, nt, pps, nsteps, topk, page):
    kit_refs = refs[:pps]
    ks_refs = refs[pps:2 * pps]
    vs_refs = refs[2 * pps:3 * pps]
    kint_ref, ksn_ref, vsn_ref, sg_ref, o_ref, key_scr, s_scr, v_scr = refs[3 * pps:]
    s_id = pl.program_id(1)
    gsz = LANES // nt

    def process(blk, kit, ks, vs, causal_new):
        sc = _dot(qi_ref[...], kit.astype(MXU_DTYPE))
        val = jnp.maximum(sc, 0.0) * w_ref[...]
        acc = jnp.sum(val.reshape(gsz, nt, page), axis=0)
        acc = jnp.where(acc == 0.0, 0.0, acc)
        if causal_new:
            acc = jnp.where(_iota(acc.shape, 1) <= _iota(acc.shape, 0), acc, -jnp.inf)
        key_scr[blk] = _f2key(acc)
        s_scr[blk] = _dot_nt(qs_ref[...], ks.astype(MXU_DTYPE))
        v_scr[blk] = vs.astype(MXU_DTYPE)

    @pl.when(s_id < nsteps)
    def _():
        for r in range(pps):
            process(s_id * pps + r, kit_refs[r][...], ks_refs[r][...], vs_refs[r][...], False)

    @pl.when(s_id == nsteps)
    def _():
        nblk = nsteps * pps + 1
        process(nblk - 1, kint_ref[...], ksn_ref[...], vsn_ref[...], True)

        keys = jnp.concatenate([key_scr[c] for c in range(nblk)], axis=1)

        def count_ge_all(cand):
            return jnp.sum(jnp.where(keys >= cand, 1.0, 0.0), axis=1, keepdims=True)

        def bit_body(it, prefix):
            cand = prefix + lax.shift_left(jnp.int32(1), 31 - it)
            return jnp.where(count_ge_all(cand) >= topk, cand, prefix)

        kth = lax.fori_loop(0, 32, bit_body, jnp.full((nt, 1), INT_MIN, i32))
        _demote_surplus_ties_lane(key_scr, nblk, nt, topk, kth, count_ge_all)
        thr = jnp.maximum(kth, KEY_NEG_INF + 1)

        unroll = max(u for u in (5, 4, 3, 2, 1) if nblk % u == 0)

        def sel_scores(c):
            msk = key_scr[c] >= thr
            return msk[None], s_scr[c].reshape(gsz, nt, page)

        def max_body(it, mx):
            for u in range(unroll):
                msk, s3 = sel_scores(it * unroll + u)
                mx = jnp.maximum(mx, jnp.where(msk, s3, NEG_BIG))
            return mx

        mx = lax.fori_loop(0, nblk // unroll, max_body, jnp.full((gsz, nt, page), NEG_BIG, f32))
        m = jnp.max(mx, axis=2, keepdims=True)

        def att_body(it, carry):
            lacc, acc = carry
            for u in range(unroll):
                c = it * unroll + u
                msk, s3 = sel_scores(c)
                p = jnp.where(msk, jnp.exp2(s3 - m), 0.0)
                lacc = lacc + p
                acc = acc + _dot(p.reshape(gsz * nt, page).astype(MXU_DTYPE), v_scr[c])
            return lacc, acc

        lacc, acc = lax.fori_loop(0, nblk // unroll, att_body,
                                  (jnp.zeros((gsz, nt, page), f32), jnp.zeros((LANES, D_DSA), f32)))
        l = jnp.sum(lacc, axis=2, keepdims=True).reshape(gsz * nt, 1)
        o_ref[...] = (acc / l * sg_ref[...].astype(f32)).astype(o_ref.dtype)


def _dsa_sample(page_table, qs_rows, qi_rows, wcol, cache_kit, cache_ks, cache_vs, kit_new, ks_new, vs_new, sg_perm,
                *, layer, nt, topk):
    db = qs_rows.shape[0]
    page = cache_ks.shape[2]
    npages = page_table.shape[1]
    pps = math.gcd(PAGES_PER_STEP, npages)
    nsteps = npages // pps
    nblk = npages + 1
    assert page == LANES

    def page_map(r):
        def f(bb, s, pt):
            return (layer, pt[bb, jnp.minimum(s * pps + r, npages - 1)], 0, 0)
        return f

    def pspecs(shape):
        return [pl.BlockSpec((None, None) + shape, page_map(r)) for r in range(pps)]

    per_b = lambda shape: pl.BlockSpec((None,) + shape, lambda bb, s, pt: (bb, 0, 0))
    grid_spec = pltpu.PrefetchScalarGridSpec(
        num_scalar_prefetch=1,
        grid=(db, nsteps + 1),
        in_specs=[
            per_b((LANES, D_DSA)), per_b((LANES, D_IDX)), per_b((LANES, 1)),
            *pspecs((D_IDX, page)), *pspecs((page, D_DSA)), *pspecs((page, D_DSA)),
            per_b((D_IDX, page)), per_b((page, D_DSA)), per_b((page, D_DSA)),
            per_b((LANES, D_DSA)),
        ],
        out_specs=per_b((LANES, D_DSA)),
        scratch_shapes=[
            pltpu.VMEM((nblk, nt, page), i32),
            pltpu.VMEM((nblk, LANES, page), f32),
            pltpu.VMEM((nblk, page, D_DSA), MXU_DTYPE),
        ],
    )
    return pl.pallas_call(
        functools.partial(_dsa_sample_kernel, nt=nt, pps=pps, nsteps=nsteps, topk=topk, page=page),
        grid_spec=grid_spec,
        out_shape=jax.ShapeDtypeStruct((db, LANES, D_DSA), MXU_DTYPE),
        compiler_params=_cparams(("arbitrary", "arbitrary")),
        name="dsa_sample",
    )(page_table, qs_rows, qi_rows, wcol, *([cache_kit] * pps), *([cache_ks] * pps), *([cache_vs] * pps),
      kit_new, ks_new, vs_new, sg_perm)


def _rope_tables(pos, head_dim):
    half = head_dim // 2
    lane = jnp.arange(LANES)
    inv = ROPE_THETA ** (-(lane % half).astype(f32) / half)
    ang = pos.astype(f32)[:, None] * inv[None, :]
    sign = jnp.where((lane % head_dim) < half, -1.0, 1.0).astype(f32)
    return jnp.cos(ang), jnp.sin(ang) * sign[None, :]


def _tile_gain(g, n):
    return jnp.tile(g, n // g.shape[0]).reshape(1, n).astype(f32)


def _mixer_inputs(x2d, pos_tab, w_in_t, layer, seg, gl, *, tm, stacked=None):
    cos64, sin64, cos128, sin128 = pos_tab
    m = x2d.shape[0]
    h = _rmsnorm(x2d, gl["g_in"], min(tm, 256))
    tn = 512
    proj = functools.partial(_proj, h, tm=tm, nt=True)
    wseg = lambda name: _win_desc(w_in_t, layer, seg[name][0], seg[name][1] - seg[name][0], tn)
    rows = lambda n, dt: _rows_out(m, n, dt, tm, tn)
    rope64 = [("row", cos64), ("row", sin64)]
    rope128 = [("row", cos128), ("row", sin128)]
    nqd = seg["qd"][1] - seg["qd"][0]
    nqs = seg["qs"][1] - seg["qs"][0]
    nqm = seg["qm"][1] - seg["qm"][0]
    ngate = seg["gate"][1] - seg["gate"][0]
    norm_rope64 = functools.partial(_chunks_norm_rope, gsize=D_DH, half=D_DH // 2)
    gq64 = [("col", _tile_gain(gl["g_q_diff"], nqd) * (D_DH ** -0.5 * LOG2E))] + rope64
    gk64 = [("col", _tile_gain(gl["g_k_diff"], nqd))] + rope64
    gq128 = [("col", _tile_gain(gl["g_q_dsa"], nqs) * (D_DSA ** -0.5 * LOG2E))] + rope128
    qi_scale = D_IDX ** -0.5 * H_IDX ** -0.5
    out = {}
    (out["qd"],) = proj(wseg("qd"), norm_rope64, _store_rows, [rows(nqd, MXU_DTYPE)], gq64, name="proj_qd")
    (out["qs"],) = proj(wseg("qs"), functools.partial(_chunks_norm_rope, gsize=D_DSA, half=D_DSA // 2), _store_rows,
                        [rows(nqs, MXU_DTYPE)], gq128, name="proj_qs")
    (out["qi"],) = proj(wseg("qi"), functools.partial(_chunks_rope, half=D_IDX // 2), _store_rows,
                        [rows(H_IDX * D_IDX, MXU_DTYPE)], [("row", cos64 * qi_scale), ("row", sin64 * qi_scale)],
                        name="proj_qi")
    (out["qm"],) = proj(wseg("qm"), _chunks_norm256, _store_rows, [rows(nqm, MXU_DTYPE)],
                        [("col", _tile_gain(gl["g_q_mem"], nqm) * (D_MEM ** -0.5 * LOG2E))], name="proj_qm")
    (out["sg"],) = proj(wseg("gate"), _chunks_silu, _store_rows, [rows(ngate, MXU_DTYPE)], name="proj_gate")

    ks0, vs0, ki0 = seg["ks"][0], seg["vs"][0], seg["ki"][0]
    assert vs0 - ks0 == LANES and ks0 % LANES == 0 and ki0 % LANES == 0
    skip = (ki0 - vs0) // LANES - 1
    w_ksv = dict(array=w_in_t, n=3 * LANES, tn=LANES,
                 spec=pl.BlockSpec((None, LANES, w_in_t.shape[2]),
                                   lambda i, j: (layer, ks0 // LANES + j + jnp.where(j == 2, skip, 0), 0)))
    g_kiwi = jnp.concatenate([gl["g_k_idx"], jnp.ones((LANES - D_IDX,), f32)]).reshape(1, LANES)
    aux_ksv = [("const", gl["g_k_dsa"].reshape(1, D_DSA))] + rope128 + [("const", g_kiwi)] + rope64
    blk = lambda dt: dict(shape=(m, LANES), dtype=dt, block=(tm, LANES), index=lambda i, j: (i, 0), alias=None)
    if stacked is None:
        out["kd"], out["kd_c"] = proj(wseg("kd"), norm_rope64, _store_rows,
                                      [rows(nqd, f32), rows(nqd, MXU_DTYPE)], gk64, name="proj_kd")
        out["vd"], out["vd_c"] = proj(wseg("vd"), _chunks_raw, _store_rows,
                                      [rows(nqd, f32), rows(nqd, MXU_DTYPE)], name="proj_vd")
        out["ks"], out["ks_c"], out["vs"], out["vs_c"], out["kiwi"], out["ki2"] = proj(
            w_ksv, _chunks_ks_vs_kiwi, _store_ks_vs_kiwi,
            [blk(f32), blk(MXU_DTYPE), blk(f32), blk(MXU_DTYPE), blk(f32), blk(MXU_DTYPE)], aux_ksv,
            name="proj_ks_vs_kiwi", pass_col=True)
        return out

    l = layer
    b, t = stacked["b"], stacked["t"]
    nrb = t // tm
    nh = nqd // LANES
    hb = _col_tile(nqd, tn) // LANES
    vkb = stacked["diff_tk"]
    assert tm % vkb == 0
    depth = stacked["pdk"].shape[0]
    out["pdk"], out["kd_c"] = proj(
        wseg("kd"), norm_rope64, _store_kd_prompt,
        [dict(shape=(depth, b, nh, LANES, t), dtype=f32, block=(None, None, hb, LANES, tm),
              index=lambda i, j: (l, i // nrb, j, 0, i % nrb), alias=stacked["pdk"]),
         rows(nqd, MXU_DTYPE)], gk64, name="proj_kd")
    out["pdv"], out["vd_c"] = proj(
        wseg("vd"), _chunks_raw, _store_vd_prompt,
        [dict(shape=(depth, b, nh, t, LANES), dtype=f32, block=(None, None, hb, tm, LANES),
              index=lambda i, j: (l, i // nrb, j, i % nrb, 0), alias=stacked["pdv"]),
         dict(shape=(b, nh, t // vkb, LANES, vkb), dtype=MXU_DTYPE, block=(None, hb, tm // vkb, LANES, vkb),
              index=lambda i, j: (i // nrb, j, i % nrb, 0, 0), alias=None)], name="proj_vd")
    tok_major = lambda key: dict(shape=(depth, b, t, D_DSA), dtype=f32, block=(None, None, tm, D_DSA),
                                 index=lambda i, j: (l, i // nrb, i % nrb, 0), alias=stacked[key])
    out["psk"], out["ks_c"], out["psv"], out["vs_c"], out["kiwi"], out["ki2"], out["pik"] = proj(
        w_ksv, _chunks_ks_vs_kiwi, _store_ks_vs_kiwi,
        [tok_major("psk"), blk(MXU_DTYPE), tok_major("psv"), blk(MXU_DTYPE), blk(f32), blk(MXU_DTYPE),
         dict(shape=(depth, b, D_IDX, t), dtype=f32, block=(None, None, D_IDX, tm),
              index=lambda i, j: (l, i // nrb, 0, i % nrb), alias=stacked["pik"])], aux_ksv, name="proj_ks_vs_kiwi",
        pass_col=True)
    return out


def kernel(x_prompt, x_sample, mem_prompt, cache_diff_k, cache_diff_v, cache_dsa_k, cache_dsa_v, cache_idx_k,
           cache_mem_k, cache_mem_v, page_table, w_in, w_out, w_mem_kv, g_in, g_mem, g_q_diff, g_k_diff,
           g_sub_diff, lam_q1, lam_k1, lam_q2, lam_k2, g_q_dsa, g_k_dsa, g_k_idx, g_q_mem, g_k_mem):
    depth = w_in.shape[0]
    b, t, d = x_prompt.shape
    db, nt, _ = x_sample.shape
    n_mem = mem_prompt.shape[1]
    n_phys, page = cache_dsa_k.shape[1], cache_dsa_k.shape[2]
    npages = page_table.shape[1]
    past = npages * page
    h_diff = cache_diff_k.shape[3]
    w_diff = h_diff * 2 * D_DH
    h_dsa = (3 * d // 8) // D_DSA
    w_dsa = h_dsa * D_DSA
    h_mem = cache_mem_k.shape[3]
    w_mem = h_mem * D_MEM
    gsz = LANES // nt
    assert w_diff + w_dsa + w_mem == d and w_diff == w_dsa and LANES % nt == 0 and h_dsa <= gsz and gsz == H_IDX
    topk_p = min(TOPK_MAX, t // 4)
    topk_s = min(TOPK_MAX, (past + nt) // 4)
    m_p = b * t
    m_s = db * nt
    tm_p = min(1024, t)
    assert t % tm_p == 0

    widths = (w_diff, w_diff, w_diff, w_dsa, D_DSA, D_DSA, H_IDX * D_IDX, D_IDX, H_IDX, w_mem, d)
    offs = [0]
    for wdt in widths:
        offs.append(offs[-1] + wdt)
    names = ("qd", "kd", "vd", "qs", "ks", "vs", "qi", "ki", "wi", "qm", "gate")
    seg = {n: (offs[k], offs[k + 1]) for k, n in enumerate(names)}

    pos_p = jnp.arange(t, dtype=i32)
    pos_s = jnp.tile(past + jnp.arange(nt, dtype=i32), db)
    tab_p = _rope_tables(pos_p, D_DH) + _rope_tables(pos_p, D_DSA)
    tab_s = _rope_tables(pos_s, D_DH) + _rope_tables(pos_s, D_DSA)

    cdkt = cache_diff_k.transpose(0, 1, 3, 4, 5, 2).reshape(depth, n_phys, h_diff, 2 * D_DH, page)
    cdv = cache_diff_v.transpose(0, 1, 3, 2, 4)
    ckit = cache_idx_k.transpose(0, 1, 3, 2)
    cmk = cache_mem_k.reshape(depth, db, n_mem, w_mem)
    cmv = cache_mem_v.reshape(depth, db, n_mem, w_mem)
    w_in_t = jnp.swapaxes(w_in, 1, 2)

    x_p = x_prompt.reshape(m_p, d)
    x_s = x_sample.reshape(m_s, d)
    mem2d = mem_prompt.reshape(b * n_mem, d)
    stk = dict(pdk=jnp.zeros((depth, b, h_diff, 2 * D_DH, t), f32), pdv=jnp.zeros((depth, b, h_diff, t, 2 * D_DH), f32),
               psk=jnp.zeros((depth, b, t, D_DSA), f32), psv=jnp.zeros((depth, b, t, D_DSA), f32),
               pik=jnp.zeros((depth, b, D_IDX, t), f32))
    outs = {k: [] for k in ("pmk", "pmv", "sdk", "sdv", "ssk", "ssv", "sik")}

    for l in range(depth):
        gl = dict(g_in=g_in[l], g_q_diff=g_q_diff[l], g_k_diff=g_k_diff[l], g_q_dsa=g_q_dsa[l],
                  g_k_dsa=g_k_dsa[l], g_k_idx=g_k_idx[l], g_q_mem=g_q_mem[l])
        lam_init = 0.8 - 0.6 * math.exp(-0.3 * l)
        lam = (jnp.exp(jnp.sum(lam_q1[l] * lam_k1[l])) - jnp.exp(jnp.sum(lam_q2[l] * lam_k2[l])) + lam_init)
        lam = lam.astype(f32).reshape(1)
        g_sub = g_sub_diff[l].reshape(1, 2 * D_DH)
        post = 1.0 - lam_init

        diff_tq, diff_tk = min(DIFF_TQ, t), min(DIFF_TK, t)
        tp = _mixer_inputs(x_p, tab_p, w_in_t, l, seg, gl, tm=tm_p, stacked=dict(stk, b=b, t=t, diff_tk=diff_tk))
        for key in ("pdk", "pdv", "psk", "psv", "pik"):
            stk[key] = tp[key]
        hm = _rmsnorm(mem2d, g_mem[l], 256)
        mrows = lambda dt: [_rows_out(b * n_mem, w_mem, dt, b * n_mem, 512)]
        mk, = _proj(hm, _wcols_desc(w_mem_kv, l, 0, w_mem, 512), _chunks_norm256,
                    _store_rows, mrows(f32), [("col", _tile_gain(g_k_mem[l], w_mem))], tm=b * n_mem, nt=False,
                    name="proj_mk")
        mv, = _proj(hm, _wcols_desc(w_mem_kv, l, w_mem, w_mem, 512), _chunks_raw, _store_rows, mrows(f32),
                    tm=b * n_mem, nt=False, name="proj_mv")
        r3 = lambda a: a.reshape(b, t, a.shape[-1])
        sg = r3(tp["sg"])
        od = _diff_prompt(lam, r3(tp["qd"]), r3(tp["kd_c"]), tp["vd_c"], g_sub, sg, post_scale=post,
                          tq=diff_tq, tk=diff_tk)
        tc = 256
        vst = tp["vs_c"].reshape(b, t // tc, tc, D_DSA).transpose(0, 1, 3, 2)
        os_ = _dsa_prompt(r3(tp["qs"]), r3(tp["qi"]), r3(tp["kiwi"]), r3(tp["ki2"]), r3(tp["ks_c"]), vst, sg,
                          topk=topk_p, tc=tc)
        om = _mem_attn(r3(tp["qm"]), mk.reshape(1, b, n_mem, w_mem), mv.reshape(1, b, n_mem, w_mem), sg,
                       layer=0, sg_col0=w_diff + w_dsa, tq=min(512, t))
        x_p = _outproj(od.reshape(m_p, w_diff), os_.reshape(m_p, w_dsa), om.reshape(m_p, w_mem), w_out, x_p,
                       layer=l, tm=tm_p, tn=512)
        outs["pmk"].append(mk.reshape(b, n_mem, h_mem, D_MEM))
        outs["pmv"].append(mv.reshape(b, n_mem, h_mem, D_MEM))

        ts = _mixer_inputs(x_s, tab_s, w_in_t, l, seg, gl, tm=m_s)
        s3 = lambda a: a.reshape(db, nt, a.shape[-1])
        sg_s = s3(ts["sg"])
        tokpad = lambda a, axis: jnp.pad(a, [(0, page - nt) if ax == axis else (0, 0) for ax in range(a.ndim)])
        q5 = ts["qd"].reshape(db, nt, h_diff, 2, D_DH).transpose(0, 2, 3, 1, 4)
        eye_c = jnp.eye(2, dtype=MXU_DTYPE)
        qm = (q5[:, :, :, :, None, :] * eye_c[None, None, :, None, :, None]).reshape(db, h_diff * 2 * nt, 2 * D_DH)
        kd4 = ts["kd"].reshape(db, nt, h_diff, 2 * D_DH)
        vd4 = ts["vd"].reshape(db, nt, h_diff, 2 * D_DH)
        od_s = _diff_sample(page_table, lam, qm, cdkt, cdv, tokpad(kd4.transpose(0, 2, 3, 1), 3),
                            tokpad(vd4.transpose(0, 2, 1, 3), 2), g_sub, sg_s, layer=l, nt=nt, post_scale=post)
        qs4 = ts["qs"].reshape(db, nt, h_dsa, D_DSA)
        slot_major = lambda a: a.transpose(0, 2, 1, 3).reshape(db, LANES, a.shape[-1])
        qs_rows = slot_major(jnp.pad(qs4, ((0, 0), (0, 0), (0, gsz - h_dsa), (0, 0))))
        qi_rows = slot_major(ts["qi"].reshape(db, nt, H_IDX, D_IDX))
        wcol = slot_major(ts["kiwi"][:, D_IDX:D_IDX + H_IDX].reshape(db, nt, H_IDX, 1))
        sg_dsa = sg_s[:, :, w_diff:w_diff + w_dsa].reshape(db, nt, h_dsa, D_DSA)
        sg_perm = slot_major(jnp.pad(sg_dsa, ((0, 0), (0, 0), (0, gsz - h_dsa), (0, 0))))
        ki_new = s3(ts["kiwi"][:, :D_IDX])
        os_s = _dsa_sample(page_table, qs_rows, qi_rows, wcol, ckit, cache_dsa_k, cache_dsa_v,
                           tokpad(ki_new.transpose(0, 2, 1), 2), tokpad(s3(ts["ks"]), 1), tokpad(s3(ts["vs"]), 1),
                           sg_perm, layer=l, nt=nt, topk=topk_s)
        os_s = os_s.reshape(db, gsz, nt, D_DSA)[:, :h_dsa].transpose(0, 2, 1, 3).reshape(m_s, w_dsa)
        om_s = _mem_attn(s3(ts["qm"]), cmk, cmv, sg_s, layer=l, sg_col0=w_diff + w_dsa, tq=nt)
        x_s = _outproj(od_s.reshape(m_s, w_diff), os_s, om_s.reshape(m_s, w_mem), w_out, x_s, layer=l, tm=m_s,
                       tn=512)
        outs["sdk"].append(ts["kd"].reshape(db, nt, h_diff, 2, D_DH))
        outs["sdv"].append(ts["vd"].reshape(db, nt, h_diff, 2 * D_DH))
        outs["ssk"].append(ts["ks"].reshape(db, nt, D_DSA))
        outs["ssv"].append(ts["vs"].reshape(db, nt, D_DSA))
        outs["sik"].append(ki_new)

    st = lambda k: jnp.stack(outs[k])
    p_diff_k = stk["pdk"].reshape(depth, b, h_diff, 2, D_DH, t).transpose(0, 1, 5, 2, 3, 4)
    p_diff_v = stk["pdv"].transpose(0, 1, 3, 2, 4)
    p_idx_k = stk["pik"].transpose(0, 1, 3, 2)
    return (x_p.reshape(b, t, d), x_s.reshape(db, nt, d),
            p_diff_k, p_diff_v, stk["psk"], stk["psv"], p_idx_k, st("pmk"), st("pmv"),
            st("sdk"), st("sdv"), st("ssk"), st("ssv"), st("sik"))
```

```python
import functools
import math

import jax
import jax.numpy as jnp
from jax import lax
from jax.experimental import pallas as pl
from jax.experimental.pallas import tpu as pltpu

EPS = 1e-6
ROPE_THETA = 10000.0
TOPK_MAX = 256
LANES = 128
D_DH = 64
D_DSA = 128
D_IDX = 64
H_IDX = 16
D_MEM = 256
NEG_BIG = -1e30
KEY_NEG_INF = -2139095041
INT_MIN = -2147483648
MXU_DTYPE = jnp.bfloat16
VMEM_LIMIT_BYTES = 52 * 1024 * 1024
PAGES_PER_STEP = 8
DIFF_TQ = 512
DIFF_TK = 512
LOG2E = math.log2(math.e)

f32 = jnp.float32
i32 = jnp.int32


def _cparams(sem):
    return pltpu.CompilerParams(dimension_semantics=sem, vmem_limit_bytes=VMEM_LIMIT_BYTES)


def _dot(a, b):
    return jnp.dot(a, b, preferred_element_type=f32)


def _dot_nt(a, b):
    return lax.dot_general(a, b, (((1,), (1,)), ((), ())), preferred_element_type=f32)


def _iota(shape, dim):
    return lax.broadcasted_iota(i32, shape, dim)


def _group_mat(gsize, value):
    r = _iota((LANES, LANES), 0) // gsize
    c = _iota((LANES, LANES), 1) // gsize
    return jnp.where(r == c, value, 0.0).astype(MXU_DTYPE)


def _group_sum(x, gmat):
    return _dot(x.astype(MXU_DTYPE), gmat)


def _rope_chunk(n, cos, sin_signed, half):
    if 2 * half == LANES:
        rot = pltpu.roll(n, half, 1)
    else:
        first = (_iota(n.shape, 1) % (2 * half)) < half
        rot = jnp.where(first, pltpu.roll(n, LANES - half, 1), pltpu.roll(n, half, 1))
    return n * cos + rot * sin_signed


def _f2key(x):
    b = pltpu.bitcast(x, i32)
    return b ^ (lax.shift_right_arithmetic(b, 31) & 0x7FFFFFFF)


def _rmsnorm_kernel(x_ref, g_ref, o_ref):
    x = x_ref[...]
    ms = jnp.mean(x * x, axis=-1, keepdims=True)
    o_ref[...] = (x * lax.rsqrt(ms + EPS) * g_ref[...]).astype(o_ref.dtype)


def _rmsnorm(x, g, tm):
    m, d = x.shape
    return pl.pallas_call(
        _rmsnorm_kernel,
        grid=(m // tm,),
        in_specs=[pl.BlockSpec((tm, d), lambda i: (i, 0)), pl.BlockSpec((1, d), lambda i: (0, 0))],
        out_specs=pl.BlockSpec((tm, d), lambda i: (i, 0)),
        out_shape=jax.ShapeDtypeStruct((m, d), MXU_DTYPE),
        compiler_params=_cparams(("arbitrary",)),
        name="rmsnorm",
    )(x, g.reshape(1, d))


def _chunks_raw(z, aux, store):
    for c in range(z.shape[1] // LANES):
        store(c, z[:, c * LANES:(c + 1) * LANES])


def _chunks_silu(z, aux, store):
    for c in range(z.shape[1] // LANES):
        zc = z[:, c * LANES:(c + 1) * LANES]
        store(c, zc / (1.0 + jnp.exp(-zc)))


def _chunks_norm_rope(z, aux, store, *, gsize, half):
    gain, cos, sin = aux
    gmat = _group_mat(gsize, 1.0 / gsize)
    for c in range(z.shape[1] // LANES):
        sl = slice(c * LANES, (c + 1) * LANES)
        zc = z[:, sl]
        n = zc * lax.rsqrt(_group_sum(zc * zc, gmat) + EPS) * gain[:, sl]
        store(c, _rope_chunk(n, cos, sin, half))


def _chunks_rope(z, aux, store, *, half):
    cos, sin = aux
    for c in range(z.shape[1] // LANES):
        store(c, _rope_chunk(z[:, c * LANES:(c + 1) * LANES], cos, sin, half))


def _chunks_norm256(z, aux, store):
    (gain,) = aux
    gmat = _group_mat(LANES, 1.0 / D_MEM)
    for c in range(z.shape[1] // D_MEM):
        a = z[:, c * D_MEM:c * D_MEM + LANES]
        b = z[:, c * D_MEM + LANES:(c + 1) * D_MEM]
        inv = lax.rsqrt(_group_sum(a * a, gmat) + _group_sum(b * b, gmat) + EPS)
        for k, v in enumerate((a, b)):
            sl = slice(c * D_MEM + k * LANES, c * D_MEM + (k + 1) * LANES)
            store(2 * c + k, v * inv * gain[:, sl])


def _chunks_kiwi(z, aux, store):
    gain, cos, sin = aux
    z = z[:, :]
    gmat = _group_mat(D_IDX, 1.0 / D_IDX)
    n = z * lax.rsqrt(_group_sum(z * z, gmat) + EPS) * gain
    r = _rope_chunk(n, cos, sin, D_IDX // 2)
    lo = _iota(z.shape, 1) < D_IDX
    store(0, jnp.where(lo, r, z))
    r_lo = jnp.where(lo, r, 0.0)
    store(1, r_lo + pltpu.roll(r_lo, D_IDX, 1))


def _store_rows(outs, c, r):
    for o in outs:
        o[:, c * LANES:(c + 1) * LANES] = r.astype(o.dtype)


def _store_kd_prompt(outs, c, r):
    outs[0][c] = r.T
    outs[1][:, c * LANES:(c + 1) * LANES] = r.astype(outs[1].dtype)


def _store_vd_prompt(outs, c, r):
    outs[0][c] = r
    rt = r.T.astype(outs[1].dtype)
    kb = outs[1].shape[-1]
    for kk in range(r.shape[0] // kb):
        outs[1][c, kk] = rt[:, kk * kb:(kk + 1) * kb]


def _chunks_ks_vs_kiwi(z, aux, store, *, col):
    gain_k, cos128, sin128, gain_kiwi, cos64, sin64 = aux
    j = col

    @pl.when(j == 0)
    def _():
        _chunks_norm_rope(z, (gain_k, cos128, sin128), functools.partial(store, 0), gsize=D_DSA, half=D_DSA // 2)

    @pl.when(j == 1)
    def _():
        store(1, 0, z[:, :])

    @pl.when(j == 2)
    def _():
        _chunks_kiwi(z, (gain_kiwi, cos64, sin64), functools.partial(store, 2))


def _store_ks_vs_kiwi(outs, which, c, r):
    if which < 2:
        outs[2 * which][...] = r
        outs[2 * which + 1][...] = r.astype(outs[2 * which + 1].dtype)
    elif c == 0:
        outs[4][...] = r
        if len(outs) > 6:
            outs[6][...] = r.T[:D_IDX, :]
    else:
        outs[5][...] = r.astype(outs[5].dtype)


class _LazyTile:
    def __init__(self, ref):
        self.ref = ref
        self.shape = ref.shape

    def __getitem__(self, idx):
        return self.ref[idx]


def _proj_kernel(*refs, chunk_fn, store_fn, n_aux, n_alias, nt, nj, ntiles, pass_col):
    h_ref, w_ref = refs[:2]
    aux_refs = refs[2:2 + n_aux]
    outs = refs[2 + n_aux + n_alias:-1]
    z_scr = refs[-1]
    s = pl.program_id(0)

    def epilogue():
        kw = dict(col=(s - 1) % nj) if pass_col else {}
        chunk_fn(_LazyTile(z_scr), [r[...] for r in aux_refs], functools.partial(store_fn, outs), **kw)

    def matmul():
        w = w_ref[...].astype(MXU_DTYPE)
        z_scr[...] = _dot_nt(h_ref[...], w) if nt else _dot(h_ref[...], w)

    @pl.when(s == 0)
    def _():
        matmul()

    @pl.when(jnp.logical_and(s > 0, s < ntiles))
    def _():
        epilogue()
        matmul()

    @pl.when(s == ntiles)
    def _():
        epilogue()


def _win_desc(w_in_t, layer, off, n, tn):
    tn = _col_tile(n, tn)
    k = w_in_t.shape[2]
    if off % tn == 0:
        return dict(array=w_in_t, spec=pl.BlockSpec((None, tn, k), lambda i, j: (layer, off // tn + j, 0)), n=n, tn=tn)
    row0 = layer * w_in_t.shape[1] + off
    assert row0 % 8 == 0
    spec = pl.BlockSpec((pl.Element(tn), pl.Element(k)), lambda i, j: (pl.multiple_of(row0 + j * tn, 8), 0))
    return dict(array=w_in_t.reshape(-1, k), spec=spec, n=n, tn=tn)


def _wcols_desc(w, layer, off, n, tn):
    tn = _col_tile(n, tn)
    assert off % tn == 0
    return dict(array=w, spec=pl.BlockSpec((None, w.shape[1], tn), lambda i, j: (layer, 0, off // tn + j)), n=n, tn=tn)


def _col_tile(n, tn):
    return math.gcd(n, tn)


def _rows_out(m, n, dtype, tm, tn):
    tn = _col_tile(n, tn)
    return dict(shape=(m, n), dtype=dtype, block=(tm, tn), index=lambda i, j: (i, j), alias=None)


def _proj(h, wd, chunk_fn, store_fn, outs, aux=(), *, tm, nt, name, pass_col=False):
    m, k = h.shape
    n, tn = wd["n"], wd["tn"]
    assert m % tm == 0 and tn % LANES == 0 and n % tn == 0
    nj = n // tn
    ntiles = (m // tm) * nj
    cur = lambda f: (lambda s: f(jnp.minimum(s, ntiles - 1) // nj, jnp.minimum(s, ntiles - 1) % nj))
    prev = lambda f: (lambda s: f(jnp.maximum(s - 1, 0) // nj, jnp.maximum(s - 1, 0) % nj))
    in_specs = [pl.BlockSpec((tm, k), cur(lambda i, j: (i, 0))),
                pl.BlockSpec(wd["spec"].block_shape, cur(wd["spec"].index_map))]
    args = [h, wd["array"]]
    for kind, a in aux:
        if kind == "col":
            in_specs.append(pl.BlockSpec((1, tn), prev(lambda i, j: (0, j))))
        elif kind == "const":
            in_specs.append(pl.BlockSpec((1, LANES), prev(lambda i, j: (0, 0))))
        else:
            nrb = a.shape[0] // tm
            in_specs.append(pl.BlockSpec((tm, LANES), prev(lambda i, j, nrb=nrb: (i % nrb, 0))))
        args.append(a)
    aliases = {}
    for k_out, o in enumerate(outs):
        if o["alias"] is not None:
            aliases[len(args)] = k_out
            in_specs.append(pl.BlockSpec(memory_space=pl.ANY))
            args.append(o["alias"])
    return pl.pallas_call(
        functools.partial(_proj_kernel, chunk_fn=chunk_fn, store_fn=store_fn, n_aux=len(aux),
                          n_alias=len(aliases), nt=nt, nj=nj, ntiles=ntiles, pass_col=pass_col),
        grid=(ntiles + 1,),
        in_specs=in_specs,
        out_specs=[pl.BlockSpec(o["block"], prev(o["index"])) for o in outs],
        out_shape=[jax.ShapeDtypeStruct(o["shape"], o["dtype"]) for o in outs],
        scratch_shapes=[pltpu.VMEM((tm, tn), f32)],
        input_output_aliases=aliases,
        compiler_params=_cparams(("arbitrary",)),
        name=name,
    )(*args)


def _outproj_kernel(od_ref, os_ref, om_ref, w_ref, x_ref, o_ref, *, wd, ws):
    acc = _dot(od_ref[...], w_ref[0:wd, :].astype(MXU_DTYPE))
    acc += _dot(os_ref[...], w_ref[wd:wd + ws, :].astype(MXU_DTYPE))
    acc += _dot(om_ref[...], w_ref[wd + ws:, :].astype(MXU_DTYPE))
    o_ref[...] = x_ref[...] + acc


def _outproj(od, os_, om, w, x, *, layer, tm, tn):
    m, d = x.shape
    wd, ws, wm = od.shape[1], os_.shape[1], om.shape[1]
    return pl.pallas_call(
        functools.partial(_outproj_kernel, wd=wd, ws=ws),
        grid=(m // tm, d // tn),
        in_specs=[
            pl.BlockSpec((tm, wd), lambda i, j: (i, 0)),
            pl.BlockSpec((tm, ws), lambda i, j: (i, 0)),
            pl.BlockSpec((tm, wm), lambda i, j: (i, 0)),
            pl.BlockSpec((None, wd + ws + wm, tn), lambda i, j: (layer, 0, j)),
            pl.BlockSpec((tm, tn), lambda i, j: (i, j)),
        ],
        out_specs=pl.BlockSpec((tm, tn), lambda i, j: (i, j)),
        out_shape=jax.ShapeDtypeStruct((m, d), f32),
        compiler_params=_cparams(("arbitrary", "arbitrary")),
        name="outproj",
    )(od, os_, om, w, x)


def _sub_rmsnorm_gate(od, g, post_scale, sg):
    ms = jnp.mean(od * od, axis=-1, keepdims=True)
    return od * lax.rsqrt(ms + EPS) * g * post_scale * sg.astype(f32)


def _diff_prompt_kernel(pi_ref, pj_ref, lam_ref, q_ref, k_ref, vt_ref, g_ref, sg_ref, o_ref,
                        m_scr, l_scr, acc_scr, *, tq, tk, hb, post_scale):
    p_id = pl.program_id(2)
    i = pi_ref[p_id]
    j = pj_ref[p_id]

    @pl.when(j == 0)
    def _():
        m_scr[...] = jnp.full_like(m_scr, NEG_BIG)
        l_scr[...] = jnp.zeros_like(l_scr)
        acc_scr[...] = jnp.zeros_like(acc_scr)

    nchain = 2 * hb
    h2 = tq // 2
    split_diag = tq == tk and h2 % LANES == 0

    def masked_q(r):
        q = q_ref[:, (r // 2) * LANES:(r // 2 + 1) * LANES]
        lo = _iota(q.shape, 1) < D_DH
        zero = jnp.zeros_like(q)
        return jnp.where(lo, q, zero) if r % 2 == 0 else jnp.where(lo, zero, q)

    def keys(r):
        return k_ref[:, (r // 2) * LANES:(r // 2 + 1) * LANES]

    def scores(r):
        return _dot_nt(keys(r), masked_q(r))

    def scores_diag(r):
        qc, kc = masked_q(r), keys(r)
        return _dot_nt(kc[:h2], qc), _dot_nt(kc[h2:], qc[h2:])

    def causal(s, key0, qry0):
        return jnp.where(j * tk + key0 + _iota(s.shape, 0) <= i * tq + qry0 + _iota(s.shape, 1), s, NEG_BIG)

    def body(masked):
        s_next = scores(0)
        for r in range(nchain):
            s = s_next
            if r + 1 < nchain:
                s_next = scores(r + 1)
            if masked:
                s = causal(s, 0, 0)
            m = m_scr[r]
            m_new = jnp.maximum(m, jnp.max(s, axis=0, keepdims=True))
            alpha = jnp.exp2(m - m_new)
            p = jnp.exp2(s - m_new)
            l_scr[r] = alpha * l_scr[r] + jnp.sum(p, axis=0, keepdims=True)
            acc_scr[r] = alpha * acc_scr[r] + _dot(vt_ref[r // 2], p.astype(MXU_DTYPE))
            m_scr[r] = m_new

    def body_diag():
        s_next = scores_diag(0)
        for r in range(nchain):
            sa, sb = s_next
            if r + 1 < nchain:
                s_next = scores_diag(r + 1)
            sa = jnp.concatenate([causal(sa[:, :h2], 0, 0), sa[:, h2:]], axis=1)
            sb = causal(sb, h2, h2)
            m = m_scr[r]
            mb = jnp.concatenate([jnp.full((1, h2), NEG_BIG, f32), jnp.max(sb, axis=0, keepdims=True)], axis=1)
            m_new = jnp.maximum(m, jnp.maximum(jnp.max(sa, axis=0, keepdims=True), mb))
            alpha = jnp.exp2(m - m_new)
            pa = jnp.exp2(sa - m_new)
            pb = jnp.exp2(sb - m_new[:, h2:])
            lb = jnp.concatenate([jnp.zeros((1, h2), f32), jnp.sum(pb, axis=0, keepdims=True)], axis=1)
            l_scr[r] = alpha * l_scr[r] + jnp.sum(pa, axis=0, keepdims=True) + lb
            vt = vt_ref[r // 2]
            pva = _dot(vt[:, :h2], pa.astype(MXU_DTYPE))
            pvb = _dot(vt[:, h2:], pb.astype(MXU_DTYPE))
            acc_scr[r] = alpha * acc_scr[r] + pva + jnp.concatenate([jnp.zeros((LANES, h2), f32), pvb], axis=1)
            m_scr[r] = m_new

    crosses_diagonal = (j + 1) * tk - 1 > i * tq
    pl.when(crosses_diagonal)(body_diag if split_diag else functools.partial(body, True))
    pl.when(jnp.logical_not(crosses_diagonal))(functools.partial(body, False))

    @pl.when(j == ((i + 1) * tq - 1) // tk)
    def _():
        for hh in range(hb):
            hsl = slice(hh * LANES, (hh + 1) * LANES)
            od_t = acc_scr[2 * hh] / l_scr[2 * hh] - lam_ref[0] * (acc_scr[2 * hh + 1] / l_scr[2 * hh + 1])
            o_ref[:, hsl] = _sub_rmsnorm_gate(od_t.T, g_ref[...], post_scale, sg_ref[:, hsl]).astype(o_ref.dtype)


def _diff_prompt(lam, qd, kd, vdt, g_sub, sg, *, post_scale, tq, tk):
    b, t, w = qd.shape
    nh = w // LANES
    hb = max(g for g in (4, 2, 1) if nh % g == 0)
    pairs = [(i, j) for i in range(t // tq) for j in range(((i + 1) * tq - 1) // tk + 1)]
    pi = jnp.asarray([p[0] for p in pairs], i32)
    pj = jnp.asarray([p[1] for p in pairs], i32)
    grid_spec = pltpu.PrefetchScalarGridSpec(
        num_scalar_prefetch=2,
        grid=(b, nh // hb, len(pairs)),
        in_specs=[
            pl.BlockSpec(memory_space=pltpu.SMEM),
            pl.BlockSpec((None, tq, hb * LANES), lambda bb, h, p, pi, pj: (bb, pi[p], h)),
            pl.BlockSpec((None, tk, hb * LANES), lambda bb, h, p, pi, pj: (bb, pj[p], h)),
            pl.BlockSpec((None, hb, None, LANES, tk), lambda bb, h, p, pi, pj: (bb, h, pj[p], 0, 0)),
            pl.BlockSpec((1, LANES), lambda bb, h, p, pi, pj: (0, 0)),
            pl.BlockSpec((None, tq, hb * LANES), lambda bb, h, p, pi, pj: (bb, pi[p], h)),
        ],
        out_specs=pl.BlockSpec((None, tq, hb * LANES), lambda bb, h, p, pi, pj: (bb, pi[p], h)),
        scratch_shapes=[
            pltpu.VMEM((2 * hb, 1, tq), f32),
            pltpu.VMEM((2 * hb, 1, tq), f32),
            pltpu.VMEM((2 * hb, LANES, tq), f32),
        ],
    )
    return pl.pallas_call(
        functools.partial(_diff_prompt_kernel, tq=tq, tk=tk, hb=hb, post_scale=post_scale),
        grid_spec=grid_spec,
        out_shape=jax.ShapeDtypeStruct((b, t, w), MXU_DTYPE),
        compiler_params=_cparams(("arbitrary", "arbitrary", "arbitrary")),
        name="diff_prompt",
    )(pi, pj, lam, qd, kd, vdt, g_sub, sg)


def _kth_largest_key_sub(key_scr, nblk, rows, topk):
    def count_ge(cand):
        def body(c, acc):
            k = key_scr[pl.ds(pl.multiple_of(c * rows, rows), rows), :]
            return acc + jnp.sum((k >= cand).astype(i32).reshape(rows // 8, 8, LANES), axis=0)

        acc = lax.fori_loop(0, nblk, body, jnp.zeros((8, LANES), i32))
        return jnp.sum(acc, axis=0, keepdims=True)

    def bit_body(it, prefix):
        cand = prefix + lax.shift_left(jnp.int32(1), 31 - it)
        return jnp.where(count_ge(cand) >= topk, cand, prefix)

    kth = lax.fori_loop(0, 32, bit_body, jnp.full((1, LANES), INT_MIN, i32))
    return kth, count_ge


def _demote_surplus_ties_sub(key_scr, nblk, rows, topk, kth, count_ge):
    tie = (count_ge(kth) > topk) & (kth > KEY_NEG_INF)

    @pl.when(jnp.max(tie.astype(i32)) > 0)
    def _():
        need = (topk - count_ge(kth + 1)).astype(f32)
        tri = (_iota((rows, rows), 0) >= _iota((rows, rows), 1)).astype(f32).astype(MXU_DTYPE)

        def body(c, run):
            sl = pl.ds(pl.multiple_of(c * rows, rows), rows)
            k = key_scr[sl, :]
            eq = k == kth
            incl = _dot(tri, eq.astype(f32).astype(MXU_DTYPE))
            drop = eq & ((run + incl) > need)
            key_scr[sl, :] = jnp.where(drop, KEY_NEG_INF, k)
            return run + incl[rows - 1:rows, :]

        lax.fori_loop(0, nblk, body, jnp.zeros((1, LANES), f32))


def _demote_surplus_ties_lane(key_scr, nblk, nq, topk, kth, count_ge):
    tie = (count_ge(kth) > topk) & (kth > KEY_NEG_INF)

    @pl.when(jnp.max(jnp.where(tie, 1.0, 0.0)) > 0.0)
    def _():
        need = topk - count_ge(kth + 1)
        triu = (_iota((LANES, LANES), 0) <= _iota((LANES, LANES), 1)).astype(f32).astype(MXU_DTYPE)
        pad = jnp.zeros((16 - nq % 16, LANES), f32) if nq % 16 else None

        def body(c, run):
            k = key_scr[c]
            eq = k == kth
            eqf = jnp.where(eq, 1.0, 0.0)
            if pad is not None:
                eqf = jnp.concatenate([eqf, pad], axis=0)
            incl = _dot(eqf.astype(MXU_DTYPE), triu)[:nq]
            drop = eq & ((run + incl) > need)
            key_scr[c] = jnp.where(drop, KEY_NEG_INF, k)
            return run + incl[:, LANES - 1:LANES]

        lax.fori_loop(0, nblk, body, jnp.zeros((nq, 1), f32))


def _dsa_prompt_kernel(qs_ref, qi_ref, kiwi_ref, ki2_ref, ks_ref, vst_ref, sg_ref, o_ref,
                       key_scr, qm_scr, wt_scr, qst_scr, acc_scr, *, nh, topk, tc):
    qb = pl.program_id(1)
    tq = LANES
    nch = (qb * tq + tq + tc - 1) // tc

    lo = _iota((tq, LANES), 1) < D_IDX
    for h in range(H_IDX):
        chunk = qi_ref[:, (h // 2) * LANES:(h // 2 + 1) * LANES]
        keep = lo if h % 2 == 0 else jnp.logical_not(lo)
        qm_scr[h * tq:(h + 1) * tq, :] = jnp.where(keep, chunk, jnp.zeros_like(chunk))
    wt_scr[...] = kiwi_ref[...].T
    for h in range(nh):
        qst_scr[h * tq:(h + 1) * tq, :] = qs_ref[:, h * LANES:(h + 1) * LANES]

    tpos = qb * tq + _iota((1, LANES), 1)

    def idx_body(c, carry):
        sl = pl.ds(pl.multiple_of(c * tc, tc), tc)
        kc = ki2_ref[sl, :]
        acc = jnp.zeros((tc, LANES), f32)
        for hp in range(H_IDX // 2):
            sc = _dot_nt(kc, qm_scr[2 * hp * tq:(2 * hp + 2) * tq, :])
            for k in range(2):
                h = 2 * hp + k
                acc = acc + jnp.maximum(sc[:, k * tq:(k + 1) * tq], 0.0) * wt_scr[D_IDX + h:D_IDX + h + 1, :]
        acc = jnp.where(acc == 0.0, 0.0, acc)
        kpos = c * tc + _iota((tc, LANES), 0)
        acc = jnp.where(kpos <= tpos, acc, -jnp.inf)
        key_scr[sl, :] = _f2key(acc)
        return carry

    lax.fori_loop(0, nch, idx_body, 0)

    kth, count_ge = _kth_largest_key_sub(key_scr, nch, tc, topk)
    _demote_surplus_ties_sub(key_scr, nch, tc, topk, kth, count_ge)
    thr = jnp.maximum(kth, KEY_NEG_INF + 1)

    acc_scr[...] = jnp.zeros_like(acc_scr)

    def att_body(c, carry):
        m, l = carry
        sl = pl.ds(pl.multiple_of(c * tc, tc), tc)
        s = _dot_nt(ks_ref[sl, :], qst_scr[...])
        msk = key_scr[sl, :] >= thr
        s = jnp.where(jnp.concatenate([msk] * nh, axis=1), s, NEG_BIG)
        m_new = jnp.maximum(m, jnp.max(s, axis=0, keepdims=True))
        alpha = jnp.exp2(m - m_new)
        p = jnp.exp2(s - m_new)
        l = alpha * l + jnp.sum(p, axis=0, keepdims=True)
        acc_scr[...] = alpha * acc_scr[...] + _dot(vst_ref[c], p.astype(MXU_DTYPE))
        return m_new, l

    init = (jnp.full((1, nh * tq), NEG_BIG, f32), jnp.zeros((1, nh * tq), f32))
    _, l = lax.fori_loop(0, nch, att_body, init)
    out_t = acc_scr[...] / l
    for h in range(nh):
        sl = slice(h * LANES, (h + 1) * LANES)
        o_ref[:, sl] = (out_t[:, sl].T * sg_ref[:, sl].astype(f32)).astype(o_ref.dtype)


def _dsa_prompt(qs, qi, kiwi, ki2, ks, vst, sg, *, topk, tc=256):
    b, t, w = qs.shape
    nh = w // LANES
    tq = LANES
    assert t % tc == 0 and tc % tq == 0
    return pl.pallas_call(
        functools.partial(_dsa_prompt_kernel, nh=nh, topk=topk, tc=tc),
        grid=(b, t // tq),
        in_specs=[
            pl.BlockSpec((None, tq, w), lambda bb, i: (bb, i, 0)),
            pl.BlockSpec((None, tq, H_IDX * D_IDX), lambda bb, i: (bb, i, 0)),
            pl.BlockSpec((None, tq, LANES), lambda bb, i: (bb, i, 0)),
            pl.BlockSpec((None, t, LANES), lambda bb, i: (bb, 0, 0)),
            pl.BlockSpec((None, t, LANES), lambda bb, i: (bb, 0, 0)),
            pl.BlockSpec((None, t // tc, LANES, tc), lambda bb, i: (bb, 0, 0, 0)),
            pl.BlockSpec((None, tq, w), lambda bb, i: (bb, i, 1)),
        ],
        out_specs=pl.BlockSpec((None, tq, w), lambda bb, i: (bb, i, 0)),
        out_shape=jax.ShapeDtypeStruct((b, t, w), MXU_DTYPE),
        scratch_shapes=[
            pltpu.VMEM((t, LANES), i32),
            pltpu.VMEM((H_IDX * tq, LANES), MXU_DTYPE),
            pltpu.VMEM((LANES, LANES), f32),
            pltpu.VMEM((nh * tq, LANES), MXU_DTYPE),
            pltpu.VMEM((LANES, nh * tq), f32),
        ],
        compiler_params=_cparams(("arbitrary", "arbitrary")),
        name="dsa_prompt",
    )(qs, qi, kiwi, ki2, ks, vst, sg)


def _mem_attn_kernel(q_ref, k_ref, v_ref, sg_ref, o_ref, *, nh):
    def scores(h):
        hs = slice(h * D_MEM, (h + 1) * D_MEM)
        return _dot_nt(q_ref[:, hs], k_ref[:, hs].astype(MXU_DTYPE))

    s_next = scores(0)
    for h in range(nh):
        s = s_next
        if h + 1 < nh:
            s_next = scores(h + 1)
        hs = slice(h * D_MEM, (h + 1) * D_MEM)
        m = jnp.max(s, axis=1, keepdims=True)
        p = jnp.exp2(s - m)
        l = jnp.sum(p, axis=1, keepdims=True)
        o = _dot(p.astype(MXU_DTYPE), v_ref[:, hs].astype(MXU_DTYPE)) / l
        o_ref[:, hs] = (o * sg_ref[:, hs].astype(f32)).astype(o_ref.dtype)


def _mem_attn(q, k, v, sg, *, layer, sg_col0, tq):
    b, t, w = q.shape
    nm = k.shape[2]
    assert sg_col0 % w == 0
    return pl.pallas_call(
        functools.partial(_mem_attn_kernel, nh=w // D_MEM),
        grid=(b, t // tq),
        in_specs=[
            pl.BlockSpec((None, tq, w), lambda bb, i: (bb, i, 0)),
            pl.BlockSpec((None, None, nm, w), lambda bb, i: (layer, bb, 0, 0)),
            pl.BlockSpec((None, None, nm, w), lambda bb, i: (layer, bb, 0, 0)),
            pl.BlockSpec((None, tq, w), lambda bb, i: (bb, i, sg_col0 // w)),
        ],
        out_specs=pl.BlockSpec((None, tq, w), lambda bb, i: (bb, i, 0)),
        out_shape=jax.ShapeDtypeStruct((b, t, w), MXU_DTYPE),
        compiler_params=_cparams(("arbitrary", "arbitrary")),
        name="mem_attn",
    )(q, k, v, sg)


def _diff_sample_kernel(pt_ref, lam_ref, qm_ref, *refs, nh, nt, pps, nsteps, post_scale):
    kt_refs = refs[:pps]
    v_refs = refs[pps:2 * pps]
    knewt_ref, vnew_ref, g_ref, sg_ref, o_ref, m_scr, l_scr, acc_scr = refs[2 * pps:]
    s_id = pl.program_id(1)
    nr = 2 * nt

    @pl.when(s_id == 0)
    def _():
        m_scr[...] = jnp.full_like(m_scr, NEG_BIG)
        l_scr[...] = jnp.zeros_like(l_scr)
        acc_scr[...] = jnp.zeros_like(acc_scr)

    def update(kts, vs, masked):
        s = jnp.concatenate(
            [jnp.concatenate([_dot(qm_ref[h * nr:(h + 1) * nr, :], kt[h].astype(MXU_DTYPE)) for kt in kts], axis=1)
             for h in range(nh)], axis=0)
        if masked:
            tpos = _iota(s.shape, 0) % nt
            s = jnp.where(_iota(s.shape, 1) <= tpos, s, NEG_BIG)
        m = m_scr[...]
        m_new = jnp.maximum(m, jnp.max(s, axis=1, keepdims=True))
        alpha = jnp.exp2(m - m_new)
        p = jnp.exp2(s - m_new)
        l_scr[...] = alpha * l_scr[...] + jnp.sum(p, axis=1, keepdims=True)
        pb = p.astype(MXU_DTYPE)
        kw = s.shape[1] // len(kts)
        pv = []
        for h in range(nh):
            acc = None
            for r, v in enumerate(vs):
                d = _dot(pb[h * nr:(h + 1) * nr, r * kw:(r + 1) * kw], v[h].astype(MXU_DTYPE))
                acc = d if acc is None else acc + d
            pv.append(acc)
        acc_scr[...] = alpha * acc_scr[...] + jnp.concatenate(pv, axis=0)
        m_scr[...] = m_new

    @pl.when(s_id < nsteps)
    def _():
        update(kt_refs, v_refs, False)

    @pl.when(s_id == nsteps)
    def _():
        update([knewt_ref], [vnew_ref], True)
        o = acc_scr[...] / l_scr[...]
        for h in range(nh):
            od = o[h * nr:h * nr + nt] - lam_ref[0] * o[h * nr + nt:(h + 1) * nr]
            sl = slice(h * LANES, (h + 1) * LANES)
            o_ref[:, sl] = _sub_rmsnorm_gate(od, g_ref[...], post_scale, sg_ref[:, sl]).astype(o_ref.dtype)


def _diff_sample(page_table, lam, qm, cache_kt, cache_v, knewt, vnew, g_sub, sg, *, layer, nt, post_scale):
    db, rows, _ = qm.shape
    nh = cache_v.shape[2]
    page = cache_v.shape[3]
    npages = page_table.shape[1]
    pps = math.gcd(PAGES_PER_STEP, npages)
    nsteps = npages // pps
    w = nh * LANES

    def page_map(r):
        def f(bb, s, pt):
            return (layer, pt[bb, jnp.minimum(s * pps + r, npages - 1)], 0, 0, 0)
        return f

    kt_specs = [pl.BlockSpec((None, None, nh, LANES, page), page_map(r)) for r in range(pps)]
    v_specs = [pl.BlockSpec((None, None, nh, page, LANES), page_map(r)) for r in range(pps)]
    grid_spec = pltpu.PrefetchScalarGridSpec(
        num_scalar_prefetch=1,
        grid=(db, nsteps + 1),
        in_specs=[
            pl.BlockSpec(memory_space=pltpu.SMEM),
            pl.BlockSpec((None, rows, LANES), lambda bb, s, pt: (bb, 0, 0)),
            *kt_specs, *v_specs,
            pl.BlockSpec((None, nh, LANES, page), lambda bb, s, pt: (bb, 0, 0, 0)),
            pl.BlockSpec((None, nh, page, LANES), lambda bb, s, pt: (bb, 0, 0, 0)),
            pl.BlockSpec((1, LANES), lambda bb, s, pt: (0, 0)),
            pl.BlockSpec((None, nt, w), lambda bb, s, pt: (bb, 0, 0)),
        ],
        out_specs=pl.BlockSpec((None, nt, w), lambda bb, s, pt: (bb, 0, 0)),
        scratch_shapes=[
            pltpu.VMEM((rows, 1), f32),
            pltpu.VMEM((rows, 1), f32),
            pltpu.VMEM((rows, LANES), f32),
        ],
    )
    return pl.pallas_call(
        functools.partial(_diff_sample_kernel, nh=nh, nt=nt, pps=pps, nsteps=nsteps, post_scale=post_scale),
        grid_spec=grid_spec,
        out_shape=jax.ShapeDtypeStruct((db, nt, w), MXU_DTYPE),
        compiler_params=_cparams(("arbitrary", "arbitrary")),
        name="diff_sample",
    )(page_table, lam, qm, *([cache_kt] * pps), *([cache_v] * pps), knewt, vnew, g_sub, sg)


def _dsa_sample_kernel(pt_ref, qs_ref, qi_ref, w_ref, *refs, nt, pps, nsteps, topk, page):
    kit_refs = refs[:pps]
    ks_refs = refs[pps:2 * pps]
    vs_refs = refs[2 * pps:3 * pps]
    kint_ref, ksn_ref, vsn_ref, sg_ref, o_ref, key_scr, s_scr, v_scr = refs[3 * pps:]
    s_id = pl.program_id(1)
    gsz = LANES // nt

    def process(blk, kit, ks, vs, causal_new):
        sc = _dot(qi_ref[...], kit.astype(MXU_DTYPE))
        val = jnp.maximum(sc, 0.0) * w_ref[...]
        acc = jnp.sum(val.reshape(gsz, nt, page), axis=0)
        acc = jnp.where(acc == 0.0, 0.0, acc)
        if causal_new:
            acc = jnp.where(_iota(acc.shape, 1) <= _iota(acc.shape, 0), acc, -jnp.inf)
        key_scr[blk] = _f2key(acc)
        s_scr[blk] = _dot_nt(qs_ref[...], ks.astype(MXU_DTYPE))
        v_scr[blk] = vs.astype(MXU_DTYPE)

    @pl.when(s_id < nsteps)
    def _():
        for r in range(pps):
            process(s_id * pps + r, kit_refs[r][...], ks_refs[r][...], vs_refs[r][...], False)

    @pl.when(s_id == nsteps)
    def _():
        nblk = nsteps * pps + 1
        process(nblk - 1, kint_ref[...], ksn_ref[...], vsn_ref[...], True)

        keys = jnp.concatenate([key_scr[c] for c in range(nblk)], axis=1)

        def count_ge_all(cand):
            return jnp.sum(jnp.where(keys >= cand, 1.0, 0.0), axis=1, keepdims=True)

        def bit_body(it, prefix):
            cand = prefix + lax.shift_left(jnp.int32(1), 31 - it)
            return jnp.where(count_ge_all(cand) >= topk, cand, prefix)

        kth = lax.fori_loop(0, 32, bit_body, jnp.full((nt, 1), INT_MIN, i32))
        _demote_surplus_ties_lane(key_scr, nblk, nt, topk, kth, count_ge_all)
        thr = jnp.maximum(kth, KEY_NEG_INF + 1)

        unroll = max(u for u in (5, 4, 3, 2, 1) if nblk % u == 0)

        def sel_scores(c):
            msk = key_scr[c] >= thr
            return msk[None], s_scr[c].reshape(gsz, nt, page)

        def max_body(it, mx):
            for u in range(unroll):
                msk, s3 = sel_scores(it * unroll + u)
                mx = jnp.maximum(mx, jnp.where(msk, s3, NEG_BIG))
            return mx

        mx = lax.fori_loop(0, nblk // unroll, max_body, jnp.full((gsz, nt, page), NEG_BIG, f32))
        m = jnp.max(mx, axis=2, keepdims=True)

        def att_body(it, carry):
            lacc, acc = carry
            for u in range(unroll):
                c = it * unroll + u
                msk, s3 = sel_scores(c)
                p = jnp.where(msk, jnp.exp2(s3 - m), 0.0)
                lacc = lacc + p
                acc = acc + _dot(p.reshape(gsz * nt, page).astype(MXU_DTYPE), v_scr[c])
            return lacc, acc

        lacc, acc = lax.fori_loop(0, nblk // unroll, att_body,
                                  (jnp.zeros((gsz, nt, page), f32), jnp.zeros((LANES, D_DSA), f32)))
        l = jnp.sum(lacc, axis=2, keepdims=True).reshape(gsz * nt, 1)
        o_ref[...] = (acc / l * sg_ref[...].astype(f32)).astype(o_ref.dtype)


def _dsa_sample(page_table, qs_rows, qi_rows, wcol, cache_kit, cache_ks, cache_vs, kit_new, ks_new, vs_new, sg_perm,
                *, layer, nt, topk):
    db = qs_rows.shape[0]
    page = cache_ks.shape[2]
    npages = page_table.shape[1]
    pps = math.gcd(PAGES_PER_STEP, npages)
    nsteps = npages // pps
    nblk = npages + 1
    assert page == LANES

    def page_map(r):
        def f(bb, s, pt):
            return (layer, pt[bb, jnp.minimum(s * pps + r, npages - 1)], 0, 0)
        return f

    def pspecs(shape):
        return [pl.BlockSpec((None, None) + shape, page_map(r)) for r in range(pps)]

    per_b = lambda shape: pl.BlockSpec((None,) + shape, lambda bb, s, pt: (bb, 0, 0))
    grid_spec = pltpu.PrefetchScalarGridSpec(
        num_scalar_prefetch=1,
        grid=(db, nsteps + 1),
        in_specs=[
            per_b((LANES, D_DSA)), per_b((LANES, D_IDX)), per_b((LANES, 1)),
            *pspecs((D_IDX, page)), *pspecs((page, D_DSA)), *pspecs((page, D_DSA)),
            per_b((D_IDX, page)), per_b((page, D_DSA)), per_b((page, D_DSA)),
            per_b((LANES, D_DSA)),
        ],
        out_specs=per_b((LANES, D_DSA)),
        scratch_shapes=[
            pltpu.VMEM((nblk, nt, page), i32),
            pltpu.VMEM((nblk, LANES, page), f32),
            pltpu.VMEM((nblk, page, D_DSA), MXU_DTYPE),
        ],
    )
    return pl.pallas_call(
        functools.partial(_dsa_sample_kernel, nt=nt, pps=pps, nsteps=nsteps, topk=topk, page=page),
        grid_spec=grid_spec,
        out_shape=jax.ShapeDtypeStruct((db, LANES, D_DSA), MXU_DTYPE),
        compiler_params=_cparams(("arbitrary", "arbitrary")),
        name="dsa_sample",
    )(page_table, qs_rows, qi_rows, wcol, *([cache_kit] * pps), *([cache_ks] * pps), *([cache_vs] * pps),
      kit_new, ks_new, vs_new, sg_perm)


def _rope_tables(pos, head_dim):
    half = head_dim // 2
    lane = jnp.arange(LANES)
    inv = ROPE_THETA ** (-(lane % half).astype(f32) / half)
    ang = pos.astype(f32)[:, None] * inv[None, :]
    sign = jnp.where((lane % head_dim) < half, -1.0, 1.0).astype(f32)
    return jnp.cos(ang), jnp.sin(ang) * sign[None, :]


def _tile_gain(g, n):
    return jnp.tile(g, n // g.shape[0]).reshape(1, n).astype(f32)


def _mixer_inputs(x2d, pos_tab, w_in_t, layer, seg, gl, *, tm, stacked=None):
    cos64, sin64, cos128, sin128 = pos_tab
    m = x2d.shape[0]
    h = _rmsnorm(x2d, gl["g_in"], min(tm, 256))
    tn = 512
    proj = functools.partial(_proj, h, tm=tm, nt=True)
    wseg = lambda name: _win_desc(w_in_t, layer, seg[name][0], seg[name][1] - seg[name][0], tn)
    rows = lambda n, dt: _rows_out(m, n, dt, tm, tn)
    rope64 = [("row", cos64), ("row", sin64)]
    rope128 = [("row", cos128), ("row", sin128)]
    nqd = seg["qd"][1] - seg["qd"][0]
    nqs = seg["qs"][1] - seg["qs"][0]
    nqm = seg["qm"][1] - seg["qm"][0]
    ngate = seg["gate"][1] - seg["gate"][0]
    norm_rope64 = functools.partial(_chunks_norm_rope, gsize=D_DH, half=D_DH // 2)
    gq64 = [("col", _tile_gain(gl["g_q_diff"], nqd) * (D_DH ** -0.5 * LOG2E))] + rope64
    gk64 = [("col", _tile_gain(gl["g_k_diff"], nqd))] + rope64
    gq128 = [("col", _tile_gain(gl["g_q_dsa"], nqs) * (D_DSA ** -0.5 * LOG2E))] + rope128
    qi_scale = D_IDX ** -0.5 * H_IDX ** -0.5
    out = {}
    (out["qd"],) = proj(wseg("qd"), norm_rope64, _store_rows, [rows(nqd, MXU_DTYPE)], gq64, name="proj_qd")
    (out["qs"],) = proj(wseg("qs"), functools.partial(_chunks_norm_rope, gsize=D_DSA, half=D_DSA // 2), _store_rows,
                        [rows(nqs, MXU_DTYPE)], gq128, name="proj_qs")
    (out["qi"],) = proj(wseg("qi"), functools.partial(_chunks_rope, half=D_IDX // 2), _store_rows,
                        [rows(H_IDX * D_IDX, MXU_DTYPE)], [("row", cos64 * qi_scale), ("row", sin64 * qi_scale)],
                        name="proj_qi")
    (out["qm"],) = proj(wseg("qm"), _chunks_norm256, _store_rows, [rows(nqm, MXU_DTYPE)],
                        [("col", _tile_gain(gl["g_q_mem"], nqm) * (D_MEM ** -0.5 * LOG2E))], name="proj_qm")
    (out["sg"],) = proj(wseg("gate"), _chunks_silu, _store_rows, [rows(ngate, MXU_DTYPE)], name="proj_gate")

    ks0, vs0, ki0 = seg["ks"][0], seg["vs"][0], seg["ki"][0]
    assert vs0 - ks0 == LANES and ks0 % LANES == 0 and ki0 % LANES == 0
    skip = (ki0 - vs0) // LANES - 1
    w_ksv = dict(array=w_in_t, n=3 * LANES, tn=LANES,
                 spec=pl.BlockSpec((None, LANES, w_in_t.shape[2]),
                                   lambda i, j: (layer, ks0 // LANES + j + jnp.where(j == 2, skip, 0), 0)))
    g_kiwi = jnp.concatenate([gl["g_k_idx"], jnp.ones((LANES - D_IDX,), f32)]).reshape(1, LANES)
    aux_ksv = [("const", gl["g_k_dsa"].reshape(1, D_DSA))] + rope128 + [("const", g_kiwi)] + rope64
    blk = lambda dt: dict(shape=(m, LANES), dtype=dt, block=(tm, LANES), index=lambda i, j: (i, 0), alias=None)
    if stacked is None:
        out["kd"], out["kd_c"] = proj(wseg("kd"), norm_rope64, _store_rows,
                                      [rows(nqd, f32), rows(nqd, MXU_DTYPE)], gk64, name="proj_kd")
        out["vd"], out["vd_c"] = proj(wseg("vd"), _chunks_raw, _store_rows,
                                      [rows(nqd, f32), rows(nqd, MXU_DTYPE)], name="proj_vd")
        out["ks"], out["ks_c"], out["vs"], out["vs_c"], out["kiwi"], out["ki2"] = proj(
            w_ksv, _chunks_ks_vs_kiwi, _store_ks_vs_kiwi,
            [blk(f32), blk(MXU_DTYPE), blk(f32), blk(MXU_DTYPE), blk(f32), blk(MXU_DTYPE)], aux_ksv,
            name="proj_ks_vs_kiwi", pass_col=True)
        return out

    l = layer
    b, t = stacked["b"], stacked["t"]
    nrb = t // tm
    nh = nqd // LANES
    hb = _col_tile(nqd, tn) // LANES
    vkb = stacked["diff_tk"]
    assert tm % vkb == 0
    depth = stacked["pdk"].shape[0]
    out["pdk"], out["kd_c"] = proj(
        wseg("kd"), norm_rope64, _store_kd_prompt,
        [dict(shape=(depth, b, nh, LANES, t), dtype=f32, block=(None, None, hb, LANES, tm),
              index=lambda i, j: (l, i // nrb, j, 0, i % nrb), alias=stacked["pdk"]),
         rows(nqd, MXU_DTYPE)], gk64, name="proj_kd")
    out["pdv"], out["vd_c"] = proj(
        wseg("vd"), _chunks_raw, _store_vd_prompt,
        [dict(shape=(depth, b, nh, t, LANES), dtype=f32, block=(None, None, hb, tm, LANES),
              index=lambda i, j: (l, i // nrb, j, i % nrb, 0), alias=stacked["pdv"]),
         dict(shape=(b, nh, t // vkb, LANES, vkb), dtype=MXU_DTYPE, block=(None, hb, tm // vkb, LANES, vkb),
              index=lambda i, j: (i // nrb, j, i % nrb, 0, 0), alias=None)], name="proj_vd")
    tok_major = lambda key: dict(shape=(depth, b, t, D_DSA), dtype=f32, block=(None, None, tm, D_DSA),
                                 index=lambda i, j: (l, i // nrb, i % nrb, 0), alias=stacked[key])
    out["psk"], out["ks_c"], out["psv"], out["vs_c"], out["kiwi"], out["ki2"], out["pik"] = proj(
        w_ksv, _chunks_ks_vs_kiwi, _store_ks_vs_kiwi,
        [tok_major("psk"), blk(MXU_DTYPE), tok_major("psv"), blk(MXU_DTYPE), blk(f32), blk(MXU_DTYPE),
         dict(shape=(depth, b, D_IDX, t), dtype=f32, block=(None, None, D_IDX, tm),
              index=lambda i, j: (l, i // nrb, 0, i % nrb), alias=stacked["pik"])], aux_ksv, name="proj_ks_vs_kiwi",
        pass_col=True)
    return out


def kernel(x_prompt, x_sample, mem_prompt, cache_diff_k, cache_diff_v, cache_dsa_k, cache_dsa_v, cache_idx_k,
           cache_mem_k, cache_mem_v, page_table, w_in, w_out, w_mem_kv, g_in, g_mem, g_q_diff, g_k_diff,
           g_sub_diff, lam_q1, lam_k1, lam_q2, lam_k2, g_q_dsa, g_k_dsa, g_k_idx, g_q_mem, g_k_mem):
    depth = w_in.shape[0]
    b, t, d = x_prompt.shape
    db, nt, _ = x_sample.shape
    n_mem = mem_prompt.shape[1]
    n_phys, page = cache_dsa_k.shape[1], cache_dsa_k.shape[2]
    npages = page_table.shape[1]
    past = npages * page
    h_diff = cache_diff_k.shape[3]
    w_diff = h_diff * 2 * D_DH
    h_dsa = (3 * d // 8) // D_DSA
    w_dsa = h_dsa * D_DSA
    h_mem = cache_mem_k.shape[3]
    w_mem = h_mem * D_MEM
    gsz = LANES // nt
    assert w_diff + w_dsa + w_mem == d and w_diff == w_dsa and LANES % nt == 0 and h_dsa <= gsz and gsz == H_IDX
    topk_p = min(TOPK_MAX, t // 4)
    topk_s = min(TOPK_MAX, (past + nt) // 4)
    m_p = b * t
    m_s = db * nt
    tm_p = min(1024, t)
    assert t % tm_p == 0

    widths = (w_diff, w_diff, w_diff, w_dsa, D_DSA, D_DSA, H_IDX * D_IDX, D_IDX, H_IDX, w_mem, d)
    offs = [0]
    for wdt in widths:
        offs.append(offs[-1] + wdt)
    names = ("qd", "kd", "vd", "qs", "ks", "vs", "qi", "ki", "wi", "qm", "gate")
    seg = {n: (offs[k], offs[k + 1]) for k, n in enumerate(names)}

    pos_p = jnp.arange(t, dtype=i32)
    pos_s = jnp.tile(past + jnp.arange(nt, dtype=i32), db)
    tab_p = _rope_tables(pos_p, D_DH) + _rope_tables(pos_p, D_DSA)
    tab_s = _rope_tables(pos_s, D_DH) + _rope_tables(pos_s, D_DSA)

    cdkt = cache_diff_k.transpose(0, 1, 3, 4, 5, 2).reshape(depth, n_phys, h_diff, 2 * D_DH, page)
    cdv = cache_diff_v.transpose(0, 1, 3, 2, 4)
    ckit = cache_idx_k.transpose(0, 1, 3, 2)
    cmk = cache_mem_k.reshape(depth, db, n_mem, w_mem)
    cmv = cache_mem_v.reshape(depth, db, n_mem, w_mem)
    w_in_t = jnp.swapaxes(w_in, 1, 2)

    x_p = x_prompt.reshape(m_p, d)
    x_s = x_sample.reshape(m_s, d)
    mem2d = mem_prompt.reshape(b * n_mem, d)
    stk = dict(pdk=jnp.zeros((depth, b, h_diff, 2 * D_DH, t), f32), pdv=jnp.zeros((depth, b, h_diff, t, 2 * D_DH), f32),
               psk=jnp.zeros((depth, b, t, D_DSA), f32), psv=jnp.zeros((depth, b, t, D_DSA), f32),
               pik=jnp.zeros((depth, b, D_IDX, t), f32))
    outs = {k: [] for k in ("pmk", "pmv", "sdk", "sdv", "ssk", "ssv", "sik")}

    for l in range(depth):
        gl = dict(g_in=g_in[l], g_q_diff=g_q_diff[l], g_k_diff=g_k_diff[l], g_q_dsa=g_q_dsa[l],
                  g_k_dsa=g_k_dsa[l], g_k_idx=g_k_idx[l], g_q_mem=g_q_mem[l])
        lam_init = 0.8 - 0.6 * math.exp(-0.3 * l)
        lam = (jnp.exp(jnp.sum(lam_q1[l] * lam_k1[l])) - jnp.exp(jnp.sum(lam_q2[l] * lam_k2[l])) + lam_init)
        lam = lam.astype(f32).reshape(1)
        g_sub = g_sub_diff[l].reshape(1, 2 * D_DH)
        post = 1.0 - lam_init

        diff_tq, diff_tk = min(DIFF_TQ, t), min(DIFF_TK, t)
        tp = _mixer_inputs(x_p, tab_p, w_in_t, l, seg, gl, tm=tm_p, stacked=dict(stk, b=b, t=t, diff_tk=diff_tk))
        for key in ("pdk", "pdv", "psk", "psv", "pik"):
            stk[key] = tp[key]
        hm = _rmsnorm(mem2d, g_mem[l], 256)
        mrows = lambda dt: [_rows_out(b * n_mem, w_mem, dt, b * n_mem, 512)]
        mk, = _proj(hm, _wcols_desc(w_mem_kv, l, 0, w_mem, 512), _chunks_norm256,
                    _store_rows, mrows(f32), [("col", _tile_gain(g_k_mem[l], w_mem))], tm=b * n_mem, nt=False,
                    name="proj_mk")
        mv, = _proj(hm, _wcols_desc(w_mem_kv, l, w_mem, w_mem, 512), _chunks_raw, _store_rows, mrows(f32),
                    tm=b * n_mem, nt=False, name="proj_mv")
        r3 = lambda a: a.reshape(b, t, a.shape[-1])
        sg = r3(tp["sg"])
        od = _diff_prompt(lam, r3(tp["qd"]), r3(tp["kd_c"]), tp["vd_c"], g_sub, sg, post_scale=post,
                          tq=diff_tq, tk=diff_tk)
        tc = 256
        vst = tp["vs_c"].reshape(b, t // tc, tc, D_DSA).transpose(0, 1, 3, 2)
        os_ = _dsa_prompt(r3(tp["qs"]), r3(tp["qi"]), r3(tp["kiwi"]), r3(tp["ki2"]), r3(tp["ks_c"]), vst, sg,
                          topk=topk_p, tc=tc)
        om = _mem_attn(r3(tp["qm"]), mk.reshape(1, b, n_mem, w_mem), mv.reshape(1, b, n_mem, w_mem), sg,
                       layer=0, sg_col0=w_diff + w_dsa, tq=min(512, t))
        x_p = _outproj(od.reshape(m_p, w_diff), os_.reshape(m_p, w_dsa), om.reshape(m_p, w_mem), w_out, x_p,
                       layer=l, tm=tm_p, tn=512)
        outs["pmk"].append(mk.reshape(b, n_mem, h_mem, D_MEM))
        outs["pmv"].append(mv.reshape(b, n_mem, h_mem, D_MEM))

        ts = _mixer_inputs(x_s, tab_s, w_in_t, l, seg, gl, tm=m_s)
        s3 = lambda a: a.reshape(db, nt, a.shape[-1])
        sg_s = s3(ts["sg"])
        tokpad = lambda a, axis: jnp.pad(a, [(0, page - nt) if ax == axis else (0, 0) for ax in range(a.ndim)])
        q5 = ts["qd"].reshape(db, nt, h_diff, 2, D_DH).transpose(0, 2, 3, 1, 4)
        eye_c = jnp.eye(2, dtype=MXU_DTYPE)
        qm = (q5[:, :, :, :, None, :] * eye_c[None, None, :, None, :, None]).reshape(db, h_diff * 2 * nt, 2 * D_DH)
        kd4 = ts["kd"].reshape(db, nt, h_diff, 2 * D_DH)
        vd4 = ts["vd"].reshape(db, nt, h_diff, 2 * D_DH)
        od_s = _diff_sample(page_table, lam, qm, cdkt, cdv, tokpad(kd4.transpose(0, 2, 3, 1), 3),
                            tokpad(vd4.transpose(0, 2, 1, 3), 2), g_sub, sg_s, layer=l, nt=nt, post_scale=post)
        qs4 = ts["qs"].reshape(db, nt, h_dsa, D_DSA)
        slot_major = lambda a: a.transpose(0, 2, 1, 3).reshape(db, LANES, a.shape[-1])
        qs_rows = slot_major(jnp.pad(qs4, ((0, 0), (0, 0), (0, gsz - h_dsa), (0, 0))))
        qi_rows = slot_major(ts["qi"].reshape(db, nt, H_IDX, D_IDX))
        wcol = slot_major(ts["kiwi"][:, D_IDX:D_IDX + H_IDX].reshape(db, nt, H_IDX, 1))
        sg_dsa = sg_s[:, :, w_diff:w_diff + w_dsa].reshape(db, nt, h_dsa, D_DSA)
        sg_perm = slot_major(jnp.pad(sg_dsa, ((0, 0), (0, 0), (0, gsz - h_dsa), (0, 0))))
        ki_new = s3(ts["kiwi"][:, :D_IDX])
        os_s = _dsa_sample(page_table, qs_rows, qi_rows, wcol, ckit, cache_dsa_k, cache_dsa_v,
                           tokpad(ki_new.transpose(0, 2, 1), 2), tokpad(s3(ts["ks"]), 1), tokpad(s3(ts["vs"]), 1),
                           sg_perm, layer=l, nt=nt, topk=topk_s)
        os_s = os_s.reshape(db, gsz, nt, D_DSA)[:, :h_dsa].transpose(0, 2, 1, 3).reshape(m_s, w_dsa)
        om_s = _mem_attn(s3(ts["qm"]), cmk, cmv, sg_s, layer=l, sg_col0=w_diff + w_dsa, tq=nt)
        x_s = _outproj(od_s.reshape(m_s, w_diff), os_s, om_s.reshape(m_s, w_mem), w_out, x_s, layer=l, tm=m_s,
                       tn=512)
        outs["sdk"].append(ts["kd"].reshape(db, nt, h_diff, 2, D_DH))
        outs["sdv"].append(ts["vd"].reshape(db, nt, h_diff, 2 * D_DH))
        outs["ssk"].append(ts["ks"].reshape(db, nt, D_DSA))
        outs["ssv"].append(ts["vs"].reshape(db, nt, D_DSA))
        outs["sik"].append(ki_new)

    st = lambda k: jnp.stack(outs[k])
    p_diff_k = stk["pdk"].reshape(depth, b, h_diff, 2, D_DH, t).transpose(0, 1, 5, 2, 3, 4)
    p_diff_v = stk["pdv"].transpose(0, 1, 3, 2, 4)
    p_idx_k = stk["pik"].transpose(0, 1, 3, 2)
    return (x_p.reshape(b, t, d), x_s.reshape(db, nt, d),
            p_diff_k, p_diff_v, stk["psk"], stk["psv"], p_idx_k, st("pmk"), st("pmv"),
            st("sdk"), st("sdv"), st("ssk"), st("ssv"), st("sik"))
```

```python
import functools
import math

import jax
import jax.numpy as jnp
from jax import lax
from jax.experimental import pallas as pl
from jax.experimental.pallas import tpu as pltpu

EPS = 1e-6
ROPE_THETA = 10000.0
TOPK_MAX = 256
LANES = 128
D_DH = 64
D_DSA = 128
D_IDX = 64
H_IDX = 16
D_MEM = 256
NEG_BIG = -1e30
KEY_NEG_INF = -2139095041
INT_MIN = -2147483648
MXU_DTYPE = jnp.bfloat16
VMEM_LIMIT_BYTES = 52 * 1024 * 1024
PAGES_PER_STEP = 8
DIFF_TQ = 512
DIFF_TK = 512
LOG2E = math.log2(math.e)

f32 = jnp.float32
i32 = jnp.int32


def _cparams(sem):
    return pltpu.CompilerParams(dimension_semantics=sem, vmem_limit_bytes=VMEM_LIMIT_BYTES)


def _dot(a, b):
    return jnp.dot(a, b, preferred_element_type=f32)


def _dot_nt(a, b):
    return lax.dot_general(a, b, (((1,), (1,)), ((), ())), preferred_element_type=f32)


def _iota(shape, dim):
    return lax.broadcasted_iota(i32, shape, dim)


def _group_mat(gsize, value):
    r = _iota((LANES, LANES), 0) // gsize
    c = _iota((LANES, LANES), 1) // gsize
    return jnp.where(r == c, value, 0.0).astype(MXU_DTYPE)


def _group_sum(x, gmat):
    return _dot(x.astype(MXU_DTYPE), gmat)


def _rope_chunk(n, cos, sin_signed, half):
    if 2 * half == LANES:
        rot = pltpu.roll(n, half, 1)
    else:
        first = (_iota(n.shape, 1) % (2 * half)) < half
        rot = jnp.where(first, pltpu.roll(n, LANES - half, 1), pltpu.roll(n, half, 1))
    return n * cos + rot * sin_signed


def _f2key(x):
    b = pltpu.bitcast(x, i32)
    return b ^ (lax.shift_right_arithmetic(b, 31) & 0x7FFFFFFF)


def _rmsnorm_kernel(x_ref, g_ref, o_ref):
    x = x_ref[...]
    ms = jnp.mean(x * x, axis=-1, keepdims=True)
    o_ref[...] = (x * lax.rsqrt(ms + EPS) * g_ref[...]).astype(o_ref.dtype)


def _rmsnorm(x, g, tm):
    m, d = x.shape
    return pl.pallas_call(
        _rmsnorm_kernel,
        grid=(m // tm,),
        in_specs=[pl.BlockSpec((tm, d), lambda i: (i, 0)), pl.BlockSpec((1, d), lambda i: (0, 0))],
        out_specs=pl.BlockSpec((tm, d), lambda i: (i, 0)),
        out_shape=jax.ShapeDtypeStruct((m, d), MXU_DTYPE),
        compiler_params=_cparams(("arbitrary",)),
        name="rmsnorm",
    )(x, g.reshape(1, d))


def _chunks_raw(z, aux, store):
    for c in range(z.shape[1] // LANES):
        store(c, z[:, c * LANES:(c + 1) * LANES])


def _chunks_silu(z, aux, store):
    for c in range(z.shape[1] // LANES):
        zc = z[:, c * LANES:(c + 1) * LANES]
        store(c, zc / (1.0 + jnp.exp(-zc)))


def _chunks_norm_rope(z, aux, store, *, gsize, half):
    gain, cos, sin = aux
    gmat = _group_mat(gsize, 1.0 / gsize)
    for c in range(z.shape[1] // LANES):
        sl = slice(c * LANES, (c + 1) * LANES)
        zc = z[:, sl]
        n = zc * lax.rsqrt(_group_sum(zc * zc, gmat) + EPS) * gain[:, sl]
        store(c, _rope_chunk(n, cos, sin, half))


def _chunks_rope(z, aux, store, *, half):
    cos, sin = aux
    for c in range(z.shape[1] // LANES):
        store(c, _rope_chunk(z[:, c * LANES:(c + 1) * LANES], cos, sin, half))


def _chunks_norm256(z, aux, store):
    (gain,) = aux
    gmat = _group_mat(LANES, 1.0 / D_MEM)
    for c in range(z.shape[1] // D_MEM):
        a = z[:, c * D_MEM:c * D_MEM + LANES]
        b = z[:, c * D_MEM + LANES:(c + 1) * D_MEM]
        inv = lax.rsqrt(_group_sum(a * a, gmat) + _group_sum(b * b, gmat) + EPS)
        for k, v in enumerate((a, b)):
            sl = slice(c * D_MEM + k * LANES, c * D_MEM + (k + 1) * LANES)
            store(2 * c + k, v * inv * gain[:, sl])


def _chunks_kiwi(z, aux, store):
    gain, cos, sin = aux
    z = z[:, :]
    gmat = _group_mat(D_IDX, 1.0 / D_IDX)
    n = z * lax.rsqrt(_group_sum(z * z, gmat) + EPS) * gain
    r = _rope_chunk(n, cos, sin, D_IDX // 2)
    lo = _iota(z.shape, 1) < D_IDX
    store(0, jnp.where(lo, r, z))
    r_lo = jnp.where(lo, r, 0.0)
    store(1, r_lo + pltpu.roll(r_lo, D_IDX, 1))


def _store_rows(outs, c, r):
    for o in outs:
        o[:, c * LANES:(c + 1) * LANES] = r.astype(o.dtype)


def _store_kd_prompt(outs, c, r):
    outs[0][c] = r.T
    outs[1][:, c * LANES:(c + 1) * LANES] = r.astype(outs[1].dtype)


def _store_vd_prompt(outs, c, r):
    outs[0][c] = r
    rt = r.T.astype(outs[1].dtype)
    kb = outs[1].shape[-1]
    for kk in range(r.shape[0] // kb):
        outs[1][c, kk] = rt[:, kk * kb:(kk + 1) * kb]


def _chunks_ks_vs_kiwi(z, aux, store, *, col):
    gain_k, cos128, sin128, gain_kiwi, cos64, sin64 = aux
    j = col

    @pl.when(j == 0)
    def _():
        _chunks_norm_rope(z, (gain_k, cos128, sin128), functools.partial(store, 0), gsize=D_DSA, half=D_DSA // 2)

    @pl.when(j == 1)
    def _():
        store(1, 0, z[:, :])

    @pl.when(j == 2)
    def _():
        _chunks_kiwi(z, (gain_kiwi, cos64, sin64), functools.partial(store, 2))


def _store_ks_vs_kiwi(outs, which, c, r):
    if which < 2:
        outs[2 * which][...] = r
        outs[2 * which + 1][...] = r.astype(outs[2 * which + 1].dtype)
    elif c == 0:
        outs[4][...] = r
        if len(outs) > 6:
            outs[6][...] = r.T[:D_IDX, :]
    else:
        outs[5][...] = r.astype(outs[5].dtype)


class _LazyTile:
    def __init__(self, ref):
        self.ref = ref
        self.shape = ref.shape

    def __getitem__(self, idx):
        return self.ref[idx]


def _proj_kernel(*refs, chunk_fn, store_fn, n_aux, n_alias, nt, nj, ntiles, pass_col):
    h_ref, w_ref = refs[:2]
    aux_refs = refs[2:2 + n_aux]
    outs = refs[2 + n_aux + n_alias:-1]
    z_scr = refs[-1]
    s = pl.program_id(0)

    def epilogue():
        kw = dict(col=(s - 1) % nj) if pass_col else {}
        chunk_fn(_LazyTile(z_scr), [r[...] for r in aux_refs], functools.partial(store_fn, outs), **kw)

    def matmul():
        w = w_ref[...].astype(MXU_DTYPE)
        z_scr[...] = _dot_nt(h_ref[...], w) if nt else _dot(h_ref[...], w)

    @pl.when(s == 0)
    def _():
        matmul()

    @pl.when(jnp.logical_and(s > 0, s < ntiles))
    def _():
        epilogue()
        matmul()

    @pl.when(s == ntiles)
    def _():
        epilogue()


def _win_desc(w_in_t, layer, off, n, tn):
    tn = _col_tile(n, tn)
    k = w_in_t.shape[2]
    if off % tn == 0:
        return dict(array=w_in_t, spec=pl.BlockSpec((None, tn, k), lambda i, j: (layer, off // tn + j, 0)), n=n, tn=tn)
    row0 = layer * w_in_t.shape[1] + off
    assert row0 % 8 == 0
    spec = pl.BlockSpec((pl.Element(tn), pl.Element(k)), lambda i, j: (pl.multiple_of(row0 + j * tn, 8), 0))
    return dict(array=w_in_t.reshape(-1, k), spec=spec, n=n, tn=tn)


def _wcols_desc(w, layer, off, n, tn):
    tn = _col_tile(n, tn)
    assert off % tn == 0
    return dict(array=w, spec=pl.BlockSpec((None, w.shape[1], tn), lambda i, j: (layer, 0, off // tn + j)), n=n, tn=tn)


def _col_tile(n, tn):
    return math.gcd(n, tn)


def _rows_out(m, n, dtype, tm, tn):
    tn = _col_tile(n, tn)
    return dict(shape=(m, n), dtype=dtype, block=(tm, tn), index=lambda i, j: (i, j), alias=None)


def _proj(h, wd, chunk_fn, store_fn, outs, aux=(), *, tm, nt, name, pass_col=False):
    m, k = h.shape
    n, tn = wd["n"], wd["tn"]
    assert m % tm == 0 and tn % LANES == 0 and n % tn == 0
    nj = n // tn
    ntiles = (m // tm) * nj
    cur = lambda f: (lambda s: f(jnp.minimum(s, ntiles - 1) // nj, jnp.minimum(s, ntiles - 1) % nj))
    prev = lambda f: (lambda s: f(jnp.maximum(s - 1, 0) // nj, jnp.maximum(s - 1, 0) % nj))
    in_specs = [pl.BlockSpec((tm, k), cur(lambda i, j: (i, 0))),
                pl.BlockSpec(wd["spec"].block_shape, cur(wd["spec"].index_map))]
    args = [h, wd["array"]]
    for kind, a in aux:
        if kind == "col":
            in_specs.append(pl.BlockSpec((1, tn), prev(lambda i, j: (0, j))))
        elif kind == "const":
            in_specs.append(pl.BlockSpec((1, LANES), prev(lambda i, j: (0, 0))))
        else:
            nrb = a.shape[0] // tm
            in_specs.append(pl.BlockSpec((tm, LANES), prev(lambda i, j, nrb=nrb: (i % nrb, 0))))
        args.append(a)
    aliases = {}
    for k_out, o in enumerate(outs):
        if o["alias"] is not None:
            aliases[len(args)] = k_out
            in_specs.append(pl.BlockSpec(memory_space=pl.ANY))
            args.append(o["alias"])
    return pl.pallas_call(
        functools.partial(_proj_kernel, chunk_fn=chunk_fn, store_fn=store_fn, n_aux=len(aux),
                          n_alias=len(aliases), nt=nt, nj=nj, ntiles=ntiles, pass_col=pass_col),
        grid=(ntiles + 1,),
        in_specs=in_specs,
        out_specs=[pl.BlockSpec(o["block"], prev(o["index"])) for o in outs],
        out_shape=[jax.ShapeDtypeStruct(o["shape"], o["dtype"]) for o in outs],
        scratch_shapes=[pltpu.VMEM((tm, tn), f32)],
        input_output_aliases=aliases,
        compiler_params=_cparams(("arbitrary",)),
        name=name,
    )(*args)


def _outproj_kernel(od_ref, os_ref, om_ref, w_ref, x_ref, o_ref, *, wd, ws):
    acc = _dot(od_ref[...], w_ref[0:wd, :].astype(MXU_DTYPE))
    acc += _dot(os_ref[...], w_ref[wd:wd + ws, :].astype(MXU_DTYPE))
    acc += _dot(om_ref[...], w_ref[wd + ws:, :].astype(MXU_DTYPE))
    o_ref[...] = x_ref[...] + acc


def _outproj(od, os_, om, w, x, *, layer, tm, tn):
    m, d = x.shape
    wd, ws, wm = od.shape[1], os_.shape[1], om.shape[1]
    return pl.pallas_call(
        functools.partial(_outproj_kernel, wd=wd, ws=ws),
        grid=(m // tm, d // tn),
        in_specs=[
            pl.BlockSpec((tm, wd), lambda i, j: (i, 0)),
            pl.BlockSpec((tm, ws), lambda i, j: (i, 0)),
            pl.BlockSpec((tm, wm), lambda i, j: (i, 0)),
            pl.BlockSpec((None, wd + ws + wm, tn), lambda i, j: (layer, 0, j)),
            pl.BlockSpec((tm, tn), lambda i, j: (i, j)),
        ],
        out_specs=pl.BlockSpec((tm, tn), lambda i, j: (i, j)),
        out_shape=jax.ShapeDtypeStruct((m, d), f32),
        compiler_params=_cparams(("arbitrary", "arbitrary")),
        name="outproj",
    )(od, os_, om, w, x)


def _sub_rmsnorm_gate(od, g, post_scale, sg):
    ms = jnp.mean(od * od, axis=-1, keepdims=True)
    return od * lax.rsqrt(ms + EPS) * g * post_scale * sg.astype(f32)


def _diff_prompt_kernel(pi_ref, pj_ref, lam_ref, q_ref, k_ref, vt_ref, g_ref, sg_ref, o_ref,
                        m_scr, l_scr, acc_scr, *, tq, tk, hb, post_scale):
    p_id = pl.program_id(2)
    i = pi_ref[p_id]
    j = pj_ref[p_id]

    @pl.when(j == 0)
    def _():
        m_scr[...] = jnp.full_like(m_scr, NEG_BIG)
        l_scr[...] = jnp.zeros_like(l_scr)
        acc_scr[...] = jnp.zeros_like(acc_scr)

    nchain = 2 * hb
    h2 = tq // 2
    split_diag = tq == tk and h2 % LANES == 0

    def masked_q(r):
        q = q_ref[:, (r // 2) * LANES:(r // 2 + 1) * LANES]
        lo = _iota(q.shape, 1) < D_DH
        zero = jnp.zeros_like(q)
        return jnp.where(lo, q, zero) if r % 2 == 0 else jnp.where(lo, zero, q)

    def keys(r):
        return k_ref[:, (r // 2) * LANES:(r // 2 + 1) * LANES]

    def scores(r):
        return _dot_nt(keys(r), masked_q(r))

    def scores_diag(r):
        qc, kc = masked_q(r), keys(r)
        return _dot_nt(kc[:h2], qc), _dot_nt(kc[h2:], qc[h2:])

    def causal(s, key0, qry0):
        return jnp.where(j * tk + key0 + _iota(s.shape, 0) <= i * tq + qry0 + _iota(s.shape, 1), s, NEG_BIG)

    def body(masked):
        s_next = scores(0)
        for r in range(nchain):
            s = s_next
            if r + 1 < nchain:
                s_next = scores(r + 1)
            if masked:
                s = causal(s, 0, 0)
            m = m_scr[r]
            m_new = jnp.maximum(m, jnp.max(s, axis=0, keepdims=True))
            alpha = jnp.exp2(m - m_new)
            p = jnp.exp2(s - m_new)
            l_scr[r] = alpha * l_scr[r] + jnp.sum(p, axis=0, keepdims=True)
            acc_scr[r] = alpha * acc_scr[r] + _dot(vt_ref[r // 2], p.astype(MXU_DTYPE))
            m_scr[r] = m_new

    def body_diag():
        s_next = scores_diag(0)
        for r in range(nchain):
            sa, sb = s_next
            if r + 1 < nchain:
                s_next = scores_diag(r + 1)
            sa = jnp.concatenate([causal(sa[:, :h2], 0, 0), sa[:, h2:]], axis=1)
            sb = causal(sb, h2, h2)
            m = m_scr[r]
            mb = jnp.concatenate([jnp.full((1, h2), NEG_BIG, f32), jnp.max(sb, axis=0, keepdims=True)], axis=1)
            m_new = jnp.maximum(m, jnp.maximum(jnp.max(sa, axis=0, keepdims=True), mb))
            alpha = jnp.exp2(m - m_new)
            pa = jnp.exp2(sa - m_new)
            pb = jnp.exp2(sb - m_new[:, h2:])
            lb = jnp.concatenate([jnp.zeros((1, h2), f32), jnp.sum(pb, axis=0, keepdims=True)], axis=1)
            l_scr[r] = alpha * l_scr[r] + jnp.sum(pa, axis=0, keepdims=True) + lb
            vt = vt_ref[r // 2]
            pva = _dot(vt[:, :h2], pa.astype(MXU_DTYPE))
            pvb = _dot(vt[:, h2:], pb.astype(MXU_DTYPE))
            acc_scr[r] = alpha * acc_scr[r] + pva + jnp.concatenate([jnp.zeros((LANES, h2), f32), pvb], axis=1)
            m_scr[r] = m_new

    crosses_diagonal = (j + 1) * tk - 1 > i * tq
    pl.when(crosses_diagonal)(body_diag if split_diag else functools.partial(body, True))
    pl.when(jnp.logical_not(crosses_diagonal))(functools.partial(body, False))

    @pl.when(j == ((i + 1) * tq - 1) // tk)
    def _():
        for hh in range(hb):
            hsl = slice(hh * LANES, (hh + 1) * LANES)
            od_t = acc_scr[2 * hh] / l_scr[2 * hh] - lam_ref[0] * (acc_scr[2 * hh + 1] / l_scr[2 * hh + 1])
            o_ref[:, hsl] = _sub_rmsnorm_gate(od_t.T, g_ref[...], post_scale, sg_ref[:, hsl]).astype(o_ref.dtype)


def _diff_prompt(lam, qd, kd, vdt, g_sub, sg, *, post_scale, tq, tk):
    b, t, w = qd.shape
    nh = w // LANES
    hb = max(g for g in (6, 4, 2, 1) if nh % g == 0)
    pairs = [(i, j) for i in range(t // tq) for j in range(((i + 1) * tq - 1) // tk + 1)]
    pi = jnp.asarray([p[0] for p in pairs], i32)
    pj = jnp.asarray([p[1] for p in pairs], i32)
    grid_spec = pltpu.PrefetchScalarGridSpec(
        num_scalar_prefetch=2,
        grid=(b, nh // hb, len(pairs)),
        in_specs=[
            pl.BlockSpec(memory_space=pltpu.SMEM),
            pl.BlockSpec((None, tq, hb * LANES), lambda bb, h, p, pi, pj: (bb, pi[p], h)),
            pl.BlockSpec((None, tk, hb * LANES), lambda bb, h, p, pi, pj: (bb, pj[p], h)),
            pl.BlockSpec((None, hb, None, LANES, tk), lambda bb, h, p, pi, pj: (bb, h, pj[p], 0, 0)),
            pl.BlockSpec((1, LANES), lambda bb, h, p, pi, pj: (0, 0)),
            pl.BlockSpec((None, tq, hb * LANES), lambda bb, h, p, pi, pj: (bb, pi[p], h)),
        ],
        out_specs=pl.BlockSpec((None, tq, hb * LANES), lambda bb, h, p, pi, pj: (bb, pi[p], h)),
        scratch_shapes=[
            pltpu.VMEM((2 * hb, 1, tq), f32),
            pltpu.VMEM((2 * hb, 1, tq), f32),
            pltpu.VMEM((2 * hb, LANES, tq), f32),
        ],
    )
    return pl.pallas_call(
        functools.partial(_diff_prompt_kernel, tq=tq, tk=tk, hb=hb, post_scale=post_scale),
        grid_spec=grid_spec,
        out_shape=jax.ShapeDtypeStruct((b, t, w), MXU_DTYPE),
        compiler_params=_cparams(("arbitrary", "arbitrary", "arbitrary")),
        name="diff_prompt",
    )(pi, pj, lam, qd, kd, vdt, g_sub, sg)


def _kth_largest_key_sub(key_scr, nblk, rows, topk):
    def count_ge(cand):
        def body(c, acc):
            k = key_scr[pl.ds(pl.multiple_of(c * rows, rows), rows), :]
            return acc + jnp.sum((k >= cand).astype(i32).reshape(rows // 8, 8, LANES), axis=0)

        acc = lax.fori_loop(0, nblk, body, jnp.zeros((8, LANES), i32))
        return jnp.sum(acc, axis=0, keepdims=True)

    def bit_body(it, prefix):
        cand = prefix + lax.shift_left(jnp.int32(1), 31 - it)
        return jnp.where(count_ge(cand) >= topk, cand, prefix)

    kth = lax.fori_loop(0, 32, bit_body, jnp.full((1, LANES), INT_MIN, i32))
    return kth, count_ge


def _demote_surplus_ties_sub(key_scr, nblk, rows, topk, kth, count_ge):
    tie = (count_ge(kth) > topk) & (kth > KEY_NEG_INF)

    @pl.when(jnp.max(tie.astype(i32)) > 0)
    def _():
        need = (topk - count_ge(kth + 1)).astype(f32)
        tri = (_iota((rows, rows), 0) >= _iota((rows, rows), 1)).astype(f32).astype(MXU_DTYPE)

        def body(c, run):
            sl = pl.ds(pl.multiple_of(c * rows, rows), rows)
            k = key_scr[sl, :]
            eq = k == kth
            incl = _dot(tri, eq.astype(f32).astype(MXU_DTYPE))
            drop = eq & ((run + incl) > need)
            key_scr[sl, :] = jnp.where(drop, KEY_NEG_INF, k)
            return run + incl[rows - 1:rows, :]

        lax.fori_loop(0, nblk, body, jnp.zeros((1, LANES), f32))


def _demote_surplus_ties_lane(key_scr, nblk, nq, topk, kth, count_ge):
    tie = (count_ge(kth) > topk) & (kth > KEY_NEG_INF)

    @pl.when(jnp.max(jnp.where(tie, 1.0, 0.0)) > 0.0)
    def _():
        need = topk - count_ge(kth + 1)
        triu = (_iota((LANES, LANES), 0) <= _iota((LANES, LANES), 1)).astype(f32).astype(MXU_DTYPE)
        pad = jnp.zeros((16 - nq % 16, LANES), f32) if nq % 16 else None

        def body(c, run):
            k = key_scr[c]
            eq = k == kth
            eqf = jnp.where(eq, 1.0, 0.0)
            if pad is not None:
                eqf = jnp.concatenate([eqf, pad], axis=0)
            incl = _dot(eqf.astype(MXU_DTYPE), triu)[:nq]
            drop = eq & ((run + incl) > need)
            key_scr[c] = jnp.where(drop, KEY_NEG_INF, k)
            return run + incl[:, LANES - 1:LANES]

        lax.fori_loop(0, nblk, body, jnp.zeros((nq, 1), f32))


def _dsa_prompt_kernel(qs_ref, qi_ref, kiwi_ref, ki2_ref, ks_ref, vst_ref, sg_ref, o_ref,
                       key_scr, qm_scr, wt_scr, qst_scr, acc_scr, *, nh, topk, tc):
    qb = pl.program_id(1)
    tq = LANES
    nch = (qb * tq + tq + tc - 1) // tc

    lo = _iota((tq, LANES), 1) < D_IDX
    for h in range(H_IDX):
        chunk = qi_ref[:, (h // 2) * LANES:(h // 2 + 1) * LANES]
        keep = lo if h % 2 == 0 else jnp.logical_not(lo)
        qm_scr[h * tq:(h + 1) * tq, :] = jnp.where(keep, chunk, jnp.zeros_like(chunk))
    wt_scr[...] = kiwi_ref[...].T
    for h in range(nh):
        qst_scr[h * tq:(h + 1) * tq, :] = qs_ref[:, h * LANES:(h + 1) * LANES]

    tpos = qb * tq + _iota((1, LANES), 1)

    def idx_body(c, carry):
        sl = pl.ds(pl.multiple_of(c * tc, tc), tc)
        kc = ki2_ref[sl, :]
        acc = jnp.zeros((tc, LANES), f32)
        for hp in range(H_IDX // 2):
            sc = _dot_nt(kc, qm_scr[2 * hp * tq:(2 * hp + 2) * tq, :])
            for k in range(2):
                h = 2 * hp + k
                acc = acc + jnp.maximum(sc[:, k * tq:(k + 1) * tq], 0.0) * wt_scr[D_IDX + h:D_IDX + h + 1, :]
        acc = jnp.where(acc == 0.0, 0.0, acc)
        kpos = c * tc + _iota((tc, LANES), 0)
        acc = jnp.where(kpos <= tpos, acc, -jnp.inf)
        key_scr[sl, :] = _f2key(acc)
        return carry

    lax.fori_loop(0, nch, idx_body, 0)

    kth, count_ge = _kth_largest_key_sub(key_scr, nch, tc, topk)
    _demote_surplus_ties_sub(key_scr, nch, tc, topk, kth, count_ge)
    thr = jnp.maximum(kth, KEY_NEG_INF + 1)

    acc_scr[...] = jnp.zeros_like(acc_scr)

    def att_body(c, carry):
        m, l = carry
        sl = pl.ds(pl.multiple_of(c * tc, tc), tc)
        s = _dot_nt(ks_ref[sl, :], qst_scr[...])
        msk = key_scr[sl, :] >= thr
        s = jnp.where(jnp.concatenate([msk] * nh, axis=1), s, NEG_BIG)
        m_new = jnp.maximum(m, jnp.max(s, axis=0, keepdims=True))
        alpha = jnp.exp2(m - m_new)
        p = jnp.exp2(s - m_new)
        l = alpha * l + jnp.sum(p, axis=0, keepdims=True)
        acc_scr[...] = alpha * acc_scr[...] + _dot(vst_ref[c], p.astype(MXU_DTYPE))
        return m_new, l

    init = (jnp.full((1, nh * tq), NEG_BIG, f32), jnp.zeros((1, nh * tq), f32))
    _, l = lax.fori_loop(0, nch, att_body, init)
    out_t = acc_scr[...] / l
    for h in range(nh):
        sl = slice(h * LANES, (h + 1) * LANES)
        o_ref[:, sl] = (out_t[:, sl].T * sg_ref[:, sl].astype(f32)).astype(o_ref.dtype)


def _dsa_prompt(qs, qi, kiwi, ki2, ks, vst, sg, *, topk, tc=256):
    b, t, w = qs.shape
    nh = w // LANES
    tq = LANES
    assert t % tc == 0 and tc % tq == 0
    return pl.pallas_call(
        functools.partial(_dsa_prompt_kernel, nh=nh, topk=topk, tc=tc),
        grid=(b, t // tq),
        in_specs=[
            pl.BlockSpec((None, tq, w), lambda bb, i: (bb, i, 0)),
            pl.BlockSpec((None, tq, H_IDX * D_IDX), lambda bb, i: (bb, i, 0)),
            pl.BlockSpec((None, tq, LANES), lambda bb, i: (bb, i, 0)),
            pl.BlockSpec((None, t, LANES), lambda bb, i: (bb, 0, 0)),
            pl.BlockSpec((None, t, LANES), lambda bb, i: (bb, 0, 0)),
            pl.BlockSpec((None, t // tc, LANES, tc), lambda bb, i: (bb, 0, 0, 0)),
            pl.BlockSpec((None, tq, w), lambda bb, i: (bb, i, 1)),
        ],
        out_specs=pl.BlockSpec((None, tq, w), lambda bb, i: (bb, i, 0)),
        out_shape=jax.ShapeDtypeStruct((b, t, w), MXU_DTYPE),
        scratch_shapes=[
            pltpu.VMEM((t, LANES), i32),
            pltpu.VMEM((H_IDX * tq, LANES), MXU_DTYPE),
            pltpu.VMEM((LANES, LANES), f32),
            pltpu.VMEM((nh * tq, LANES), MXU_DTYPE),
            pltpu.VMEM((LANES, nh * tq), f32),
        ],
        compiler_params=_cparams(("arbitrary", "arbitrary")),
        name="dsa_prompt",
    )(qs, qi, kiwi, ki2, ks, vst, sg)


def _mem_attn_kernel(q_ref, k_ref, v_ref, sg_ref, o_ref, *, nh):
    def scores(h):
        hs = slice(h * D_MEM, (h + 1) * D_MEM)
        return _dot_nt(q_ref[:, hs], k_ref[:, hs].astype(MXU_DTYPE))

    s_next = scores(0)
    for h in range(nh):
        s = s_next
        if h + 1 < nh:
            s_next = scores(h + 1)
        hs = slice(h * D_MEM, (h + 1) * D_MEM)
        m = jnp.max(s, axis=1, keepdims=True)
        p = jnp.exp2(s - m)
        l = jnp.sum(p, axis=1, keepdims=True)
        o = _dot(p.astype(MXU_DTYPE), v_ref[:, hs].astype(MXU_DTYPE)) / l
        o_ref[:, hs] = (o * sg_ref[:, hs].astype(f32)).astype(o_ref.dtype)


def _mem_attn(q, k, v, sg, *, layer, sg_col0, tq):
    b, t, w = q.shape
    nm = k.shape[2]
    assert sg_col0 % w == 0
    return pl.pallas_call(
        functools.partial(_mem_attn_kernel, nh=w // D_MEM),
        grid=(b, t // tq),
        in_specs=[
            pl.BlockSpec((None, tq, w), lambda bb, i: (bb, i, 0)),
            pl.BlockSpec((None, None, nm, w), lambda bb, i: (layer, bb, 0, 0)),
            pl.BlockSpec((None, None, nm, w), lambda bb, i: (layer, bb, 0, 0)),
            pl.BlockSpec((None, tq, w), lambda bb, i: (bb, i, sg_col0 // w)),
        ],
        out_specs=pl.BlockSpec((None, tq, w), lambda bb, i: (bb, i, 0)),
        out_shape=jax.ShapeDtypeStruct((b, t, w), MXU_DTYPE),
        compiler_params=_cparams(("arbitrary", "arbitrary")),
        name="mem_attn",
    )(q, k, v, sg)


def _diff_sample_kernel(pt_ref, lam_ref, qm_ref, *refs, nh, nt, pps, nsteps, post_scale):
    kt_refs = refs[:pps]
    v_refs = refs[pps:2 * pps]
    knewt_ref, vnew_ref, g_ref, sg_ref, o_ref, m_scr, l_scr, acc_scr = refs[2 * pps:]
    s_id = pl.program_id(1)
    nr = 2 * nt

    @pl.when(s_id == 0)
    def _():
        m_scr[...] = jnp.full_like(m_scr, NEG_BIG)
        l_scr[...] = jnp.zeros_like(l_scr)
        acc_scr[...] = jnp.zeros_like(acc_scr)

    def update(kts, vs, masked):
        s = jnp.concatenate(
            [jnp.concatenate([_dot(qm_ref[h * nr:(h + 1) * nr, :], kt[h].astype(MXU_DTYPE)) for kt in kts], axis=1)
             for h in range(nh)], axis=0)
        if masked:
            tpos = _iota(s.shape, 0) % nt
            s = jnp.where(_iota(s.shape, 1) <= tpos, s, NEG_BIG)
        m = m_scr[...]
        m_new = jnp.maximum(m, jnp.max(s, axis=1, keepdims=True))
        alpha = jnp.exp2(m - m_new)
        p = jnp.exp2(s - m_new)
        l_scr[...] = alpha * l_scr[...] + jnp.sum(p, axis=1, keepdims=True)
        pb = p.astype(MXU_DTYPE)
        kw = s.shape[1] // len(kts)
        pv = []
        for h in range(nh):
            acc = None
            for r, v in enumerate(vs):
                d = _dot(pb[h * nr:(h + 1) * nr, r * kw:(r + 1) * kw], v[h].astype(MXU_DTYPE))
                acc = d if acc is None else acc + d
            pv.append(acc)
        acc_scr[...] = alpha * acc_scr[...] + jnp.concatenate(pv, axis=0)
        m_scr[...] = m_new

    @pl.when(s_id < nsteps)
    def _():
        update(kt_refs, v_refs, False)

    @pl.when(s_id == nsteps)
    def _():
        update([knewt_ref], [vnew_ref], True)
        o = acc_scr[...] / l_scr[...]
        for h in range(nh):
            od = o[h * nr:h * nr + nt] - lam_ref[0] * o[h * nr + nt:(h + 1) * nr]
            sl = slice(h * LANES, (h + 1) * LANES)
            o_ref[:, sl] = _sub_rmsnorm_gate(od, g_ref[...], post_scale, sg_ref[:, sl]).astype(o_ref.dtype)


def _diff_sample(page_table, lam, qm, cache_kt, cache_v, knewt, vnew, g_sub, sg, *, layer, nt, post_scale):
    db, rows, _ = qm.shape
    nh = cache_v.shape[2]
    page = cache_v.shape[3]
    npages = page_table.shape[1]
    pps = math.gcd(PAGES_PER_STEP, npages)
    nsteps = npages // pps
    w = nh * LANES

    def page_map(r):
        def f(bb, s, pt):
            return (layer, pt[bb, jnp.minimum(s * pps + r, npages - 1)], 0, 0, 0)
        return f

    kt_specs = [pl.BlockSpec((None, None, nh, LANES, page), page_map(r)) for r in range(pps)]
    v_specs = [pl.BlockSpec((None, None, nh, page, LANES), page_map(r)) for r in range(pps)]
    grid_spec = pltpu.PrefetchScalarGridSpec(
        num_scalar_prefetch=1,
        grid=(db, nsteps + 1),
        in_specs=[
            pl.BlockSpec(memory_space=pltpu.SMEM),
            pl.BlockSpec((None, rows, LANES), lambda bb, s, pt: (bb, 0, 0)),
            *kt_specs, *v_specs,
            pl.BlockSpec((None, nh, LANES, page), lambda bb, s, pt: (bb, 0, 0, 0)),
            pl.BlockSpec((None, nh, page, LANES), lambda bb, s, pt: (bb, 0, 0, 0)),
            pl.BlockSpec((1, LANES), lambda bb, s, pt: (0, 0)),
            pl.BlockSpec((None, nt, w), lambda bb, s, pt: (bb, 0, 0)),
        ],
        out_specs=pl.BlockSpec((None, nt, w), lambda bb, s, pt: (bb, 0, 0)),
        scratch_shapes=[
            pltpu.VMEM((rows, 1), f32),
            pltpu.VMEM((rows, 1), f32),
            pltpu.VMEM((rows, LANES), f32),
        ],
    )
    return pl.pallas_call(
        functools.partial(_diff_sample_kernel, nh=nh, nt=nt, pps=pps, nsteps=nsteps, post_scale=post_scale),
        grid_spec=grid_spec,
        out_shape=jax.ShapeDtypeStruct((db, nt, w), MXU_DTYPE),
        compiler_params=_cparams(("arbitrary", "arbitrary")),
        name="diff_sample",
    )(page_table, lam, qm, *([cache_kt] * pps), *([cache_v] * pps), knewt, vnew, g_sub, sg)


def _dsa_sample_kernel(pt_ref, qs_ref, qi_ref, w_ref, *refs, nt, pps, nsteps, topk, page):
    kit_refs = refs[:pps]
    ks_refs = refs[pps:2 * pps]
    vs_refs = refs[2 * pps:3 * pps]
    kint_ref, ksn_ref, vsn_ref, sg_ref, o_ref, key_scr, s_scr, v_scr = refs[3 * pps:]
    s_id = pl.program_id(1)
    gsz = LANES // nt

    def process(blk, kit, ks, vs, causal_new):
        sc = _dot(qi_ref[...], kit.astype(MXU_DTYPE))
        val = jnp.maximum(sc, 0.0) * w_ref[...]
        acc = jnp.sum(val.reshape(gsz, nt, page), axis=0)
        acc = jnp.where(acc == 0.0, 0.0, acc)
        if causal_new:
            acc = jnp.where(_iota(acc.shape, 1) <= _iota(acc.shape, 0), acc, -jnp.inf)
        key_scr[blk] = _f2key(acc)
        s_scr[blk] = _dot_nt(qs_ref[...], ks.astype(MXU_DTYPE))
        v_scr[blk] = vs.astype(MXU_DTYPE)

    @pl.when(s_id < nsteps)
    def _():
        for r in range(pps):
            process(s_id * pps + r, kit_refs[r][...], ks_refs[r][...], vs_refs[r][...], False)

    @pl.when(s_id == nsteps)
    def _():
        nblk = nsteps * pps + 1
        process(nblk - 1, kint_ref[...], ksn_ref[...], vsn_ref[...], True)

        keys = jnp.concatenate([key_scr[c] for c in range(nblk)], axis=1)

        def count_ge_all(cand):
            return jnp.sum(jnp.where(keys >= cand, 1.0, 0.0), axis=1, keepdims=True)

        def bit_body(it, prefix):
            cand = prefix + lax.shift_left(jnp.int32(1), 31 - it)
            return jnp.where(count_ge_all(cand) >= topk, cand, prefix)

        kth = lax.fori_loop(0, 32, bit_body, jnp.full((nt, 1), INT_MIN, i32))
        _demote_surplus_ties_lane(key_scr, nblk, nt, topk, kth, count_ge_all)
        thr = jnp.maximum(kth, KEY_NEG_INF + 1)

        unroll = max(u for u in (5, 4, 3, 2, 1) if nblk % u == 0)

        def sel_scores(c):
            msk = key_scr[c] >= thr
            return msk[None], s_scr[c].reshape(gsz, nt, page)

        def max_body(it, mx):
            for u in range(unroll):
                msk, s3 = sel_scores(it * unroll + u)
                mx = jnp.maximum(mx, jnp.where(msk, s3, NEG_BIG))
            return mx

        mx = lax.fori_loop(0, nblk // unroll, max_body, jnp.full((gsz, nt, page), NEG_BIG, f32))
        m = jnp.max(mx, axis=2, keepdims=True)

        def att_body(it, carry):
            lacc, acc = carry
            for u in range(unroll):
                c = it * unroll + u
                msk, s3 = sel_scores(c)
                p = jnp.where(msk, jnp.exp2(s3 - m), 0.0)
                lacc = lacc + p
                acc = acc + _dot(p.reshape(gsz * nt, page).astype(MXU_DTYPE), v_scr[c])
            return lacc, acc

        lacc, acc = lax.fori_loop(0, nblk // unroll, att_body,
                                  (jnp.zeros((gsz, nt, page), f32), jnp.zeros((LANES, D_DSA), f32)))
        l = jnp.sum(lacc, axis=2, keepdims=True).reshape(gsz * nt, 1)
        o_ref[...] = (acc / l * sg_ref[...].astype(f32)).astype(o_ref.dtype)


def _dsa_sample(page_table, qs_rows, qi_rows, wcol, cache_kit, cache_ks, cache_vs, kit_new, ks_new, vs_new, sg_perm,
                *, layer, nt, topk):
    db = qs_rows.shape[0]
    page = cache_ks.shape[2]
    npages = page_table.shape[1]
    pps = math.gcd(PAGES_PER_STEP, npages)
    nsteps = npages // pps
    nblk = npages + 1
    assert page == LANES

    def page_map(r):
        def f(bb, s, pt):
            return (layer, pt[bb, jnp.minimum(s * pps + r, npages - 1)], 0, 0)
        return f

    def pspecs(shape):
        return [pl.BlockSpec((None, None) + shape, page_map(r)) for r in range(pps)]

    per_b = lambda shape: pl.BlockSpec((None,) + shape, lambda bb, s, pt: (bb, 0, 0))
    grid_spec = pltpu.PrefetchScalarGridSpec(
        num_scalar_prefetch=1,
        grid=(db, nsteps + 1),
        in_specs=[
            per_b((LANES, D_DSA)), per_b((LANES, D_IDX)), per_b((LANES, 1)),
            *pspecs((D_IDX, page)), *pspecs((page, D_DSA)), *pspecs((page, D_DSA)),
            per_b((D_IDX, page)), per_b((page, D_DSA)), per_b((page, D_DSA)),
            per_b((LANES, D_DSA)),
        ],
        out_specs=per_b((LANES, D_DSA)),
        scratch_shapes=[
            pltpu.VMEM((nblk, nt, page), i32),
            pltpu.VMEM((nblk, LANES, page), f32),
            pltpu.VMEM((nblk, page, D_DSA), MXU_DTYPE),
        ],
    )
    return pl.pallas_call(
        functools.partial(_dsa_sample_kernel, nt=nt, pps=pps, nsteps=nsteps, topk=topk, page=page),
        grid_spec=grid_spec,
        out_shape=jax.ShapeDtypeStruct((db, LANES, D_DSA), MXU_DTYPE),
        compiler_params=_cparams(("arbitrary", "arbitrary")),
        name="dsa_sample",
    )(page_table, qs_rows, qi_rows, wcol, *([cache_kit] * pps), *([cache_ks] * pps), *([cache_vs] * pps),
      kit_new, ks_new, vs_new, sg_perm)


def _rope_tables(pos, head_dim):
    half = head_dim // 2
    lane = jnp.arange(LANES)
    inv = ROPE_THETA ** (-(lane % half).astype(f32) / half)
    ang = pos.astype(f32)[:, None] * inv[None, :]
    sign = jnp.where((lane % head_dim) < half, -1.0, 1.0).astype(f32)
    return jnp.cos(ang), jnp.sin(ang) * sign[None, :]


def _tile_gain(g, n):
    return jnp.tile(g, n // g.shape[0]).reshape(1, n).astype(f32)


def _mixer_inputs(x2d, pos_tab, w_in_t, layer, seg, gl, *, tm, stacked=None):
    cos64, sin64, cos128, sin128 = pos_tab
    m = x2d.shape[0]
    h = _rmsnorm(x2d, gl["g_in"], min(tm, 256))
    tn = 512
    proj = functools.partial(_proj, h, tm=tm, nt=True)
    wseg = lambda name: _win_desc(w_in_t, layer, seg[name][0], seg[name][1] - seg[name][0], tn)
    rows = lambda n, dt: _rows_out(m, n, dt, tm, tn)
    rope64 = [("row", cos64), ("row", sin64)]
    rope128 = [("row", cos128), ("row", sin128)]
    nqd = seg["qd"][1] - seg["qd"][0]
    nqs = seg["qs"][1] - seg["qs"][0]
    nqm = seg["qm"][1] - seg["qm"][0]
    ngate = seg["gate"][1] - seg["gate"][0]
    norm_rope64 = functools.partial(_chunks_norm_rope, gsize=D_DH, half=D_DH // 2)
    gq64 = [("col", _tile_gain(gl["g_q_diff"], nqd) * (D_DH ** -0.5 * LOG2E))] + rope64
    gk64 = [("col", _tile_gain(gl["g_k_diff"], nqd))] + rope64
    gq128 = [("col", _tile_gain(gl["g_q_dsa"], nqs) * (D_DSA ** -0.5 * LOG2E))] + rope128
    qi_scale = D_IDX ** -0.5 * H_IDX ** -0.5
    out = {}
    (out["qd"],) = proj(wseg("qd"), norm_rope64, _store_rows, [rows(nqd, MXU_DTYPE)], gq64, name="proj_qd")
    (out["qs"],) = proj(wseg("qs"), functools.partial(_chunks_norm_rope, gsize=D_DSA, half=D_DSA // 2), _store_rows,
                        [rows(nqs, MXU_DTYPE)], gq128, name="proj_qs")
    (out["qi"],) = proj(wseg("qi"), functools.partial(_chunks_rope, half=D_IDX // 2), _store_rows,
                        [rows(H_IDX * D_IDX, MXU_DTYPE)], [("row", cos64 * qi_scale), ("row", sin64 * qi_scale)],
                        name="proj_qi")
    (out["qm"],) = proj(wseg("qm"), _chunks_norm256, _store_rows, [rows(nqm, MXU_DTYPE)],
                        [("col", _tile_gain(gl["g_q_mem"], nqm) * (D_MEM ** -0.5 * LOG2E))], name="proj_qm")
    (out["sg"],) = proj(wseg("gate"), _chunks_silu, _store_rows, [rows(ngate, MXU_DTYPE)], name="proj_gate")

    ks0, vs0, ki0 = seg["ks"][0], seg["vs"][0], seg["ki"][0]
    assert vs0 - ks0 == LANES and ks0 % LANES == 0 and ki0 % LANES == 0
    skip = (ki0 - vs0) // LANES - 1
    w_ksv = dict(array=w_in_t, n=3 * LANES, tn=LANES,
                 spec=pl.BlockSpec((None, LANES, w_in_t.shape[2]),
                                   lambda i, j: (layer, ks0 // LANES + j + jnp.where(j == 2, skip, 0), 0)))
    g_kiwi = jnp.concatenate([gl["g_k_idx"], jnp.ones((LANES - D_IDX,), f32)]).reshape(1, LANES)
    aux_ksv = [("const", gl["g_k_dsa"].reshape(1, D_DSA))] + rope128 + [("const", g_kiwi)] + rope64
    blk = lambda dt: dict(shape=(m, LANES), dtype=dt, block=(tm, LANES), index=lambda i, j: (i, 0), alias=None)
    if stacked is None:
        out["kd"], out["kd_c"] = proj(wseg("kd"), norm_rope64, _store_rows,
                                      [rows(nqd, f32), rows(nqd, MXU_DTYPE)], gk64, name="proj_kd")
        out["vd"], out["vd_c"] = proj(wseg("vd"), _chunks_raw, _store_rows,
                                      [rows(nqd, f32), rows(nqd, MXU_DTYPE)], name="proj_vd")
        out["ks"], out["ks_c"], out["vs"], out["vs_c"], out["kiwi"], out["ki2"] = proj(
            w_ksv, _chunks_ks_vs_kiwi, _store_ks_vs_kiwi,
            [blk(f32), blk(MXU_DTYPE), blk(f32), blk(MXU_DTYPE), blk(f32), blk(MXU_DTYPE)], aux_ksv,
            name="proj_ks_vs_kiwi", pass_col=True)
        return out

    l = layer
    b, t = stacked["b"], stacked["t"]
    nrb = t // tm
    nh = nqd // LANES
    hb = _col_tile(nqd, tn) // LANES
    vkb = stacked["diff_tk"]
    assert tm % vkb == 0
    depth = stacked["pdk"].shape[0]
    out["pdk"], out["kd_c"] = proj(
        wseg("kd"), norm_rope64, _store_kd_prompt,
        [dict(shape=(depth, b, nh, LANES, t), dtype=f32, block=(None, None, hb, LANES, tm),
              index=lambda i, j: (l, i // nrb, j, 0, i % nrb), alias=stacked["pdk"]),
         rows(nqd, MXU_DTYPE)], gk64, name="proj_kd")
    out["pdv"], out["vd_c"] = proj(
        wseg("vd"), _chunks_raw, _store_vd_prompt,
        [dict(shape=(depth, b, nh, t, LANES), dtype=f32, block=(None, None, hb, tm, LANES),
              index=lambda i, j: (l, i // nrb, j, i % nrb, 0), alias=stacked["pdv"]),
         dict(shape=(b, nh, t // vkb, LANES, vkb), dtype=MXU_DTYPE, block=(None, hb, tm // vkb, LANES, vkb),
              index=lambda i, j: (i // nrb, j, i % nrb, 0, 0), alias=None)], name="proj_vd")
    tok_major = lambda key: dict(shape=(depth, b, t, D_DSA), dtype=f32, block=(None, None, tm, D_DSA),
                                 index=lambda i, j: (l, i // nrb, i % nrb, 0), alias=stacked[key])
    out["psk"], out["ks_c"], out["psv"], out["vs_c"], out["kiwi"], out["ki2"], out["pik"] = proj(
        w_ksv, _chunks_ks_vs_kiwi, _store_ks_vs_kiwi,
        [tok_major("psk"), blk(MXU_DTYPE), tok_major("psv"), blk(MXU_DTYPE), blk(f32), blk(MXU_DTYPE),
         dict(shape=(depth, b, D_IDX, t), dtype=f32, block=(None, None, D_IDX, tm),
              index=lambda i, j: (l, i // nrb, 0, i % nrb), alias=stacked["pik"])], aux_ksv, name="proj_ks_vs_kiwi",
        pass_col=True)
    return out


def kernel(x_prompt, x_sample, mem_prompt, cache_diff_k, cache_diff_v, cache_dsa_k, cache_dsa_v, cache_idx_k,
           cache_mem_k, cache_mem_v, page_table, w_in, w_out, w_mem_kv, g_in, g_mem, g_q_diff, g_k_diff,
           g_sub_diff, lam_q1, lam_k1, lam_q2, lam_k2, g_q_dsa, g_k_dsa, g_k_idx, g_q_mem, g_k_mem):
    depth = w_in.shape[0]
    b, t, d = x_prompt.shape
    db, nt, _ = x_sample.shape
    n_mem = mem_prompt.shape[1]
    n_phys, page = cache_dsa_k.shape[1], cache_dsa_k.shape[2]
    npages = page_table.shape[1]
    past = npages * page
    h_diff = cache_diff_k.shape[3]
    w_diff = h_diff * 2 * D_DH
    h_dsa = (3 * d // 8) // D_DSA
    w_dsa = h_dsa * D_DSA
    h_mem = cache_mem_k.shape[3]
    w_mem = h_mem * D_MEM
    gsz = LANES // nt
    assert w_diff + w_dsa + w_mem == d and w_diff == w_dsa and LANES % nt == 0 and h_dsa <= gsz and gsz == H_IDX
    topk_p = min(TOPK_MAX, t // 4)
    topk_s = min(TOPK_MAX, (past + nt) // 4)
    m_p = b * t
    m_s = db * nt
    tm_p = min(1024, t)
    assert t % tm_p == 0

    widths = (w_diff, w_diff, w_diff, w_dsa, D_DSA, D_DSA, H_IDX * D_IDX, D_IDX, H_IDX, w_mem, d)
    offs = [0]
    for wdt in widths:
        offs.append(offs[-1] + wdt)
    names = ("qd", "kd", "vd", "qs", "ks", "vs", "qi", "ki", "wi", "qm", "gate")
    seg = {n: (offs[k], offs[k + 1]) for k, n in enumerate(names)}

    pos_p = jnp.arange(t, dtype=i32)
    pos_s = jnp.tile(past + jnp.arange(nt, dtype=i32), db)
    tab_p = _rope_tables(pos_p, D_DH) + _rope_tables(pos_p, D_DSA)
    tab_s = _rope_tables(pos_s, D_DH) + _rope_tables(pos_s, D_DSA)

    cdkt = cache_diff_k.transpose(0, 1, 3, 4, 5, 2).reshape(depth, n_phys, h_diff, 2 * D_DH, page)
    cdv = cache_diff_v.transpose(0, 1, 3, 2, 4)
    ckit = cache_idx_k.transpose(0, 1, 3, 2)
    cmk = cache_mem_k.reshape(depth, db, n_mem, w_mem)
    cmv = cache_mem_v.reshape(depth, db, n_mem, w_mem)
    w_in_t = jnp.swapaxes(w_in, 1, 2)

    x_p = x_prompt.reshape(m_p, d)
    x_s = x_sample.reshape(m_s, d)
    mem2d = mem_prompt.reshape(b * n_mem, d)
    stk = dict(pdk=jnp.zeros((depth, b, h_diff, 2 * D_DH, t), f32), pdv=jnp.zeros((depth, b, h_diff, t, 2 * D_DH), f32),
               psk=jnp.zeros((depth, b, t, D_DSA), f32), psv=jnp.zeros((depth, b, t, D_DSA), f32),
               pik=jnp.zeros((depth, b, D_IDX, t), f32))
    outs = {k: [] for k in ("pmk", "pmv", "sdk", "sdv", "ssk", "ssv", "sik")}

    for l in range(depth):
        gl = dict(g_in=g_in[l], g_q_diff=g_q_diff[l], g_k_diff=g_k_diff[l], g_q_dsa=g_q_dsa[l],
                  g_k_dsa=g_k_dsa[l], g_k_idx=g_k_idx[l], g_q_mem=g_q_mem[l])
        lam_init = 0.8 - 0.6 * math.exp(-0.3 * l)
        lam = (jnp.exp(jnp.sum(lam_q1[l] * lam_k1[l])) - jnp.exp(jnp.sum(lam_q2[l] * lam_k2[l])) + lam_init)
        lam = lam.astype(f32).reshape(1)
        g_sub = g_sub_diff[l].reshape(1, 2 * D_DH)
        post = 1.0 - lam_init

        diff_tq, diff_tk = min(DIFF_TQ, t), min(DIFF_TK, t)
        tp = _mixer_inputs(x_p, tab_p, w_in_t, l, seg, gl, tm=tm_p, stacked=dict(stk, b=b, t=t, diff_tk=diff_tk))
        for key in ("pdk", "pdv", "psk", "psv", "pik"):
            stk[key] = tp[key]
        hm = _rmsnorm(mem2d, g_mem[l], 256)
        mrows = lambda dt: [_rows_out(b * n_mem, w_mem, dt, b * n_mem, 512)]
        mk, = _proj(hm, _wcols_desc(w_mem_kv, l, 0, w_mem, 512), _chunks_norm256,
                    _store_rows, mrows(f32), [("col", _tile_gain(g_k_mem[l], w_mem))], tm=b * n_mem, nt=False,
                    name="proj_mk")
        mv, = _proj(hm, _wcols_desc(w_mem_kv, l, w_mem, w_mem, 512), _chunks_raw, _store_rows, mrows(f32),
                    tm=b * n_mem, nt=False, name="proj_mv")
        r3 = lambda a: a.reshape(b, t, a.shape[-1])
        sg = r3(tp["sg"])
        od = _diff_prompt(lam, r3(tp["qd"]), r3(tp["kd_c"]), tp["vd_c"], g_sub, sg, post_scale=post,
                          tq=diff_tq, tk=diff_tk)
        tc = 256
        vst = tp["vs_c"].reshape(b, t // tc, tc, D_DSA).transpose(0, 1, 3, 2)
        os_ = _dsa_prompt(r3(tp["qs"]), r3(tp["qi"]), r3(tp["kiwi"]), r3(tp["ki2"]), r3(tp["ks_c"]), vst, sg,
                          topk=topk_p, tc=tc)
        om = _mem_attn(r3(tp["qm"]), mk.reshape(1, b, n_mem, w_mem), mv.reshape(1, b, n_mem, w_mem), sg,
                       layer=0, sg_col0=w_diff + w_dsa, tq=min(512, t))
        x_p = _outproj(od.reshape(m_p, w_diff), os_.reshape(m_p, w_dsa), om.reshape(m_p, w_mem), w_out, x_p,
                       layer=l, tm=tm_p, tn=512)
        outs["pmk"].append(mk.reshape(b, n_mem, h_mem, D_MEM))
        outs["pmv"].append(mv.reshape(b, n_mem, h_mem, D_MEM))

        ts = _mixer_inputs(x_s, tab_s, w_in_t, l, seg, gl, tm=m_s)
        s3 = lambda a: a.reshape(db, nt, a.shape[-1])
        sg_s = s3(ts["sg"])
        tokpad = lambda a, axis: jnp.pad(a, [(0, page - nt) if ax == axis else (0, 0) for ax in range(a.ndim)])
        q5 = ts["qd"].reshape(db, nt, h_diff, 2, D_DH).transpose(0, 2, 3, 1, 4)
        eye_c = jnp.eye(2, dtype=MXU_DTYPE)
        qm = (q5[:, :, :, :, None, :] * eye_c[None, None, :, None, :, None]).reshape(db, h_diff * 2 * nt, 2 * D_DH)
        kd4 = ts["kd"].reshape(db, nt, h_diff, 2 * D_DH)
        vd4 = ts["vd"].reshape(db, nt, h_diff, 2 * D_DH)
        od_s = _diff_sample(page_table, lam, qm, cdkt, cdv, tokpad(kd4.transpose(0, 2, 3, 1), 3),
                            tokpad(vd4.transpose(0, 2, 1, 3), 2), g_sub, sg_s, layer=l, nt=nt, post_scale=post)
        qs4 = ts["qs"].reshape(db, nt, h_dsa, D_DSA)
        slot_major = lambda a: a.transpose(0, 2, 1, 3).reshape(db, LANES, a.shape[-1])
        qs_rows = slot_major(jnp.pad(qs4, ((0, 0), (0, 0), (0, gsz - h_dsa), (0, 0))))
        qi_rows = slot_major(ts["qi"].reshape(db, nt, H_IDX, D_IDX))
        wcol = slot_major(ts["kiwi"][:, D_IDX:D_IDX + H_IDX].reshape(db, nt, H_IDX, 1))
        sg_dsa = sg_s[:, :, w_diff:w_diff + w_dsa].reshape(db, nt, h_dsa, D_DSA)
        sg_perm = slot_major(jnp.pad(sg_dsa, ((0, 0), (0, 0), (0, gsz - h_dsa), (0, 0))))
        ki_new = s3(ts["kiwi"][:, :D_IDX])
        os_s = _dsa_sample(page_table, qs_rows, qi_rows, wcol, ckit, cache_dsa_k, cache_dsa_v,
                           tokpad(ki_new.transpose(0, 2, 1), 2), tokpad(s3(ts["ks"]), 1), tokpad(s3(ts["vs"]), 1),
                           sg_perm, layer=l, nt=nt, topk=topk_s)
        os_s = os_s.reshape(db, gsz, nt, D_DSA)[:, :h_dsa].transpose(0, 2, 1, 3).reshape(m_s, w_dsa)
        om_s = _mem_attn(s3(ts["qm"]), cmk, cmv, sg_s, layer=l, sg_col0=w_diff + w_dsa, tq=nt)
        x_s = _outproj(od_s.reshape(m_s, w_diff), os_s, om_s.reshape(m_s, w_mem), w_out, x_s, layer=l, tm=m_s,
                       tn=512)
        outs["sdk"].append(ts["kd"].reshape(db, nt, h_diff, 2, D_DH))
        outs["sdv"].append(ts["vd"].reshape(db, nt, h_diff, 2 * D_DH))
        outs["ssk"].append(ts["ks"].reshape(db, nt, D_DSA))
        outs["ssv"].append(ts["vs"].reshape(db, nt, D_DSA))
        outs["sik"].append(ki_new)

    st = lambda k: jnp.stack(outs[k])
    p_diff_k = stk["pdk"].reshape(depth, b, h_diff, 2, D_DH, t).transpose(0, 1, 5, 2, 3, 4)
    p_diff_v = stk["pdv"].transpose(0, 1, 3, 2, 4)
    p_idx_k = stk["pik"].transpose(0, 1, 3, 2)
    return (x_p.reshape(b, t, d), x_s.reshape(db, nt, d),
            p_diff_k, p_diff_v, stk["psk"], stk["psv"], p_idx_k, st("pmk"), st("pmv"),
            st("sdk"), st("sdv"), st("ssk"), st("ssv"), st("sik"))
```

```python
import functools
import math

import jax
import jax.numpy as jnp
from jax import lax
from jax.experimental import pallas as pl
from jax.experimental.pallas import tpu as pltpu

EPS = 1e-6
ROPE_THETA = 10000.0
TOPK_MAX = 256
LANES = 128
D_DH = 64
D_DSA = 128
D_IDX = 64
H_IDX = 16
D_MEM = 256
NEG_BIG = -1e30
KEY_NEG_INF = -2139095041
INT_MIN = -2147483648
MXU_DTYPE = jnp.bfloat16
VMEM_LIMIT_BYTES = 52 * 1024 * 1024
PAGES_PER_STEP = 8
DSA_PAGES_PER_STEP = 16
DIFF_TQ = 512
DIFF_TK = 512
LOG2E = math.log2(math.e)

f32 = jnp.float32
i32 = jnp.int32


def _cparams(sem):
    return pltpu.CompilerParams(dimension_semantics=sem, vmem_limit_bytes=VMEM_LIMIT_BYTES)


def _dot(a, b):
    return jnp.dot(a, b, preferred_element_type=f32)


def _dot_nt(a, b):
    return lax.dot_general(a, b, (((1,), (1,)), ((), ())), preferred_element_type=f32)


def _iota(shape, dim):
    return lax.broadcasted_iota(i32, shape, dim)


def _group_mat(gsize, value):
    r = _iota((LANES, LANES), 0) // gsize
    c = _iota((LANES, LANES), 1) // gsize
    return jnp.where(r == c, value, 0.0).astype(MXU_DTYPE)


def _group_sum(x, gmat):
    return _dot(x.astype(MXU_DTYPE), gmat)


def _rope_chunk(n, cos, sin_signed, half):
    if 2 * half == LANES:
        rot = pltpu.roll(n, half, 1)
    else:
        first = (_iota(n.shape, 1) % (2 * half)) < half
        rot = jnp.where(first, pltpu.roll(n, LANES - half, 1), pltpu.roll(n, half, 1))
    return n * cos + rot * sin_signed


def _f2key(x):
    b = pltpu.bitcast(x, i32)
    return b ^ (lax.shift_right_arithmetic(b, 31) & 0x7FFFFFFF)


def _rmsnorm_kernel(x_ref, g_ref, o_ref):
    x = x_ref[...]
    ms = jnp.mean(x * x, axis=-1, keepdims=True)
    o_ref[...] = (x * lax.rsqrt(ms + EPS) * g_ref[...]).astype(o_ref.dtype)


def _rmsnorm(x, g, tm):
    m, d = x.shape
    return pl.pallas_call(
        _rmsnorm_kernel,
        grid=(m // tm,),
        in_specs=[pl.BlockSpec((tm, d), lambda i: (i, 0)), pl.BlockSpec((1, d), lambda i: (0, 0))],
        out_specs=pl.BlockSpec((tm, d), lambda i: (i, 0)),
        out_shape=jax.ShapeDtypeStruct((m, d), MXU_DTYPE),
        compiler_params=_cparams(("arbitrary",)),
        name="rmsnorm",
    )(x, g.reshape(1, d))


def _chunks_raw(z, aux, store):
    for c in range(z.shape[1] // LANES):
        store(c, z[:, c * LANES:(c + 1) * LANES])


def _chunks_silu(z, aux, store):
    for c in range(z.shape[1] // LANES):
        zc = z[:, c * LANES:(c + 1) * LANES]
        store(c, zc / (1.0 + jnp.exp(-zc)))


def _chunks_norm_rope(z, aux, store, *, gsize, half):
    gain, cos, sin = aux
    gmat = _group_mat(gsize, 1.0 / gsize)
    for c in range(z.shape[1] // LANES):
        sl = slice(c * LANES, (c + 1) * LANES)
        zc = z[:, sl]
        n = zc * lax.rsqrt(_group_sum(zc * zc, gmat) + EPS) * gain[:, sl]
        store(c, _rope_chunk(n, cos, sin, half))


def _chunks_rope(z, aux, store, *, half):
    cos, sin = aux
    for c in range(z.shape[1] // LANES):
        store(c, _rope_chunk(z[:, c * LANES:(c + 1) * LANES], cos, sin, half))


def _chunks_norm256(z, aux, store):
    (gain,) = aux
    gmat = _group_mat(LANES, 1.0 / D_MEM)
    for c in range(z.shape[1] // D_MEM):
        a = z[:, c * D_MEM:c * D_MEM + LANES]
        b = z[:, c * D_MEM + LANES:(c + 1) * D_MEM]
        inv = lax.rsqrt(_group_sum(a * a, gmat) + _group_sum(b * b, gmat) + EPS)
        for k, v in enumerate((a, b)):
            sl = slice(c * D_MEM + k * LANES, c * D_MEM + (k + 1) * LANES)
            store(2 * c + k, v * inv * gain[:, sl])


def _chunks_kiwi(z, aux, store):
    gain, cos, sin = aux
    z = z[:, :]
    gmat = _group_mat(D_IDX, 1.0 / D_IDX)
    n = z * lax.rsqrt(_group_sum(z * z, gmat) + EPS) * gain
    r = _rope_chunk(n, cos, sin, D_IDX // 2)
    lo = _iota(z.shape, 1) < D_IDX
    store(0, jnp.where(lo, r, z))
    r_lo = jnp.where(lo, r, 0.0)
    store(1, r_lo + pltpu.roll(r_lo, D_IDX, 1))


def _store_rows(outs, c, r):
    for o in outs:
        o[:, c * LANES:(c + 1) * LANES] = r.astype(o.dtype)


def _store_kd_prompt(outs, c, r):
    outs[0][c] = r.T
    outs[1][:, c * LANES:(c + 1) * LANES] = r.astype(outs[1].dtype)


def _store_vd_prompt(outs, c, r):
    outs[0][c] = r
    rt = r.T.astype(outs[1].dtype)
    kb = outs[1].shape[-1]
    for kk in range(r.shape[0] // kb):
        outs[1][c, kk] = rt[:, kk * kb:(kk + 1) * kb]


def _chunks_ks_vs_kiwi(z, aux, store, *, col):
    gain_k, cos128, sin128, gain_kiwi, cos64, sin64 = aux
    j = col

    @pl.when(j == 0)
    def _():
        _chunks_norm_rope(z, (gain_k, cos128, sin128), functools.partial(store, 0), gsize=D_DSA, half=D_DSA // 2)

    @pl.when(j == 1)
    def _():
        store(1, 0, z[:, :])

    @pl.when(j == 2)
    def _():
        _chunks_kiwi(z, (gain_kiwi, cos64, sin64), functools.partial(store, 2))


def _store_ks_vs_kiwi(outs, which, c, r):
    if which < 2:
        outs[2 * which][...] = r
        outs[2 * which + 1][...] = r.astype(outs[2 * which + 1].dtype)
    elif c == 0:
        outs[4][...] = r
        if len(outs) > 6:
            outs[6][...] = r.T[:D_IDX, :]
    else:
        outs[5][...] = r.astype(outs[5].dtype)


class _LazyTile:
    def __init__(self, ref):
        self.ref = ref
        self.shape = ref.shape

    def __getitem__(self, idx):
        return self.ref[idx]


def _proj_kernel(*refs, chunk_fn, store_fn, n_aux, n_alias, nt, nj, ntiles, pass_col):
    h_ref, w_ref = refs[:2]
    aux_refs = refs[2:2 + n_aux]
    outs = refs[2 + n_aux + n_alias:-1]
    z_scr = refs[-1]
    s = pl.program_id(0)

    def epilogue():
        kw = dict(col=(s - 1) % nj) if pass_col else {}
        chunk_fn(_LazyTile(z_scr), [r[...] for r in aux_refs], functools.partial(store_fn, outs), **kw)

    def matmul():
        w = w_ref[...].astype(MXU_DTYPE)
        z_scr[...] = _dot_nt(h_ref[...], w) if nt else _dot(h_ref[...], w)

    @pl.when(s == 0)
    def _():
        matmul()

    @pl.when(jnp.logical_and(s > 0, s < ntiles))
    def _():
        epilogue()
        matmul()

    @pl.when(s == ntiles)
    def _():
        epilogue()


def _win_desc(w_in_t, layer, off, n, tn):
    tn = _col_tile(n, tn)
    k = w_in_t.shape[2]
    if off % tn == 0:
        return dict(array=w_in_t, spec=pl.BlockSpec((None, tn, k), lambda i, j: (layer, off // tn + j, 0)), n=n, tn=tn)
    row0 = layer * w_in_t.shape[1] + off
    assert row0 % 8 == 0
    spec = pl.BlockSpec((pl.Element(tn), pl.Element(k)), lambda i, j: (pl.multiple_of(row0 + j * tn, 8), 0))
    return dict(array=w_in_t.reshape(-1, k), spec=spec, n=n, tn=tn)


def _wcols_desc(w, layer, off, n, tn):
    tn = _col_tile(n, tn)
    assert off % tn == 0
    return dict(array=w, spec=pl.BlockSpec((None, w.shape[1], tn), lambda i, j: (layer, 0, off // tn + j)), n=n, tn=tn)


def _col_tile(n, tn):
    return math.gcd(n, tn)


def _rows_out(m, n, dtype, tm, tn):
    tn = _col_tile(n, tn)
    return dict(shape=(m, n), dtype=dtype, block=(tm, tn), index=lambda i, j: (i, j), alias=None)


def _proj(h, wd, chunk_fn, store_fn, outs, aux=(), *, tm, nt, name, pass_col=False):
    m, k = h.shape
    n, tn = wd["n"], wd["tn"]
    assert m % tm == 0 and tn % LANES == 0 and n % tn == 0
    nj = n // tn
    ntiles = (m // tm) * nj
    cur = lambda f: (lambda s: f(jnp.minimum(s, ntiles - 1) // nj, jnp.minimum(s, ntiles - 1) % nj))
    prev = lambda f: (lambda s: f(jnp.maximum(s - 1, 0) // nj, jnp.maximum(s - 1, 0) % nj))
    in_specs = [pl.BlockSpec((tm, k), cur(lambda i, j: (i, 0))),
                pl.BlockSpec(wd["spec"].block_shape, cur(wd["spec"].index_map))]
    args = [h, wd["array"]]
    for kind, a in aux:
        if kind == "col":
            in_specs.append(pl.BlockSpec((1, tn), prev(lambda i, j: (0, j))))
        elif kind == "const":
            in_specs.append(pl.BlockSpec((1, LANES), prev(lambda i, j: (0, 0))))
        else:
            nrb = a.shape[0] // tm
            in_specs.append(pl.BlockSpec((tm, LANES), prev(lambda i, j, nrb=nrb: (i % nrb, 0))))
        args.append(a)
    aliases = {}
    for k_out, o in enumerate(outs):
        if o["alias"] is not None:
            aliases[len(args)] = k_out
            in_specs.append(pl.BlockSpec(memory_space=pl.ANY))
            args.append(o["alias"])
    return pl.pallas_call(
        functools.partial(_proj_kernel, chunk_fn=chunk_fn, store_fn=store_fn, n_aux=len(aux),
                          n_alias=len(aliases), nt=nt, nj=nj, ntiles=ntiles, pass_col=pass_col),
        grid=(ntiles + 1,),
        in_specs=in_specs,
        out_specs=[pl.BlockSpec(o["block"], prev(o["index"])) for o in outs],
        out_shape=[jax.ShapeDtypeStruct(o["shape"], o["dtype"]) for o in outs],
        scratch_shapes=[pltpu.VMEM((tm, tn), f32)],
        input_output_aliases=aliases,
        compiler_params=_cparams(("arbitrary",)),
        name=name,
    )(*args)


def _outproj_kernel(od_ref, os_ref, om_ref, w_ref, x_ref, o_ref, *, wd, ws):
    acc = _dot(od_ref[...], w_ref[0:wd, :].astype(MXU_DTYPE))
    acc += _dot(os_ref[...], w_ref[wd:wd + ws, :].astype(MXU_DTYPE))
    acc += _dot(om_ref[...], w_ref[wd + ws:, :].astype(MXU_DTYPE))
    o_ref[...] = x_ref[...] + acc


def _outproj(od, os_, om, w, x, *, layer, tm, tn):
    m, d = x.shape
    wd, ws, wm = od.shape[1], os_.shape[1], om.shape[1]
    return pl.pallas_call(
        functools.partial(_outproj_kernel, wd=wd, ws=ws),
        grid=(m // tm, d // tn),
        in_specs=[
            pl.BlockSpec((tm, wd), lambda i, j: (i, 0)),
            pl.BlockSpec((tm, ws), lambda i, j: (i, 0)),
            pl.BlockSpec((tm, wm), lambda i, j: (i, 0)),
            pl.BlockSpec((None, wd + ws + wm, tn), lambda i, j: (layer, 0, j)),
            pl.BlockSpec((tm, tn), lambda i, j: (i, j)),
        ],
        out_specs=pl.BlockSpec((tm, tn), lambda i, j: (i, j)),
        out_shape=jax.ShapeDtypeStruct((m, d), f32),
        compiler_params=_cparams(("arbitrary", "arbitrary")),
        name="outproj",
    )(od, os_, om, w, x)


def _sub_rmsnorm_gate(od, g, post_scale, sg):
    ms = jnp.mean(od * od, axis=-1, keepdims=True)
    return od * lax.rsqrt(ms + EPS) * g * post_scale * sg.astype(f32)


def _diff_prompt_kernel(pi_ref, pj_ref, lam_ref, q_ref, k_ref, vt_ref, g_ref, sg_ref, o_ref,
                        m_scr, l_scr, acc_scr, *, tq, tk, hb, post_scale):
    p_id = pl.program_id(2)
    i = pi_ref[p_id]
    j = pj_ref[p_id]

    @pl.when(j == 0)
    def _():
        m_scr[...] = jnp.full_like(m_scr, NEG_BIG)
        l_scr[...] = jnp.zeros_like(l_scr)
        acc_scr[...] = jnp.zeros_like(acc_scr)

    nchain = 2 * hb
    h2 = tq // 2
    split_diag = tq == tk and h2 % LANES == 0

    def masked_q(r):
        q = q_ref[:, (r // 2) * LANES:(r // 2 + 1) * LANES]
        lo = _iota(q.shape, 1) < D_DH
        zero = jnp.zeros_like(q)
        return jnp.where(lo, q, zero) if r % 2 == 0 else jnp.where(lo, zero, q)

    def keys(r):
        return k_ref[:, (r // 2) * LANES:(r // 2 + 1) * LANES]

    def scores(r):
        return _dot_nt(keys(r), masked_q(r))

    def scores_diag(r):
        qc, kc = masked_q(r), keys(r)
        return _dot_nt(kc[:h2], qc), _dot_nt(kc[h2:], qc[h2:])

    def causal(s, key0, qry0):
        return jnp.where(j * tk + key0 + _iota(s.shape, 0) <= i * tq + qry0 + _iota(s.shape, 1), s, NEG_BIG)

    def body(masked):
        s_next = scores(0)
        for r in range(nchain):
            s = s_next
            if r + 1 < nchain:
                s_next = scores(r + 1)
            if masked:
                s = causal(s, 0, 0)
            m = m_scr[r]
            m_new = jnp.maximum(m, jnp.max(s, axis=0, keepdims=True))
            alpha = jnp.exp2(m - m_new)
            p = jnp.exp2(s - m_new)
            l_scr[r] = alpha * l_scr[r] + jnp.sum(p, axis=0, keepdims=True)
            acc_scr[r] = alpha * acc_scr[r] + _dot(vt_ref[r // 2], p.astype(MXU_DTYPE))
            m_scr[r] = m_new

    def body_diag():
        s_next = scores_diag(0)
        for r in range(nchain):
            sa, sb = s_next
            if r + 1 < nchain:
                s_next = scores_diag(r + 1)
            sa = jnp.concatenate([causal(sa[:, :h2], 0, 0), sa[:, h2:]], axis=1)
            sb = causal(sb, h2, h2)
            m = m_scr[r]
            mb = jnp.concatenate([jnp.full((1, h2), NEG_BIG, f32), jnp.max(sb, axis=0, keepdims=True)], axis=1)
            m_new = jnp.maximum(m, jnp.maximum(jnp.max(sa, axis=0, keepdims=True), mb))
            alpha = jnp.exp2(m - m_new)
            pa = jnp.exp2(sa - m_new)
            pb = jnp.exp2(sb - m_new[:, h2:])
            lb = jnp.concatenate([jnp.zeros((1, h2), f32), jnp.sum(pb, axis=0, keepdims=True)], axis=1)
            l_scr[r] = alpha * l_scr[r] + jnp.sum(pa, axis=0, keepdims=True) + lb
            vt = vt_ref[r // 2]
            pva = _dot(vt[:, :h2], pa.astype(MXU_DTYPE))
            pvb = _dot(vt[:, h2:], pb.astype(MXU_DTYPE))
            acc_scr[r] = alpha * acc_scr[r] + pva + jnp.concatenate([jnp.zeros((LANES, h2), f32), pvb], axis=1)
            m_scr[r] = m_new

    crosses_diagonal = (j + 1) * tk - 1 > i * tq
    pl.when(crosses_diagonal)(body_diag if split_diag else functools.partial(body, True))
    pl.when(jnp.logical_not(crosses_diagonal))(functools.partial(body, False))

    @pl.when(j == ((i + 1) * tq - 1) // tk)
    def _():
        for hh in range(hb):
            hsl = slice(hh * LANES, (hh + 1) * LANES)
            od_t = acc_scr[2 * hh] / l_scr[2 * hh] - lam_ref[0] * (acc_scr[2 * hh + 1] / l_scr[2 * hh + 1])
            o_ref[:, hsl] = _sub_rmsnorm_gate(od_t.T, g_ref[...], post_scale, sg_ref[:, hsl]).astype(o_ref.dtype)


def _diff_prompt(lam, qd, kd, vdt, g_sub, sg, *, post_scale, tq, tk):
    b, t, w = qd.shape
    nh = w // LANES
    hb = max(g for g in (12, 6, 4, 2, 1) if nh % g == 0)
    pairs = [(i, j) for i in range(t // tq) for j in range(((i + 1) * tq - 1) // tk + 1)]
    pi = jnp.asarray([p[0] for p in pairs], i32)
    pj = jnp.asarray([p[1] for p in pairs], i32)
    grid_spec = pltpu.PrefetchScalarGridSpec(
        num_scalar_prefetch=2,
        grid=(b, nh // hb, len(pairs)),
        in_specs=[
            pl.BlockSpec(memory_space=pltpu.SMEM),
            pl.BlockSpec((None, tq, hb * LANES), lambda bb, h, p, pi, pj: (bb, pi[p], h)),
            pl.BlockSpec((None, tk, hb * LANES), lambda bb, h, p, pi, pj: (bb, pj[p], h)),
            pl.BlockSpec((None, hb, None, LANES, tk), lambda bb, h, p, pi, pj: (bb, h, pj[p], 0, 0)),
            pl.BlockSpec((1, LANES), lambda bb, h, p, pi, pj: (0, 0)),
            pl.BlockSpec((None, tq, hb * LANES), lambda bb, h, p, pi, pj: (bb, pi[p], h)),
        ],
        out_specs=pl.BlockSpec((None, tq, hb * LANES), lambda bb, h, p, pi, pj: (bb, pi[p], h)),
        scratch_shapes=[
            pltpu.VMEM((2 * hb, 1, tq), f32),
            pltpu.VMEM((2 * hb, 1, tq), f32),
            pltpu.VMEM((2 * hb, LANES, tq), f32),
        ],
    )
    return pl.pallas_call(
        functools.partial(_diff_prompt_kernel, tq=tq, tk=tk, hb=hb, post_scale=post_scale),
        grid_spec=grid_spec,
        out_shape=jax.ShapeDtypeStruct((b, t, w), MXU_DTYPE),
        compiler_params=_cparams(("arbitrary", "arbitrary", "arbitrary")),
        name="diff_prompt",
    )(pi, pj, lam, qd, kd, vdt, g_sub, sg)


def _kth_largest_key_sub(key_scr, nblk, rows, topk):
    def count_ge(cand):
        def body(c, acc):
            k = key_scr[pl.ds(pl.multiple_of(c * rows, rows), rows), :]
            return acc + jnp.sum((k >= cand).astype(i32).reshape(rows // 8, 8, LANES), axis=0)

        acc = lax.fori_loop(0, nblk, body, jnp.zeros((8, LANES), i32))
        return jnp.sum(acc, axis=0, keepdims=True)

    def bit_body(it, prefix):
        cand = prefix + lax.shift_left(jnp.int32(1), 31 - it)
        return jnp.where(count_ge(cand) >= topk, cand, prefix)

    kth = lax.fori_loop(0, 32, bit_body, jnp.full((1, LANES), INT_MIN, i32))
    return kth, count_ge


def _demote_surplus_ties_sub(key_scr, nblk, rows, topk, kth, count_ge):
    tie = (count_ge(kth) > topk) & (kth > KEY_NEG_INF)

    @pl.when(jnp.max(tie.astype(i32)) > 0)
    def _():
        need = (topk - count_ge(kth + 1)).astype(f32)
        tri = (_iota((rows, rows), 0) >= _iota((rows, rows), 1)).astype(f32).astype(MXU_DTYPE)

        def body(c, run):
            sl = pl.ds(pl.multiple_of(c * rows, rows), rows)
            k = key_scr[sl, :]
            eq = k == kth
            incl = _dot(tri, eq.astype(f32).astype(MXU_DTYPE))
            drop = eq & ((run + incl) > need)
            key_scr[sl, :] = jnp.where(drop, KEY_NEG_INF, k)
            return run + incl[rows - 1:rows, :]

        lax.fori_loop(0, nblk, body, jnp.zeros((1, LANES), f32))


def _demote_surplus_ties_lane(key_scr, nblk, nq, topk, kth, count_ge):
    tie = (count_ge(kth) > topk) & (kth > KEY_NEG_INF)

    @pl.when(jnp.max(jnp.where(tie, 1.0, 0.0)) > 0.0)
    def _():
        need = topk - count_ge(kth + 1)
        triu = (_iota((LANES, LANES), 0) <= _iota((LANES, LANES), 1)).astype(f32).astype(MXU_DTYPE)
        pad = jnp.zeros((16 - nq % 16, LANES), f32) if nq % 16 else None

        def body(c, run):
            k = key_scr[c]
            eq = k == kth
            eqf = jnp.where(eq, 1.0, 0.0)
            if pad is not None:
                eqf = jnp.concatenate([eqf, pad], axis=0)
            incl = _dot(eqf.astype(MXU_DTYPE), triu)[:nq]
            drop = eq & ((run + incl) > need)
            key_scr[c] = jnp.where(drop, KEY_NEG_INF, k)
            return run + incl[:, LANES - 1:LANES]

        lax.fori_loop(0, nblk, body, jnp.zeros((nq, 1), f32))


def _dsa_prompt_kernel(qs_ref, qi_ref, kiwi_ref, ki2_ref, ks_ref, vst_ref, sg_ref, o_ref,
                       key_scr, qm_scr, wt_scr, qst_scr, acc_scr, *, nh, topk, tc):
    qb = pl.program_id(1)
    tq = LANES
    nch = (qb * tq + tq + tc - 1) // tc

    lo = _iota((tq, LANES), 1) < D_IDX
    for h in range(H_IDX):
        chunk = qi_ref[:, (h // 2) * LANES:(h // 2 + 1) * LANES]
        keep = lo if h % 2 == 0 else jnp.logical_not(lo)
        qm_scr[h * tq:(h + 1) * tq, :] = jnp.where(keep, chunk, jnp.zeros_like(chunk))
    wt_scr[...] = kiwi_ref[...].T
    for h in range(nh):
        qst_scr[h * tq:(h + 1) * tq, :] = qs_ref[:, h * LANES:(h + 1) * LANES]

    tpos = qb * tq + _iota((1, LANES), 1)

    def idx_body(c, carry):
        sl = pl.ds(pl.multiple_of(c * tc, tc), tc)
        kc = ki2_ref[sl, :]
        acc = jnp.zeros((tc, LANES), f32)
        for hp in range(H_IDX // 2):
            sc = _dot_nt(kc, qm_scr[2 * hp * tq:(2 * hp + 2) * tq, :])
            for k in range(2):
                h = 2 * hp + k
                acc = acc + jnp.maximum(sc[:, k * tq:(k + 1) * tq], 0.0) * wt_scr[D_IDX + h:D_IDX + h + 1, :]
        acc = jnp.where(acc == 0.0, 0.0, acc)
        kpos = c * tc + _iota((tc, LANES), 0)
        acc = jnp.where(kpos <= tpos, acc, -jnp.inf)
        key_scr[sl, :] = _f2key(acc)
        return carry

    lax.fori_loop(0, nch, idx_body, 0)

    kth, count_ge = _kth_largest_key_sub(key_scr, nch, tc, topk)
    _demote_surplus_ties_sub(key_scr, nch, tc, topk, kth, count_ge)
    thr = jnp.maximum(kth, KEY_NEG_INF + 1)

    acc_scr[...] = jnp.zeros_like(acc_scr)

    def att_body(c, carry):
        m, l = carry
        sl = pl.ds(pl.multiple_of(c * tc, tc), tc)
        s = _dot_nt(ks_ref[sl, :], qst_scr[...])
        msk = key_scr[sl, :] >= thr
        s = jnp.where(jnp.concatenate([msk] * nh, axis=1), s, NEG_BIG)
        m_new = jnp.maximum(m, jnp.max(s, axis=0, keepdims=True))
        alpha = jnp.exp2(m - m_new)
        p = jnp.exp2(s - m_new)
        l = alpha * l + jnp.sum(p, axis=0, keepdims=True)
        acc_scr[...] = alpha * acc_scr[...] + _dot(vst_ref[c], p.astype(MXU_DTYPE))
        return m_new, l

    init = (jnp.full((1, nh * tq), NEG_BIG, f32), jnp.zeros((1, nh * tq), f32))
    _, l = lax.fori_loop(0, nch, att_body, init)
    out_t = acc_scr[...] / l
    for h in range(nh):
        sl = slice(h * LANES, (h + 1) * LANES)
        o_ref[:, sl] = (out_t[:, sl].T * sg_ref[:, sl].astype(f32)).astype(o_ref.dtype)


def _dsa_prompt(qs, qi, kiwi, ki2, ks, vst, sg, *, topk, tc=256):
    b, t, w = qs.shape
    nh = w // LANES
    tq = LANES
    assert t % tc == 0 and tc % tq == 0
    return pl.pallas_call(
        functools.partial(_dsa_prompt_kernel, nh=nh, topk=topk, tc=tc),
        grid=(b, t // tq),
        in_specs=[
            pl.BlockSpec((None, tq, w), lambda bb, i: (bb, i, 0)),
            pl.BlockSpec((None, tq, H_IDX * D_IDX), lambda bb, i: (bb, i, 0)),
            pl.BlockSpec((None, tq, LANES), lambda bb, i: (bb, i, 0)),
            pl.BlockSpec((None, t, LANES), lambda bb, i: (bb, 0, 0)),
            pl.BlockSpec((None, t, LANES), lambda bb, i: (bb, 0, 0)),
            pl.BlockSpec((None, t // tc, LANES, tc), lambda bb, i: (bb, 0, 0, 0)),
            pl.BlockSpec((None, tq, w), lambda bb, i: (bb, i, 1)),
        ],
        out_specs=pl.BlockSpec((None, tq, w), lambda bb, i: (bb, i, 0)),
        out_shape=jax.ShapeDtypeStruct((b, t, w), MXU_DTYPE),
        scratch_shapes=[
            pltpu.VMEM((t, LANES), i32),
            pltpu.VMEM((H_IDX * tq, LANES), MXU_DTYPE),
            pltpu.VMEM((LANES, LANES), f32),
            pltpu.VMEM((nh * tq, LANES), MXU_DTYPE),
            pltpu.VMEM((LANES, nh * tq), f32),
        ],
        compiler_params=_cparams(("arbitrary", "arbitrary")),
        name="dsa_prompt",
    )(qs, qi, kiwi, ki2, ks, vst, sg)


def _mem_attn_kernel(q_ref, k_ref, v_ref, sg_ref, o_ref, *, nh):
    def scores(h):
        hs = slice(h * D_MEM, (h + 1) * D_MEM)
        return _dot_nt(q_ref[:, hs], k_ref[:, hs].astype(MXU_DTYPE))

    s_next = scores(0)
    for h in range(nh):
        s = s_next
        if h + 1 < nh:
            s_next = scores(h + 1)
        hs = slice(h * D_MEM, (h + 1) * D_MEM)
        m = jnp.max(s, axis=1, keepdims=True)
        p = jnp.exp2(s - m)
        l = jnp.sum(p, axis=1, keepdims=True)
        o = _dot(p.astype(MXU_DTYPE), v_ref[:, hs].astype(MXU_DTYPE)) / l
        o_ref[:, hs] = (o * sg_ref[:, hs].astype(f32)).astype(o_ref.dtype)


def _mem_attn(q, k, v, sg, *, layer, sg_col0, tq):
    b, t, w = q.shape
    nm = k.shape[2]
    assert sg_col0 % w == 0
    return pl.pallas_call(
        functools.partial(_mem_attn_kernel, nh=w // D_MEM),
        grid=(b, t // tq),
        in_specs=[
            pl.BlockSpec((None, tq, w), lambda bb, i: (bb, i, 0)),
            pl.BlockSpec((None, None, nm, w), lambda bb, i: (layer, bb, 0, 0)),
            pl.BlockSpec((None, None, nm, w), lambda bb, i: (layer, bb, 0, 0)),
            pl.BlockSpec((None, tq, w), lambda bb, i: (bb, i, sg_col0 // w)),
        ],
        out_specs=pl.BlockSpec((None, tq, w), lambda bb, i: (bb, i, 0)),
        out_shape=jax.ShapeDtypeStruct((b, t, w), MXU_DTYPE),
        compiler_params=_cparams(("arbitrary", "arbitrary")),
        name="mem_attn",
    )(q, k, v, sg)


def _diff_sample_kernel(pt_ref, lam_ref, qm_ref, *refs, nh, nt, pps, nsteps, post_scale):
    kt_refs = refs[:pps]
    v_refs = refs[pps:2 * pps]
    knewt_ref, vnew_ref, g_ref, sg_ref, o_ref, m_scr, l_scr, acc_scr = refs[2 * pps:]
    s_id = pl.program_id(1)
    nr = 2 * nt

    @pl.when(s_id == 0)
    def _():
        m_scr[...] = jnp.full_like(m_scr, NEG_BIG)
        l_scr[...] = jnp.zeros_like(l_scr)
        acc_scr[...] = jnp.zeros_like(acc_scr)

    def update(kts, vs, masked):
        s = jnp.concatenate(
            [jnp.concatenate([_dot(qm_ref[h * nr:(h + 1) * nr, :], kt[h].astype(MXU_DTYPE)) for kt in kts], axis=1)
             for h in range(nh)], axis=0)
        if masked:
            tpos = _iota(s.shape, 0) % nt
            s = jnp.where(_iota(s.shape, 1) <= tpos, s, NEG_BIG)
        m = m_scr[...]
        m_new = jnp.maximum(m, jnp.max(s, axis=1, keepdims=True))
        alpha = jnp.exp2(m - m_new)
        p = jnp.exp2(s - m_new)
        l_scr[...] = alpha * l_scr[...] + jnp.sum(p, axis=1, keepdims=True)
        pb = p.astype(MXU_DTYPE)
        kw = s.shape[1] // len(kts)
        pv = []
        for h in range(nh):
            acc = None
            for r, v in enumerate(vs):
                d = _dot(pb[h * nr:(h + 1) * nr, r * kw:(r + 1) * kw], v[h].astype(MXU_DTYPE))
                acc = d if acc is None else acc + d
            pv.append(acc)
        acc_scr[...] = alpha * acc_scr[...] + jnp.concatenate(pv, axis=0)
        m_scr[...] = m_new

    @pl.when(s_id < nsteps)
    def _():
        update(kt_refs, v_refs, False)

    @pl.when(s_id == nsteps)
    def _():
        update([knewt_ref], [vnew_ref], True)
        o = acc_scr[...] / l_scr[...]
        for h in range(nh):
            od = o[h * nr:h * nr + nt] - lam_ref[0] * o[h * nr + nt:(h + 1) * nr]
            sl = slice(h * LANES, (h + 1) * LANES)
            o_ref[:, sl] = _sub_rmsnorm_gate(od, g_ref[...], post_scale, sg_ref[:, sl]).astype(o_ref.dtype)


def _diff_sample(page_table, lam, qm, cache_kt, cache_v, knewt, vnew, g_sub, sg, *, layer, nt, post_scale):
    db, rows, _ = qm.shape
    nh = cache_v.shape[2]
    page = cache_v.shape[3]
    npages = page_table.shape[1]
    pps = math.gcd(PAGES_PER_STEP, npages)
    nsteps = npages // pps
    w = nh * LANES

    def page_map(r):
        def f(bb, s, pt):
            return (layer, pt[bb, jnp.minimum(s * pps + r, npages - 1)], 0, 0, 0)
        return f

    kt_specs = [pl.BlockSpec((None, None, nh, LANES, page), page_map(r)) for r in range(pps)]
    v_specs = [pl.BlockSpec((None, None, nh, page, LANES), page_map(r)) for r in range(pps)]
    grid_spec = pltpu.PrefetchScalarGridSpec(
        num_scalar_prefetch=1,
        grid=(db, nsteps + 1),
        in_specs=[
            pl.BlockSpec(memory_space=pltpu.SMEM),
            pl.BlockSpec((None, rows, LANES), lambda bb, s, pt: (bb, 0, 0)),
            *kt_specs, *v_specs,
            pl.BlockSpec((None, nh, LANES, page), lambda bb, s, pt: (bb, 0, 0, 0)),
            pl.BlockSpec((None, nh, page, LANES), lambda bb, s, pt: (bb, 0, 0, 0)),
            pl.BlockSpec((1, LANES), lambda bb, s, pt: (0, 0)),
            pl.BlockSpec((None, nt, w), lambda bb, s, pt: (bb, 0, 0)),
        ],
        out_specs=pl.BlockSpec((None, nt, w), lambda bb, s, pt: (bb, 0, 0)),
        scratch_shapes=[
            pltpu.VMEM((rows, 1), f32),
            pltpu.VMEM((rows, 1), f32),
            pltpu.VMEM((rows, LANES), f32),
        ],
    )
    return pl.pallas_call(
        functools.partial(_diff_sample_kernel, nh=nh, nt=nt, pps=pps, nsteps=nsteps, post_scale=post_scale),
        grid_spec=grid_spec,
        out_shape=jax.ShapeDtypeStruct((db, nt, w), MXU_DTYPE),
        compiler_params=_cparams(("arbitrary", "arbitrary")),
        name="diff_sample",
    )(page_table, lam, qm, *([cache_kt] * pps), *([cache_v] * pps), knewt, vnew, g_sub, sg)


def _dsa_sample_kernel(pt_ref, qs_ref, qi_ref, w_ref, *refs, nt, pps, nsteps, topk, page):
    kit_refs = refs[:pps]
    ks_refs = refs[pps:2 * pps]
    vs_refs = refs[2 * pps:3 * pps]
    kint_ref, ksn_ref, vsn_ref, sg_ref, o_ref, key_scr, s_scr, v_scr = refs[3 * pps:]
    s_id = pl.program_id(1)
    gsz = LANES // nt

    def process(blk, kit, ks, vs, causal_new):
        sc = _dot(qi_ref[...], kit.astype(MXU_DTYPE))
        val = jnp.maximum(sc, 0.0) * w_ref[...]
        acc = jnp.sum(val.reshape(gsz, nt, page), axis=0)
        acc = jnp.where(acc == 0.0, 0.0, acc)
        if causal_new:
            acc = jnp.where(_iota(acc.shape, 1) <= _iota(acc.shape, 0), acc, -jnp.inf)
        key_scr[blk] = _f2key(acc)
        s_scr[blk] = _dot_nt(qs_ref[...], ks.astype(MXU_DTYPE))
        v_scr[blk] = vs.astype(MXU_DTYPE)

    @pl.when(s_id < nsteps)
    def _():
        for r in range(pps):
            process(s_id * pps + r, kit_refs[r][...], ks_refs[r][...], vs_refs[r][...], False)

    @pl.when(s_id == nsteps)
    def _():
        nblk = nsteps * pps + 1
        process(nblk - 1, kint_ref[...], ksn_ref[...], vsn_ref[...], True)

        keys = jnp.concatenate([key_scr[c] for c in range(nblk)], axis=1)

        def count_ge_all(cand):
            return jnp.sum(jnp.where(keys >= cand, 1.0, 0.0), axis=1, keepdims=True)

        def bit_body(it, prefix):
            cand = prefix + lax.shift_left(jnp.int32(1), 31 - it)
            return jnp.where(count_ge_all(cand) >= topk, cand, prefix)

        kth = lax.fori_loop(0, 32, bit_body, jnp.full((nt, 1), INT_MIN, i32))
        _demote_surplus_ties_lane(key_scr, nblk, nt, topk, kth, count_ge_all)
        thr = jnp.maximum(kth, KEY_NEG_INF + 1)

        unroll = max(u for u in (5, 4, 3, 2, 1) if nblk % u == 0)

        def sel_scores(c):
            msk = key_scr[c] >= thr
            return msk[None], s_scr[c].reshape(gsz, nt, page)

        def max_body(it, mx):
            for u in range(unroll):
                msk, s3 = sel_scores(it * unroll + u)
                mx = jnp.maximum(mx, jnp.where(msk, s3, NEG_BIG))
            return mx

        mx = lax.fori_loop(0, nblk // unroll, max_body, jnp.full((gsz, nt, page), NEG_BIG, f32))
        m = jnp.max(mx, axis=2, keepdims=True)

        def att_body(it, carry):
            lacc, acc = carry
            for u in range(unroll):
                c = it * unroll + u
                msk, s3 = sel_scores(c)
                p = jnp.where(msk, jnp.exp2(s3 - m), 0.0)
                lacc = lacc + p
                acc = acc + _dot(p.reshape(gsz * nt, page).astype(MXU_DTYPE), v_scr[c])
            return lacc, acc

        lacc, acc = lax.fori_loop(0, nblk // unroll, att_body,
                                  (jnp.zeros((gsz, nt, page), f32), jnp.zeros((LANES, D_DSA), f32)))
        l = jnp.sum(lacc, axis=2, keepdims=True).reshape(gsz * nt, 1)
        o_ref[...] = (acc / l * sg_ref[...].astype(f32)).astype(o_ref.dtype)


def _dsa_sample(page_table, qs_rows, qi_rows, wcol, cache_kit, cache_ks, cache_vs, kit_new, ks_new, vs_new, sg_perm,
                *, layer, nt, topk):
    db = qs_rows.shape[0]
    page = cache_ks.shape[2]
    npages = page_table.shape[1]
    pps = math.gcd(DSA_PAGES_PER_STEP, npages)
    nsteps = npages // pps
    nblk = npages + 1
    assert page == LANES

    def page_map(r):
        def f(bb, s, pt):
            return (layer, pt[bb, jnp.minimum(s * pps + r, npages - 1)], 0, 0)
        return f

    def pspecs(shape):
        return [pl.BlockSpec((None, None) + shape, page_map(r)) for r in range(pps)]

    per_b = lambda shape: pl.BlockSpec((None,) + shape, lambda bb, s, pt: (bb, 0, 0))
    grid_spec = pltpu.PrefetchScalarGridSpec(
        num_scalar_prefetch=1,
        grid=(db, nsteps + 1),
        in_specs=[
            per_b((LANES, D_DSA)), per_b((LANES, D_IDX)), per_b((LANES, 1)),
            *pspecs((D_IDX, page)), *pspecs((page, D_DSA)), *pspecs((page, D_DSA)),
            per_b((D_IDX, page)), per_b((page, D_DSA)), per_b((page, D_DSA)),
            per_b((LANES, D_DSA)),
        ],
        out_specs=per_b((LANES, D_DSA)),
        scratch_shapes=[
            pltpu.VMEM((nblk, nt, page), i32),
            pltpu.VMEM((nblk, LANES, page), f32),
            pltpu.VMEM((nblk, page, D_DSA), MXU_DTYPE),
        ],
    )
    return pl.pallas_call(
        functools.partial(_dsa_sample_kernel, nt=nt, pps=pps, nsteps=nsteps, topk=topk, page=page),
        grid_spec=grid_spec,
        out_shape=jax.ShapeDtypeStruct((db, LANES, D_DSA), MXU_DTYPE),
        compiler_params=_cparams(("arbitrary", "arbitrary")),
        name="dsa_sample",
    )(page_table, qs_rows, qi_rows, wcol, *([cache_kit] * pps), *([cache_ks] * pps), *([cache_vs] * pps),
      kit_new, ks_new, vs_new, sg_perm)


def _rope_tables(pos, head_dim):
    half = head_dim // 2
    lane = jnp.arange(LANES)
    inv = ROPE_THETA ** (-(lane % half).astype(f32) / half)
    ang = pos.astype(f32)[:, None] * inv[None, :]
    sign = jnp.where((lane % head_dim) < half, -1.0, 1.0).astype(f32)
    return jnp.cos(ang), jnp.sin(ang) * sign[None, :]


def _tile_gain(g, n):
    return jnp.tile(g, n // g.shape[0]).reshape(1, n).astype(f32)


def _mixer_inputs(x2d, pos_tab, w_in_t, layer, seg, gl, *, tm, stacked=None):
    cos64, sin64, cos128, sin128 = pos_tab
    m = x2d.shape[0]
    h = _rmsnorm(x2d, gl["g_in"], min(tm, 512))
    tn = 512
    proj = functools.partial(_proj, h, tm=tm, nt=True)
    wseg = lambda name: _win_desc(w_in_t, layer, seg[name][0], seg[name][1] - seg[name][0], tn)
    rows = lambda n, dt: _rows_out(m, n, dt, tm, tn)
    rope64 = [("row", cos64), ("row", sin64)]
    rope128 = [("row", cos128), ("row", sin128)]
    nqd = seg["qd"][1] - seg["qd"][0]
    nqs = seg["qs"][1] - seg["qs"][0]
    nqm = seg["qm"][1] - seg["qm"][0]
    ngate = seg["gate"][1] - seg["gate"][0]
    norm_rope64 = functools.partial(_chunks_norm_rope, gsize=D_DH, half=D_DH // 2)
    gq64 = [("col", _tile_gain(gl["g_q_diff"], nqd) * (D_DH ** -0.5 * LOG2E))] + rope64
    gk64 = [("col", _tile_gain(gl["g_k_diff"], nqd))] + rope64
    gq128 = [("col", _tile_gain(gl["g_q_dsa"], nqs) * (D_DSA ** -0.5 * LOG2E))] + rope128
    qi_scale = D_IDX ** -0.5 * H_IDX ** -0.5
    out = {}
    (out["qd"],) = proj(wseg("qd"), norm_rope64, _store_rows, [rows(nqd, MXU_DTYPE)], gq64, name="proj_qd")
    (out["qs"],) = proj(wseg("qs"), functools.partial(_chunks_norm_rope, gsize=D_DSA, half=D_DSA // 2), _store_rows,
                        [rows(nqs, MXU_DTYPE)], gq128, name="proj_qs")
    (out["qi"],) = proj(wseg("qi"), functools.partial(_chunks_rope, half=D_IDX // 2), _store_rows,
                        [rows(H_IDX * D_IDX, MXU_DTYPE)], [("row", cos64 * qi_scale), ("row", sin64 * qi_scale)],
                        name="proj_qi")
    (out["qm"],) = proj(wseg("qm"), _chunks_norm256, _store_rows, [rows(nqm, MXU_DTYPE)],
                        [("col", _tile_gain(gl["g_q_mem"], nqm) * (D_MEM ** -0.5 * LOG2E))], name="proj_qm")
    (out["sg"],) = proj(wseg("gate"), _chunks_silu, _store_rows, [rows(ngate, MXU_DTYPE)], name="proj_gate")

    ks0, vs0, ki0 = seg["ks"][0], seg["vs"][0], seg["ki"][0]
    assert vs0 - ks0 == LANES and ks0 % LANES == 0 and ki0 % LANES == 0
    skip = (ki0 - vs0) // LANES - 1
    w_ksv = dict(array=w_in_t, n=3 * LANES, tn=LANES,
                 spec=pl.BlockSpec((None, LANES, w_in_t.shape[2]),
                                   lambda i, j: (layer, ks0 // LANES + j + jnp.where(j == 2, skip, 0), 0)))
    g_kiwi = jnp.concatenate([gl["g_k_idx"], jnp.ones((LANES - D_IDX,), f32)]).reshape(1, LANES)
    aux_ksv = [("const", gl["g_k_dsa"].reshape(1, D_DSA))] + rope128 + [("const", g_kiwi)] + rope64
    blk = lambda dt: dict(shape=(m, LANES), dtype=dt, block=(tm, LANES), index=lambda i, j: (i, 0), alias=None)
    if stacked is None:
        out["kd"], out["kd_c"] = proj(wseg("kd"), norm_rope64, _store_rows,
                                      [rows(nqd, f32), rows(nqd, MXU_DTYPE)], gk64, name="proj_kd")
        out["vd"], out["vd_c"] = proj(wseg("vd"), _chunks_raw, _store_rows,
                                      [rows(nqd, f32), rows(nqd, MXU_DTYPE)], name="proj_vd")
        out["ks"], out["ks_c"], out["vs"], out["vs_c"], out["kiwi"], out["ki2"] = proj(
            w_ksv, _chunks_ks_vs_kiwi, _store_ks_vs_kiwi,
            [blk(f32), blk(MXU_DTYPE), blk(f32), blk(MXU_DTYPE), blk(f32), blk(MXU_DTYPE)], aux_ksv,
            name="proj_ks_vs_kiwi", pass_col=True)
        return out

    l = layer
    b, t = stacked["b"], stacked["t"]
    nrb = t // tm
    nh = nqd // LANES
    hb = _col_tile(nqd, tn) // LANES
    vkb = stacked["diff_tk"]
    assert tm % vkb == 0
    depth = stacked["pdk"].shape[0]
    out["pdk"], out["kd_c"] = proj(
        wseg("kd"), norm_rope64, _store_kd_prompt,
        [dict(shape=(depth, b, nh, LANES, t), dtype=f32, block=(None, None, hb, LANES, tm),
              index=lambda i, j: (l, i // nrb, j, 0, i % nrb), alias=stacked["pdk"]),
         rows(nqd, MXU_DTYPE)], gk64, name="proj_kd")
    out["pdv"], out["vd_c"] = proj(
        wseg("vd"), _chunks_raw, _store_vd_prompt,
        [dict(shape=(depth, b, nh, t, LANES), dtype=f32, block=(None, None, hb, tm, LANES),
              index=lambda i, j: (l, i // nrb, j, i % nrb, 0), alias=stacked["pdv"]),
         dict(shape=(b, nh, t // vkb, LANES, vkb), dtype=MXU_DTYPE, block=(None, hb, tm // vkb, LANES, vkb),
              index=lambda i, j: (i // nrb, j, i % nrb, 0, 0), alias=None)], name="proj_vd")
    tok_major = lambda key: dict(shape=(depth, b, t, D_DSA), dtype=f32, block=(None, None, tm, D_DSA),
                                 index=lambda i, j: (l, i // nrb, i % nrb, 0), alias=stacked[key])
    out["psk"], out["ks_c"], out["psv"], out["vs_c"], out["kiwi"], out["ki2"], out["pik"] = proj(
        w_ksv, _chunks_ks_vs_kiwi, _store_ks_vs_kiwi,
        [tok_major("psk"), blk(MXU_DTYPE), tok_major("psv"), blk(MXU_DTYPE), blk(f32), blk(MXU_DTYPE),
         dict(shape=(depth, b, D_IDX, t), dtype=f32, block=(None, None, D_IDX, tm),
              index=lambda i, j: (l, i // nrb, 0, i % nrb), alias=stacked["pik"])], aux_ksv, name="proj_ks_vs_kiwi",
        pass_col=True)
    return out


def kernel(x_prompt, x_sample, mem_prompt, cache_diff_k, cache_diff_v, cache_dsa_k, cache_dsa_v, cache_idx_k,
           cache_mem_k, cache_mem_v, page_table, w_in, w_out, w_mem_kv, g_in, g_mem, g_q_diff, g_k_diff,
           g_sub_diff, lam_q1, lam_k1, lam_q2, lam_k2, g_q_dsa, g_k_dsa, g_k_idx, g_q_mem, g_k_mem):
    depth = w_in.shape[0]
    b, t, d = x_prompt.shape
    db, nt, _ = x_sample.shape
    n_mem = mem_prompt.shape[1]
    n_phys, page = cache_dsa_k.shape[1], cache_dsa_k.shape[2]
    npages = page_table.shape[1]
    past = npages * page
    h_diff = cache_diff_k.shape[3]
    w_diff = h_diff * 2 * D_DH
    h_dsa = (3 * d // 8) // D_DSA
    w_dsa = h_dsa * D_DSA
    h_mem = cache_mem_k.shape[3]
    w_mem = h_mem * D_MEM
    gsz = LANES // nt
    assert w_diff + w_dsa + w_mem == d and w_diff == w_dsa and LANES % nt == 0 and h_dsa <= gsz and gsz == H_IDX
    topk_p = min(TOPK_MAX, t // 4)
    topk_s = min(TOPK_MAX, (past + nt) // 4)
    m_p = b * t
    m_s = db * nt
    tm_p = min(1024, t)
    assert t % tm_p == 0

    widths = (w_diff, w_diff, w_diff, w_dsa, D_DSA, D_DSA, H_IDX * D_IDX, D_IDX, H_IDX, w_mem, d)
    offs = [0]
    for wdt in widths:
        offs.append(offs[-1] + wdt)
    names = ("qd", "kd", "vd", "qs", "ks", "vs", "qi", "ki", "wi", "qm", "gate")
    seg = {n: (offs[k], offs[k + 1]) for k, n in enumerate(names)}

    pos_p = jnp.arange(t, dtype=i32)
    pos_s = jnp.tile(past + jnp.arange(nt, dtype=i32), db)
    tab_p = _rope_tables(pos_p, D_DH) + _rope_tables(pos_p, D_DSA)
    tab_s = _rope_tables(pos_s, D_DH) + _rope_tables(pos_s, D_DSA)

    cdkt = cache_diff_k.transpose(0, 1, 3, 4, 5, 2).reshape(depth, n_phys, h_diff, 2 * D_DH, page)
    cdv = cache_diff_v.transpose(0, 1, 3, 2, 4)
    ckit = cache_idx_k.transpose(0, 1, 3, 2)
    cmk = cache_mem_k.reshape(depth, db, n_mem, w_mem)
    cmv = cache_mem_v.reshape(depth, db, n_mem, w_mem)
    w_in_t = jnp.swapaxes(w_in, 1, 2)

    x_p = x_prompt.reshape(m_p, d)
    x_s = x_sample.reshape(m_s, d)
    mem2d = mem_prompt.reshape(b * n_mem, d)
    stk = dict(pdk=jnp.zeros((depth, b, h_diff, 2 * D_DH, t), f32), pdv=jnp.zeros((depth, b, h_diff, t, 2 * D_DH), f32),
               psk=jnp.zeros((depth, b, t, D_DSA), f32), psv=jnp.zeros((depth, b, t, D_DSA), f32),
               pik=jnp.zeros((depth, b, D_IDX, t), f32))
    outs = {k: [] for k in ("pmk", "pmv", "sdk", "sdv", "ssk", "ssv", "sik")}

    for l in range(depth):
        gl = dict(g_in=g_in[l], g_q_diff=g_q_diff[l], g_k_diff=g_k_diff[l], g_q_dsa=g_q_dsa[l],
                  g_k_dsa=g_k_dsa[l], g_k_idx=g_k_idx[l], g_q_mem=g_q_mem[l])
        lam_init = 0.8 - 0.6 * math.exp(-0.3 * l)
        lam = (jnp.exp(jnp.sum(lam_q1[l] * lam_k1[l])) - jnp.exp(jnp.sum(lam_q2[l] * lam_k2[l])) + lam_init)
        lam = lam.astype(f32).reshape(1)
        g_sub = g_sub_diff[l].reshape(1, 2 * D_DH)
        post = 1.0 - lam_init

        diff_tq, diff_tk = min(DIFF_TQ, t), min(DIFF_TK, t)
        tp = _mixer_inputs(x_p, tab_p, w_in_t, l, seg, gl, tm=tm_p, stacked=dict(stk, b=b, t=t, diff_tk=diff_tk))
        for key in ("pdk", "pdv", "psk", "psv", "pik"):
            stk[key] = tp[key]
        hm = _rmsnorm(mem2d, g_mem[l], 512)
        mrows = lambda dt: [_rows_out(b * n_mem, w_mem, dt, b * n_mem, 512)]
        mk, = _proj(hm, _wcols_desc(w_mem_kv, l, 0, w_mem, 512), _chunks_norm256,
                    _store_rows, mrows(f32), [("col", _tile_gain(g_k_mem[l], w_mem))], tm=b * n_mem, nt=False,
                    name="proj_mk")
        mv, = _proj(hm, _wcols_desc(w_mem_kv, l, w_mem, w_mem, 512), _chunks_raw, _store_rows, mrows(f32),
                    tm=b * n_mem, nt=False, name="proj_mv")
        r3 = lambda a: a.reshape(b, t, a.shape[-1])
        sg = r3(tp["sg"])
        od = _diff_prompt(lam, r3(tp["qd"]), r3(tp["kd_c"]), tp["vd_c"], g_sub, sg, post_scale=post,
                          tq=diff_tq, tk=diff_tk)
        tc = 256
        vst = tp["vs_c"].reshape(b, t // tc, tc, D_DSA).transpose(0, 1, 3, 2)
        os_ = _dsa_prompt(r3(tp["qs"]), r3(tp["qi"]), r3(tp["kiwi"]), r3(tp["ki2"]), r3(tp["ks_c"]), vst, sg,
                          topk=topk_p, tc=tc)
        om = _mem_attn(r3(tp["qm"]), mk.reshape(1, b, n_mem, w_mem), mv.reshape(1, b, n_mem, w_mem), sg,
                       layer=0, sg_col0=w_diff + w_dsa, tq=min(1024, t))
        x_p = _outproj(od.reshape(m_p, w_diff), os_.reshape(m_p, w_dsa), om.reshape(m_p, w_mem), w_out, x_p,
                       layer=l, tm=tm_p, tn=512)
        outs["pmk"].append(mk.reshape(b, n_mem, h_mem, D_MEM))
        outs["pmv"].append(mv.reshape(b, n_mem, h_mem, D_MEM))

        ts = _mixer_inputs(x_s, tab_s, w_in_t, l, seg, gl, tm=m_s)
        s3 = lambda a: a.reshape(db, nt, a.shape[-1])
        sg_s = s3(ts["sg"])
        tokpad = lambda a, axis: jnp.pad(a, [(0, page - nt) if ax == axis else (0, 0) for ax in range(a.ndim)])
        q5 = ts["qd"].reshape(db, nt, h_diff, 2, D_DH).transpose(0, 2, 3, 1, 4)
        eye_c = jnp.eye(2, dtype=MXU_DTYPE)
        qm = (q5[:, :, :, :, None, :] * eye_c[None, None, :, None, :, None]).reshape(db, h_diff * 2 * nt, 2 * D_DH)
        kd4 = ts["kd"].reshape(db, nt, h_diff, 2 * D_DH)
        vd4 = ts["vd"].reshape(db, nt, h_diff, 2 * D_DH)
        od_s = _diff_sample(page_table, lam, qm, cdkt, cdv, tokpad(kd4.transpose(0, 2, 3, 1), 3),
                            tokpad(vd4.transpose(0, 2, 1, 3), 2), g_sub, sg_s, layer=l, nt=nt, post_scale=post)
        qs4 = ts["qs"].reshape(db, nt, h_dsa, D_DSA)
        slot_major = lambda a: a.transpose(0, 2, 1, 3).reshape(db, LANES, a.shape[-1])
        qs_rows = slot_major(jnp.pad(qs4, ((0, 0), (0, 0), (0, gsz - h_dsa), (0, 0))))
        qi_rows = slot_major(ts["qi"].reshape(db, nt, H_IDX, D_IDX))
        wcol = slot_major(ts["kiwi"][:, D_IDX:D_IDX + H_IDX].reshape(db, nt, H_IDX, 1))
        sg_dsa = sg_s[:, :, w_diff:w_diff + w_dsa].reshape(db, nt, h_dsa, D_DSA)
        sg_perm = slot_major(jnp.pad(sg_dsa, ((0, 0), (0, 0), (0, gsz - h_dsa), (0, 0))))
        ki_new = s3(ts["kiwi"][:, :D_IDX])
        os_s = _dsa_sample(page_table, qs_rows, qi_rows, wcol, ckit, cache_dsa_k, cache_dsa_v,
                           tokpad(ki_new.transpose(0, 2, 1), 2), tokpad(s3(ts["ks"]), 1), tokpad(s3(ts["vs"]), 1),
                           sg_perm, layer=l, nt=nt, topk=topk_s)
        os_s = os_s.reshape(db, gsz, nt, D_DSA)[:, :h_dsa].transpose(0, 2, 1, 3).reshape(m_s, w_dsa)
        om_s = _mem_attn(s3(ts["qm"]), cmk, cmv, sg_s, layer=l, sg_col0=w_diff + w_dsa, tq=nt)
        x_s = _outproj(od_s.reshape(m_s, w_diff), os_s, om_s.reshape(m_s, w_mem), w_out, x_s, layer=l, tm=m_s,
                       tn=512)
        outs["sdk"].append(ts["kd"].reshape(db, nt, h_diff, 2, D_DH))
        outs["sdv"].append(ts["vd"].reshape(db, nt, h_diff, 2 * D_DH))
        outs["ssk"].append(ts["ks"].reshape(db, nt, D_DSA))
        outs["ssv"].append(ts["vs"].reshape(db, nt, D_DSA))
        outs["sik"].append(ki_new)

    st = lambda k: jnp.stack(outs[k])
    p_diff_k = stk["pdk"].reshape(depth, b, h_diff, 2, D_DH, t).transpose(0, 1, 5, 2, 3, 4)
    p_diff_v = stk["pdv"].transpose(0, 1, 3, 2, 4)
    p_idx_k = stk["pik"].transpose(0, 1, 3, 2)
    return (x_p.reshape(b, t, d), x_s.reshape(db, nt, d),
            p_diff_k, p_diff_v, stk["psk"], stk["psv"], p_idx_k, st("pmk"), st("pmv"),
            st("sdk"), st("sdv"), st("ssk"), st("ssv"), st("sik"))
```
